```python
import jax
import jax.numpy as jnp
from jax import lax

D_MODEL = 2048
BATCH = 4
SEQ = 2048
DEPTH = 1

M_HEADS = 4
M_HEAD_DIM = D_MODEL // 8
M_WIDTH = M_HEADS * M_HEAD_DIM
M_CONV = 4
M_CHUNK = 128

A_HEADS = 16
A_KV_HEADS = 4
A_GROUP = A_HEADS // A_KV_HEADS
A_HEAD_DIM = D_MODEL // 32
A_WIDTH = A_HEADS * A_HEAD_DIM
A_KV_WIDTH = A_KV_HEADS * A_HEAD_DIM
A_CMP_BLOCK = 32
A_CMP_STRIDE = 16
A_CMP_HIDDEN = 4 * A_HEAD_DIM
A_SEL_BLOCK = 64
A_SEL_TOPK = 8
A_WINDOW = 512
Q_BLOCK = 128

RMS_EPS = 1e-6
LN_EPS = 1e-6
NEG_INF = -1e30
POS_INF = 1e30

IN_SPLITS = (
    M_WIDTH, M_WIDTH, M_WIDTH, M_WIDTH, M_HEADS, M_HEADS,
    A_WIDTH, A_KV_WIDTH, A_KV_WIDTH, A_KV_WIDTH, A_KV_WIDTH,
    A_KV_WIDTH, A_KV_WIDTH, 3 * A_HEADS, A_WIDTH,
    D_MODEL, D_MODEL,
)
D_IN = sum(IN_SPLITS)

kernel_name = 'hybrid_mlstm_nsa_gated_merge'


def split_cols(p, sizes):
    idx, acc = [], 0
    for s in sizes[:-1]:
        acc += s
        idx.append(acc)
    return jnp.split(p, idx, axis=-1)


def rms_norm(x, g):
    xf = x.astype(jnp.float32)
    y = xf * lax.rsqrt(jnp.mean(xf * xf, axis=-1, keepdims=True) + RMS_EPS)
    return (y * g.astype(jnp.float32)).astype(x.dtype)


def causal_depthwise_conv(x, w, b):
    k = w.shape[0]
    xp = jnp.pad(x, ((0, 0), (k - 1, 0), (0, 0)))
    y = lax.conv_general_dilated(xp, w[:, None, :], (1,), 'VALID',
                                 dimension_numbers=('NWC', 'WIO', 'NWC'),
                                 feature_group_count=x.shape[-1])
    return y + b


def mlstm_chunkwise(q, k, v, i_pre, f_pre):
    bsz, nh, s, d = q.shape
    nc = s // M_CHUNK
    k = k * d ** -0.5
    logf = jax.nn.log_sigmoid(f_pre)

    def to_chunks(t):
        return jnp.moveaxis(t.reshape(bsz, nh, nc, M_CHUNK, *t.shape[3:]), 2, 0)

    xs = (to_chunks(q), to_chunks(k), to_chunks(v), to_chunks(i_pre), to_chunks(logf))
    causal = jnp.tril(jnp.ones((M_CHUNK, M_CHUNK), dtype=bool))

    def step(carry, inp):
        c_st, n_st, m_st = carry
        qb, kb, vb, ib, fb = inp
        bcum = jnp.cumsum(fb, axis=-1)
        g_tot = bcum[..., -1]
        d_intra = jnp.where(causal, bcum[..., :, None] - bcum[..., None, :] + ib[..., None, :], NEG_INF)
        d_inter = bcum + m_st[..., None]
        m_t = jnp.maximum(d_inter, jnp.max(d_intra, axis=-1))
        w_intra = jnp.exp(d_intra - m_t[..., None])
        w_inter = jnp.exp(d_inter - m_t)
        sc = jnp.einsum('bhtd,bhsd->bhts', qb, kb) * w_intra
        num = jnp.einsum('bhts,bhsd->bhtd', sc, vb) + w_inter[..., None] * jnp.einsum('bhtd,bhde->bhte', qb, c_st)
        den = jnp.sum(sc, axis=-1) + w_inter * jnp.einsum('bhtd,bhd->bht', qb, n_st)
        h = num / jnp.maximum(jnp.abs(den), jnp.exp(-m_t))[..., None]
        d_state = g_tot[..., None] - bcum + ib
        m_new = jnp.maximum(g_tot + m_st, jnp.max(d_state, axis=-1))
        w_s = jnp.exp(d_state - m_new[..., None])
        decay = jnp.exp(g_tot + m_st - m_new)
        c_new = decay[..., None, None] * c_st + jnp.einsum('bhs,bhsd,bhse->bhde', w_s, kb, vb)
        n_new = decay[..., None] * n_st + jnp.einsum('bhs,bhsd->bhd', w_s, kb)
        return (c_new, n_new, m_new), h

    init = (jnp.zeros((bsz, nh, d, d), q.dtype), jnp.zeros((bsz, nh, d), q.dtype), jnp.zeros((bsz, nh), q.dtype))
    _, hs = lax.scan(step, init, xs)
    return jnp.moveaxis(hs, 0, 2).reshape(bsz, nh, s, d)


def mlstm_branch(x_m, v, o_pre, i_pre, f_pre, conv_w, conv_b, w_q, w_k, b_i, b_f, norm_g, skip):
    bsz, s, _ = x_m.shape
    x_c = jax.nn.silu(causal_depthwise_conv(x_m, conv_w, conv_b))
    xc_h = x_c.reshape(bsz, s, M_HEADS, M_HEAD_DIM)
    q = jnp.einsum('bshd,hde->bhse', xc_h, w_q)
    k = jnp.einsum('bshd,hde->bhse', xc_h, w_k)
    vh = v.reshape(bsz, s, M_HEADS, M_HEAD_DIM).transpose(0, 2, 1, 3)
    ig = jnp.swapaxes(i_pre + b_i, 1, 2)
    fg = jnp.swapaxes(f_pre + b_f, 1, 2)
    h = mlstm_chunkwise(q, k, vh, ig, fg)
    h = jnp.swapaxes(h, 1, 2) * jax.nn.sigmoid(o_pre).reshape(bsz, s, M_HEADS, M_HEAD_DIM)
    mu = jnp.mean(h, axis=-1, keepdims=True)
    var = jnp.mean(jnp.square(h - mu), axis=-1, keepdims=True)
    hn = ((h - mu) * lax.rsqrt(var + LN_EPS)).reshape(bsz, s, M_WIDTH)
    return hn * norm_g + skip * x_c


def compress_blocks(t, pe, w1, w2):
    bsz, g, s, dh = t.shape
    n_cmp = (s - A_CMP_BLOCK) // A_CMP_STRIDE + 1
    idx = jnp.arange(n_cmp)[:, None] * A_CMP_STRIDE + jnp.arange(A_CMP_BLOCK)[None, :]
    blocks = (t[:, :, idx] + pe).reshape(bsz, g, n_cmp, A_CMP_BLOCK * dh)
    return jax.nn.silu(blocks @ w1) @ w2


def selected_attention(qh, k, v, sel_idx):
    bsz, g, hpg, s, dh = qh.shape
    n_sel = s // A_SEL_BLOCK
    topk = sel_idx.shape[-1]
    n_qb = s // Q_BLOCK
    kb = k.reshape(bsz, g, n_sel, A_SEL_BLOCK, dh)
    vb = v.reshape(bsz, g, n_sel, A_SEL_BLOCK, dh)
    gather = jax.vmap(jax.vmap(lambda blocks, ix: blocks[ix]))
    q_blk = jnp.moveaxis(qh.reshape(bsz, g, hpg, n_qb, Q_BLOCK, dh), 3, 0)
    i_blk = jnp.moveaxis(sel_idx.reshape(bsz, g, n_qb, Q_BLOCK, topk), 2, 0)
    t_blk = jnp.arange(s).reshape(n_qb, Q_BLOCK)

    def one_block(args):
        qq, ii, tt = args
        kg = gather(kb, ii)
        vg = gather(vb, ii)
        pos = ii[..., None] * A_SEL_BLOCK + jnp.arange(A_SEL_BLOCK)
        valid = (pos <= tt[:, None, None])[:, :, None]
        sc = jnp.where(valid, jnp.einsum('bghqd,bgqkld->bghqkl', qq, kg), NEG_INF)
        p = jax.nn.softmax(sc.reshape(*sc.shape[:4], -1), axis=-1).reshape(sc.shape)
        return jnp.einsum('bghqkl,bgqkld->bghqd', p, vg)

    o = lax.map(one_block, (q_blk, i_blk, t_blk))
    return jnp.moveaxis(o, 0, 3).reshape(bsz, g, hpg, s, dh)


def window_attention(qh, k, v):
    bsz, g, hpg, s, dh = qh.shape
    n_qb = s // Q_BLOCK
    n_wb = -(-A_WINDOW // Q_BLOCK)
    pad = n_wb * Q_BLOCK
    band_len = pad + Q_BLOCK

    def banded(t):
        tp = jnp.pad(t, ((0, 0), (0, 0), (pad, 0), (0, 0))).reshape(bsz, g, n_qb + n_wb, Q_BLOCK, dh)
        return jnp.concatenate([tp[:, :, i:i + n_qb] for i in range(n_wb + 1)], axis=3)

    kb, vb = banded(k), banded(v)
    qb = qh.reshape(bsz, g, hpg, n_qb, Q_BLOCK, dh)
    rel = pad + jnp.arange(Q_BLOCK)[:, None] - jnp.arange(band_len)[None, :]
    in_band = (rel >= 0) & (rel < A_WINDOW)
    real = (jnp.arange(n_qb)[:, None] * Q_BLOCK + jnp.arange(band_len)[None, :]) >= pad
    valid = in_band[None] & real[:, None, :]
    sc = jnp.where(valid, jnp.einsum('bghnqd,bgnkd->bghnqk', qb, kb), NEG_INF)
    p = jax.nn.softmax(sc, axis=-1)
    return jnp.einsum('bghnqk,bgnkd->bghnqd', p, vb).reshape(bsz, g, hpg, s, dh)


def nsa_branch(q, k_c, v_c, k_s, v_s, k_w, v_w, gate_pre, pe_k, w1_k, w2_k, pe_v, w1_v, w2_v):
    bsz, s, _ = q.shape
    qh = q.reshape(bsz, s, A_KV_HEADS, A_GROUP, A_HEAD_DIM).transpose(0, 2, 3, 1, 4) * A_HEAD_DIM ** -0.5

    def kvh(t):
        return t.reshape(bsz, s, A_KV_HEADS, A_HEAD_DIM).transpose(0, 2, 1, 3)

    t_pos = jnp.arange(s)
    kc = compress_blocks(kvh(k_c), pe_k, w1_k, w2_k)
    vc = compress_blocks(kvh(v_c), pe_v, w1_v, w2_v)
    n_cmp = kc.shape[2]
    cmp_start = jnp.arange(n_cmp) * A_CMP_STRIDE
    cmp_end = cmp_start + A_CMP_BLOCK - 1
    cmp_valid = cmp_end[None, :] <= t_pos[:, None]
    s_c = jnp.where(cmp_valid, jnp.einsum('bghtd,bgnd->bghtn', qh, kc), NEG_INF)
    p_c = jax.nn.softmax(s_c, axis=-1) * cmp_valid
    o_c = jnp.einsum('bghtn,bgnd->bghtd', p_c, vc)
    n_sel = s // A_SEL_BLOCK
    sel_start = jnp.arange(n_sel) * A_SEL_BLOCK
    overlap = ((cmp_start[:, None] < sel_start[None, :] + A_SEL_BLOCK)
               & (cmp_end[:, None] >= sel_start[None, :])).astype(p_c.dtype)
    imp = jnp.einsum('bghtn,nj->bgtj', p_c, overlap)
    cur = (t_pos // A_SEL_BLOCK)[:, None]
    j = jnp.arange(n_sel)[None, :]
    forced = (j == 0) | (j == cur) | (j == cur - 1)
    score = jnp.where(forced, POS_INF, jnp.where(j <= cur, imp, NEG_INF))
    _, sel_idx = lax.top_k(score, min(A_SEL_TOPK, n_sel))
    o_s = selected_attention(qh, kvh(k_s), kvh(v_s), sel_idx)
    o_w = window_attention(qh, kvh(k_w), kvh(v_w))
    g = jax.nn.sigmoid(gate_pre).reshape(bsz, s, A_KV_HEADS, A_GROUP, 3).transpose(0, 2, 3, 1, 4)
    o = g[..., 0:1] * o_c + g[..., 1:2] * o_s + g[..., 2:3] * o_w
    return o.transpose(0, 3, 1, 2, 4).reshape(bsz, s, A_WIDTH)


def setup_inputs(seed: int = 0) -> dict:
    key = jax.random.key(seed)
    ks = jax.random.split(key, 24)

    def nrm(k, shape, scale):
        return jax.random.normal(k, shape, jnp.float32) * scale

    L = DEPTH
    cmp_in = A_CMP_BLOCK * A_HEAD_DIM
    return {
        'x': nrm(ks[0], (BATCH, SEQ, D_MODEL), 1.0),
        'pre_norm_g': 1.0 + nrm(ks[1], (L, D_MODEL), 0.02),
        'w_in': nrm(ks[2], (L, D_MODEL, D_IN), D_MODEL ** -0.5),
        'm_conv_w': nrm(ks[3], (L, M_CONV, M_WIDTH), M_CONV ** -0.5),
        'm_conv_b': nrm(ks[4], (L, M_WIDTH), 0.01),
        'm_w_q': nrm(ks[5], (L, M_HEADS, M_HEAD_DIM, M_HEAD_DIM), M_HEAD_DIM ** -0.5),
        'm_w_k': nrm(ks[6], (L, M_HEADS, M_HEAD_DIM, M_HEAD_DIM), M_HEAD_DIM ** -0.5),
        'm_b_i': nrm(ks[7], (L, M_HEADS), 0.1),
        'm_b_f': jnp.linspace(3.0, 6.0, M_HEADS, dtype=jnp.float32)[None, :] + nrm(ks[8], (L, M_HEADS), 0.1),
        'm_norm_g': 1.0 + nrm(ks[9], (L, M_WIDTH), 0.02),
        'm_skip': 1.0 + nrm(ks[10], (L, M_WIDTH), 0.02),
        'a_pe_k': nrm(ks[11], (L, A_CMP_BLOCK, A_HEAD_DIM), 0.1),
        'a_w1_k': nrm(ks[12], (L, cmp_in, A_CMP_HIDDEN), cmp_in ** -0.5),
        'a_w2_k': nrm(ks[13], (L, A_CMP_HIDDEN, A_HEAD_DIM), A_CMP_HIDDEN ** -0.5),
        'a_pe_v': nrm(ks[14], (L, A_CMP_BLOCK, A_HEAD_DIM), 0.1),
        'a_w1_v': nrm(ks[15], (L, cmp_in, A_CMP_HIDDEN), cmp_in ** -0.5),
        'a_w2_v': nrm(ks[16], (L, A_CMP_HIDDEN, A_HEAD_DIM), A_CMP_HIDDEN ** -0.5),
        'w_up_m': nrm(ks[17], (L, M_WIDTH, D_MODEL), M_WIDTH ** -0.5),
        'w_up_a': nrm(ks[18], (L, A_WIDTH, D_MODEL), A_WIDTH ** -0.5),
        'w_out': nrm(ks[19], (L, D_MODEL, D_MODEL), D_MODEL ** -0.5),
        'post_norm_g': 1.0 + nrm(ks[20], (L, D_MODEL), 0.02),
    }


def reference(x, pre_norm_g, w_in, m_conv_w, m_conv_b, m_w_q, m_w_k, m_b_i, m_b_f, m_norm_g, m_skip,
              a_pe_k, a_w1_k, a_w2_k, a_pe_v, a_w1_v, a_w2_v, w_up_m, w_up_a, w_out, post_norm_g):
    f32 = jnp.float32
    for l in range(DEPTH):
        h = rms_norm(x, pre_norm_g[l])
        (m_x, m_v, m_o, m_z, m_i, m_f,
         a_q, a_kc, a_vc, a_ks, a_vs, a_kw, a_vw, a_g, a_z,
         g_m, g_a) = split_cols((h @ w_in[l]).astype(f32), IN_SPLITS)
        y_m = mlstm_branch(m_x, m_v, m_o, m_i, m_f,
                           m_conv_w[l].astype(f32), m_conv_b[l].astype(f32),
                           m_w_q[l].astype(f32), m_w_k[l].astype(f32),
                           m_b_i[l].astype(f32), m_b_f[l].astype(f32),
                           m_norm_g[l].astype(f32), m_skip[l].astype(f32)) * jax.nn.silu(m_z)
        y_a = nsa_branch(a_q, a_kc, a_vc, a_ks, a_vs, a_kw, a_vw, a_g,
                         a_pe_k[l].astype(f32), a_w1_k[l].astype(f32), a_w2_k[l].astype(f32),
                         a_pe_v[l].astype(f32), a_w1_v[l].astype(f32), a_w2_v[l].astype(f32)) * jax.nn.silu(a_z)
        merged = (jax.nn.sigmoid(g_m) * (y_m @ w_up_m[l].astype(f32))
                  + jax.nn.sigmoid(g_a) * (y_a @ w_up_a[l].astype(f32)))
        out = merged.astype(x.dtype) @ w_out[l]
        x = x + rms_norm(out, post_norm_g[l])
    return x
```

```python
import functools

import jax
import jax.numpy as jnp
from jax import lax
from jax.experimental import pallas as pl
from jax.experimental.pallas import tpu as pltpu

F32 = jnp.float32
BF16 = jnp.bfloat16

M_HEADS = 4
M_CONV = 4
M_CHUNK = 128
A_HEADS = 16
A_KV_HEADS = 4
A_GROUP = A_HEADS // A_KV_HEADS
A_HEAD_DIM = 64
A_CMP_BLOCK = 32
A_CMP_STRIDE = 16
A_SEL_BLOCK = 64
A_SEL_TOPK = 8
A_WINDOW = 512
Q_BLOCK = 128
RMS_EPS = 1e-6
LN_EPS = 1e-6
NEG_INF = -1e30
POS_INF = 1e30

LANES = 128
VMEM_LIMIT_BYTES = 48 * 1024 * 1024


def _nt_dot(a, b):
    return lax.dot_general(a, b, (((1,), (1,)), ((), ())), preferred_element_type=F32)


def _dot(a, b):
    return jnp.dot(a, b, preferred_element_type=F32)


def _sigmoid(x):
    return 1.0 / (1.0 + jnp.exp(-x))


def _params(*sem):
    return pltpu.CompilerParams(dimension_semantics=sem, vmem_limit_bytes=VMEM_LIMIT_BYTES)


def _inproj_kernel(x_ref, g_ref, w_ref, ws_ref, o_ref, os_ref, h_scr):
    @pl.when(pl.program_id(1) == 0)
    def _():
        xf = x_ref[...]
        ms = jnp.mean(xf * xf, axis=-1, keepdims=True)
        hb = (xf * lax.rsqrt(ms + RMS_EPS) * g_ref[...]).astype(BF16)
        h_scr[...] = hb
        os_ref[...] = _dot(hb, ws_ref[...])

    o_ref[...] = _dot(h_scr[...], w_ref[...]).astype(o_ref.dtype)


def _inproj(x2d, g, w_main, w_small, tm=512, tn=512):
    t, d = x2d.shape
    n = w_main.shape[1]
    ns = w_small.shape[1]
    return pl.pallas_call(
        _inproj_kernel,
        grid=(t // tm, n // tn),
        in_specs=[
            pl.BlockSpec((tm, d), lambda i, j: (i, 0)),
            pl.BlockSpec((1, d), lambda i, j: (0, 0)),
            pl.BlockSpec((d, tn), lambda i, j: (0, j)),
            pl.BlockSpec((d, ns), lambda i, j: (0, 0)),
        ],
        out_specs=[
            pl.BlockSpec((tm, tn), lambda i, j: (i, j)),
            pl.BlockSpec((tm, ns), lambda i, j: (i, 0)),
        ],
        out_shape=[jax.ShapeDtypeStruct((t, n), BF16), jax.ShapeDtypeStruct((t, ns), F32)],
        scratch_shapes=[pltpu.VMEM((tm, d), BF16)],
        compiler_params=_params("parallel", "arbitrary"),
        name="inproj",
    )(x2d, g, w_main, w_small)


def _conv_qk_kernel(x_ref, cw_ref, cb_ref, wq_ref, wk_ref, xc_ref, q_ref, k_ref, pad_ref, *, rows):
    s, c = x_ref.shape
    front = 8
    pad_ref[0:front, :] = jnp.zeros((front, c), F32)
    pad_ref[front:front + s, :] = x_ref[...].astype(F32)
    cw = cw_ref[...]
    cb = cb_ref[...]
    wq = wq_ref[...]
    wk = wk_ref[...]
    k_scale = c ** -0.5
    for r in range(s // rows):
        acc = jnp.broadcast_to(cb, (rows, c))
        for kk in range(M_CONV):
            start = front + r * rows + kk - (M_CONV - 1)
            acc = acc + cw[kk:kk + 1, :] * pad_ref[start:start + rows, :]
        xcb = (acc * _sigmoid(acc)).astype(BF16)
        xc_ref[r * rows:(r + 1) * rows, :] = xcb
        q_ref[r * rows:(r + 1) * rows, :] = _dot(xcb, wq).astype(BF16)
        k_ref[r * rows:(r + 1) * rows, :] = (_dot(xcb, wk) * k_scale).astype(BF16)


def _conv_qk(p, conv_w, conv_b, wq, wk, bsz, seq):
    t = p.shape[0]
    heads, hd = wq.shape[0], wq.shape[1]
    width = heads * hd
    blk = pl.BlockSpec((seq, hd), lambda b, h: (b, h))
    out = jax.ShapeDtypeStruct((t, width), BF16)
    return pl.pallas_call(
        functools.partial(_conv_qk_kernel, rows=256),
        grid=(bsz, heads),
        in_specs=[
            blk,
            pl.BlockSpec((M_CONV, hd), lambda b, h: (0, h)),
            pl.BlockSpec((1, hd), lambda b, h: (0, h)),
            pl.BlockSpec((None, hd, hd), lambda b, h: (h, 0, 0)),
            pl.BlockSpec((None, hd, hd), lambda b, h: (h, 0, 0)),
        ],
        out_specs=[blk, blk, blk],
        out_shape=[out, out, out],
        scratch_shapes=[pltpu.VMEM((seq + 8, hd), F32)],
        compiler_params=_params("parallel", "parallel"),
        name="conv_qk",
    )(p, conv_w, conv_b, wq, wk)


def _log_sigmoid(x):
    return jnp.minimum(x, 0.0) - jnp.log1p(jnp.exp(-jnp.abs(x)))


def _mlstm_kernel(bi_ref, bf_ref, q_ref, k_ref, v_ref, o_ref, z_ref, xc_ref, ig_ref, fg_ref, ng_ref, sk_ref,
                  y_ref, c_scr, n_scr, m_scr):
    head = pl.program_id(1)

    @pl.when(pl.program_id(2) == 0)
    def _():
        c_scr[...] = jnp.zeros_like(c_scr)
        n_scr[...] = jnp.zeros_like(n_scr)
        m_scr[...] = jnp.zeros_like(m_scr)

    ln = q_ref.shape[0]
    ib = ig_ref[...] + bi_ref[head]
    fb = _log_sigmoid(fg_ref[...] + bf_ref[head])
    row = lax.broadcasted_iota(jnp.int32, (ln, ln), 0)
    col = lax.broadcasted_iota(jnp.int32, (ln, ln), 1)
    tril = col <= row
    eye = col == row
    bcum_c = jnp.sum(jnp.where(tril, fb, 0.0), axis=1, keepdims=True)
    bcum_r = jnp.sum(jnp.where(eye, bcum_c, 0.0), axis=0, keepdims=True)
    g_tot = jnp.sum(fb, axis=1, keepdims=True)
    e_r = ib - bcum_r
    m_st = m_scr[...]
    d_intra = jnp.where(tril, bcum_c + e_r, NEG_INF)
    d_inter = bcum_c + m_st
    m_t = jnp.maximum(d_inter, jnp.max(d_intra, axis=1, keepdims=True))
    w_intra = jnp.exp(d_intra - m_t)
    w_inter = jnp.exp(d_inter - m_t)

    qb = q_ref[...]
    kb = k_ref[...]
    vb = v_ref[...]
    c_st = c_scr[...]
    n_st = n_scr[...]
    sc = _nt_dot(qb, kb) * w_intra
    num = _dot(sc.astype(BF16), vb) + w_inter * _dot(qb, c_st.astype(BF16))
    qn = jnp.sum(qb.astype(F32) * n_st, axis=1, keepdims=True)
    den = jnp.sum(sc, axis=1, keepdims=True) + w_inter * qn
    h = num / jnp.maximum(jnp.abs(den), jnp.exp(-m_t))

    d_state = g_tot + e_r
    m_new = jnp.maximum(g_tot + m_st, jnp.max(d_state, axis=1, keepdims=True))
    w_s_r = jnp.exp(d_state - m_new)
    w_s_c = jnp.sum(jnp.where(eye, w_s_r, 0.0), axis=1, keepdims=True)
    decay = jnp.exp(g_tot + m_st - m_new)
    kw = kb.astype(F32) * w_s_c
    kv = lax.dot_general(kw.astype(BF16), vb, (((0,), (0,)), ((), ())), preferred_element_type=F32)
    c_scr[...] = decay * c_st + kv
    n_scr[...] = decay * n_st + jnp.sum(kw, axis=0, keepdims=True)
    m_scr[...] = m_new

    hg = h * _sigmoid(o_ref[...].astype(F32))
    mu = jnp.mean(hg, axis=-1, keepdims=True)
    hc = hg - mu
    var = jnp.mean(hc * hc, axis=-1, keepdims=True)
    hn = hc * lax.rsqrt(var + LN_EPS)
    z = z_ref[...].astype(F32)
    y = (hn * ng_ref[...] + sk_ref[...] * xc_ref[...].astype(F32)) * (z * _sigmoid(z))
    y_ref[...] = y.astype(y_ref.dtype)


def _mlstm(p, xc, q, k, ig, fg, b_i, b_f, norm_g, skip, bsz, seq):
    t, width = q.shape
    heads = M_HEADS
    hd = width // heads
    ln = M_CHUNK
    nc = seq // ln
    rows = lambda b, h, c: (b * nc + c, h)
    smem = pl.BlockSpec(memory_space=pltpu.SMEM)
    gate = pl.BlockSpec((None, None, None, 1, ln), lambda b, h, c: (b, h, c, 0, 0))
    vec = pl.BlockSpec((1, hd), lambda b, h, c: (0, h))
    return pl.pallas_call(
        _mlstm_kernel,
        grid=(bsz, heads, nc),
        in_specs=[
            smem, smem,
            pl.BlockSpec((ln, hd), rows),
            pl.BlockSpec((ln, hd), rows),
            pl.BlockSpec((ln, hd), lambda b, h, c: (b * nc + c, heads + h)),
            pl.BlockSpec((ln, hd), lambda b, h, c: (b * nc + c, 2 * heads + h)),
            pl.BlockSpec((ln, hd), lambda b, h, c: (b * nc + c, 3 * heads + h)),
            pl.BlockSpec((ln, hd), rows),
            gate, gate, vec, vec,
        ],
        out_specs=pl.BlockSpec((ln, hd), rows),
        out_shape=jax.ShapeDtypeStruct((t, width), BF16),
        scratch_shapes=[pltpu.VMEM((hd, hd), F32), pltpu.VMEM((1, hd), F32), pltpu.VMEM((1, 1), F32)],
        compiler_params=_params("parallel", "parallel", "arbitrary"),
        name="mlstm",
    )(b_i, b_f, q, k, p, p, p, xc, ig, fg, norm_g, skip)


def _compress_kernel(t_ref, pe_ref, w1_ref, w2_ref, o_ref):
    t = t_ref[...].astype(F32)
    n_sub, half = t.shape
    pe = pe_ref[...]
    za = _dot((t + pe[0:1, :]).astype(BF16), w1_ref[0:half, :])
    zb = _dot((t + pe[1:2, :]).astype(BF16), w1_ref[half:2 * half, :])
    pre = za + pltpu.roll(zb, n_sub - 1, 0)
    hid = (pre * _sigmoid(pre)).astype(BF16)
    out = _dot(hid, w2_ref[...])
    rid = lax.broadcasted_iota(jnp.int32, out.shape, 0)
    o_ref[...] = jnp.where(rid < n_sub - 1, out, 0.0).astype(o_ref.dtype)


def _compress(tkv, pe, w1, w2):
    two, bg, n_sub, half = tkv.shape
    hid = w1.shape[-1]
    dh = w2.shape[-1]
    return pl.pallas_call(
        _compress_kernel,
        grid=(two, bg),
        in_specs=[
            pl.BlockSpec((None, None, n_sub, half), lambda i, j: (i, j, 0, 0)),
            pl.BlockSpec((None, 2, half), lambda i, j: (i, 0, 0)),
            pl.BlockSpec((None, 2 * half, hid), lambda i, j: (i, 0, 0)),
            pl.BlockSpec((None, hid, dh), lambda i, j: (i, 0, 0)),
        ],
        out_specs=pl.BlockSpec((None, None, n_sub, dh), lambda i, j: (i, j, 0, 0)),
        out_shape=jax.ShapeDtypeStruct((two, bg, n_sub, dh), BF16),
        compiler_params=_params("parallel", "parallel"),
        name="compress",
    )(tkv, pe, w1, w2)


def _nsa_kernel(q_ref, kc_ref, vc_ref, ks_ref, vs_ref, kw_ref, vw_ref, gate_ref, z_ref, o_ref, msk_ref):
    qi = pl.program_id(2)
    tq = q_ref.shape[0]
    dh = A_HEAD_DIM
    hpg = A_GROUP
    r = hpg * tq
    n_cmp = kc_ref.shape[0]
    seq = ks_ref.shape[0]
    tk = LANES
    n_kv = seq // tk
    n_sel = seq // A_SEL_BLOCK
    t0 = qi * tq

    qf = q_ref[...].astype(F32) * (dh ** -0.5)
    q4 = jnp.concatenate([qf[:, h * dh:(h + 1) * dh] for h in range(hpg)], axis=0).astype(BF16)

    row = lax.broadcasted_iota(jnp.int32, (r, tk), 0)
    col = lax.broadcasted_iota(jnp.int32, (r, tk), 1)
    t_abs = t0 + (row & (tq - 1))

    s_c = _nt_dot(q4, kc_ref[...])
    valid = (col * A_CMP_STRIDE + (A_CMP_BLOCK - 1)) <= t_abs
    s_m = jnp.where(valid, s_c, NEG_INF)
    mx = jnp.max(s_m, axis=1, keepdims=True)
    e = jnp.where(valid, jnp.exp(s_m - mx), 0.0)
    lsum = jnp.sum(e, axis=1, keepdims=True)
    p_c = e * (1.0 / jnp.where(lsum > 0.0, lsum, 1.0))
    o_c = _dot(p_c.astype(BF16), vc_ref[...])
    p_sum = p_c[0:tq]
    for h in range(1, hpg):
        p_sum = p_sum + p_c[h * tq:(h + 1) * tq]

    jn = lax.broadcasted_iota(jnp.int32, (n_sel, n_cmp), 0)
    nn = lax.broadcasted_iota(jnp.int32, (n_sel, n_cmp), 1)
    ov = ((nn * A_CMP_STRIDE < (jn + 1) * A_SEL_BLOCK)
          & (nn * A_CMP_STRIDE + (A_CMP_BLOCK - 1) >= jn * A_SEL_BLOCK))
    ov_t = jnp.where(ov, 1.0, 0.0).astype(BF16)
    p_hi = p_sum.astype(BF16)
    r1 = p_sum - p_hi.astype(F32)
    p_mid = r1.astype(BF16)
    p_lo = (r1 - p_mid.astype(F32)).astype(BF16)
    imp_t = _nt_dot(ov_t, p_hi) + _nt_dot(ov_t, p_mid) + _nt_dot(ov_t, p_lo)

    jr = lax.broadcasted_iota(jnp.int32, (n_sel, tq), 0)
    tl = t0 + lax.broadcasted_iota(jnp.int32, (n_sel, tq), 1)
    cur = lax.shift_right_logical(tl, 6)
    forced = (jr == 0) | (jr == cur) | (jr == cur - 1)
    score = jnp.where(forced, POS_INF, jnp.where(jr <= cur, imp_t, NEG_INF))
    rank = jnp.zeros((n_sel, tq), F32)
    for jp in range(n_sel):
        sj = score[jp:jp + 1, :]
        beats = (sj > score) | ((sj == score) & (jr > jp))
        rank = rank + jnp.where(beats, 1.0, 0.0)
    sel_t = jnp.where(rank < float(min(A_SEL_TOPK, n_sel)), 1.0, 0.0).astype(BF16)
    ident = jnp.where(lax.broadcasted_iota(jnp.int32, (tq, tq), 0)
                      == lax.broadcasted_iota(jnp.int32, (tq, tq), 1), 1.0, 0.0).astype(BF16)
    sel = _nt_dot(ident, sel_t).astype(BF16)

    tq_abs = t0 + lax.broadcasted_iota(jnp.int32, (tq, tk), 0)
    ccol = lax.broadcasted_iota(jnp.int32, (tq, tk), 1)
    ej_r = lax.broadcasted_iota(jnp.int32, (n_sel, tk), 0)
    ej_c = lax.broadcasted_iota(jnp.int32, (n_sel, tk), 1)
    for jj in range(n_kv):
        expand = jnp.where(lax.shift_right_logical(ej_c + jj * tk, 6) == ej_r, 1.0, 0.0).astype(BF16)
        picked = _dot(sel, expand)
        allowed = (picked > 0.5) & ((ccol + jj * tk) <= tq_abs)
        msk_ref[jj] = jnp.where(allowed, 1.0, 0.0)

    def flash_step(s, v, carry):
        m, l, acc = carry
        m_new = jnp.maximum(m, jnp.max(s, axis=1, keepdims=True))
        alpha = jnp.exp(m - m_new)
        p = jnp.exp(s - m_new)
        l = alpha * l + jnp.sum(p, axis=1, keepdims=True)
        acc = alpha * acc + _dot(p.astype(BF16), v)
        return m_new, l, acc

    init = (jnp.full((r, 1), NEG_INF, F32), jnp.zeros((r, 1), F32), jnp.zeros((r, dh), F32))

    def sel_body(j, carry):
        off = pl.multiple_of(j * tk, tk)
        s = _nt_dot(q4, ks_ref[pl.ds(off, tk), :])
        flag = msk_ref[j]
        flag4 = jnp.concatenate([flag] * hpg, axis=0)
        s = jnp.where(flag4 > 0.5, s, NEG_INF)
        return flash_step(s, vs_ref[pl.ds(off, tk), :], carry)

    _, l_s, acc_s = lax.fori_loop(0, qi + 1, sel_body, init)
    o_s = acc_s * (1.0 / l_s)

    def win_body(j, carry):
        off = pl.multiple_of(j * tk, tk)
        s = _nt_dot(q4, kw_ref[pl.ds(off, tk), :])
        c_abs = off + col
        ok = (c_abs <= t_abs) & (c_abs > t_abs - A_WINDOW)
        s = jnp.where(ok, s, NEG_INF)
        return flash_step(s, vw_ref[pl.ds(off, tk), :], carry)

    n_wb = -(-A_WINDOW // tk)
    _, l_w, acc_w = lax.fori_loop(jnp.maximum(qi - n_wb, 0), qi + 1, win_body, init)
    o_w = acc_w * (1.0 / l_w)

    gs = _sigmoid(gate_ref[...])
    outs = []
    for h in range(hpg):
        sl = slice(h * tq, (h + 1) * tq)
        outs.append(gs[:, 3 * h:3 * h + 1] * o_c[sl] + gs[:, 3 * h + 1:3 * h + 2] * o_s[sl]
                    + gs[:, 3 * h + 2:3 * h + 3] * o_w[sl])
    o_all = jnp.concatenate(outs, axis=1)
    z = z_ref[...].astype(F32)
    o_ref[...] = (o_all * (z * _sigmoid(z))).astype(o_ref.dtype)


def _nsa(p, kvc, kv4, gates, q_col, z_col, bsz, seq):
    t = p.shape[0]
    g = A_KV_HEADS
    dh = A_HEAD_DIM
    gw = A_GROUP * dh
    tq = Q_BLOCK
    nq = seq // tq
    n_cmp = kvc.shape[2]
    q_col0 = q_col // gw
    z_col0 = z_col // gw
    cmp_spec = lambda i: pl.BlockSpec((None, None, n_cmp, dh), lambda b, gg, qi: (i, b * g + gg, 0, 0))
    kv_spec = lambda i: pl.BlockSpec((None, None, None, seq, dh), lambda b, gg, qi: (i, b, gg, 0, 0))
    return pl.pallas_call(
        _nsa_kernel,
        grid=(bsz, g, nq),
        in_specs=[
            pl.BlockSpec((tq, gw), lambda b, gg, qi: (b * nq + qi, q_col0 + gg)),
            cmp_spec(0), cmp_spec(1),
            kv_spec(0), kv_spec(1), kv_spec(2), kv_spec(3),
            pl.BlockSpec((None, None, tq, 3 * A_GROUP), lambda b, gg, qi: (b, gg, qi, 0)),
            pl.BlockSpec((tq, gw), lambda b, gg, qi: (b * nq + qi, z_col0 + gg)),
        ],
        out_specs=pl.BlockSpec((tq, gw), lambda b, gg, qi: (b * nq + qi, gg)),
        out_shape=jax.ShapeDtypeStruct((t, g * gw), BF16),
        scratch_shapes=[pltpu.VMEM((seq // LANES, tq, LANES), F32)],
        compiler_params=_params("parallel", "parallel", "arbitrary"),
        name="nsa",
    )(p, kvc, kvc, kv4, kv4, kv4, kv4, gates, p)


def _merge_kernel(ym_ref, ya_ref, wm_ref, wa_ref, gm_ref, ga_ref, o_ref):
    um = _dot(ym_ref[...], wm_ref[...])
    ua = _dot(ya_ref[...], wa_ref[...])
    o = _sigmoid(gm_ref[...].astype(F32)) * um + _sigmoid(ga_ref[...].astype(F32)) * ua
    o_ref[...] = o.astype(o_ref.dtype)


def _merge(ym, ya, wm, wa, p, gm_col, ga_col, tm=512, tn=512):
    t, kdim = ym.shape
    n = wm.shape[1]
    return pl.pallas_call(
        _merge_kernel,
        grid=(t // tm, n // tn),
        in_specs=[
            pl.BlockSpec((tm, kdim), lambda i, j: (i, 0)),
            pl.BlockSpec((tm, kdim), lambda i, j: (i, 0)),
            pl.BlockSpec((kdim, tn), lambda i, j: (0, j)),
            pl.BlockSpec((kdim, tn), lambda i, j: (0, j)),
            pl.BlockSpec((tm, tn), lambda i, j: (i, gm_col // tn + j)),
            pl.BlockSpec((tm, tn), lambda i, j: (i, ga_col // tn + j)),
        ],
        out_specs=pl.BlockSpec((tm, tn), lambda i, j: (i, j)),
        out_shape=jax.ShapeDtypeStruct((t, n), BF16),
        compiler_params=_params("parallel", "parallel"),
        name="merge",
    )(ym, ya, wm, wa, p, p)


def _outproj_kernel(mg_ref, w_ref, g_ref, x_ref, o_ref):
    out = _dot(mg_ref[...], w_ref[...])
    ms = jnp.mean(out * out, axis=-1, keepdims=True)
    o_ref[...] = x_ref[...] + out * lax.rsqrt(ms + RMS_EPS) * g_ref[...]


def _outproj(mg, w, g, x2d, tm=256):
    t, d = x2d.shape
    return pl.pallas_call(
        _outproj_kernel,
        grid=(t // tm,),
        in_specs=[
            pl.BlockSpec((tm, d), lambda i: (i, 0)),
            pl.BlockSpec((d, d), lambda i: (0, 0)),
            pl.BlockSpec((1, d), lambda i: (0, 0)),
            pl.BlockSpec((tm, d), lambda i: (i, 0)),
        ],
        out_specs=pl.BlockSpec((tm, d), lambda i: (i, 0)),
        out_shape=jax.ShapeDtypeStruct((t, d), F32),
        compiler_params=_params("parallel"),
        name="outproj",
    )(mg, w, g, x2d)


def _layer(x, pre_g, w_in, conv_w, conv_b, w_q, w_k, b_i, b_f, norm_g, skip,
           pe_k, w1_k, w2_k, pe_v, w1_v, w2_v, w_up_m, w_up_a, w_out, post_g):
    bsz, seq, d = x.shape
    t = bsz * seq
    mw = w_up_m.shape[0]
    aw = w_up_a.shape[0]
    kvw = A_KV_HEADS * A_HEAD_DIM
    n_gate = 3 * A_HEADS

    o_i = 4 * mw
    o_f = o_i + M_HEADS
    o_q = o_f + M_HEADS
    o_g = o_q + aw + 6 * kvw
    o_z = o_g + n_gate
    w_main = jnp.concatenate([w_in[:, :o_i], w_in[:, o_q:o_g], w_in[:, o_z:]], axis=1).astype(BF16)
    n_small = 2 * M_HEADS + n_gate
    w_small = jnp.concatenate(
        [w_in[:, o_i:o_q], w_in[:, o_g:o_z], jnp.zeros((d, LANES - n_small), w_in.dtype)], axis=1).astype(BF16)
    kv_col = 4 * mw + aw
    gm_col = kv_col + 6 * kvw + aw
    ga_col = gm_col + d

    x2d = x.reshape(t, d)
    p, ps = _inproj(x2d, pre_g.reshape(1, d), w_main, w_small)

    xc, q, k = _conv_qk(p, conv_w, conv_b.reshape(1, mw), w_q.astype(BF16), w_k.astype(BF16), bsz, seq)
    nc = seq // M_CHUNK
    gates_m = ps[:, :2 * M_HEADS].reshape(bsz, seq, 2, M_HEADS).transpose(2, 0, 3, 1)
    gates_m = gates_m.reshape(2, bsz, M_HEADS, nc, 1, M_CHUNK)
    y_m = _mlstm(p, xc, q, k, gates_m[0], gates_m[1], b_i, b_f,
                 norm_g.reshape(1, mw), skip.reshape(1, mw), bsz, seq)

    kv6 = p[:, kv_col:kv_col + 6 * kvw].reshape(bsz, seq, 6, A_KV_HEADS, A_HEAD_DIM).transpose(2, 0, 3, 1, 4)
    n_sub = seq // A_CMP_STRIDE
    t_cmp = kv6[0:2].reshape(2, bsz * A_KV_HEADS, n_sub, A_CMP_STRIDE * A_HEAD_DIM)
    half = A_CMP_STRIDE * A_HEAD_DIM
    pe = jnp.stack([pe_k.reshape(2, half), pe_v.reshape(2, half)])
    w1 = jnp.stack([w1_k, w1_v]).astype(BF16)
    w2 = jnp.stack([w2_k, w2_v]).astype(BF16)
    kvc = _compress(t_cmp, pe, w1, w2)
    gates_a = ps[:, 2 * M_HEADS:n_small].reshape(bsz, seq, A_KV_HEADS, 3 * A_GROUP).transpose(0, 2, 1, 3)
    y_a = _nsa(p, kvc, kv6[2:6], gates_a, 4 * mw, kv_col + 6 * kvw, bsz, seq)

    mg = _merge(y_m, y_a, w_up_m.astype(BF16), w_up_a.astype(BF16), p, gm_col, ga_col)
    out = _outproj(mg, w_out.astype(BF16), post_g.reshape(1, d), x2d)
    return out.reshape(bsz, seq, d)


def kernel(x, pre_norm_g, w_in, m_conv_w, m_conv_b, m_w_q, m_w_k, m_b_i, m_b_f, m_norm_g, m_skip, a_pe_k, a_w1_k, a_w2_k, a_pe_v, a_w1_v, a_w2_v, w_up_m, w_up_a, w_out, post_norm_g):
    depth = w_in.shape[0]
    for l in range(depth):
        x = _layer(x, pre_norm_g[l], w_in[l], m_conv_w[l], m_conv_b[l], m_w_q[l], m_w_k[l], m_b_i[l], m_b_f[l],
                   m_norm_g[l], m_skip[l], a_pe_k[l], a_w1_k[l], a_w2_k[l], a_pe_v[l], a_w1_v[l], a_w2_v[l],
                   w_up_m[l], w_up_a[l], w_out[l], post_norm_g[l])
    return x
```

```python
import functools

import jax
import jax.numpy as jnp
from jax import lax
from jax.experimental import pallas as pl
from jax.experimental.pallas import tpu as pltpu

F32 = jnp.float32
BF16 = jnp.bfloat16

M_HEADS = 4
M_CONV = 4
M_CHUNK = 128
A_HEADS = 16
A_KV_HEADS = 4
A_GROUP = A_HEADS // A_KV_HEADS
A_HEAD_DIM = 64
A_CMP_BLOCK = 32
A_CMP_STRIDE = 16
A_SEL_BLOCK = 64
A_SEL_TOPK = 8
A_WINDOW = 512
Q_BLOCK = 128
SEL_KEYS_PER_STEP = 512
RMS_EPS = 1e-6
LN_EPS = 1e-6
NEG_INF = -1e30
POS_INF = 1e30

LANES = 128
VMEM_LIMIT_BYTES = 48 * 1024 * 1024


def _nt_dot(a, b):
    return lax.dot_general(a, b, (((1,), (1,)), ((), ())), preferred_element_type=F32)


def _dot(a, b):
    return jnp.dot(a, b, preferred_element_type=F32)


def _sigmoid(x):
    return 1.0 / (1.0 + jnp.exp(-x))


def _params(*sem):
    return pltpu.CompilerParams(dimension_semantics=sem, vmem_limit_bytes=VMEM_LIMIT_BYTES)


def _inproj_kernel(x_ref, g_ref, w_ref, ws_ref, o_ref, os_ref, h_scr):
    @pl.when(pl.program_id(1) == 0)
    def _():
        xf = x_ref[...]
        ms = jnp.mean(xf * xf, axis=-1, keepdims=True)
        hb = (xf * lax.rsqrt(ms + RMS_EPS) * g_ref[...]).astype(BF16)
        h_scr[...] = hb
        os_ref[...] = _dot(hb, ws_ref[...])

    o_ref[...] = _dot(h_scr[...], w_ref[...]).astype(o_ref.dtype)


def _inproj(x2d, g, w_main, w_small, tm=512, tn=512):
    t, d = x2d.shape
    n = w_main.shape[1]
    ns = w_small.shape[1]
    return pl.pallas_call(
        _inproj_kernel,
        grid=(t // tm, n // tn),
        in_specs=[
            pl.BlockSpec((tm, d), lambda i, j: (i, 0)),
            pl.BlockSpec((1, d), lambda i, j: (0, 0)),
            pl.BlockSpec((d, tn), lambda i, j: (0, j)),
            pl.BlockSpec((d, ns), lambda i, j: (0, 0)),
        ],
        out_specs=[
            pl.BlockSpec((tm, tn), lambda i, j: (i, j)),
            pl.BlockSpec((tm, ns), lambda i, j: (i, 0)),
        ],
        out_shape=[jax.ShapeDtypeStruct((t, n), BF16), jax.ShapeDtypeStruct((t, ns), F32)],
        scratch_shapes=[pltpu.VMEM((tm, d), BF16)],
        compiler_params=_params("parallel", "arbitrary"),
        name="inproj",
    )(x2d, g, w_main, w_small)


def _conv_qk_kernel(x_ref, cw_ref, cb_ref, wq_ref, wk_ref, xc_ref, q_ref, k_ref, pad_ref, *, rows):
    s, c = x_ref.shape
    front = 8
    pad_ref[0:front, :] = jnp.zeros((front, c), F32)
    pad_ref[front:front + s, :] = x_ref[...].astype(F32)
    cw = cw_ref[...]
    cb = cb_ref[...]
    wq = wq_ref[...]
    wk = wk_ref[...]
    k_scale = c ** -0.5
    for r in range(s // rows):
        acc = jnp.broadcast_to(cb, (rows, c))
        for kk in range(M_CONV):
            start = front + r * rows + kk - (M_CONV - 1)
            acc = acc + cw[kk:kk + 1, :] * pad_ref[start:start + rows, :]
        xcb = (acc * _sigmoid(acc)).astype(BF16)
        xc_ref[r * rows:(r + 1) * rows, :] = xcb
        q_ref[r * rows:(r + 1) * rows, :] = _dot(xcb, wq).astype(BF16)
        k_ref[r * rows:(r + 1) * rows, :] = (_dot(xcb, wk) * k_scale).astype(BF16)


def _conv_qk(p, conv_w, conv_b, wq, wk, bsz, seq):
    t = p.shape[0]
    heads, hd = wq.shape[0], wq.shape[1]
    width = heads * hd
    blk = pl.BlockSpec((seq, hd), lambda b, h: (b, h))
    out = jax.ShapeDtypeStruct((t, width), BF16)
    return pl.pallas_call(
        functools.partial(_conv_qk_kernel, rows=256),
        grid=(bsz, heads),
        in_specs=[
            blk,
            pl.BlockSpec((M_CONV, hd), lambda b, h: (0, h)),
            pl.BlockSpec((1, hd), lambda b, h: (0, h)),
            pl.BlockSpec((None, hd, hd), lambda b, h: (h, 0, 0)),
            pl.BlockSpec((None, hd, hd), lambda b, h: (h, 0, 0)),
        ],
        out_specs=[blk, blk, blk],
        out_shape=[out, out, out],
        scratch_shapes=[pltpu.VMEM((seq + 8, hd), F32)],
        compiler_params=_params("parallel", "parallel"),
        name="conv_qk",
    )(p, conv_w, conv_b, wq, wk)


def _log_sigmoid(x):
    return jnp.minimum(x, 0.0) - jnp.log1p(jnp.exp(-jnp.abs(x)))


def _mlstm_kernel(bi_ref, bf_ref, q_ref, k_ref, v_ref, o_ref, z_ref, xc_ref, ig_ref, fg_ref, ng_ref, sk_ref,
                  y_ref, c_scr, n_scr, m_scr):
    head = pl.program_id(1)

    @pl.when(pl.program_id(2) == 0)
    def _():
        c_scr[...] = jnp.zeros_like(c_scr)
        n_scr[...] = jnp.zeros_like(n_scr)
        m_scr[...] = jnp.zeros_like(m_scr)

    ln = q_ref.shape[0]
    ib = ig_ref[...] + bi_ref[head]
    fb = _log_sigmoid(fg_ref[...] + bf_ref[head])
    row = lax.broadcasted_iota(jnp.int32, (ln, ln), 0)
    col = lax.broadcasted_iota(jnp.int32, (ln, ln), 1)
    tril = col <= row
    eye = col == row
    bcum_c = jnp.sum(jnp.where(tril, fb, 0.0), axis=1, keepdims=True)
    bcum_r = jnp.sum(jnp.where(eye, bcum_c, 0.0), axis=0, keepdims=True)
    g_tot = jnp.sum(fb, axis=1, keepdims=True)
    e_r = ib - bcum_r
    m_st = m_scr[...]
    d_intra = jnp.where(tril, bcum_c + e_r, NEG_INF)
    d_inter = bcum_c + m_st
    m_t = jnp.maximum(d_inter, jnp.max(d_intra, axis=1, keepdims=True))
    w_intra = jnp.exp(d_intra - m_t)
    w_inter = jnp.exp(d_inter - m_t)

    qb = q_ref[...]
    kb = k_ref[...]
    vb = v_ref[...]
    c_st = c_scr[...]
    n_st = n_scr[...]
    sc = _nt_dot(qb, kb) * w_intra
    num = _dot(sc.astype(BF16), vb) + w_inter * _dot(qb, c_st.astype(BF16))
    qn = jnp.sum(qb.astype(F32) * n_st, axis=1, keepdims=True)
    den = jnp.sum(sc, axis=1, keepdims=True) + w_inter * qn
    h = num / jnp.maximum(jnp.abs(den), jnp.exp(-m_t))

    d_state = g_tot + e_r
    m_new = jnp.maximum(g_tot + m_st, jnp.max(d_state, axis=1, keepdims=True))
    w_s_r = jnp.exp(d_state - m_new)
    w_s_c = jnp.sum(jnp.where(eye, w_s_r, 0.0), axis=1, keepdims=True)
    decay = jnp.exp(g_tot + m_st - m_new)
    kw = kb.astype(F32) * w_s_c
    kv = lax.dot_general(kw.astype(BF16), vb, (((0,), (0,)), ((), ())), preferred_element_type=F32)
    c_scr[...] = decay * c_st + kv
    n_scr[...] = decay * n_st + jnp.sum(kw, axis=0, keepdims=True)
    m_scr[...] = m_new

    hg = h * _sigmoid(o_ref[...].astype(F32))
    mu = jnp.mean(hg, axis=-1, keepdims=True)
    hc = hg - mu
    var = jnp.mean(hc * hc, axis=-1, keepdims=True)
    hn = hc * lax.rsqrt(var + LN_EPS)
    z = z_ref[...].astype(F32)
    y = (hn * ng_ref[...] + sk_ref[...] * xc_ref[...].astype(F32)) * (z * _sigmoid(z))
    y_ref[...] = y.astype(y_ref.dtype)


def _mlstm(p, xc, q, k, ig, fg, b_i, b_f, norm_g, skip, bsz, seq):
    t, width = q.shape
    heads = M_HEADS
    hd = width // heads
    ln = M_CHUNK
    nc = seq // ln
    rows = lambda b, h, c: (b * nc + c, h)
    smem = pl.BlockSpec(memory_space=pltpu.SMEM)
    gate = pl.BlockSpec((None, None, None, 1, ln), lambda b, h, c: (b, h, c, 0, 0))
    vec = pl.BlockSpec((1, hd), lambda b, h, c: (0, h))
    return pl.pallas_call(
        _mlstm_kernel,
        grid=(bsz, heads, nc),
        in_specs=[
            smem, smem,
            pl.BlockSpec((ln, hd), rows),
            pl.BlockSpec((ln, hd), rows),
            pl.BlockSpec((ln, hd), lambda b, h, c: (b * nc + c, heads + h)),
            pl.BlockSpec((ln, hd), lambda b, h, c: (b * nc + c, 2 * heads + h)),
            pl.BlockSpec((ln, hd), lambda b, h, c: (b * nc + c, 3 * heads + h)),
            pl.BlockSpec((ln, hd), rows),
            gate, gate, vec, vec,
        ],
        out_specs=pl.BlockSpec((ln, hd), rows),
        out_shape=jax.ShapeDtypeStruct((t, width), BF16),
        scratch_shapes=[pltpu.VMEM((hd, hd), F32), pltpu.VMEM((1, hd), F32), pltpu.VMEM((1, 1), F32)],
        compiler_params=_params("parallel", "parallel", "arbitrary"),
        name="mlstm",
    )(b_i, b_f, q, k, p, p, p, xc, ig, fg, norm_g, skip)


def _compress_kernel(t_ref, pe_ref, w1_ref, w2_ref, o_ref):
    t = t_ref[...].astype(F32)
    n_sub, half = t.shape
    pe = pe_ref[...]
    za = _dot((t + pe[0:1, :]).astype(BF16), w1_ref[0:half, :])
    zb = _dot((t + pe[1:2, :]).astype(BF16), w1_ref[half:2 * half, :])
    pre = za + pltpu.roll(zb, n_sub - 1, 0)
    hid = (pre * _sigmoid(pre)).astype(BF16)
    out = _dot(hid, w2_ref[...])
    rid = lax.broadcasted_iota(jnp.int32, out.shape, 0)
    o_ref[...] = jnp.where(rid < n_sub - 1, out, 0.0).astype(o_ref.dtype)


def _compress(tkv, pe, w1, w2):
    two, bg, n_sub, half = tkv.shape
    hid = w1.shape[-1]
    dh = w2.shape[-1]
    return pl.pallas_call(
        _compress_kernel,
        grid=(two, bg),
        in_specs=[
            pl.BlockSpec((None, None, n_sub, half), lambda i, j: (i, j, 0, 0)),
            pl.BlockSpec((None, 2, half), lambda i, j: (i, 0, 0)),
            pl.BlockSpec((None, 2 * half, hid), lambda i, j: (i, 0, 0)),
            pl.BlockSpec((None, hid, dh), lambda i, j: (i, 0, 0)),
        ],
        out_specs=pl.BlockSpec((None, None, n_sub, dh), lambda i, j: (i, j, 0, 0)),
        out_shape=jax.ShapeDtypeStruct((two, bg, n_sub, dh), BF16),
        compiler_params=_params("parallel", "parallel"),
        name="compress",
    )(tkv, pe, w1, w2)


def _mask_heads(x, ok, fill, tq):
    n = x.shape[1] // tq
    return jnp.concatenate([jnp.where(ok, x[:, h * tq:(h + 1) * tq], fill) for h in range(n)], axis=1)


def _nsa_kernel(qt_ref, kc_ref, vct_ref, ks_ref, vst_ref, kw_ref, vwt_ref, gate_ref, z_ref, o_ref, sel_ref):
    qi = pl.program_id(2)
    tq = qt_ref.shape[1]
    dh = A_HEAD_DIM
    hpg = A_GROUP
    r = hpg * tq
    n_cmp = kc_ref.shape[0]
    kt = vst_ref.shape[2]
    tk = vwt_ref.shape[2]
    seq = ks_ref.shape[0]
    n_sel = seq // A_SEL_BLOCK
    t0 = qi * tq

    qt = qt_ref[...] * jnp.asarray(dh ** -0.5, BF16)
    q4t = jnp.concatenate([qt[h * dh:(h + 1) * dh, :] for h in range(hpg)], axis=1)

    cn = lax.broadcasted_iota(jnp.int32, (n_cmp, tq), 0)
    ct = t0 + lax.broadcasted_iota(jnp.int32, (n_cmp, tq), 1)
    valid = (cn * A_CMP_STRIDE + (A_CMP_BLOCK - 1)) <= ct
    s_m = _mask_heads(_dot(kc_ref[...], q4t), valid, NEG_INF, tq)
    mx = jnp.max(s_m, axis=0, keepdims=True)
    e = _mask_heads(jnp.exp(s_m - mx), valid, 0.0, tq)
    lsum = jnp.sum(e, axis=0, keepdims=True)
    p_c = e * (1.0 / jnp.where(lsum > 0.0, lsum, 1.0))
    o_c = _dot(vct_ref[...], p_c.astype(BF16))
    p_sum = p_c[:, 0:tq]
    for h in range(1, hpg):
        p_sum = p_sum + p_c[:, h * tq:(h + 1) * tq]

    jn = lax.broadcasted_iota(jnp.int32, (n_sel, n_cmp), 0)
    nn = lax.broadcasted_iota(jnp.int32, (n_sel, n_cmp), 1)
    ov = ((nn * A_CMP_STRIDE < (jn + 1) * A_SEL_BLOCK)
          & (nn * A_CMP_STRIDE + (A_CMP_BLOCK - 1) >= jn * A_SEL_BLOCK))
    ov_t = jnp.where(ov, 1.0, 0.0).astype(BF16)
    p_hi = p_sum.astype(BF16)
    r1 = p_sum - p_hi.astype(F32)
    p_mid = r1.astype(BF16)
    p_lo = (r1 - p_mid.astype(F32)).astype(BF16)
    imp_t = _dot(ov_t, p_hi) + _dot(ov_t, p_mid) + _dot(ov_t, p_lo)

    jr = lax.broadcasted_iota(jnp.int32, (n_sel, tq), 0)
    tl = t0 + lax.broadcasted_iota(jnp.int32, (n_sel, tq), 1)
    cur = lax.shift_right_logical(tl, 6)
    forced = (jr == 0) | (jr == cur) | (jr == cur - 1)
    score = jnp.where(forced, POS_INF, jnp.where(jr <= cur, imp_t, NEG_INF))
    rank = jnp.zeros((n_sel, tq), F32)
    for jp in range(n_sel):
        sj = score[jp:jp + 1, :]
        beats = (sj > score) | ((sj == score) & (jr > jp))
        rank = rank + jnp.where(beats, 1.0, 0.0)
    sel_ref[...] = jnp.where(rank < float(min(A_SEL_TOPK, n_sel)), 1.0, 0.0)

    sub_s = lax.broadcasted_iota(jnp.int32, (kt, tq), 0)
    t_abs_s = t0 + lax.broadcasted_iota(jnp.int32, (kt, tq), 1)
    blocks_per_iter = kt // A_SEL_BLOCK

    def sel_body(j, carry):
        m, l, acc = carry
        off = pl.multiple_of(j * kt, kt)
        s = _dot(ks_ref[pl.ds(off, kt), :], q4t)
        flags = jnp.concatenate(
            [jnp.broadcast_to(sel_ref[pl.ds(j * blocks_per_iter + i, 1), :], (A_SEL_BLOCK, tq))
             for i in range(blocks_per_iter)], axis=0)
        ok = (flags > 0.5) & ((off + sub_s) <= t_abs_s)
        s = _mask_heads(s, ok, NEG_INF, tq)
        m_new = jnp.maximum(m, jnp.max(s, axis=0, keepdims=True))
        alpha = jnp.exp(m - m_new)
        p = jnp.exp(s - m_new)
        l = alpha * l + jnp.sum(p, axis=0, keepdims=True)
        acc = alpha * acc + _dot(vst_ref[j], p.astype(BF16))
        return m_new, l, acc

    init = (jnp.full((1, r), NEG_INF, F32), jnp.zeros((1, r), F32), jnp.zeros((dh, r), F32))
    _, l_s, acc_s = lax.fori_loop(0, (t0 + tq - 1) // kt + 1, sel_body, init)
    o_s = acc_s * (1.0 / l_s)

    n_wb = -(-A_WINDOW // tk)
    wk = (n_wb + 1) * tk
    j0 = jnp.maximum(qi - n_wb, 0)
    off_w = pl.multiple_of(j0 * tk, tk)
    c_abs = off_w + lax.broadcasted_iota(jnp.int32, (wk, tq), 0)
    t_abs_w = t0 + lax.broadcasted_iota(jnp.int32, (wk, tq), 1)
    ok_w = (c_abs <= t_abs_w) & (c_abs > t_abs_w - A_WINDOW)
    s_w = _mask_heads(_dot(kw_ref[pl.ds(off_w, wk), :], q4t), ok_w, NEG_INF, tq)
    p_w = jnp.exp(s_w - jnp.max(s_w, axis=0, keepdims=True))
    l_w = jnp.sum(p_w, axis=0, keepdims=True)
    p_wb = p_w.astype(BF16)
    acc_w = _dot(vwt_ref[j0], p_wb[0:tk, :])
    for i in range(1, n_wb + 1):
        acc_w = acc_w + _dot(vwt_ref[j0 + i], p_wb[i * tk:(i + 1) * tk, :])
    o_w = acc_w * (1.0 / l_w)

    gs = _sigmoid(gate_ref[...])
    outs = []
    for h in range(hpg):
        sl = slice(h * tq, (h + 1) * tq)
        outs.append(gs[3 * h:3 * h + 1, :] * o_c[:, sl] + gs[3 * h + 1:3 * h + 2, :] * o_s[:, sl]
                    + gs[3 * h + 2:3 * h + 3, :] * o_w[:, sl])
    o_all = jnp.concatenate(outs, axis=0).T
    z = z_ref[...].astype(F32)
    o_ref[...] = (o_all * (z * _sigmoid(z))).astype(o_ref.dtype)


def _nsa(p, qt, kvc, vct, kv6, vst, vwt, gates, z_col, bsz, seq):
    t = p.shape[0]
    g = A_KV_HEADS
    dh = A_HEAD_DIM
    gw = A_GROUP * dh
    tq = Q_BLOCK
    nq = seq // tq
    n_cmp = kvc.shape[2]
    k_spec = lambda i: pl.BlockSpec((None, None, None, seq, dh), lambda b, gg, qi: (i, b, gg, 0, 0))
    vt_spec = lambda a: pl.BlockSpec((None, None) + a.shape[2:], lambda b, gg, qi: (b, gg, 0, 0, 0))
    return pl.pallas_call(
        _nsa_kernel,
        grid=(bsz, g, nq),
        in_specs=[
            pl.BlockSpec((None, gw, tq), lambda b, gg, qi: (b, gg, qi)),
            pl.BlockSpec((None, None, n_cmp, dh), lambda b, gg, qi: (0, b * g + gg, 0, 0)),
            pl.BlockSpec((None, dh, n_cmp), lambda b, gg, qi: (b * g + gg, 0, 0)),
            k_spec(2), vt_spec(vst), k_spec(4), vt_spec(vwt),
            pl.BlockSpec((None, None, 3 * A_GROUP, tq), lambda b, gg, qi: (b, gg, 0, qi)),
            pl.BlockSpec((tq, gw), lambda b, gg, qi: (b * nq + qi, z_col // gw + gg)),
        ],
        out_specs=pl.BlockSpec((tq, gw), lambda b, gg, qi: (b * nq + qi, gg)),
        out_shape=jax.ShapeDtypeStruct((t, g * gw), BF16),
        scratch_shapes=[pltpu.VMEM((seq // A_SEL_BLOCK, tq), F32)],
        compiler_params=_params("parallel", "parallel", "arbitrary"),
        name="nsa",
    )(qt, kvc, vct, kv6, vst, kv6, vwt, gates, p)


def _merge_kernel(ym_ref, ya_ref, wm_ref, wa_ref, gm_ref, ga_ref, o_ref):
    um = _dot(ym_ref[...], wm_ref[...])
    ua = _dot(ya_ref[...], wa_ref[...])
    o = _sigmoid(gm_ref[...].astype(F32)) * um + _sigmoid(ga_ref[...].astype(F32)) * ua
    o_ref[...] = o.astype(o_ref.dtype)


def _merge(ym, ya, wm, wa, p, gm_col, ga_col, tm=512, tn=512):
    t, kdim = ym.shape
    n = wm.shape[1]
    return pl.pallas_call(
        _merge_kernel,
        grid=(t // tm, n // tn),
        in_specs=[
            pl.BlockSpec((tm, kdim), lambda i, j: (i, 0)),
            pl.BlockSpec((tm, kdim), lambda i, j: (i, 0)),
            pl.BlockSpec((kdim, tn), lambda i, j: (0, j)),
            pl.BlockSpec((kdim, tn), lambda i, j: (0, j)),
            pl.BlockSpec((tm, tn), lambda i, j: (i, gm_col // tn + j)),
            pl.BlockSpec((tm, tn), lambda i, j: (i, ga_col // tn + j)),
        ],
        out_specs=pl.BlockSpec((tm, tn), lambda i, j: (i, j)),
        out_shape=jax.ShapeDtypeStruct((t, n), BF16),
        compiler_params=_params("parallel", "parallel"),
        name="merge",
    )(ym, ya, wm, wa, p, p)


def _outproj_kernel(mg_ref, w_ref, g_ref, x_ref, o_ref):
    out = _dot(mg_ref[...], w_ref[...])
    ms = jnp.mean(out * out, axis=-1, keepdims=True)
    o_ref[...] = x_ref[...] + out * lax.rsqrt(ms + RMS_EPS) * g_ref[...]


def _outproj(mg, w, g, x2d, tm=256):
    t, d = x2d.shape
    return pl.pallas_call(
        _outproj_kernel,
        grid=(t // tm,),
        in_specs=[
            pl.BlockSpec((tm, d), lambda i: (i, 0)),
            pl.BlockSpec((d, d), lambda i: (0, 0)),
            pl.BlockSpec((1, d), lambda i: (0, 0)),
            pl.BlockSpec((tm, d), lambda i: (i, 0)),
        ],
        out_specs=pl.BlockSpec((tm, d), lambda i: (i, 0)),
        out_shape=jax.ShapeDtypeStruct((t, d), F32),
        compiler_params=_params("parallel"),
        name="outproj",
    )(mg, w, g, x2d)


def _layer(x, pre_g, w_in, conv_w, conv_b, w_q, w_k, b_i, b_f, norm_g, skip,
           pe_k, w1_k, w2_k, pe_v, w1_v, w2_v, w_up_m, w_up_a, w_out, post_g):
    bsz, seq, d = x.shape
    t = bsz * seq
    mw = w_up_m.shape[0]
    aw = w_up_a.shape[0]
    g = A_KV_HEADS
    dh = A_HEAD_DIM
    kvw = g * dh
    n_gate = 3 * A_HEADS

    o_i = 4 * mw
    o_f = o_i + M_HEADS
    o_q = o_f + M_HEADS
    o_g = o_q + aw + 6 * kvw
    o_z = o_g + n_gate
    w_main = jnp.concatenate([w_in[:, :o_i], w_in[:, o_q:o_g], w_in[:, o_z:]], axis=1).astype(BF16)
    n_small = 2 * M_HEADS + n_gate
    w_small = jnp.concatenate(
        [w_in[:, o_i:o_q], w_in[:, o_g:o_z], jnp.zeros((d, LANES - n_small), w_in.dtype)], axis=1).astype(BF16)
    q_col = 4 * mw
    kv_col = q_col + aw
    z_col = kv_col + 6 * kvw
    gm_col = z_col + aw
    ga_col = gm_col + d

    x2d = x.reshape(t, d)
    p, ps = _inproj(x2d, pre_g.reshape(1, d), w_main, w_small)

    xc, q, k = _conv_qk(p, conv_w, conv_b.reshape(1, mw), w_q.astype(BF16), w_k.astype(BF16), bsz, seq)
    nc = seq // M_CHUNK
    gates_m = ps[:, :2 * M_HEADS].reshape(bsz, seq, 2, M_HEADS).transpose(2, 0, 3, 1)
    gates_m = gates_m.reshape(2, bsz, M_HEADS, nc, 1, M_CHUNK)
    y_m = _mlstm(p, xc, q, k, gates_m[0], gates_m[1], b_i, b_f,
                 norm_g.reshape(1, mw), skip.reshape(1, mw), bsz, seq)

    kv = p[:, kv_col:kv_col + 6 * kvw].reshape(bsz, seq, 6, g, dh)
    kv6 = kv.transpose(2, 0, 3, 1, 4)
    def value_tiles_t(v, tile):
        return v.reshape(bsz, seq // tile, tile, g, dh).transpose(0, 3, 1, 4, 2)

    vst = value_tiles_t(kv[:, :, 3], SEL_KEYS_PER_STEP)
    vwt = value_tiles_t(kv[:, :, 5], LANES)
    qt = p[:, q_col:q_col + aw].reshape(bsz, seq, aw).transpose(0, 2, 1)
    n_sub = seq // A_CMP_STRIDE
    half = A_CMP_STRIDE * dh
    t_cmp = kv6[0:2].reshape(2, bsz * g, n_sub, half)
    pe = jnp.stack([pe_k.reshape(2, half), pe_v.reshape(2, half)])
    w1 = jnp.stack([w1_k, w1_v]).astype(BF16)
    w2 = jnp.stack([w2_k, w2_v]).astype(BF16)
    kvc = _compress(t_cmp, pe, w1, w2)
    vct = jnp.swapaxes(kvc[1], 1, 2)
    gates_a = ps[:, 2 * M_HEADS:n_small].reshape(bsz, seq, g, 3 * A_GROUP).transpose(0, 2, 3, 1)
    y_a = _nsa(p, qt, kvc, vct, kv6, vst, vwt, gates_a, z_col, bsz, seq)

    mg = _merge(y_m, y_a, w_up_m.astype(BF16), w_up_a.astype(BF16), p, gm_col, ga_col)
    out = _outproj(mg, w_out.astype(BF16), post_g.reshape(1, d), x2d)
    return out.reshape(bsz, seq, d)


def kernel(x, pre_norm_g, w_in, m_conv_w, m_conv_b, m_w_q, m_w_k, m_b_i, m_b_f, m_norm_g, m_skip, a_pe_k, a_w1_k, a_w2_k, a_pe_v, a_w1_v, a_w2_v, w_up_m, w_up_a, w_out, post_norm_g):
    depth = w_in.shape[0]
    for l in range(depth):
        x = _layer(x, pre_norm_g[l], w_in[l], m_conv_w[l], m_conv_b[l], m_w_q[l], m_w_k[l], m_b_i[l], m_b_f[l],
                   m_norm_g[l], m_skip[l], a_pe_k[l], a_w1_k[l], a_w2_k[l], a_pe_v[l], a_w1_v[l], a_w2_v[l],
                   w_up_m[l], w_up_a[l], w_out[l], post_norm_g[l])
    return x
```

```python
import functools

import jax
import jax.numpy as jnp
from jax import lax
from jax.experimental import pallas as pl
from jax.experimental.pallas import tpu as pltpu

F32 = jnp.float32
BF16 = jnp.bfloat16

M_HEADS = 4
M_CONV = 4
M_CHUNK = 128
A_HEADS = 16
A_KV_HEADS = 4
A_GROUP = A_HEADS // A_KV_HEADS
A_HEAD_DIM = 64
A_CMP_BLOCK = 32
A_CMP_STRIDE = 16
A_SEL_BLOCK = 64
A_SEL_TOPK = 8
A_WINDOW = 512
Q_BLOCK = 128
SEL_KEYS_PER_STEP = 512
RMS_EPS = 1e-6
LN_EPS = 1e-6
NEG_INF = -1e30
POS_INF = 1e30

LANES = 128
VMEM_LIMIT_BYTES = 48 * 1024 * 1024


def _nt_dot(a, b):
    return lax.dot_general(a, b, (((1,), (1,)), ((), ())), preferred_element_type=F32)


def _dot(a, b):
    return jnp.dot(a, b, preferred_element_type=F32)


def _sigmoid(x):
    return 1.0 / (1.0 + jnp.exp(-x))


def _params(*sem):
    return pltpu.CompilerParams(dimension_semantics=sem, vmem_limit_bytes=VMEM_LIMIT_BYTES)


def _inproj_kernel(x_ref, g_ref, w_ref, ws_ref, o_ref, os_ref, h_scr):
    @pl.when(pl.program_id(1) == 0)
    def _():
        xf = x_ref[...]
        ms = jnp.mean(xf * xf, axis=-1, keepdims=True)
        hb = (xf * lax.rsqrt(ms + RMS_EPS) * g_ref[...]).astype(BF16)
        h_scr[...] = hb
        os_ref[...] = _dot(hb, ws_ref[...])

    o_ref[...] = _dot(h_scr[...], w_ref[...]).astype(o_ref.dtype)


def _inproj(x2d, g, w_main, w_small, tm=1024, tn=512):
    t, d = x2d.shape
    n = w_main.shape[1]
    ns = w_small.shape[1]
    return pl.pallas_call(
        _inproj_kernel,
        grid=(t // tm, n // tn),
        in_specs=[
            pl.BlockSpec((tm, d), lambda i, j: (i, 0)),
            pl.BlockSpec((1, d), lambda i, j: (0, 0)),
            pl.BlockSpec((d, tn), lambda i, j: (0, j)),
            pl.BlockSpec((d, ns), lambda i, j: (0, 0)),
        ],
        out_specs=[
            pl.BlockSpec((tm, tn), lambda i, j: (i, j)),
            pl.BlockSpec((tm, ns), lambda i, j: (i, 0)),
        ],
        out_shape=[jax.ShapeDtypeStruct((t, n), BF16), jax.ShapeDtypeStruct((t, ns), F32)],
        scratch_shapes=[pltpu.VMEM((tm, d), BF16)],
        compiler_params=_params("parallel", "arbitrary"),
        name="inproj",
    )(x2d, g, w_main, w_small)


def _conv_qk_kernel(x_ref, cw_ref, cb_ref, wq_ref, wk_ref, xc_ref, q_ref, k_ref, pad_ref, *, rows):
    s, c = x_ref.shape
    front = 8
    pad_ref[0:front, :] = jnp.zeros((front, c), F32)
    pad_ref[front:front + s, :] = x_ref[...].astype(F32)
    cw = cw_ref[...]
    cb = cb_ref[...]
    wq = wq_ref[...]
    wk = wk_ref[...]
    k_scale = c ** -0.5
    for r in range(s // rows):
        acc = jnp.broadcast_to(cb, (rows, c))
        for kk in range(M_CONV):
            start = front + r * rows + kk - (M_CONV - 1)
            acc = acc + cw[kk:kk + 1, :] * pad_ref[start:start + rows, :]
        xcb = (acc * _sigmoid(acc)).astype(BF16)
        xc_ref[r * rows:(r + 1) * rows, :] = xcb
        q_ref[r * rows:(r + 1) * rows, :] = _dot(xcb, wq).astype(BF16)
        k_ref[r * rows:(r + 1) * rows, :] = (_dot(xcb, wk) * k_scale).astype(BF16)


def _conv_qk(p, mx_col, conv_w, conv_b, wq, wk, bsz, seq):
    t = p.shape[0]
    heads, hd = wq.shape[0], wq.shape[1]
    width = heads * hd
    blk = pl.BlockSpec((seq, hd), lambda b, h: (b, h))
    out = jax.ShapeDtypeStruct((t, width), BF16)
    return pl.pallas_call(
        functools.partial(_conv_qk_kernel, rows=256),
        grid=(bsz, heads),
        in_specs=[
            pl.BlockSpec((seq, hd), lambda b, h: (b, mx_col // hd + h)),
            pl.BlockSpec((M_CONV, hd), lambda b, h: (0, h)),
            pl.BlockSpec((1, hd), lambda b, h: (0, h)),
            pl.BlockSpec((None, hd, hd), lambda b, h: (h, 0, 0)),
            pl.BlockSpec((None, hd, hd), lambda b, h: (h, 0, 0)),
        ],
        out_specs=[blk, blk, blk],
        out_shape=[out, out, out],
        scratch_shapes=[pltpu.VMEM((seq + 8, hd), F32)],
        compiler_params=_params("parallel", "parallel"),
        name="conv_qk",
    )(p, conv_w, conv_b, wq, wk)


def _log_sigmoid(x):
    return jnp.minimum(x, 0.0) - jnp.log1p(jnp.exp(-jnp.abs(x)))


def _mlstm_kernel(bias_ref, q_ref, k_ref, v_ref, o_ref, z_ref, xc_ref, g_ref, ng_ref, sk_ref,
                  y_ref, c_scr, n_scr, m_scr):
    @pl.when(pl.program_id(1) == 0)
    def _():
        c_scr[...] = jnp.zeros_like(c_scr)
        n_scr[...] = jnp.zeros_like(n_scr)
        m_scr[...] = jnp.zeros_like(m_scr)

    ln = q_ref.shape[0]
    heads, hd = c_scr.shape[0], c_scr.shape[1]
    gates = g_ref[...] + bias_ref[...]
    fb_all = _log_sigmoid(gates[heads:2 * heads, :])
    row = lax.broadcasted_iota(jnp.int32, (ln, ln), 0)
    col = lax.broadcasted_iota(jnp.int32, (ln, ln), 1)
    tril = col <= row
    eye = col == row
    for hh in range(heads):
        sl = slice(hh * hd, (hh + 1) * hd)
        ib = gates[hh:hh + 1, :]
        fb = fb_all[hh:hh + 1, :]
        bcum_c = jnp.sum(jnp.where(tril, fb, 0.0), axis=1, keepdims=True)
        bcum_r = jnp.sum(jnp.where(eye, bcum_c, 0.0), axis=0, keepdims=True)
        g_tot = jnp.sum(fb, axis=1, keepdims=True)
        e_r = ib - bcum_r
        m_st = m_scr[hh]
        d_intra = jnp.where(tril, bcum_c + e_r, NEG_INF)
        d_inter = bcum_c + m_st
        m_t = jnp.maximum(d_inter, jnp.max(d_intra, axis=1, keepdims=True))
        w_intra = jnp.exp(d_intra - m_t)
        w_inter = jnp.exp(d_inter - m_t)

        qb = q_ref[:, sl]
        kb = k_ref[:, sl]
        vb = v_ref[:, sl]
        c_st = c_scr[hh]
        n_st = n_scr[hh]
        sc = _nt_dot(qb, kb) * w_intra
        num = _dot(sc.astype(BF16), vb) + w_inter * _dot(qb, c_st.astype(BF16))
        qn = jnp.sum(qb.astype(F32) * n_st, axis=1, keepdims=True)
        den = jnp.sum(sc, axis=1, keepdims=True) + w_inter * qn
        h = num / jnp.maximum(jnp.abs(den), jnp.exp(-m_t))

        d_state = g_tot + e_r
        m_new = jnp.maximum(g_tot + m_st, jnp.max(d_state, axis=1, keepdims=True))
        w_s_r = jnp.exp(d_state - m_new)
        w_s_c = jnp.sum(jnp.where(eye, w_s_r, 0.0), axis=1, keepdims=True)
        decay = jnp.exp(g_tot + m_st - m_new)
        kw = kb.astype(F32) * w_s_c
        kv = lax.dot_general(kw.astype(BF16), vb, (((0,), (0,)), ((), ())), preferred_element_type=F32)
        c_scr[hh] = decay * c_st + kv
        n_scr[hh] = decay * n_st + jnp.sum(kw, axis=0, keepdims=True)
        m_scr[hh] = m_new

        hg = h * _sigmoid(o_ref[:, sl].astype(F32))
        mu = jnp.mean(hg, axis=-1, keepdims=True)
        hc = hg - mu
        var = jnp.mean(hc * hc, axis=-1, keepdims=True)
        hn = hc * lax.rsqrt(var + LN_EPS)
        z = z_ref[:, sl].astype(F32)
        y = (hn * ng_ref[:, sl] + sk_ref[:, sl] * xc_ref[:, sl].astype(F32)) * (z * _sigmoid(z))
        y_ref[:, sl] = y.astype(y_ref.dtype)


def _mlstm(p, mx_col, xc, q, k, gates, bias, norm_g, skip, bsz, seq):
    t, width = q.shape
    pc = mx_col // width
    heads = M_HEADS
    hd = width // heads
    ln = M_CHUNK
    nc = seq // ln
    rows = lambda b, c: (b * nc + c, 0)
    vec = pl.BlockSpec((1, width), lambda b, c: (0, 0))
    return pl.pallas_call(
        _mlstm_kernel,
        grid=(bsz, nc),
        in_specs=[
            pl.BlockSpec((2 * heads, 1), lambda b, c: (0, 0)),
            pl.BlockSpec((ln, width), rows),
            pl.BlockSpec((ln, width), rows),
            pl.BlockSpec((ln, width), lambda b, c: (b * nc + c, pc + 1)),
            pl.BlockSpec((ln, width), lambda b, c: (b * nc + c, pc + 2)),
            pl.BlockSpec((ln, width), lambda b, c: (b * nc + c, pc + 3)),
            pl.BlockSpec((ln, width), rows),
            pl.BlockSpec((None, None, 2 * heads, ln), lambda b, c: (b, c, 0, 0)),
            vec, vec,
        ],
        out_specs=pl.BlockSpec((ln, width), rows),
        out_shape=jax.ShapeDtypeStruct((t, width), BF16),
        scratch_shapes=[pltpu.VMEM((heads, hd, hd), F32), pltpu.VMEM((heads, 1, hd), F32),
                        pltpu.VMEM((heads, 1, 1), F32)],
        compiler_params=_params("parallel", "arbitrary"),
        name="mlstm",
    )(bias, q, k, p, p, p, xc, gates, norm_g, skip)


def _compress_kernel(t_ref, pe_ref, w1_ref, w2_ref, o_ref):
    t = t_ref[...].astype(F32)
    n_sub, half = t.shape
    pe = pe_ref[...]
    za = _dot((t + pe[0:1, :]).astype(BF16), w1_ref[0:half, :])
    zb = _dot((t + pe[1:2, :]).astype(BF16), w1_ref[half:2 * half, :])
    pre = za + pltpu.roll(zb, n_sub - 1, 0)
    hid = (pre * _sigmoid(pre)).astype(BF16)
    out = _dot(hid, w2_ref[...])
    rid = lax.broadcasted_iota(jnp.int32, out.shape, 0)
    o_ref[...] = jnp.where(rid < n_sub - 1, out, 0.0).astype(o_ref.dtype)


def _compress(tkv, pe, w1, w2):
    two, bg, n_sub, half = tkv.shape
    hid = w1.shape[-1]
    dh = w2.shape[-1]
    return pl.pallas_call(
        _compress_kernel,
        grid=(two, bg),
        in_specs=[
            pl.BlockSpec((None, None, n_sub, half), lambda i, j: (i, j, 0, 0)),
            pl.BlockSpec((None, 2, half), lambda i, j: (i, 0, 0)),
            pl.BlockSpec((None, 2 * half, hid), lambda i, j: (i, 0, 0)),
            pl.BlockSpec((None, hid, dh), lambda i, j: (i, 0, 0)),
        ],
        out_specs=pl.BlockSpec((None, None, n_sub, dh), lambda i, j: (i, j, 0, 0)),
        out_shape=jax.ShapeDtypeStruct((two, bg, n_sub, dh), BF16),
        compiler_params=_params("parallel", "parallel"),
        name="compress",
    )(tkv, pe, w1, w2)


def _mask_heads(x, ok, fill, tq):
    n = x.shape[1] // tq
    return jnp.concatenate([jnp.where(ok, x[:, h * tq:(h + 1) * tq], fill) for h in range(n)], axis=1)


def _nsa_kernel(qt_ref, kc_ref, vct_ref, ks_ref, vst_ref, kw_ref, vwt_ref, gate_ref, z_ref, o_ref, sel_ref):
    qi = pl.program_id(2)
    tq = qt_ref.shape[1]
    dh = A_HEAD_DIM
    hpg = A_GROUP
    r = hpg * tq
    n_cmp = kc_ref.shape[0]
    kt = vst_ref.shape[2]
    tk = vwt_ref.shape[2]
    seq = ks_ref.shape[0]
    n_sel = seq // A_SEL_BLOCK
    t0 = qi * tq

    qt = qt_ref[...] * jnp.asarray(dh ** -0.5, BF16)
    q4t = jnp.concatenate([qt[h * dh:(h + 1) * dh, :] for h in range(hpg)], axis=1)

    cn = lax.broadcasted_iota(jnp.int32, (n_cmp, tq), 0)
    ct = t0 + lax.broadcasted_iota(jnp.int32, (n_cmp, tq), 1)
    valid = (cn * A_CMP_STRIDE + (A_CMP_BLOCK - 1)) <= ct
    s_m = _mask_heads(_dot(kc_ref[...], q4t), valid, NEG_INF, tq)
    mx = jnp.max(s_m, axis=0, keepdims=True)
    e = _mask_heads(jnp.exp(s_m - mx), valid, 0.0, tq)
    lsum = jnp.sum(e, axis=0, keepdims=True)
    p_c = e * (1.0 / jnp.where(lsum > 0.0, lsum, 1.0))
    o_c = _dot(vct_ref[...], p_c.astype(BF16))
    p_sum = p_c[:, 0:tq]
    for h in range(1, hpg):
        p_sum = p_sum + p_c[:, h * tq:(h + 1) * tq]

    jn = lax.broadcasted_iota(jnp.int32, (n_sel, n_cmp), 0)
    nn = lax.broadcasted_iota(jnp.int32, (n_sel, n_cmp), 1)
    ov = ((nn * A_CMP_STRIDE < (jn + 1) * A_SEL_BLOCK)
          & (nn * A_CMP_STRIDE + (A_CMP_BLOCK - 1) >= jn * A_SEL_BLOCK))
    ov_t = jnp.where(ov, 1.0, 0.0).astype(BF16)
    p_hi = p_sum.astype(BF16)
    r1 = p_sum - p_hi.astype(F32)
    p_mid = r1.astype(BF16)
    p_lo = (r1 - p_mid.astype(F32)).astype(BF16)
    imp_t = _dot(ov_t, p_hi) + _dot(ov_t, p_mid) + _dot(ov_t, p_lo)

    jr = lax.broadcasted_iota(jnp.int32, (n_sel, tq), 0)
    tl = t0 + lax.broadcasted_iota(jnp.int32, (n_sel, tq), 1)
    cur = lax.shift_right_logical(tl, 6)
    forced = (jr == 0) | (jr == cur) | (jr == cur - 1)
    score = jnp.where(forced, POS_INF, jnp.where(jr <= cur, imp_t, NEG_INF))
    rank = jnp.zeros((n_sel, tq), F32)
    for jp in range(n_sel):
        sj = score[jp:jp + 1, :]
        beats = (sj > score) | ((sj == score) & (jr > jp))
        rank = rank + jnp.where(beats, 1.0, 0.0)
    sel_ref[...] = jnp.where(rank < float(min(A_SEL_TOPK, n_sel)), 1.0, 0.0)

    sub_s = lax.broadcasted_iota(jnp.int32, (kt, tq), 0)
    t_abs_s = t0 + lax.broadcasted_iota(jnp.int32, (kt, tq), 1)
    blocks_per_iter = kt // A_SEL_BLOCK

    def sel_body(j, carry):
        m, l, acc = carry
        off = pl.multiple_of(j * kt, kt)
        s = _dot(ks_ref[pl.ds(off, kt), :], q4t)
        flags = jnp.concatenate(
            [jnp.broadcast_to(sel_ref[pl.ds(j * blocks_per_iter + i, 1), :], (A_SEL_BLOCK, tq))
             for i in range(blocks_per_iter)], axis=0)
        ok = (flags > 0.5) & ((off + sub_s) <= t_abs_s)
        s = _mask_heads(s, ok, NEG_INF, tq)
        m_new = jnp.maximum(m, jnp.max(s, axis=0, keepdims=True))
        alpha = jnp.exp(m - m_new)
        p = jnp.exp(s - m_new)
        l = alpha * l + jnp.sum(p, axis=0, keepdims=True)
        acc = alpha * acc + _dot(vst_ref[j], p.astype(BF16))
        return m_new, l, acc

    init = (jnp.full((1, r), NEG_INF, F32), jnp.zeros((1, r), F32), jnp.zeros((dh, r), F32))
    _, l_s, acc_s = lax.fori_loop(0, (t0 + tq - 1) // kt + 1, sel_body, init)
    o_s = acc_s * (1.0 / l_s)

    n_wb = -(-A_WINDOW // tk)
    wk = (n_wb + 1) * tk
    j0 = jnp.maximum(qi - n_wb, 0)
    off_w = pl.multiple_of(j0 * tk, tk)
    c_abs = off_w + lax.broadcasted_iota(jnp.int32, (wk, tq), 0)
    t_abs_w = t0 + lax.broadcasted_iota(jnp.int32, (wk, tq), 1)
    ok_w = (c_abs <= t_abs_w) & (c_abs > t_abs_w - A_WINDOW)
    s_w = _mask_heads(_dot(kw_ref[pl.ds(off_w, wk), :], q4t), ok_w, NEG_INF, tq)
    p_w = jnp.exp(s_w - jnp.max(s_w, axis=0, keepdims=True))
    l_w = jnp.sum(p_w, axis=0, keepdims=True)
    p_wb = p_w.astype(BF16)
    acc_w = _dot(vwt_ref[j0], p_wb[0:tk, :])
    for i in range(1, n_wb + 1):
        acc_w = acc_w + _dot(vwt_ref[j0 + i], p_wb[i * tk:(i + 1) * tk, :])
    o_w = acc_w * (1.0 / l_w)

    gs = _sigmoid(gate_ref[...])
    outs = []
    for h in range(hpg):
        sl = slice(h * tq, (h + 1) * tq)
        outs.append(gs[3 * h:3 * h + 1, :] * o_c[:, sl] + gs[3 * h + 1:3 * h + 2, :] * o_s[:, sl]
                    + gs[3 * h + 2:3 * h + 3, :] * o_w[:, sl])
    o_all = jnp.concatenate(outs, axis=0).T
    z = z_ref[...].astype(F32)
    o_ref[...] = (o_all * (z * _sigmoid(z))).astype(o_ref.dtype)


def _nsa(p, qt, kvc, vct, kv6, vst, vwt, gates, z_col, bsz, seq):
    t = p.shape[0]
    g = A_KV_HEADS
    dh = A_HEAD_DIM
    gw = A_GROUP * dh
    tq = Q_BLOCK
    nq = seq // tq
    n_cmp = kvc.shape[2]
    k_spec = lambda i: pl.BlockSpec((None, None, None, seq, dh), lambda b, gg, qi: (i, b, gg, 0, 0))
    vt_spec = lambda a: pl.BlockSpec((None, None) + a.shape[2:], lambda b, gg, qi: (b, gg, 0, 0, 0))
    return pl.pallas_call(
        _nsa_kernel,
        grid=(bsz, g, nq),
        in_specs=[
            pl.BlockSpec((None, gw, tq), lambda b, gg, qi: (b, gg, qi)),
            pl.BlockSpec((None, None, n_cmp, dh), lambda b, gg, qi: (0, b * g + gg, 0, 0)),
            pl.BlockSpec((None, dh, n_cmp), lambda b, gg, qi: (b * g + gg, 0, 0)),
            k_spec(2), vt_spec(vst), k_spec(4), vt_spec(vwt),
            pl.BlockSpec((None, None, 3 * A_GROUP, tq), lambda b, gg, qi: (b, gg, 0, qi)),
            pl.BlockSpec((tq, gw), lambda b, gg, qi: (b * nq + qi, z_col // gw + gg)),
        ],
        out_specs=pl.BlockSpec((tq, gw), lambda b, gg, qi: (b * nq + qi, gg)),
        out_shape=jax.ShapeDtypeStruct((t, g * gw), BF16),
        scratch_shapes=[pltpu.VMEM((seq // A_SEL_BLOCK, tq), F32)],
        compiler_params=_params("parallel", "parallel", "arbitrary"),
        name="nsa",
    )(qt, kvc, vct, kv6, vst, kv6, vwt, gates, p)


def _merge_out_kernel(ym_ref, ya_ref, wm_ref, wa_ref, gm_ref, ga_ref, wo_ref, g_ref, x_ref, o_ref):
    um = _dot(ym_ref[...], wm_ref[...])
    ua = _dot(ya_ref[...], wa_ref[...])
    mg = _sigmoid(gm_ref[...].astype(F32)) * um + _sigmoid(ga_ref[...].astype(F32)) * ua
    out = _dot(mg.astype(BF16), wo_ref[...])
    ms = jnp.mean(out * out, axis=-1, keepdims=True)
    o_ref[...] = x_ref[...] + out * lax.rsqrt(ms + RMS_EPS) * g_ref[...]


def _merge_out(ym, ya, wm, wa, p, gm_col, ga_col, wo, g, x2d, tm=256):
    t, d = x2d.shape
    kdim = ym.shape[1]
    resident = lambda shape: pl.BlockSpec(shape, lambda i: (0, 0), pipeline_mode=pl.Buffered(1))
    return pl.pallas_call(
        _merge_out_kernel,
        grid=(t // tm,),
        in_specs=[
            pl.BlockSpec((tm, kdim), lambda i: (i, 0)),
            pl.BlockSpec((tm, kdim), lambda i: (i, 0)),
            resident((kdim, d)),
            resident((kdim, d)),
            pl.BlockSpec((tm, d), lambda i: (i, gm_col // d)),
            pl.BlockSpec((tm, d), lambda i: (i, ga_col // d)),
            resident((d, d)),
            pl.BlockSpec((1, d), lambda i: (0, 0)),
            pl.BlockSpec((tm, d), lambda i: (i, 0)),
        ],
        out_specs=pl.BlockSpec((tm, d), lambda i: (i, 0)),
        out_shape=jax.ShapeDtypeStruct((t, d), F32),
        compiler_params=_params("parallel"),
        name="merge_out",
    )(ym, ya, wm, wa, p, p, wo, g, x2d)


def _layer(x, pre_g, w_in, conv_w, conv_b, w_q, w_k, b_i, b_f, norm_g, skip,
           pe_k, w1_k, w2_k, pe_v, w1_v, w2_v, w_up_m, w_up_a, w_out, post_g):
    bsz, seq, d = x.shape
    t = bsz * seq
    mw = w_up_m.shape[0]
    aw = w_up_a.shape[0]
    g = A_KV_HEADS
    dh = A_HEAD_DIM
    kvw = g * dh
    n_gate = 3 * A_HEADS

    o_i = 4 * mw
    o_f = o_i + M_HEADS
    o_q = o_f + M_HEADS
    o_g = o_q + aw + 6 * kvw
    o_z = o_g + n_gate
    o_m = o_z + aw
    w_main = jnp.concatenate([w_in[:, o_m:], w_in[:, :o_i], w_in[:, o_q:o_g], w_in[:, o_z:o_m]], axis=1).astype(BF16)
    n_small = 2 * M_HEADS + n_gate
    w_small = jnp.concatenate(
        [w_in[:, o_i:o_q], w_in[:, o_g:o_z], jnp.zeros((d, LANES - n_small), w_in.dtype)], axis=1).astype(BF16)
    gm_col = 0
    ga_col = d
    mx_col = 2 * d
    q_col = mx_col + 4 * mw
    kv_col = q_col + aw
    z_col = kv_col + 6 * kvw

    x2d = x.reshape(t, d)
    p, ps = _inproj(x2d, pre_g.reshape(1, d), w_main, w_small)

    xc, q, k = _conv_qk(p, mx_col, conv_w, conv_b.reshape(1, mw), w_q.astype(BF16), w_k.astype(BF16), bsz, seq)
    nc = seq // M_CHUNK
    gates_m = ps[:, :2 * M_HEADS].reshape(bsz, nc, M_CHUNK, 2 * M_HEADS).transpose(0, 1, 3, 2)
    bias_m = jnp.concatenate([b_i, b_f]).reshape(2 * M_HEADS, 1)
    y_m = _mlstm(p, mx_col, xc, q, k, gates_m, bias_m, norm_g.reshape(1, mw), skip.reshape(1, mw), bsz, seq)

    kv = p[:, kv_col:kv_col + 6 * kvw].reshape(bsz, seq, 6, g, dh)
    kv6 = kv.transpose(2, 0, 3, 1, 4)
    def value_tiles_t(v, tile):
        return v.reshape(bsz, seq // tile, tile, g, dh).transpose(0, 3, 1, 4, 2)

    vst = value_tiles_t(kv[:, :, 3], SEL_KEYS_PER_STEP)
    vwt = value_tiles_t(kv[:, :, 5], LANES)
    qt = p[:, q_col:q_col + aw].reshape(bsz, seq, aw).transpose(0, 2, 1)
    n_sub = seq // A_CMP_STRIDE
    half = A_CMP_STRIDE * dh
    t_cmp = kv6[0:2].reshape(2, bsz * g, n_sub, half)
    pe = jnp.stack([pe_k.reshape(2, half), pe_v.reshape(2, half)])
    w1 = jnp.stack([w1_k, w1_v]).astype(BF16)
    w2 = jnp.stack([w2_k, w2_v]).astype(BF16)
    kvc = _compress(t_cmp, pe, w1, w2)
    vct = jnp.swapaxes(kvc[1], 1, 2)
    gates_a = ps[:, 2 * M_HEADS:n_small].reshape(bsz, seq, g, 3 * A_GROUP).transpose(0, 2, 3, 1)
    y_a = _nsa(p, qt, kvc, vct, kv6, vst, vwt, gates_a, z_col, bsz, seq)

    out = _merge_out(y_m, y_a, w_up_m.astype(BF16), w_up_a.astype(BF16), p, gm_col, ga_col,
                     w_out.astype(BF16), post_g.reshape(1, d), x2d)
    return out.reshape(bsz, seq, d)


def kernel(x, pre_norm_g, w_in, m_conv_w, m_conv_b, m_w_q, m_w_k, m_b_i, m_b_f, m_norm_g, m_skip, a_pe_k, a_w1_k, a_w2_k, a_pe_v, a_w1_v, a_w2_v, w_up_m, w_up_a, w_out, post_norm_g):
    depth = w_in.shape[0]
    for l in range(depth):
        x = _layer(x, pre_norm_g[l], w_in[l], m_conv_w[l], m_conv_b[l], m_w_q[l], m_w_k[l], m_b_i[l], m_b_f[l],
                   m_norm_g[l], m_skip[l], a_pe_k[l], a_w1_k[l], a_w2_k[l], a_pe_v[l], a_w1_v[l], a_w2_v[l],
                   w_up_m[l], w_up_a[l], w_out[l], post_norm_g[l])
    return x
```

```python
import functools

import jax
import jax.numpy as jnp
from jax import lax
from jax.experimental import pallas as pl
from jax.experimental.pallas import tpu as pltpu

F32 = jnp.float32
BF16 = jnp.bfloat16

M_HEADS = 4
M_CONV = 4
M_CHUNK = 128
A_HEADS = 16
A_KV_HEADS = 4
A_GROUP = A_HEADS // A_KV_HEADS
A_HEAD_DIM = 64
A_CMP_BLOCK = 32
A_CMP_STRIDE = 16
A_SEL_BLOCK = 64
A_SEL_TOPK = 8
A_WINDOW = 512
Q_BLOCK = 128
SEL_KEYS_PER_STEP = 512
RMS_EPS = 1e-6
LN_EPS = 1e-6
NEG_INF = -1e30
POS_INF = 1e30

LANES = 128
BF16_SUBLANES = 16
LOG2_E = 1.4426950408889634
VMEM_LIMIT_BYTES = 48 * 1024 * 1024


def _nt_dot(a, b):
    return lax.dot_general(a, b, (((1,), (1,)), ((), ())), preferred_element_type=F32)


def _dot(a, b):
    return jnp.dot(a, b, preferred_element_type=F32)


def _sigmoid(x):
    return 1.0 / (1.0 + jnp.exp(-x))


def _params(*sem):
    return pltpu.CompilerParams(dimension_semantics=sem, vmem_limit_bytes=VMEM_LIMIT_BYTES)


def _inproj_kernel(x_ref, g_ref, w_ref, ws_ref, o_ref, os_ref, h_scr):
    @pl.when(pl.program_id(1) == 0)
    def _():
        xf = x_ref[...]
        ms = jnp.mean(xf * xf, axis=-1, keepdims=True)
        hb = (xf * lax.rsqrt(ms + RMS_EPS) * g_ref[...]).astype(BF16)
        h_scr[...] = hb
        os_ref[...] = _dot(hb, ws_ref[...])

    o_ref[...] = _dot(h_scr[...], w_ref[...]).astype(o_ref.dtype)


def _inproj(x2d, g, w_main, w_small, tm=1024, tn=512):
    t, d = x2d.shape
    n = w_main.shape[1]
    ns = w_small.shape[1]
    return pl.pallas_call(
        _inproj_kernel,
        grid=(t // tm, n // tn),
        in_specs=[
            pl.BlockSpec((tm, d), lambda i, j: (i, 0)),
            pl.BlockSpec((1, d), lambda i, j: (0, 0)),
            pl.BlockSpec((d, tn), lambda i, j: (0, j)),
            pl.BlockSpec((d, ns), lambda i, j: (0, 0)),
        ],
        out_specs=[
            pl.BlockSpec((tm, tn), lambda i, j: (i, j)),
            pl.BlockSpec((tm, ns), lambda i, j: (i, 0)),
        ],
        out_shape=[jax.ShapeDtypeStruct((t, n), BF16), jax.ShapeDtypeStruct((t, ns), F32)],
        scratch_shapes=[pltpu.VMEM((tm, d), BF16)],
        compiler_params=_params("parallel", "arbitrary"),
        name="inproj",
    )(x2d, g, w_main, w_small)


def _conv_qk_kernel(x_ref, cw_ref, cb_ref, wq_ref, wk_ref, xc_ref, q_ref, k_ref, pad_ref, *, rows):
    s, c = x_ref.shape
    front = 8
    pad_ref[0:front, :] = jnp.zeros((front, c), F32)
    pad_ref[front:front + s, :] = x_ref[...].astype(F32)
    cw = cw_ref[...]
    cb = cb_ref[...]
    wq = wq_ref[...]
    wk = wk_ref[...]
    k_scale = c ** -0.5
    for r in range(s // rows):
        acc = jnp.broadcast_to(cb, (rows, c))
        for kk in range(M_CONV):
            start = front + r * rows + kk - (M_CONV - 1)
            acc = acc + cw[kk:kk + 1, :] * pad_ref[start:start + rows, :]
        xcb = (acc * _sigmoid(acc)).astype(BF16)
        xc_ref[r * rows:(r + 1) * rows, :] = xcb
        q_ref[r * rows:(r + 1) * rows, :] = _dot(xcb, wq).astype(BF16)
        k_ref[r * rows:(r + 1) * rows, :] = (_dot(xcb, wk) * k_scale).astype(BF16)


def _conv_qk(p, mx_col, conv_w, conv_b, wq, wk, bsz, seq):
    t = p.shape[0]
    heads, hd = wq.shape[0], wq.shape[1]
    width = heads * hd
    blk = pl.BlockSpec((seq, hd), lambda b, h: (b, h))
    out = jax.ShapeDtypeStruct((t, width), BF16)
    return pl.pallas_call(
        functools.partial(_conv_qk_kernel, rows=256),
        grid=(bsz, heads),
        in_specs=[
            pl.BlockSpec((seq, hd), lambda b, h: (b, mx_col // hd + h)),
            pl.BlockSpec((M_CONV, hd), lambda b, h: (0, h)),
            pl.BlockSpec((1, hd), lambda b, h: (0, h)),
            pl.BlockSpec((None, hd, hd), lambda b, h: (h, 0, 0)),
            pl.BlockSpec((None, hd, hd), lambda b, h: (h, 0, 0)),
        ],
        out_specs=[blk, blk, blk],
        out_shape=[out, out, out],
        scratch_shapes=[pltpu.VMEM((seq + 8, hd), F32)],
        compiler_params=_params("parallel", "parallel"),
        name="conv_qk",
    )(p, conv_w, conv_b, wq, wk)


def _log_sigmoid(x):
    return jnp.minimum(x, 0.0) - jnp.log1p(jnp.exp(-jnp.abs(x)))


def _mlstm_kernel(bias_ref, q_ref, k_ref, v_ref, o_ref, z_ref, xc_ref, g_ref, ng_ref, sk_ref,
                  y_ref, c_scr, n_scr, m_scr):
    @pl.when(pl.program_id(1) == 0)
    def _():
        c_scr[...] = jnp.zeros_like(c_scr)
        n_scr[...] = jnp.zeros_like(n_scr)
        m_scr[...] = jnp.zeros_like(m_scr)

    ln = q_ref.shape[0]
    heads, hd = c_scr.shape[0], c_scr.shape[1]
    gates = g_ref[...] + bias_ref[...]
    fb_all = _log_sigmoid(gates[heads:2 * heads, :])
    row = lax.broadcasted_iota(jnp.int32, (ln, ln), 0)
    col = lax.broadcasted_iota(jnp.int32, (ln, ln), 1)
    tril = col <= row
    eye = col == row
    for hh in range(heads):
        sl = slice(hh * hd, (hh + 1) * hd)
        ib = gates[hh:hh + 1, :]
        fb = fb_all[hh:hh + 1, :]
        bcum_c = jnp.sum(jnp.where(tril, fb, 0.0), axis=1, keepdims=True)
        bcum_r = jnp.sum(jnp.where(eye, bcum_c, 0.0), axis=0, keepdims=True)
        g_tot = jnp.sum(fb, axis=1, keepdims=True)
        e_r = ib - bcum_r
        m_st = m_scr[hh]
        d_intra = jnp.where(tril, bcum_c + e_r, NEG_INF)
        d_inter = bcum_c + m_st
        m_t = jnp.maximum(d_inter, jnp.max(d_intra, axis=1, keepdims=True))
        w_intra = jnp.exp(d_intra - m_t)
        w_inter = jnp.exp(d_inter - m_t)

        qb = q_ref[:, sl]
        kb = k_ref[:, sl]
        vb = v_ref[:, sl]
        c_st = c_scr[hh]
        n_st = n_scr[hh]
        sc = _nt_dot(qb, kb) * w_intra
        num = _dot(sc.astype(BF16), vb) + w_inter * _dot(qb, c_st.astype(BF16))
        qn = jnp.sum(qb.astype(F32) * n_st, axis=1, keepdims=True)
        den = jnp.sum(sc, axis=1, keepdims=True) + w_inter * qn
        h = num / jnp.maximum(jnp.abs(den), jnp.exp(-m_t))

        d_state = g_tot + e_r
        m_new = jnp.maximum(g_tot + m_st, jnp.max(d_state, axis=1, keepdims=True))
        w_s_r = jnp.exp(d_state - m_new)
        w_s_c = jnp.sum(jnp.where(eye, w_s_r, 0.0), axis=1, keepdims=True)
        decay = jnp.exp(g_tot + m_st - m_new)
        kw = kb.astype(F32) * w_s_c
        kv = lax.dot_general(kw.astype(BF16), vb, (((0,), (0,)), ((), ())), preferred_element_type=F32)
        c_scr[hh] = decay * c_st + kv
        n_scr[hh] = decay * n_st + jnp.sum(kw, axis=0, keepdims=True)
        m_scr[hh] = m_new

        hg = h * _sigmoid(o_ref[:, sl].astype(F32))
        mu = jnp.mean(hg, axis=-1, keepdims=True)
        hc = hg - mu
        var = jnp.mean(hc * hc, axis=-1, keepdims=True)
        hn = hc * lax.rsqrt(var + LN_EPS)
        z = z_ref[:, sl].astype(F32)
        y = (hn * ng_ref[:, sl] + sk_ref[:, sl] * xc_ref[:, sl].astype(F32)) * (z * _sigmoid(z))
        y_ref[:, sl] = y.astype(y_ref.dtype)


def _mlstm(p, mx_col, xc, q, k, gates, bias, norm_g, skip, bsz, seq):
    t, width = q.shape
    pc = mx_col // width
    heads = M_HEADS
    hd = width // heads
    ln = M_CHUNK
    nc = seq // ln
    rows = lambda b, c: (b * nc + c, 0)
    vec = pl.BlockSpec((1, width), lambda b, c: (0, 0))
    return pl.pallas_call(
        _mlstm_kernel,
        grid=(bsz, nc),
        in_specs=[
            pl.BlockSpec((2 * heads, 1), lambda b, c: (0, 0)),
            pl.BlockSpec((ln, width), rows),
            pl.BlockSpec((ln, width), rows),
            pl.BlockSpec((ln, width), lambda b, c: (b * nc + c, pc + 1)),
            pl.BlockSpec((ln, width), lambda b, c: (b * nc + c, pc + 2)),
            pl.BlockSpec((ln, width), lambda b, c: (b * nc + c, pc + 3)),
            pl.BlockSpec((ln, width), rows),
            pl.BlockSpec((None, None, 2 * heads, ln), lambda b, c: (b, c, 0, 0)),
            vec, vec,
        ],
        out_specs=pl.BlockSpec((ln, width), rows),
        out_shape=jax.ShapeDtypeStruct((t, width), BF16),
        scratch_shapes=[pltpu.VMEM((heads, hd, hd), F32), pltpu.VMEM((heads, 1, hd), F32),
                        pltpu.VMEM((heads, 1, 1), F32)],
        compiler_params=_params("parallel", "arbitrary"),
        name="mlstm",
    )(bias, q, k, p, p, p, xc, gates, norm_g, skip)


def _compress_kernel(t_ref, pe_ref, w1_ref, w2_ref, o_ref):
    t = t_ref[...].astype(F32)
    n_sub, half = t.shape
    pe = pe_ref[...]
    za = _dot((t + pe[0:1, :]).astype(BF16), w1_ref[0:half, :])
    zb = _dot((t + pe[1:2, :]).astype(BF16), w1_ref[half:2 * half, :])
    pre = za + pltpu.roll(zb, n_sub - 1, 0)
    hid = (pre * _sigmoid(pre)).astype(BF16)
    out = _dot(hid, w2_ref[...])
    rid = lax.broadcasted_iota(jnp.int32, out.shape, 0)
    o_ref[...] = jnp.where(rid < n_sub - 1, out, 0.0).astype(o_ref.dtype)


def _compress(tkv, pe, w1, w2):
    two, bg, n_sub, half = tkv.shape
    hid = w1.shape[-1]
    dh = w2.shape[-1]
    return pl.pallas_call(
        _compress_kernel,
        grid=(two, bg),
        in_specs=[
            pl.BlockSpec((None, None, n_sub, half), lambda i, j: (i, j, 0, 0)),
            pl.BlockSpec((None, 2, half), lambda i, j: (i, 0, 0)),
            pl.BlockSpec((None, 2 * half, hid), lambda i, j: (i, 0, 0)),
            pl.BlockSpec((None, hid, dh), lambda i, j: (i, 0, 0)),
        ],
        out_specs=pl.BlockSpec((None, None, n_sub, dh), lambda i, j: (i, j, 0, 0)),
        out_shape=jax.ShapeDtypeStruct((two, bg, n_sub, dh), BF16),
        compiler_params=_params("parallel", "parallel"),
        name="compress",
    )(tkv, pe, w1, w2)


def _mask_heads(x, ok, fill, tq):
    n = x.shape[1] // tq
    return jnp.concatenate([jnp.where(ok, x[:, h * tq:(h + 1) * tq], fill) for h in range(n)], axis=1)


def _with_ones_rows(vt):
    return jnp.concatenate([vt, jnp.ones((BF16_SUBLANES, vt.shape[1]), vt.dtype)], axis=0)


def _nsa_kernel(qt_ref, kc_ref, vct_ref, ks_ref, vst_ref, kw_ref, vwt_ref, gate_ref, z_ref, o_ref, sel_ref):
    qi = pl.program_id(1)
    tq = qt_ref.shape[1]
    dh = A_HEAD_DIM
    hpg = A_GROUP
    ng = A_KV_HEADS
    r = hpg * tq
    n_cmp = kc_ref.shape[1]
    kt = vst_ref.shape[2]
    tk = vwt_ref.shape[2]
    seq = ks_ref.shape[0]
    n_sel = seq // A_SEL_BLOCK
    t0 = qi * tq

    qt = (qt_ref[...].astype(F32) * (dh ** -0.5 * LOG2_E)).astype(BF16)
    q4t = [jnp.concatenate([qt[(g * hpg + h) * dh:(g * hpg + h + 1) * dh, :] for h in range(hpg)], axis=1)
           for g in range(ng)]
    zpad = jnp.zeros((dh, r), BF16)
    q_pad = [jnp.concatenate([q4t[g] if gg == g else zpad for gg in range(ng)], axis=0) for g in range(ng)]

    cn = lax.broadcasted_iota(jnp.int32, (n_cmp, tq), 0)
    ct = t0 + lax.broadcasted_iota(jnp.int32, (n_cmp, tq), 1)
    valid = (cn * A_CMP_STRIDE + (A_CMP_BLOCK - 1)) <= ct
    jn = lax.broadcasted_iota(jnp.int32, (n_sel, n_cmp), 0)
    nn = lax.broadcasted_iota(jnp.int32, (n_sel, n_cmp), 1)
    ov = ((nn * A_CMP_STRIDE < (jn + 1) * A_SEL_BLOCK)
          & (nn * A_CMP_STRIDE + (A_CMP_BLOCK - 1) >= jn * A_SEL_BLOCK))
    ov_t = jnp.where(ov, 1.0, 0.0).astype(BF16)
    jr = lax.broadcasted_iota(jnp.int32, (n_sel, tq), 0)
    tl = t0 + lax.broadcasted_iota(jnp.int32, (n_sel, tq), 1)
    cur = lax.shift_right_logical(tl, 6)
    forced = (jr == 0) | (jr == cur) | (jr == cur - 1)
    o_c = []
    for g in range(ng):
        s_m = _mask_heads(_dot(kc_ref[g], q4t[g]), valid, NEG_INF, tq)
        mx = jnp.max(s_m, axis=0, keepdims=True)
        e = _mask_heads(jnp.exp2(s_m - mx), valid, 0.0, tq)
        lsum = jnp.sum(e, axis=0, keepdims=True)
        p_c = e * (1.0 / jnp.where(lsum > 0.0, lsum, 1.0))
        o_c.append(_dot(vct_ref[g], p_c.astype(BF16)))
        p_sum = p_c[:, 0:tq]
        for h in range(1, hpg):
            p_sum = p_sum + p_c[:, h * tq:(h + 1) * tq]
        p_hi = p_sum.astype(BF16)
        r1 = p_sum - p_hi.astype(F32)
        p_mid = r1.astype(BF16)
        p_lo = (r1 - p_mid.astype(F32)).astype(BF16)
        imp_t = _dot(ov_t, p_hi) + _dot(ov_t, p_mid) + _dot(ov_t, p_lo)
        score = jnp.where(forced, POS_INF, jnp.where(jr <= cur, imp_t, NEG_INF))
        rank = jnp.zeros((n_sel, tq), F32)
        for jp in range(n_sel):
            sj = score[jp:jp + 1, :]
            beats = (sj > score) | ((sj == score) & (jr > jp))
            rank = rank + jnp.where(beats, 1.0, 0.0)
        sel_ref[g] = jnp.where(rank < float(min(A_SEL_TOPK, n_sel)), 1.0, 0.0)

    sub_s = lax.broadcasted_iota(jnp.int32, (kt, tq), 0)
    t_abs_s = t0 + lax.broadcasted_iota(jnp.int32, (kt, tq), 1)
    blocks_per_iter = kt // A_SEL_BLOCK
    da = dh + BF16_SUBLANES

    def sel_body(j, carry):
        off = pl.multiple_of(j * kt, kt)
        k_all = ks_ref[pl.ds(off, kt), :]
        causal = (off + sub_s) <= t_abs_s
        out = []
        for g in range(ng):
            m, acc = carry[g]
            s = _dot(k_all, q_pad[g])
            flags = jnp.concatenate(
                [jnp.broadcast_to(sel_ref[g, pl.ds(j * blocks_per_iter + i, 1), :], (A_SEL_BLOCK, tq))
                 for i in range(blocks_per_iter)], axis=0)
            s = _mask_heads(s, (flags > 0.5) & causal, NEG_INF, tq)
            m_new = jnp.maximum(m, jnp.max(s, axis=0, keepdims=True))
            p = jnp.exp2(s - m_new).astype(BF16)
            vt = _with_ones_rows(vst_ref[j, g * dh:(g + 1) * dh, :])
            out.append((m_new, jnp.exp2(m - m_new) * acc + _dot(vt, p)))
        return tuple(out)

    init = tuple((jnp.full((1, r), NEG_INF, F32), jnp.zeros((da, r), F32)) for _ in range(ng))
    sel_state = lax.fori_loop(0, (t0 + tq - 1) // kt + 1, sel_body, init)

    n_wb = -(-A_WINDOW // tk)
    wk = (n_wb + 1) * tk
    j0 = jnp.maximum(qi - n_wb, 0)
    off_w = pl.multiple_of(j0 * tk, tk)
    c_abs = off_w + lax.broadcasted_iota(jnp.int32, (wk, tq), 0)
    t_abs_w = t0 + lax.broadcasted_iota(jnp.int32, (wk, tq), 1)
    ok_w = (c_abs <= t_abs_w) & (c_abs > t_abs_w - A_WINDOW)
    kw_all = kw_ref[pl.ds(off_w, wk), :]

    gs = _sigmoid(gate_ref[...])
    outs = []
    for g in range(ng):
        s_w = _mask_heads(_dot(kw_all, q_pad[g]), ok_w, NEG_INF, tq)
        p_w = jnp.exp2(s_w - jnp.max(s_w, axis=0, keepdims=True)).astype(BF16)
        acc_w = _dot(_with_ones_rows(vwt_ref[j0, g * dh:(g + 1) * dh, :]), p_w[0:tk, :])
        for i in range(1, n_wb + 1):
            acc_w = acc_w + _dot(_with_ones_rows(vwt_ref[j0 + i, g * dh:(g + 1) * dh, :]), p_w[i * tk:(i + 1) * tk, :])
        o_w = acc_w[0:dh, :] * (1.0 / acc_w[dh:dh + 1, :])
        acc_s = sel_state[g][1]
        o_s = acc_s[0:dh, :] * (1.0 / acc_s[dh:dh + 1, :])
        for h in range(hpg):
            sl = slice(h * tq, (h + 1) * tq)
            outs.append(gs[g, 3 * h:3 * h + 1, :] * o_c[g][:, sl] + gs[g, 3 * h + 1:3 * h + 2, :] * o_s[:, sl]
                        + gs[g, 3 * h + 2:3 * h + 3, :] * o_w[:, sl])
    o_all = jnp.concatenate(outs, axis=0).T
    z = z_ref[...].astype(F32)
    o_ref[...] = (o_all * (z * _sigmoid(z))).astype(o_ref.dtype)


def _nsa(p, qt, kvc, vct, vst, vwt, gates, ks_col, kw_col, z_col, bsz, seq):
    t = p.shape[0]
    g = A_KV_HEADS
    dh = A_HEAD_DIM
    aw = A_HEADS * dh
    kvw = g * dh
    tq = Q_BLOCK
    nq = seq // tq
    n_cmp = kvc.shape[3]
    whole = lambda a: pl.BlockSpec((None,) + a.shape[1:], lambda b, qi: (b,) + (0,) * (a.ndim - 1))
    return pl.pallas_call(
        _nsa_kernel,
        grid=(bsz, nq),
        in_specs=[
            pl.BlockSpec((None, aw, tq), lambda b, qi: (b, 0, qi)),
            pl.BlockSpec((None, None, g, n_cmp, dh), lambda b, qi: (0, b, 0, 0, 0)),
            whole(vct),
            pl.BlockSpec((seq, kvw), lambda b, qi: (b, ks_col // kvw)),
            whole(vst),
            pl.BlockSpec((seq, kvw), lambda b, qi: (b, kw_col // kvw)),
            whole(vwt),
            pl.BlockSpec((None, g, 3 * A_GROUP, tq), lambda b, qi: (b, 0, 0, qi)),
            pl.BlockSpec((tq, aw), lambda b, qi: (b * nq + qi, z_col // aw)),
        ],
        out_specs=pl.BlockSpec((tq, aw), lambda b, qi: (b * nq + qi, 0)),
        out_shape=jax.ShapeDtypeStruct((t, aw), BF16),
        scratch_shapes=[pltpu.VMEM((g, seq // A_SEL_BLOCK, tq), F32)],
        compiler_params=_params("parallel", "arbitrary"),
        name="nsa",
    )(qt, kvc, vct, p, vst, p, vwt, gates, p)


def _merge_out_kernel(ym_ref, ya_ref, wm_ref, wa_ref, gm_ref, ga_ref, wo_ref, g_ref, x_ref, o_ref):
    um = _dot(ym_ref[...], wm_ref[...])
    ua = _dot(ya_ref[...], wa_ref[...])
    mg = _sigmoid(gm_ref[...].astype(F32)) * um + _sigmoid(ga_ref[...].astype(F32)) * ua
    out = _dot(mg.astype(BF16), wo_ref[...])
    ms = jnp.mean(out * out, axis=-1, keepdims=True)
    o_ref[...] = x_ref[...] + out * lax.rsqrt(ms + RMS_EPS) * g_ref[...]


def _merge_out(ym, ya, wm, wa, p, gm_col, ga_col, wo, g, x2d, tm=256):
    t, d = x2d.shape
    kdim = ym.shape[1]
    resident = lambda shape: pl.BlockSpec(shape, lambda i: (0, 0), pipeline_mode=pl.Buffered(1))
    return pl.pallas_call(
        _merge_out_kernel,
        grid=(t // tm,),
        in_specs=[
            pl.BlockSpec((tm, kdim), lambda i: (i, 0)),
            pl.BlockSpec((tm, kdim), lambda i: (i, 0)),
            resident((kdim, d)),
            resident((kdim, d)),
            pl.BlockSpec((tm, d), lambda i: (i, gm_col // d)),
            pl.BlockSpec((tm, d), lambda i: (i, ga_col // d)),
            resident((d, d)),
            pl.BlockSpec((1, d), lambda i: (0, 0)),
            pl.BlockSpec((tm, d), lambda i: (i, 0)),
        ],
        out_specs=pl.BlockSpec((tm, d), lambda i: (i, 0)),
        out_shape=jax.ShapeDtypeStruct((t, d), F32),
        compiler_params=_params("parallel"),
        name="merge_out",
    )(ym, ya, wm, wa, p, p, wo, g, x2d)


def _layer(x, pre_g, w_in, conv_w, conv_b, w_q, w_k, b_i, b_f, norm_g, skip,
           pe_k, w1_k, w2_k, pe_v, w1_v, w2_v, w_up_m, w_up_a, w_out, post_g):
    bsz, seq, d = x.shape
    t = bsz * seq
    mw = w_up_m.shape[0]
    aw = w_up_a.shape[0]
    g = A_KV_HEADS
    dh = A_HEAD_DIM
    kvw = g * dh
    n_gate = 3 * A_HEADS

    o_i = 4 * mw
    o_f = o_i + M_HEADS
    o_q = o_f + M_HEADS
    o_g = o_q + aw + 6 * kvw
    o_z = o_g + n_gate
    o_m = o_z + aw
    o_kv = o_q + aw
    w_main = jnp.concatenate(
        [w_in[:, o_m:], w_in[:, :o_i], w_in[:, o_q:o_kv], w_in[:, o_z:o_m], w_in[:, o_kv:o_g]], axis=1).astype(BF16)
    n_small = 2 * M_HEADS + n_gate
    w_small = jnp.concatenate(
        [w_in[:, o_i:o_q], w_in[:, o_g:o_z], jnp.zeros((d, LANES - n_small), w_in.dtype)], axis=1).astype(BF16)
    gm_col = 0
    ga_col = d
    mx_col = 2 * d
    q_col = mx_col + 4 * mw
    z_col = q_col + aw
    kv_col = z_col + aw

    x2d = x.reshape(t, d)
    p, ps = _inproj(x2d, pre_g.reshape(1, d), w_main, w_small)

    xc, q, k = _conv_qk(p, mx_col, conv_w, conv_b.reshape(1, mw), w_q.astype(BF16), w_k.astype(BF16), bsz, seq)
    nc = seq // M_CHUNK
    gates_m = ps[:, :2 * M_HEADS].reshape(bsz, nc, M_CHUNK, 2 * M_HEADS).transpose(0, 1, 3, 2)
    bias_m = jnp.concatenate([b_i, b_f]).reshape(2 * M_HEADS, 1)
    y_m = _mlstm(p, mx_col, xc, q, k, gates_m, bias_m, norm_g.reshape(1, mw), skip.reshape(1, mw), bsz, seq)

    kv = p[:, kv_col:kv_col + 6 * kvw].reshape(bsz, seq, 6, g, dh)

    def value_tiles_t(v, tile):
        return v.reshape(bsz, seq // tile, tile, kvw).transpose(0, 1, 3, 2)

    vst = value_tiles_t(kv[:, :, 3], SEL_KEYS_PER_STEP)
    vwt = value_tiles_t(kv[:, :, 5], LANES)
    qt = p[:, q_col:q_col + aw].reshape(bsz, seq, aw).transpose(0, 2, 1)
    n_sub = seq // A_CMP_STRIDE
    half = A_CMP_STRIDE * dh
    t_cmp = kv[:, :, 0:2].transpose(2, 0, 3, 1, 4).reshape(2, bsz * g, n_sub, half)
    pe = jnp.stack([pe_k.reshape(2, half), pe_v.reshape(2, half)])
    w1 = jnp.stack([w1_k, w1_v]).astype(BF16)
    w2 = jnp.stack([w2_k, w2_v]).astype(BF16)
    kvc = _compress(t_cmp, pe, w1, w2).reshape(2, bsz, g, n_sub, dh)
    vct = jnp.swapaxes(kvc[1], 2, 3)
    gates_a = ps[:, 2 * M_HEADS:n_small].reshape(bsz, seq, g, 3 * A_GROUP).transpose(0, 2, 3, 1)
    y_a = _nsa(p, qt, kvc, vct, vst, vwt, gates_a, kv_col + 2 * kvw, kv_col + 4 * kvw, z_col, bsz, seq)

    out = _merge_out(y_m, y_a, w_up_m.astype(BF16), w_up_a.astype(BF16), p, gm_col, ga_col,
                     w_out.astype(BF16), post_g.reshape(1, d), x2d)
    return out.reshape(bsz, seq, d)


def kernel(x, pre_norm_g, w_in, m_conv_w, m_conv_b, m_w_q, m_w_k, m_b_i, m_b_f, m_norm_g, m_skip, a_pe_k, a_w1_k, a_w2_k, a_pe_v, a_w1_v, a_w2_v, w_up_m, w_up_a, w_out, post_norm_g):
    depth = w_in.shape[0]
    for l in range(depth):
        x = _layer(x, pre_norm_g[l], w_in[l], m_conv_w[l], m_conv_b[l], m_w_q[l], m_w_k[l], m_b_i[l], m_b_f[l],
                   m_norm_g[l], m_skip[l], a_pe_k[l], a_w1_k[l], a_w2_k[l], a_pe_v[l], a_w1_v[l], a_w2_v[l],
                   w_up_m[l], w_up_a[l], w_out[l], post_norm_g[l])
    return x
```

```python
import functools

import jax
import jax.numpy as jnp
from jax import lax
from jax.experimental import pallas as pl
from jax.experimental.pallas import tpu as pltpu

F32 = jnp.float32
BF16 = jnp.bfloat16

M_HEADS = 4
M_CONV = 4
M_CHUNK = 128
A_HEADS = 16
A_KV_HEADS = 4
A_GROUP = A_HEADS // A_KV_HEADS
A_HEAD_DIM = 64
A_CMP_BLOCK = 32
A_CMP_STRIDE = 16
A_SEL_BLOCK = 64
A_SEL_TOPK = 8
A_WINDOW = 512
Q_BLOCK = 128
SEL_KEYS_PER_STEP = 512
RMS_EPS = 1e-6
LN_EPS = 1e-6
NEG_INF = -1e30
POS_INF = 1e30

LANES = 128
BF16_SUBLANES = 16
LOG2_E = 1.4426950408889634
VMEM_LIMIT_BYTES = 48 * 1024 * 1024


def _nt_dot(a, b):
    return lax.dot_general(a, b, (((1,), (1,)), ((), ())), preferred_element_type=F32)


def _dot(a, b):
    return jnp.dot(a, b, preferred_element_type=F32)


def _sigmoid(x):
    return 1.0 / (1.0 + jnp.exp(-x))


def _params(*sem):
    return pltpu.CompilerParams(dimension_semantics=sem, vmem_limit_bytes=VMEM_LIMIT_BYTES)


def _inproj_kernel(x_ref, g_ref, w_ref, ws_ref, o_ref, os_ref, h_scr):
    @pl.when(pl.program_id(1) == 0)
    def _():
        xf = x_ref[...]
        ms = jnp.mean(xf * xf, axis=-1, keepdims=True)
        hb = (xf * lax.rsqrt(ms + RMS_EPS) * g_ref[...]).astype(BF16)
        h_scr[...] = hb
        os_ref[...] = _dot(hb, ws_ref[...])

    o_ref[...] = _dot(h_scr[...], w_ref[...]).astype(o_ref.dtype)


def _inproj(x2d, g, w_main, w_small, tm=1024, tn=512):
    t, d = x2d.shape
    n = w_main.shape[1]
    ns = w_small.shape[1]
    return pl.pallas_call(
        _inproj_kernel,
        grid=(t // tm, n // tn),
        in_specs=[
            pl.BlockSpec((tm, d), lambda i, j: (i, 0)),
            pl.BlockSpec((1, d), lambda i, j: (0, 0)),
            pl.BlockSpec((d, tn), lambda i, j: (0, j)),
            pl.BlockSpec((d, ns), lambda i, j: (0, 0)),
        ],
        out_specs=[
            pl.BlockSpec((tm, tn), lambda i, j: (i, j)),
            pl.BlockSpec((tm, ns), lambda i, j: (i, 0)),
        ],
        out_shape=[jax.ShapeDtypeStruct((t, n), BF16), jax.ShapeDtypeStruct((t, ns), F32)],
        scratch_shapes=[pltpu.VMEM((tm, d), BF16)],
        compiler_params=_params("parallel", "arbitrary"),
        name="inproj",
    )(x2d, g, w_main, w_small)


def _conv_qk_kernel(x_ref, cw_ref, cb_ref, wq_ref, wk_ref, xc_ref, q_ref, k_ref, pad_ref, *, rows):
    s, c = x_ref.shape
    front = 8
    pad_ref[0:front, :] = jnp.zeros((front, c), F32)
    pad_ref[front:front + s, :] = x_ref[...].astype(F32)
    cw = cw_ref[...]
    cb = cb_ref[...]
    wq = wq_ref[...]
    wk = wk_ref[...]
    k_scale = c ** -0.5
    for r in range(s // rows):
        acc = jnp.broadcast_to(cb, (rows, c))
        for kk in range(M_CONV):
            start = front + r * rows + kk - (M_CONV - 1)
            acc = acc + cw[kk:kk + 1, :] * pad_ref[start:start + rows, :]
        xcb = (acc * _sigmoid(acc)).astype(BF16)
        xc_ref[r * rows:(r + 1) * rows, :] = xcb
        q_ref[r * rows:(r + 1) * rows, :] = _dot(xcb, wq).astype(BF16)
        k_ref[r * rows:(r + 1) * rows, :] = (_dot(xcb, wk) * k_scale).astype(BF16)


def _conv_qk(p, mx_col, conv_w, conv_b, wq, wk, bsz, seq):
    t = p.shape[0]
    heads, hd = wq.shape[0], wq.shape[1]
    width = heads * hd
    blk = pl.BlockSpec((seq, hd), lambda b, h: (b, h))
    out = jax.ShapeDtypeStruct((t, width), BF16)
    return pl.pallas_call(
        functools.partial(_conv_qk_kernel, rows=256),
        grid=(bsz, heads),
        in_specs=[
            pl.BlockSpec((seq, hd), lambda b, h: (b, mx_col // hd + h)),
            pl.BlockSpec((M_CONV, hd), lambda b, h: (0, h)),
            pl.BlockSpec((1, hd), lambda b, h: (0, h)),
            pl.BlockSpec((None, hd, hd), lambda b, h: (h, 0, 0)),
            pl.BlockSpec((None, hd, hd), lambda b, h: (h, 0, 0)),
        ],
        out_specs=[blk, blk, blk],
        out_shape=[out, out, out],
        scratch_shapes=[pltpu.VMEM((seq + 8, hd), F32)],
        compiler_params=_params("parallel", "parallel"),
        name="conv_qk",
    )(p, conv_w, conv_b, wq, wk)


def _log_sigmoid(x):
    return jnp.minimum(x, 0.0) - jnp.log1p(jnp.exp(-jnp.abs(x)))


def _mlstm_kernel(bias_ref, q_ref, k_ref, v_ref, o_ref, z_ref, xc_ref, g_ref, ng_ref, sk_ref,
                  y_ref, c_scr, n_scr, m_scr):
    @pl.when(pl.program_id(1) == 0)
    def _():
        c_scr[...] = jnp.zeros_like(c_scr)
        n_scr[...] = jnp.zeros_like(n_scr)
        m_scr[...] = jnp.zeros_like(m_scr)

    ln = q_ref.shape[0]
    heads, hd = c_scr.shape[0], c_scr.shape[1]
    gates = g_ref[...].T[0:2 * heads, :] + bias_ref[...]
    fb_all = _log_sigmoid(gates[heads:2 * heads, :])
    row = lax.broadcasted_iota(jnp.int32, (ln, ln), 0)
    col = lax.broadcasted_iota(jnp.int32, (ln, ln), 1)
    tril = col <= row
    eye = col == row
    for hh in range(heads):
        sl = slice(hh * hd, (hh + 1) * hd)
        ib = gates[hh:hh + 1, :]
        fb = fb_all[hh:hh + 1, :]
        bcum_c = jnp.sum(jnp.where(tril, fb, 0.0), axis=1, keepdims=True)
        bcum_r = jnp.sum(jnp.where(eye, bcum_c, 0.0), axis=0, keepdims=True)
        g_tot = jnp.sum(fb, axis=1, keepdims=True)
        e_r = ib - bcum_r
        m_st = m_scr[hh]
        d_intra = jnp.where(tril, bcum_c + e_r, NEG_INF)
        d_inter = bcum_c + m_st
        m_t = jnp.maximum(d_inter, jnp.max(d_intra, axis=1, keepdims=True))
        w_intra = jnp.exp(d_intra - m_t)
        w_inter = jnp.exp(d_inter - m_t)

        qb = q_ref[:, sl]
        kb = k_ref[:, sl]
        vb = v_ref[:, sl]
        c_st = c_scr[hh]
        n_st = n_scr[hh]
        sc = _nt_dot(qb, kb) * w_intra
        num = _dot(sc.astype(BF16), vb) + w_inter * _dot(qb, c_st.astype(BF16))
        qn = jnp.sum(qb.astype(F32) * n_st, axis=1, keepdims=True)
        den = jnp.sum(sc, axis=1, keepdims=True) + w_inter * qn
        h = num / jnp.maximum(jnp.abs(den), jnp.exp(-m_t))

        d_state = g_tot + e_r
        m_new = jnp.maximum(g_tot + m_st, jnp.max(d_state, axis=1, keepdims=True))
        w_s_r = jnp.exp(d_state - m_new)
        w_s_c = jnp.sum(jnp.where(eye, w_s_r, 0.0), axis=1, keepdims=True)
        decay = jnp.exp(g_tot + m_st - m_new)
        kw = kb.astype(F32) * w_s_c
        kv = lax.dot_general(kw.astype(BF16), vb, (((0,), (0,)), ((), ())), preferred_element_type=F32)
        c_scr[hh] = decay * c_st + kv
        n_scr[hh] = decay * n_st + jnp.sum(kw, axis=0, keepdims=True)
        m_scr[hh] = m_new

        hg = h * _sigmoid(o_ref[:, sl].astype(F32))
        mu = jnp.mean(hg, axis=-1, keepdims=True)
        hc = hg - mu
        var = jnp.mean(hc * hc, axis=-1, keepdims=True)
        hn = hc * lax.rsqrt(var + LN_EPS)
        z = z_ref[:, sl].astype(F32)
        y = (hn * ng_ref[:, sl] + sk_ref[:, sl] * xc_ref[:, sl].astype(F32)) * (z * _sigmoid(z))
        y_ref[:, sl] = y.astype(y_ref.dtype)


def _mlstm(p, mx_col, xc, q, k, gates, bias, norm_g, skip, bsz, seq):
    t, width = q.shape
    pc = mx_col // width
    heads = M_HEADS
    hd = width // heads
    ln = M_CHUNK
    nc = seq // ln
    rows = lambda b, c: (b * nc + c, 0)
    vec = pl.BlockSpec((1, width), lambda b, c: (0, 0))
    return pl.pallas_call(
        _mlstm_kernel,
        grid=(bsz, nc),
        in_specs=[
            pl.BlockSpec((2 * heads, 1), lambda b, c: (0, 0)),
            pl.BlockSpec((ln, width), rows),
            pl.BlockSpec((ln, width), rows),
            pl.BlockSpec((ln, width), lambda b, c: (b * nc + c, pc + 1)),
            pl.BlockSpec((ln, width), lambda b, c: (b * nc + c, pc + 2)),
            pl.BlockSpec((ln, width), lambda b, c: (b * nc + c, pc + 3)),
            pl.BlockSpec((ln, width), rows),
            pl.BlockSpec((ln, LANES), rows),
            vec, vec,
        ],
        out_specs=pl.BlockSpec((ln, width), rows),
        out_shape=jax.ShapeDtypeStruct((t, width), BF16),
        scratch_shapes=[pltpu.VMEM((heads, hd, hd), F32), pltpu.VMEM((heads, 1, hd), F32),
                        pltpu.VMEM((heads, 1, 1), F32)],
        compiler_params=_params("parallel", "arbitrary"),
        name="mlstm",
    )(bias, q, k, p, p, p, xc, gates, norm_g, skip)


def _compress_kernel(*refs):
    x_refs = refs[:A_CMP_STRIDE]
    pe_ref, w1_ref, w2_ref, o_ref = refs[A_CMP_STRIDE:]
    ng, n_sub, dh = o_ref.shape
    half = A_CMP_STRIDE * dh
    xs = [x_ref[...].astype(F32) for x_ref in x_refs]
    pe = pe_ref[...]
    rid = lax.broadcasted_iota(jnp.int32, (n_sub, dh), 0)
    for g in range(ng):
        t = jnp.concatenate([x[:, g * dh:(g + 1) * dh] for x in xs], axis=1)
        za = _dot((t + pe[0:1, :]).astype(BF16), w1_ref[0:half, :])
        zb = _dot((t + pe[1:2, :]).astype(BF16), w1_ref[half:2 * half, :])
        pre = za + pltpu.roll(zb, n_sub - 1, 0)
        hid = (pre * _sigmoid(pre)).astype(BF16)
        out = _dot(hid, w2_ref[...])
        o_ref[g] = jnp.where(rid < n_sub - 1, out, 0.0).astype(o_ref.dtype)


def _compress(p, kc_col, pe, w1, w2, bsz, seq):
    t, n = p.shape
    g = A_KV_HEADS
    dh = A_HEAD_DIM
    kvw = g * dh
    n_sub = seq // A_CMP_STRIDE
    half = A_CMP_STRIDE * dh
    hid = w1.shape[-1]
    p_phase = p.reshape(t // A_CMP_STRIDE, A_CMP_STRIDE * n)
    phase = lambda r: pl.BlockSpec((n_sub, kvw), lambda i, b: (b, (r * n + kc_col) // kvw + i))
    return pl.pallas_call(
        _compress_kernel,
        grid=(2, bsz),
        in_specs=[phase(r) for r in range(A_CMP_STRIDE)] + [
            pl.BlockSpec((None, 2, half), lambda i, b: (i, 0, 0)),
            pl.BlockSpec((None, 2 * half, hid), lambda i, b: (i, 0, 0)),
            pl.BlockSpec((None, hid, dh), lambda i, b: (i, 0, 0)),
        ],
        out_specs=pl.BlockSpec((None, None, g, n_sub, dh), lambda i, b: (i, b, 0, 0, 0)),
        out_shape=jax.ShapeDtypeStruct((2, bsz, g, n_sub, dh), BF16),
        compiler_params=_params("parallel", "parallel"),
        name="compress",
    )(*([p_phase] * A_CMP_STRIDE), pe, w1, w2)


def _mask_heads(x, ok, fill, tq):
    n = x.shape[1] // tq
    return jnp.concatenate([jnp.where(ok, x[:, h * tq:(h + 1) * tq], fill) for h in range(n)], axis=1)


def _with_ones_rows(vt):
    return jnp.concatenate([vt, jnp.ones((BF16_SUBLANES, vt.shape[1]), vt.dtype)], axis=0)


def _nsa_kernel(q_ref, kc_ref, vct_ref, ks_ref, vs_ref, kw_ref, vw_ref, ps_ref, z_ref, o_ref,
                sel_ref, vst_ref, vwt_ref):
    qi = pl.program_id(1)
    tq = q_ref.shape[0]
    dh = A_HEAD_DIM
    hpg = A_GROUP
    ng = A_KV_HEADS
    r = hpg * tq
    n_cmp = kc_ref.shape[1]
    kt = vst_ref.shape[2]
    tk = vwt_ref.shape[2]
    seq = ks_ref.shape[0]
    n_sel = seq // A_SEL_BLOCK
    t0 = qi * tq

    @pl.when(qi == 0)
    def _():
        for j in range(seq // kt):
            vst_ref[j] = vs_ref[j * kt:(j + 1) * kt, :].astype(F32).T.astype(BF16)
        for j in range(seq // tk):
            vwt_ref[j] = vw_ref[j * tk:(j + 1) * tk, :].astype(F32).T.astype(BF16)

    qt = (q_ref[...].astype(F32) * (dh ** -0.5 * LOG2_E)).T.astype(BF16)
    q4t = [jnp.concatenate([qt[(g * hpg + h) * dh:(g * hpg + h + 1) * dh, :] for h in range(hpg)], axis=1)
           for g in range(ng)]
    zpad = jnp.zeros((dh, r), BF16)
    q_pad = [jnp.concatenate([q4t[g] if gg == g else zpad for gg in range(ng)], axis=0) for g in range(ng)]

    cn = lax.broadcasted_iota(jnp.int32, (n_cmp, tq), 0)
    ct = t0 + lax.broadcasted_iota(jnp.int32, (n_cmp, tq), 1)
    valid = (cn * A_CMP_STRIDE + (A_CMP_BLOCK - 1)) <= ct
    jn = lax.broadcasted_iota(jnp.int32, (n_sel, n_cmp), 0)
    nn = lax.broadcasted_iota(jnp.int32, (n_sel, n_cmp), 1)
    ov = ((nn * A_CMP_STRIDE < (jn + 1) * A_SEL_BLOCK)
          & (nn * A_CMP_STRIDE + (A_CMP_BLOCK - 1) >= jn * A_SEL_BLOCK))
    ov_t = jnp.where(ov, 1.0, 0.0).astype(BF16)
    jr = lax.broadcasted_iota(jnp.int32, (n_sel, tq), 0)
    tl = t0 + lax.broadcasted_iota(jnp.int32, (n_sel, tq), 1)
    cur = lax.shift_right_logical(tl, 6)
    forced = (jr == 0) | (jr == cur) | (jr == cur - 1)
    o_c = []
    for g in range(ng):
        s_m = _mask_heads(_dot(kc_ref[g], q4t[g]), valid, NEG_INF, tq)
        mx = jnp.max(s_m, axis=0, keepdims=True)
        e = _mask_heads(jnp.exp2(s_m - mx), valid, 0.0, tq)
        lsum = jnp.sum(e, axis=0, keepdims=True)
        p_c = e * (1.0 / jnp.where(lsum > 0.0, lsum, 1.0))
        o_c.append(_dot(vct_ref[g], p_c.astype(BF16)))
        p_sum = p_c[:, 0:tq]
        for h in range(1, hpg):
            p_sum = p_sum + p_c[:, h * tq:(h + 1) * tq]
        p_hi = p_sum.astype(BF16)
        r1 = p_sum - p_hi.astype(F32)
        p_mid = r1.astype(BF16)
        p_lo = (r1 - p_mid.astype(F32)).astype(BF16)
        imp_t = _dot(ov_t, p_hi) + _dot(ov_t, p_mid) + _dot(ov_t, p_lo)
        score = jnp.where(forced, POS_INF, jnp.where(jr <= cur, imp_t, NEG_INF))
        rank = jnp.zeros((n_sel, tq), F32)
        for jp in range(n_sel):
            sj = score[jp:jp + 1, :]
            beats = (sj > score) | ((sj == score) & (jr > jp))
            rank = rank + jnp.where(beats, 1.0, 0.0)
        sel_ref[g] = jnp.where(rank < float(min(A_SEL_TOPK, n_sel)), 1.0, 0.0)

    sub_s = lax.broadcasted_iota(jnp.int32, (kt, tq), 0)
    t_abs_s = t0 + lax.broadcasted_iota(jnp.int32, (kt, tq), 1)
    blocks_per_iter = kt // A_SEL_BLOCK
    da = dh + BF16_SUBLANES

    def sel_body(j, carry):
        off = pl.multiple_of(j * kt, kt)
        k_all = ks_ref[pl.ds(off, kt), :]
        causal = (off + sub_s) <= t_abs_s
        out = []
        for g in range(ng):
            m, acc = carry[g]
            s = _dot(k_all, q_pad[g])
            flags = jnp.concatenate(
                [jnp.broadcast_to(sel_ref[g, pl.ds(j * blocks_per_iter + i, 1), :], (A_SEL_BLOCK, tq))
                 for i in range(blocks_per_iter)], axis=0)
            s = _mask_heads(s, (flags > 0.5) & causal, NEG_INF, tq)
            m_new = jnp.maximum(m, jnp.max(s, axis=0, keepdims=True))
            p = jnp.exp2(s - m_new).astype(BF16)
            vt = _with_ones_rows(vst_ref[j, g * dh:(g + 1) * dh, :])
            out.append((m_new, jnp.exp2(m - m_new) * acc + _dot(vt, p)))
        return tuple(out)

    init = tuple((jnp.full((1, r), NEG_INF, F32), jnp.zeros((da, r), F32)) for _ in range(ng))
    sel_state = lax.fori_loop(0, (t0 + tq - 1) // kt + 1, sel_body, init)

    n_wb = -(-A_WINDOW // tk)
    wk = (n_wb + 1) * tk
    j0 = jnp.maximum(qi - n_wb, 0)
    off_w = pl.multiple_of(j0 * tk, tk)
    c_abs = off_w + lax.broadcasted_iota(jnp.int32, (wk, tq), 0)
    t_abs_w = t0 + lax.broadcasted_iota(jnp.int32, (wk, tq), 1)
    ok_w = (c_abs <= t_abs_w) & (c_abs > t_abs_w - A_WINDOW)
    kw_all = kw_ref[pl.ds(off_w, wk), :]

    gs = _sigmoid(ps_ref[...].T)
    g0 = 2 * M_HEADS
    outs = []
    for g in range(ng):
        s_w = _mask_heads(_dot(kw_all, q_pad[g]), ok_w, NEG_INF, tq)
        p_w = jnp.exp2(s_w - jnp.max(s_w, axis=0, keepdims=True)).astype(BF16)
        acc_w = _dot(_with_ones_rows(vwt_ref[j0, g * dh:(g + 1) * dh, :]), p_w[0:tk, :])
        for i in range(1, n_wb + 1):
            acc_w = acc_w + _dot(_with_ones_rows(vwt_ref[j0 + i, g * dh:(g + 1) * dh, :]), p_w[i * tk:(i + 1) * tk, :])
        o_w = acc_w[0:dh, :] * (1.0 / acc_w[dh:dh + 1, :])
        acc_s = sel_state[g][1]
        o_s = acc_s[0:dh, :] * (1.0 / acc_s[dh:dh + 1, :])
        for h in range(hpg):
            sl = slice(h * tq, (h + 1) * tq)
            gr = g0 + 3 * (g * hpg + h)
            outs.append(gs[gr:gr + 1, :] * o_c[g][:, sl] + gs[gr + 1:gr + 2, :] * o_s[:, sl]
                        + gs[gr + 2:gr + 3, :] * o_w[:, sl])
    o_all = jnp.concatenate(outs, axis=0).T
    z = z_ref[...].astype(F32)
    o_ref[...] = (o_all * (z * _sigmoid(z))).astype(o_ref.dtype)


def _nsa(p, ps, kvc, vct, q_col, kv_col, z_col, bsz, seq):
    t = p.shape[0]
    g = A_KV_HEADS
    dh = A_HEAD_DIM
    aw = A_HEADS * dh
    kvw = g * dh
    tq = Q_BLOCK
    nq = seq // tq
    n_cmp = kvc.shape[3]
    kt = SEL_KEYS_PER_STEP
    tk = LANES
    rows = lambda b, qi: b * nq + qi
    kv_piece = lambda i: pl.BlockSpec((seq, kvw), lambda b, qi: (b, kv_col // kvw + i))
    return pl.pallas_call(
        _nsa_kernel,
        grid=(bsz, nq),
        in_specs=[
            pl.BlockSpec((tq, aw), lambda b, qi: (rows(b, qi), q_col // aw)),
            pl.BlockSpec((None, None, g, n_cmp, dh), lambda b, qi: (0, b, 0, 0, 0)),
            pl.BlockSpec((None, g, dh, n_cmp), lambda b, qi: (b, 0, 0, 0)),
            kv_piece(2), kv_piece(3), kv_piece(4), kv_piece(5),
            pl.BlockSpec((tq, LANES), lambda b, qi: (rows(b, qi), 0)),
            pl.BlockSpec((tq, aw), lambda b, qi: (rows(b, qi), z_col // aw)),
        ],
        out_specs=pl.BlockSpec((tq, aw), lambda b, qi: (rows(b, qi), 0)),
        out_shape=jax.ShapeDtypeStruct((t, aw), BF16),
        scratch_shapes=[pltpu.VMEM((g, seq // A_SEL_BLOCK, tq), F32),
                        pltpu.VMEM((seq // kt, kvw, kt), BF16),
                        pltpu.VMEM((seq // tk, kvw, tk), BF16)],
        compiler_params=_params("parallel", "arbitrary"),
        name="nsa",
    )(p, kvc, vct, p, p, p, p, ps, p)


def _merge_out_kernel(ym_ref, ya_ref, wm_ref, wa_ref, gm_ref, ga_ref, wo_ref, g_ref, x_ref, o_ref):
    um = _dot(ym_ref[...], wm_ref[...])
    ua = _dot(ya_ref[...], wa_ref[...])
    mg = _sigmoid(gm_ref[...].astype(F32)) * um + _sigmoid(ga_ref[...].astype(F32)) * ua
    out = _dot(mg.astype(BF16), wo_ref[...])
    ms = jnp.mean(out * out, axis=-1, keepdims=True)
    o_ref[...] = x_ref[...] + out * lax.rsqrt(ms + RMS_EPS) * g_ref[...]


def _merge_out(ym, ya, wm, wa, p, gm_col, ga_col, wo, g, x2d, tm=256):
    t, d = x2d.shape
    kdim = ym.shape[1]
    resident = lambda shape: pl.BlockSpec(shape, lambda i: (0, 0), pipeline_mode=pl.Buffered(1))
    return pl.pallas_call(
        _merge_out_kernel,
        grid=(t // tm,),
        in_specs=[
            pl.BlockSpec((tm, kdim), lambda i: (i, 0)),
            pl.BlockSpec((tm, kdim), lambda i: (i, 0)),
            resident((kdim, d)),
            resident((kdim, d)),
            pl.BlockSpec((tm, d), lambda i: (i, gm_col // d)),
            pl.BlockSpec((tm, d), lambda i: (i, ga_col // d)),
            resident((d, d)),
            pl.BlockSpec((1, d), lambda i: (0, 0)),
            pl.BlockSpec((tm, d), lambda i: (i, 0)),
        ],
        out_specs=pl.BlockSpec((tm, d), lambda i: (i, 0)),
        out_shape=jax.ShapeDtypeStruct((t, d), F32),
        compiler_params=_params("parallel"),
        name="merge_out",
    )(ym, ya, wm, wa, p, p, wo, g, x2d)


def _layer(x, pre_g, w_in, conv_w, conv_b, w_q, w_k, b_i, b_f, norm_g, skip,
           pe_k, w1_k, w2_k, pe_v, w1_v, w2_v, w_up_m, w_up_a, w_out, post_g):
    bsz, seq, d = x.shape
    t = bsz * seq
    mw = w_up_m.shape[0]
    aw = w_up_a.shape[0]
    g = A_KV_HEADS
    dh = A_HEAD_DIM
    kvw = g * dh
    n_gate = 3 * A_HEADS

    o_i = 4 * mw
    o_f = o_i + M_HEADS
    o_q = o_f + M_HEADS
    o_g = o_q + aw + 6 * kvw
    o_z = o_g + n_gate
    o_m = o_z + aw
    o_kv = o_q + aw
    w_main = jnp.concatenate(
        [w_in[:, o_m:], w_in[:, :o_i], w_in[:, o_q:o_kv], w_in[:, o_z:o_m], w_in[:, o_kv:o_g]], axis=1).astype(BF16)
    n_small = 2 * M_HEADS + n_gate
    w_small = jnp.concatenate(
        [w_in[:, o_i:o_q], w_in[:, o_g:o_z], jnp.zeros((d, LANES - n_small), w_in.dtype)], axis=1).astype(BF16)
    gm_col = 0
    ga_col = d
    mx_col = 2 * d
    q_col = mx_col + 4 * mw
    z_col = q_col + aw
    kv_col = z_col + aw

    x2d = x.reshape(t, d)
    p, ps = _inproj(x2d, pre_g.reshape(1, d), w_main, w_small)

    xc, q, k = _conv_qk(p, mx_col, conv_w, conv_b.reshape(1, mw), w_q.astype(BF16), w_k.astype(BF16), bsz, seq)
    nc = seq // M_CHUNK
    bias_m = jnp.concatenate([b_i, b_f]).reshape(2 * M_HEADS, 1)
    y_m = _mlstm(p, mx_col, xc, q, k, ps, bias_m, norm_g.reshape(1, mw), skip.reshape(1, mw), bsz, seq)

    half = A_CMP_STRIDE * dh
    pe = jnp.stack([pe_k.reshape(2, half), pe_v.reshape(2, half)])
    w1 = jnp.stack([w1_k, w1_v]).astype(BF16)
    w2 = jnp.stack([w2_k, w2_v]).astype(BF16)
    kvc = _compress(p, kv_col, pe, w1, w2, bsz, seq)
    vct = jnp.swapaxes(kvc[1], 2, 3)
    y_a = _nsa(p, ps, kvc, vct, q_col, kv_col, z_col, bsz, seq)

    out = _merge_out(y_m, y_a, w_up_m.astype(BF16), w_up_a.astype(BF16), p, gm_col, ga_col,
                     w_out.astype(BF16), post_g.reshape(1, d), x2d)
    return out.reshape(bsz, seq, d)


def kernel(x, pre_norm_g, w_in, m_conv_w, m_conv_b, m_w_q, m_w_k, m_b_i, m_b_f, m_norm_g, m_skip, a_pe_k, a_w1_k, a_w2_k, a_pe_v, a_w1_v, a_w2_v, w_up_m, w_up_a, w_out, post_norm_g):
    depth = w_in.shape[0]
    for l in range(depth):
        x = _layer(x, pre_norm_g[l], w_in[l], m_conv_w[l], m_conv_b[l], m_w_q[l], m_w_k[l], m_b_i[l], m_b_f[l],
                   m_norm_g[l], m_skip[l], a_pe_k[l], a_w1_k[l], a_w2_k[l], a_pe_v[l], a_w1_v[l], a_w2_v[l],
                   w_up_m[l], w_up_a[l], w_out[l], post_norm_g[l])
    return x
```

```python
import functools

import jax
import jax.numpy as jnp
from jax import lax
from jax.experimental import pallas as pl
from jax.experimental.pallas import tpu as pltpu

F32 = jnp.float32
BF16 = jnp.bfloat16

M_HEADS = 4
M_CONV = 4
M_CHUNK = 128
A_HEADS = 16
A_KV_HEADS = 4
A_GROUP = A_HEADS // A_KV_HEADS
A_HEAD_DIM = 64
A_CMP_BLOCK = 32
A_CMP_STRIDE = 16
A_SEL_BLOCK = 64
A_SEL_TOPK = 8
A_WINDOW = 512
Q_BLOCK = 128
SEL_KEYS_PER_STEP = 512
RMS_EPS = 1e-6
LN_EPS = 1e-6
NEG_INF = -1e30
POS_INF = 1e30

LANES = 128
BF16_SUBLANES = 16
LOG2_E = 1.4426950408889634
VMEM_LIMIT_BYTES = 48 * 1024 * 1024


def _nt_dot(a, b):
    return lax.dot_general(a, b, (((1,), (1,)), ((), ())), preferred_element_type=F32)


def _dot(a, b):
    return jnp.dot(a, b, preferred_element_type=F32)


def _sigmoid(x):
    return 1.0 / (1.0 + jnp.exp(-x))


def _params(*sem):
    return pltpu.CompilerParams(dimension_semantics=sem, vmem_limit_bytes=VMEM_LIMIT_BYTES)


def _inproj_kernel(x_ref, g_ref, w_ref, wst_ref, o_ref, ost_ref, h_scr):
    @pl.when(pl.program_id(1) == 0)
    def _():
        xf = x_ref[...]
        ms = jnp.mean(xf * xf, axis=-1, keepdims=True)
        hb = (xf * lax.rsqrt(ms + RMS_EPS) * g_ref[...]).astype(BF16)
        h_scr[...] = hb
        ost_ref[...] = _nt_dot(wst_ref[...], hb)

    o_ref[...] = _dot(h_scr[...], w_ref[...]).astype(o_ref.dtype)


def _inproj(x2d, g, w_main, w_small_t, tm=1024, tn=512):
    t, d = x2d.shape
    n = w_main.shape[1]
    ns = w_small_t.shape[0]
    return pl.pallas_call(
        _inproj_kernel,
        grid=(t // tm, n // tn),
        in_specs=[
            pl.BlockSpec((tm, d), lambda i, j: (i, 0)),
            pl.BlockSpec((1, d), lambda i, j: (0, 0)),
            pl.BlockSpec((d, tn), lambda i, j: (0, j)),
            pl.BlockSpec((ns, d), lambda i, j: (0, 0)),
        ],
        out_specs=[
            pl.BlockSpec((tm, tn), lambda i, j: (i, j)),
            pl.BlockSpec((ns, tm), lambda i, j: (0, i)),
        ],
        out_shape=[jax.ShapeDtypeStruct((t, n), BF16), jax.ShapeDtypeStruct((ns, t), F32)],
        scratch_shapes=[pltpu.VMEM((tm, d), BF16)],
        compiler_params=_params("parallel", "arbitrary"),
        name="inproj",
    )(x2d, g, w_main, w_small_t)


def _conv_qk_kernel(x_ref, cw_ref, cb_ref, wq_ref, wk_ref, xc_ref, q_ref, k_ref, pad_ref, *, rows):
    s, c = x_ref.shape
    front = 8
    pad_ref[0:front, :] = jnp.zeros((front, c), F32)
    pad_ref[front:front + s, :] = x_ref[...].astype(F32)
    cw = cw_ref[...]
    cb = cb_ref[...]
    wq = wq_ref[...]
    wk = wk_ref[...]
    k_scale = c ** -0.5
    for r in range(s // rows):
        acc = jnp.broadcast_to(cb, (rows, c))
        for kk in range(M_CONV):
            start = front + r * rows + kk - (M_CONV - 1)
            acc = acc + cw[kk:kk + 1, :] * pad_ref[start:start + rows, :]
        xcb = (acc * _sigmoid(acc)).astype(BF16)
        xc_ref[r * rows:(r + 1) * rows, :] = xcb
        q_ref[r * rows:(r + 1) * rows, :] = _dot(xcb, wq).astype(BF16)
        k_ref[r * rows:(r + 1) * rows, :] = (_dot(xcb, wk) * k_scale).astype(BF16)


def _conv_qk(p, mx_col, conv_w, conv_b, wq, wk, bsz, seq):
    t = p.shape[0]
    heads, hd = wq.shape[0], wq.shape[1]
    width = heads * hd
    blk = pl.BlockSpec((seq, hd), lambda b, h: (b, h))
    out = jax.ShapeDtypeStruct((t, width), BF16)
    return pl.pallas_call(
        functools.partial(_conv_qk_kernel, rows=256),
        grid=(bsz, heads),
        in_specs=[
            pl.BlockSpec((seq, hd), lambda b, h: (b, mx_col // hd + h)),
            pl.BlockSpec((M_CONV, hd), lambda b, h: (0, h)),
            pl.BlockSpec((1, hd), lambda b, h: (0, h)),
            pl.BlockSpec((None, hd, hd), lambda b, h: (h, 0, 0)),
            pl.BlockSpec((None, hd, hd), lambda b, h: (h, 0, 0)),
        ],
        out_specs=[blk, blk, blk],
        out_shape=[out, out, out],
        scratch_shapes=[pltpu.VMEM((seq + 8, hd), F32)],
        compiler_params=_params("parallel", "parallel"),
        name="conv_qk",
    )(p, conv_w, conv_b, wq, wk)


def _log_sigmoid(x):
    return jnp.minimum(x, 0.0) - jnp.log1p(jnp.exp(-jnp.abs(x)))


def _mlstm_kernel(bias_ref, q_ref, k_ref, v_ref, o_ref, z_ref, xc_ref, g_ref, ng_ref, sk_ref,
                  y_ref, c_scr, n_scr, m_scr):
    @pl.when(pl.program_id(1) == 0)
    def _():
        c_scr[...] = jnp.zeros_like(c_scr)
        n_scr[...] = jnp.zeros_like(n_scr)
        m_scr[...] = jnp.zeros_like(m_scr)

    ln = q_ref.shape[0]
    heads, hd = c_scr.shape[0], c_scr.shape[1]
    gates = g_ref[...] + bias_ref[...]
    fb_all = _log_sigmoid(gates[heads:2 * heads, :])
    row = lax.broadcasted_iota(jnp.int32, (ln, ln), 0)
    col = lax.broadcasted_iota(jnp.int32, (ln, ln), 1)
    tril = col <= row
    eye = col == row
    for hh in range(heads):
        sl = slice(hh * hd, (hh + 1) * hd)
        ib = gates[hh:hh + 1, :]
        fb = fb_all[hh:hh + 1, :]
        bcum_c = jnp.sum(jnp.where(tril, fb, 0.0), axis=1, keepdims=True)
        bcum_r = jnp.sum(jnp.where(eye, bcum_c, 0.0), axis=0, keepdims=True)
        g_tot = jnp.sum(fb, axis=1, keepdims=True)
        e_r = ib - bcum_r
        m_st = m_scr[hh]
        d_intra = jnp.where(tril, bcum_c + e_r, NEG_INF)
        d_inter = bcum_c + m_st
        m_t = jnp.maximum(d_inter, jnp.max(d_intra, axis=1, keepdims=True))
        w_intra = jnp.exp(d_intra - m_t)
        w_inter = jnp.exp(d_inter - m_t)

        qb = q_ref[:, sl]
        kb = k_ref[:, sl]
        vb = v_ref[:, sl]
        c_st = c_scr[hh]
        n_st = n_scr[hh]
        sc = _nt_dot(qb, kb) * w_intra
        num = _dot(sc.astype(BF16), vb) + w_inter * _dot(qb, c_st.astype(BF16))
        qn = jnp.sum(qb.astype(F32) * n_st, axis=1, keepdims=True)
        den = jnp.sum(sc, axis=1, keepdims=True) + w_inter * qn
        h = num / jnp.maximum(jnp.abs(den), jnp.exp(-m_t))

        d_state = g_tot + e_r
        m_new = jnp.maximum(g_tot + m_st, jnp.max(d_state, axis=1, keepdims=True))
        w_s_r = jnp.exp(d_state - m_new)
        w_s_c = jnp.sum(jnp.where(eye, w_s_r, 0.0), axis=1, keepdims=True)
        decay = jnp.exp(g_tot + m_st - m_new)
        kw = kb.astype(F32) * w_s_c
        kv = lax.dot_general(kw.astype(BF16), vb, (((0,), (0,)), ((), ())), preferred_element_type=F32)
        c_scr[hh] = decay * c_st + kv
        n_scr[hh] = decay * n_st + jnp.sum(kw, axis=0, keepdims=True)
        m_scr[hh] = m_new

        hg = h * _sigmoid(o_ref[:, sl].astype(F32))
        mu = jnp.mean(hg, axis=-1, keepdims=True)
        hc = hg - mu
        var = jnp.mean(hc * hc, axis=-1, keepdims=True)
        hn = hc * lax.rsqrt(var + LN_EPS)
        z = z_ref[:, sl].astype(F32)
        y = (hn * ng_ref[:, sl] + sk_ref[:, sl] * xc_ref[:, sl].astype(F32)) * (z * _sigmoid(z))
        y_ref[:, sl] = y.astype(y_ref.dtype)


def _mlstm(p, mx_col, xc, q, k, gates, bias, norm_g, skip, bsz, seq):
    t, width = q.shape
    pc = mx_col // width
    heads = M_HEADS
    hd = width // heads
    ln = M_CHUNK
    nc = seq // ln
    rows = lambda b, c: (b * nc + c, 0)
    vec = pl.BlockSpec((1, width), lambda b, c: (0, 0))
    return pl.pallas_call(
        _mlstm_kernel,
        grid=(bsz, nc),
        in_specs=[
            pl.BlockSpec((2 * heads, 1), lambda b, c: (0, 0)),
            pl.BlockSpec((ln, width), rows),
            pl.BlockSpec((ln, width), rows),
            pl.BlockSpec((ln, width), lambda b, c: (b * nc + c, pc + 1)),
            pl.BlockSpec((ln, width), lambda b, c: (b * nc + c, pc + 2)),
            pl.BlockSpec((ln, width), lambda b, c: (b * nc + c, pc + 3)),
            pl.BlockSpec((ln, width), rows),
            pl.BlockSpec((2 * heads, ln), lambda b, c: (0, b * nc + c)),
            vec, vec,
        ],
        out_specs=pl.BlockSpec((ln, width), rows),
        out_shape=jax.ShapeDtypeStruct((t, width), BF16),
        scratch_shapes=[pltpu.VMEM((heads, hd, hd), F32), pltpu.VMEM((heads, 1, hd), F32),
                        pltpu.VMEM((heads, 1, 1), F32)],
        compiler_params=_params("parallel", "arbitrary"),
        name="mlstm",
    )(bias, q, k, p, p, p, xc, gates, norm_g, skip)


def _compress_kernel(x_ref, pe_ref, w1_ref, w2_ref, o_ref, xf_scr):
    ng, n_sub, dh = o_ref.shape
    half = A_CMP_STRIDE * dh
    n_chunk = xf_scr.shape[0]
    for c in range(n_chunk):
        xf_scr[c] = x_ref[:, c * LANES:(c + 1) * LANES].astype(F32)
    xs = [jnp.concatenate([xf_scr[c, pl.ds(r, n_sub, stride=A_CMP_STRIDE), :] for c in range(n_chunk)], axis=1)
          for r in range(A_CMP_STRIDE)]
    pe = pe_ref[...]
    rid = lax.broadcasted_iota(jnp.int32, (n_sub, dh), 0)
    for g in range(ng):
        t = jnp.concatenate([x[:, g * dh:(g + 1) * dh] for x in xs], axis=1)
        za = _dot((t + pe[0:1, :]).astype(BF16), w1_ref[0:half, :])
        zb = _dot((t + pe[1:2, :]).astype(BF16), w1_ref[half:2 * half, :])
        pre = za + pltpu.roll(zb, n_sub - 1, 0)
        hid = (pre * _sigmoid(pre)).astype(BF16)
        out = _dot(hid, w2_ref[...])
        o_ref[g] = jnp.where(rid < n_sub - 1, out, 0.0).astype(o_ref.dtype)


def _compress(p, kc_col, pe, w1, w2, bsz, seq):
    g = A_KV_HEADS
    dh = A_HEAD_DIM
    kvw = g * dh
    n_sub = seq // A_CMP_STRIDE
    half = A_CMP_STRIDE * dh
    hid = w1.shape[-1]
    return pl.pallas_call(
        _compress_kernel,
        grid=(2, bsz),
        in_specs=[
            pl.BlockSpec((seq, kvw), lambda i, b: (b, kc_col // kvw + i)),
            pl.BlockSpec((None, 2, half), lambda i, b: (i, 0, 0)),
            pl.BlockSpec((None, 2 * half, hid), lambda i, b: (i, 0, 0)),
            pl.BlockSpec((None, hid, dh), lambda i, b: (i, 0, 0)),
        ],
        out_specs=pl.BlockSpec((None, None, g, n_sub, dh), lambda i, b: (i, b, 0, 0, 0)),
        out_shape=jax.ShapeDtypeStruct((2, bsz, g, n_sub, dh), BF16),
        scratch_shapes=[pltpu.VMEM((kvw // LANES, seq, LANES), F32)],
        compiler_params=_params("parallel", "parallel"),
        name="compress",
    )(p, pe, w1, w2)


def _mask_heads(x, ok, fill, tq):
    n = x.shape[1] // tq
    return jnp.concatenate([jnp.where(ok, x[:, h * tq:(h + 1) * tq], fill) for h in range(n)], axis=1)


def _with_ones_rows(vt):
    return jnp.concatenate([vt, jnp.ones((BF16_SUBLANES, vt.shape[1]), vt.dtype)], axis=0)


def _nsa_kernel(q_ref, kc_ref, vct_ref, ks_ref, vs_ref, kw_ref, vw_ref, ps_ref, z_ref, o_ref,
                sel_ref, vst_ref, vwt_ref):
    qi = pl.program_id(1)
    tq = q_ref.shape[0]
    dh = A_HEAD_DIM
    hpg = A_GROUP
    ng = A_KV_HEADS
    r = hpg * tq
    n_cmp = kc_ref.shape[1]
    kt = vst_ref.shape[2]
    tk = vwt_ref.shape[2]
    seq = ks_ref.shape[0]
    n_sel = seq // A_SEL_BLOCK
    t0 = qi * tq

    @pl.when(qi == 0)
    def _():
        for j in range(seq // kt):
            vst_ref[j] = vs_ref[j * kt:(j + 1) * kt, :].astype(F32).T.astype(BF16)
        for j in range(seq // tk):
            vwt_ref[j] = vw_ref[j * tk:(j + 1) * tk, :].astype(F32).T.astype(BF16)

    qt = (q_ref[...].astype(F32) * (dh ** -0.5 * LOG2_E)).T.astype(BF16)
    q4t = [jnp.concatenate([qt[(g * hpg + h) * dh:(g * hpg + h + 1) * dh, :] for h in range(hpg)], axis=1)
           for g in range(ng)]
    zpad = jnp.zeros((dh, r), BF16)
    q_pad = [jnp.concatenate([q4t[g] if gg == g else zpad for gg in range(ng)], axis=0) for g in range(ng)]

    cn = lax.broadcasted_iota(jnp.int32, (n_cmp, tq), 0)
    ct = t0 + lax.broadcasted_iota(jnp.int32, (n_cmp, tq), 1)
    valid = (cn * A_CMP_STRIDE + (A_CMP_BLOCK - 1)) <= ct
    jn = lax.broadcasted_iota(jnp.int32, (n_sel, n_cmp), 0)
    nn = lax.broadcasted_iota(jnp.int32, (n_sel, n_cmp), 1)
    ov = ((nn * A_CMP_STRIDE < (jn + 1) * A_SEL_BLOCK)
          & (nn * A_CMP_STRIDE + (A_CMP_BLOCK - 1) >= jn * A_SEL_BLOCK))
    ov_t = jnp.where(ov, 1.0, 0.0).astype(BF16)
    jr = lax.broadcasted_iota(jnp.int32, (n_sel, tq), 0)
    tl = t0 + lax.broadcasted_iota(jnp.int32, (n_sel, tq), 1)
    cur = lax.shift_right_logical(tl, 6)
    forced = (jr == 0) | (jr == cur) | (jr == cur - 1)
    o_c = []
    for g in range(ng):
        s_m = _mask_heads(_dot(kc_ref[g], q4t[g]), valid, NEG_INF, tq)
        mx = jnp.max(s_m, axis=0, keepdims=True)
        e = _mask_heads(jnp.exp2(s_m - mx), valid, 0.0, tq)
        lsum = jnp.sum(e, axis=0, keepdims=True)
        p_c = e * (1.0 / jnp.where(lsum > 0.0, lsum, 1.0))
        o_c.append(_dot(vct_ref[g], p_c.astype(BF16)))
        p_sum = p_c[:, 0:tq]
        for h in range(1, hpg):
            p_sum = p_sum + p_c[:, h * tq:(h + 1) * tq]
        p_hi = p_sum.astype(BF16)
        r1 = p_sum - p_hi.astype(F32)
        p_mid = r1.astype(BF16)
        p_lo = (r1 - p_mid.astype(F32)).astype(BF16)
        imp_t = _dot(ov_t, p_hi) + _dot(ov_t, p_mid) + _dot(ov_t, p_lo)
        score = jnp.where(forced, POS_INF, jnp.where(jr <= cur, imp_t, NEG_INF))
        rank = jnp.zeros((n_sel, tq), F32)
        for jp in range(n_sel):
            sj = score[jp:jp + 1, :]
            beats = (sj > score) | ((sj == score) & (jr > jp))
            rank = rank + jnp.where(beats, 1.0, 0.0)
        sel_ref[g] = jnp.where(rank < float(min(A_SEL_TOPK, n_sel)), 1.0, 0.0)

    sub_s = lax.broadcasted_iota(jnp.int32, (kt, tq), 0)
    t_abs_s = t0 + lax.broadcasted_iota(jnp.int32, (kt, tq), 1)
    blocks_per_iter = kt // A_SEL_BLOCK
    da = dh + BF16_SUBLANES

    def sel_body(j, carry):
        off = pl.multiple_of(j * kt, kt)
        k_all = ks_ref[pl.ds(off, kt), :]
        causal = (off + sub_s) <= t_abs_s
        out = []
        for g in range(ng):
            m, acc = carry[g]
            s = _dot(k_all, q_pad[g])
            flags = jnp.concatenate(
                [jnp.broadcast_to(sel_ref[g, pl.ds(j * blocks_per_iter + i, 1), :], (A_SEL_BLOCK, tq))
                 for i in range(blocks_per_iter)], axis=0)
            s = _mask_heads(s, (flags > 0.5) & causal, NEG_INF, tq)
            m_new = jnp.maximum(m, jnp.max(s, axis=0, keepdims=True))
            p = jnp.exp2(s - m_new).astype(BF16)
            vt = _with_ones_rows(vst_ref[j, g * dh:(g + 1) * dh, :])
            out.append((m_new, jnp.exp2(m - m_new) * acc + _dot(vt, p)))
        return tuple(out)

    init = tuple((jnp.full((1, r), NEG_INF, F32), jnp.zeros((da, r), F32)) for _ in range(ng))
    sel_state = lax.fori_loop(0, (t0 + tq - 1) // kt + 1, sel_body, init)

    n_wb = -(-A_WINDOW // tk)
    wk = (n_wb + 1) * tk
    j0 = jnp.maximum(qi - n_wb, 0)
    off_w = pl.multiple_of(j0 * tk, tk)
    c_abs = off_w + lax.broadcasted_iota(jnp.int32, (wk, tq), 0)
    t_abs_w = t0 + lax.broadcasted_iota(jnp.int32, (wk, tq), 1)
    ok_w = (c_abs <= t_abs_w) & (c_abs > t_abs_w - A_WINDOW)
    kw_all = kw_ref[pl.ds(off_w, wk), :]

    gs = _sigmoid(ps_ref[...])
    g0 = 2 * M_HEADS
    outs = []
    for g in range(ng):
        s_w = _mask_heads(_dot(kw_all, q_pad[g]), ok_w, NEG_INF, tq)
        p_w = jnp.exp2(s_w - jnp.max(s_w, axis=0, keepdims=True)).astype(BF16)
        acc_w = _dot(_with_ones_rows(vwt_ref[j0, g * dh:(g + 1) * dh, :]), p_w[0:tk, :])
        for i in range(1, n_wb + 1):
            acc_w = acc_w + _dot(_with_ones_rows(vwt_ref[j0 + i, g * dh:(g + 1) * dh, :]), p_w[i * tk:(i + 1) * tk, :])
        o_w = acc_w[0:dh, :] * (1.0 / acc_w[dh:dh + 1, :])
        acc_s = sel_state[g][1]
        o_s = acc_s[0:dh, :] * (1.0 / acc_s[dh:dh + 1, :])
        for h in range(hpg):
            sl = slice(h * tq, (h + 1) * tq)
            gr = g0 + 3 * (g * hpg + h)
            outs.append(gs[gr:gr + 1, :] * o_c[g][:, sl] + gs[gr + 1:gr + 2, :] * o_s[:, sl]
                        + gs[gr + 2:gr + 3, :] * o_w[:, sl])
    o_all = jnp.concatenate(outs, axis=0).T
    z = z_ref[...].astype(F32)
    o_ref[...] = (o_all * (z * _sigmoid(z))).astype(o_ref.dtype)


def _nsa(p, ps, kvc, vct, q_col, kv_col, z_col, bsz, seq):
    t = p.shape[0]
    g = A_KV_HEADS
    dh = A_HEAD_DIM
    aw = A_HEADS * dh
    kvw = g * dh
    tq = Q_BLOCK
    nq = seq // tq
    n_cmp = kvc.shape[3]
    kt = SEL_KEYS_PER_STEP
    tk = LANES
    rows = lambda b, qi: b * nq + qi
    kv_piece = lambda i: pl.BlockSpec((seq, kvw), lambda b, qi: (b, kv_col // kvw + i))
    return pl.pallas_call(
        _nsa_kernel,
        grid=(bsz, nq),
        in_specs=[
            pl.BlockSpec((tq, aw), lambda b, qi: (rows(b, qi), q_col // aw)),
            pl.BlockSpec((None, None, g, n_cmp, dh), lambda b, qi: (0, b, 0, 0, 0)),
            pl.BlockSpec((None, g, dh, n_cmp), lambda b, qi: (b, 0, 0, 0)),
            kv_piece(2), kv_piece(3), kv_piece(4), kv_piece(5),
            pl.BlockSpec((LANES, tq), lambda b, qi: (0, rows(b, qi))),
            pl.BlockSpec((tq, aw), lambda b, qi: (rows(b, qi), z_col // aw)),
        ],
        out_specs=pl.BlockSpec((tq, aw), lambda b, qi: (rows(b, qi), 0)),
        out_shape=jax.ShapeDtypeStruct((t, aw), BF16),
        scratch_shapes=[pltpu.VMEM((g, seq // A_SEL_BLOCK, tq), F32),
                        pltpu.VMEM((seq // kt, kvw, kt), BF16),
                        pltpu.VMEM((seq // tk, kvw, tk), BF16)],
        compiler_params=_params("parallel", "arbitrary"),
        name="nsa",
    )(p, kvc, vct, p, p, p, p, ps, p)


def _merge_out_kernel(ym_ref, ya_ref, wm_ref, wa_ref, gm_ref, ga_ref, wo_ref, g_ref, x_ref, o_ref):
    um = _dot(ym_ref[...], wm_ref[...])
    ua = _dot(ya_ref[...], wa_ref[...])
    mg = _sigmoid(gm_ref[...].astype(F32)) * um + _sigmoid(ga_ref[...].astype(F32)) * ua
    out = _dot(mg.astype(BF16), wo_ref[...])
    ms = jnp.mean(out * out, axis=-1, keepdims=True)
    o_ref[...] = x_ref[...] + out * lax.rsqrt(ms + RMS_EPS) * g_ref[...]


def _merge_out(ym, ya, wm, wa, p, gm_col, ga_col, wo, g, x2d, tm=256):
    t, d = x2d.shape
    kdim = ym.shape[1]
    resident = lambda shape: pl.BlockSpec(shape, lambda i: (0, 0), pipeline_mode=pl.Buffered(1))
    return pl.pallas_call(
        _merge_out_kernel,
        grid=(t // tm,),
        in_specs=[
            pl.BlockSpec((tm, kdim), lambda i: (i, 0)),
            pl.BlockSpec((tm, kdim), lambda i: (i, 0)),
            resident((kdim, d)),
            resident((kdim, d)),
            pl.BlockSpec((tm, d), lambda i: (i, gm_col // d)),
            pl.BlockSpec((tm, d), lambda i: (i, ga_col // d)),
            resident((d, d)),
            pl.BlockSpec((1, d), lambda i: (0, 0)),
            pl.BlockSpec((tm, d), lambda i: (i, 0)),
        ],
        out_specs=pl.BlockSpec((tm, d), lambda i: (i, 0)),
        out_shape=jax.ShapeDtypeStruct((t, d), F32),
        compiler_params=_params("parallel"),
        name="merge_out",
    )(ym, ya, wm, wa, p, p, wo, g, x2d)


def _layer(x, pre_g, w_in, conv_w, conv_b, w_q, w_k, b_i, b_f, norm_g, skip,
           pe_k, w1_k, w2_k, pe_v, w1_v, w2_v, w_up_m, w_up_a, w_out, post_g):
    bsz, seq, d = x.shape
    t = bsz * seq
    mw = w_up_m.shape[0]
    aw = w_up_a.shape[0]
    g = A_KV_HEADS
    dh = A_HEAD_DIM
    kvw = g * dh
    n_gate = 3 * A_HEADS

    o_i = 4 * mw
    o_f = o_i + M_HEADS
    o_q = o_f + M_HEADS
    o_g = o_q + aw + 6 * kvw
    o_z = o_g + n_gate
    o_m = o_z + aw
    o_kv = o_q + aw
    w_main = jnp.concatenate(
        [w_in[:, o_m:], w_in[:, :o_i], w_in[:, o_q:o_kv], w_in[:, o_z:o_m], w_in[:, o_kv:o_g]], axis=1).astype(BF16)
    n_small = 2 * M_HEADS + n_gate
    w_small_t = jnp.concatenate(
        [w_in[:, o_i:o_q], w_in[:, o_g:o_z], jnp.zeros((d, LANES - n_small), w_in.dtype)], axis=1).T.astype(BF16)
    gm_col = 0
    ga_col = d
    mx_col = 2 * d
    q_col = mx_col + 4 * mw
    z_col = q_col + aw
    kv_col = z_col + aw

    x2d = x.reshape(t, d)
    p, ps = _inproj(x2d, pre_g.reshape(1, d), w_main, w_small_t)

    xc, q, k = _conv_qk(p, mx_col, conv_w, conv_b.reshape(1, mw), w_q.astype(BF16), w_k.astype(BF16), bsz, seq)
    nc = seq // M_CHUNK
    bias_m = jnp.concatenate([b_i, b_f]).reshape(2 * M_HEADS, 1)
    y_m = _mlstm(p, mx_col, xc, q, k, ps, bias_m, norm_g.reshape(1, mw), skip.reshape(1, mw), bsz, seq)

    half = A_CMP_STRIDE * dh
    pe = jnp.stack([pe_k.reshape(2, half), pe_v.reshape(2, half)])
    w1 = jnp.stack([w1_k, w1_v]).astype(BF16)
    w2 = jnp.stack([w2_k, w2_v]).astype(BF16)
    kvc = _compress(p, kv_col, pe, w1, w2, bsz, seq)
    vct = jnp.swapaxes(kvc[1], 2, 3)
    y_a = _nsa(p, ps, kvc, vct, q_col, kv_col, z_col, bsz, seq)

    out = _merge_out(y_m, y_a, w_up_m.astype(BF16), w_up_a.astype(BF16), p, gm_col, ga_col,
                     w_out.astype(BF16), post_g.reshape(1, d), x2d)
    return out.reshape(bsz, seq, d)


def kernel(x, pre_norm_g, w_in, m_conv_w, m_conv_b, m_w_q, m_w_k, m_b_i, m_b_f, m_norm_g, m_skip, a_pe_k, a_w1_k, a_w2_k, a_pe_v, a_w1_v, a_w2_v, w_up_m, w_up_a, w_out, post_norm_g):
    depth = w_in.shape[0]
    for l in range(depth):
        x = _layer(x, pre_norm_g[l], w_in[l], m_conv_w[l], m_conv_b[l], m_w_q[l], m_w_k[l], m_b_i[l], m_b_f[l],
                   m_norm_g[l], m_skip[l], a_pe_k[l], a_w1_k[l], a_w2_k[l], a_pe_v[l], a_w1_v[l], a_w2_v[l],
                   w_up_m[l], w_up_a[l], w_out[l], post_norm_g[l])
    return x
```

```python
import functools

import jax
import jax.numpy as jnp
from jax import lax
from jax.experimental import pallas as pl
from jax.experimental.pallas import tpu as pltpu

F32 = jnp.float32
BF16 = jnp.bfloat16

M_HEADS = 4
M_CONV = 4
M_CHUNK = 128
A_HEADS = 16
A_KV_HEADS = 4
A_GROUP = A_HEADS // A_KV_HEADS
A_HEAD_DIM = 64
A_CMP_BLOCK = 32
A_CMP_STRIDE = 16
A_SEL_BLOCK = 64
A_SEL_TOPK = 8
A_WINDOW = 512
Q_BLOCK = 128
SEL_KEYS_PER_STEP = 512
RMS_EPS = 1e-6
LN_EPS = 1e-6
NEG_INF = -1e30
POS_INF = 1e30

LANES = 128
BF16_SUBLANES = 16
LOG2_E = 1.4426950408889634
VMEM_LIMIT_BYTES = 48 * 1024 * 1024


def _nt_dot(a, b):
    return lax.dot_general(a, b, (((1,), (1,)), ((), ())), preferred_element_type=F32)


def _dot(a, b):
    return jnp.dot(a, b, preferred_element_type=F32)


def _sigmoid(x):
    return 1.0 / (1.0 + jnp.exp(-x))


def _params(*sem):
    return pltpu.CompilerParams(dimension_semantics=sem, vmem_limit_bytes=VMEM_LIMIT_BYTES)


def _repack_kernel(w_ref, o_ref, os_ref, *, main_pieces, small_pieces):
    col = 0
    for lo, hi in main_pieces:
        o_ref[:, col:col + hi - lo] = w_ref[:, lo:hi].astype(o_ref.dtype)
        col += hi - lo
    small = [w_ref[:, lo:hi] for lo, hi in small_pieces]
    used = sum(hi - lo for lo, hi in small_pieces)
    small.append(jnp.zeros((w_ref.shape[0], os_ref.shape[1] - used), w_ref.dtype))
    os_ref[...] = jnp.concatenate(small, axis=1).astype(os_ref.dtype)


def _repack(w_in, main_pieces, small_pieces, tr=128):
    d, n_in = w_in.shape
    n = sum(hi - lo for lo, hi in main_pieces)
    return pl.pallas_call(
        functools.partial(_repack_kernel, main_pieces=main_pieces, small_pieces=small_pieces),
        grid=(d // tr,),
        in_specs=[pl.BlockSpec((tr, n_in), lambda i: (i, 0))],
        out_specs=[pl.BlockSpec((tr, n), lambda i: (i, 0)), pl.BlockSpec((tr, LANES), lambda i: (i, 0))],
        out_shape=[jax.ShapeDtypeStruct((d, n), BF16), jax.ShapeDtypeStruct((d, LANES), BF16)],
        compiler_params=_params("parallel"),
        name="repack",
    )(w_in)


def _inproj_kernel(x_ref, g_ref, w_ref, wst_ref, o_ref, ost_ref, h_scr):
    @pl.when(pl.program_id(1) == 0)
    def _():
        xf = x_ref[...]
        ms = jnp.mean(xf * xf, axis=-1, keepdims=True)
        hb = (xf * lax.rsqrt(ms + RMS_EPS) * g_ref[...]).astype(BF16)
        h_scr[...] = hb
        ost_ref[...] = _nt_dot(wst_ref[...], hb)

    o_ref[...] = _dot(h_scr[...], w_ref[...]).astype(o_ref.dtype)


def _inproj(x2d, g, w_main, w_small_t, tm=1024, tn=512):
    t, d = x2d.shape
    n = w_main.shape[1]
    ns = w_small_t.shape[0]
    return pl.pallas_call(
        _inproj_kernel,
        grid=(t // tm, n // tn),
        in_specs=[
            pl.BlockSpec((tm, d), lambda i, j: (i, 0)),
            pl.BlockSpec((1, d), lambda i, j: (0, 0)),
            pl.BlockSpec((d, tn), lambda i, j: (0, j)),
            pl.BlockSpec((ns, d), lambda i, j: (0, 0)),
        ],
        out_specs=[
            pl.BlockSpec((tm, tn), lambda i, j: (i, j)),
            pl.BlockSpec((ns, tm), lambda i, j: (0, i)),
        ],
        out_shape=[jax.ShapeDtypeStruct((t, n), BF16), jax.ShapeDtypeStruct((ns, t), F32)],
        scratch_shapes=[pltpu.VMEM((tm, d), BF16)],
        compiler_params=_params("parallel", "arbitrary"),
        name="inproj",
    )(x2d, g, w_main, w_small_t)


def _conv_qk_kernel(x_ref, cw_ref, cb_ref, wq_ref, wk_ref, xc_ref, q_ref, k_ref, pad_ref, *, rows):
    s, c = x_ref.shape
    front = 8
    pad_ref[0:front, :] = jnp.zeros((front, c), F32)
    pad_ref[front:front + s, :] = x_ref[...].astype(F32)
    cw = cw_ref[...]
    cb = cb_ref[...]
    wq = wq_ref[...]
    wk = wk_ref[...]
    k_scale = c ** -0.5
    for r in range(s // rows):
        acc = jnp.broadcast_to(cb, (rows, c))
        for kk in range(M_CONV):
            start = front + r * rows + kk - (M_CONV - 1)
            acc = acc + cw[kk:kk + 1, :] * pad_ref[start:start + rows, :]
        xcb = (acc * _sigmoid(acc)).astype(BF16)
        xc_ref[r * rows:(r + 1) * rows, :] = xcb
        q_ref[r * rows:(r + 1) * rows, :] = _dot(xcb, wq).astype(BF16)
        k_ref[r * rows:(r + 1) * rows, :] = (_dot(xcb, wk) * k_scale).astype(BF16)


def _conv_qk(p, mx_col, conv_w, conv_b, wq, wk, bsz, seq):
    t = p.shape[0]
    heads, hd = wq.shape[0], wq.shape[1]
    width = heads * hd
    blk = pl.BlockSpec((seq, hd), lambda b, h: (b, h))
    out = jax.ShapeDtypeStruct((t, width), BF16)
    return pl.pallas_call(
        functools.partial(_conv_qk_kernel, rows=256),
        grid=(bsz, heads),
        in_specs=[
            pl.BlockSpec((seq, hd), lambda b, h: (b, mx_col // hd + h)),
            pl.BlockSpec((M_CONV, hd), lambda b, h: (0, h)),
            pl.BlockSpec((1, hd), lambda b, h: (0, h)),
            pl.BlockSpec((None, hd, hd), lambda b, h: (h, 0, 0)),
            pl.BlockSpec((None, hd, hd), lambda b, h: (h, 0, 0)),
        ],
        out_specs=[blk, blk, blk],
        out_shape=[out, out, out],
        scratch_shapes=[pltpu.VMEM((seq + 8, hd), F32)],
        compiler_params=_params("parallel", "parallel"),
        name="conv_qk",
    )(p, conv_w, conv_b, wq, wk)


def _log_sigmoid(x):
    return jnp.minimum(x, 0.0) - jnp.log1p(jnp.exp(-jnp.abs(x)))


def _mlstm_kernel(bias_ref, q_ref, k_ref, v_ref, o_ref, z_ref, xc_ref, g_ref, ng_ref, sk_ref,
                  y_ref, c_scr, n_scr, m_scr):
    @pl.when(pl.program_id(1) == 0)
    def _():
        c_scr[...] = jnp.zeros_like(c_scr)
        n_scr[...] = jnp.zeros_like(n_scr)
        m_scr[...] = jnp.zeros_like(m_scr)

    ln = q_ref.shape[0]
    heads, hd = c_scr.shape[0], c_scr.shape[1]
    gates = g_ref[...] + bias_ref[...]
    fb_all = _log_sigmoid(gates[heads:2 * heads, :])
    row = lax.broadcasted_iota(jnp.int32, (ln, ln), 0)
    col = lax.broadcasted_iota(jnp.int32, (ln, ln), 1)
    tril = col <= row
    eye = col == row
    for hh in range(heads):
        sl = slice(hh * hd, (hh + 1) * hd)
        ib = gates[hh:hh + 1, :]
        fb = fb_all[hh:hh + 1, :]
        bcum_c = jnp.sum(jnp.where(tril, fb, 0.0), axis=1, keepdims=True)
        bcum_r = jnp.sum(jnp.where(eye, bcum_c, 0.0), axis=0, keepdims=True)
        g_tot = jnp.sum(fb, axis=1, keepdims=True)
        e_r = ib - bcum_r
        m_st = m_scr[hh]
        d_intra = jnp.where(tril, bcum_c + e_r, NEG_INF)
        d_inter = bcum_c + m_st
        m_t = jnp.maximum(d_inter, jnp.max(d_intra, axis=1, keepdims=True))
        w_intra = jnp.exp(d_intra - m_t)
        w_inter = jnp.exp(d_inter - m_t)

        qb = q_ref[:, sl]
        kb = k_ref[:, sl]
        vb = v_ref[:, sl]
        c_st = c_scr[hh]
        n_st = n_scr[hh]
        sc = _nt_dot(qb, kb) * w_intra
        num = _dot(sc.astype(BF16), vb) + w_inter * _dot(qb, c_st.astype(BF16))
        qn = jnp.sum(qb.astype(F32) * n_st, axis=1, keepdims=True)
        den = jnp.sum(sc, axis=1, keepdims=True) + w_inter * qn
        h = num / jnp.maximum(jnp.abs(den), jnp.exp(-m_t))

        d_state = g_tot + e_r
        m_new = jnp.maximum(g_tot + m_st, jnp.max(d_state, axis=1, keepdims=True))
        w_s_r = jnp.exp(d_state - m_new)
        w_s_c = jnp.sum(jnp.where(eye, w_s_r, 0.0), axis=1, keepdims=True)
        decay = jnp.exp(g_tot + m_st - m_new)
        kw = kb.astype(F32) * w_s_c
        kv = lax.dot_general(kw.astype(BF16), vb, (((0,), (0,)), ((), ())), preferred_element_type=F32)
        c_scr[hh] = decay * c_st + kv
        n_scr[hh] = decay * n_st + jnp.sum(kw, axis=0, keepdims=True)
        m_scr[hh] = m_new

        hg = h * _sigmoid(o_ref[:, sl].astype(F32))
        mu = jnp.mean(hg, axis=-1, keepdims=True)
        hc = hg - mu
        var = jnp.mean(hc * hc, axis=-1, keepdims=True)
        hn = hc * lax.rsqrt(var + LN_EPS)
        z = z_ref[:, sl].astype(F32)
        y = (hn * ng_ref[:, sl] + sk_ref[:, sl] * xc_ref[:, sl].astype(F32)) * (z * _sigmoid(z))
        y_ref[:, sl] = y.astype(y_ref.dtype)


def _mlstm(p, mx_col, xc, q, k, gates, bias, norm_g, skip, bsz, seq):
    t, width = q.shape
    pc = mx_col // width
    heads = M_HEADS
    hd = width // heads
    ln = M_CHUNK
    nc = seq // ln
    rows = lambda b, c: (b * nc + c, 0)
    vec = pl.BlockSpec((1, width), lambda b, c: (0, 0))
    return pl.pallas_call(
        _mlstm_kernel,
        grid=(bsz, nc),
        in_specs=[
            pl.BlockSpec((2 * heads, 1), lambda b, c: (0, 0)),
            pl.BlockSpec((ln, width), rows),
            pl.BlockSpec((ln, width), rows),
            pl.BlockSpec((ln, width), lambda b, c: (b * nc + c, pc + 1)),
            pl.BlockSpec((ln, width), lambda b, c: (b * nc + c, pc + 2)),
            pl.BlockSpec((ln, width), lambda b, c: (b * nc + c, pc + 3)),
            pl.BlockSpec((ln, width), rows),
            pl.BlockSpec((2 * heads, ln), lambda b, c: (0, b * nc + c)),
            vec, vec,
        ],
        out_specs=pl.BlockSpec((ln, width), rows),
        out_shape=jax.ShapeDtypeStruct((t, width), BF16),
        scratch_shapes=[pltpu.VMEM((heads, hd, hd), F32), pltpu.VMEM((heads, 1, hd), F32),
                        pltpu.VMEM((heads, 1, 1), F32)],
        compiler_params=_params("parallel", "arbitrary"),
        name="mlstm",
    )(bias, q, k, p, p, p, xc, gates, norm_g, skip)


def _compress_kernel(x_ref, pe_ref, w1_ref, w2_ref, o_ref, xf_scr):
    ng, n_sub, dh = o_ref.shape
    half = A_CMP_STRIDE * dh
    n_chunk = xf_scr.shape[0]
    for c in range(n_chunk):
        xf_scr[c] = x_ref[:, c * LANES:(c + 1) * LANES].astype(F32)
    xs = [jnp.concatenate([xf_scr[c, pl.ds(r, n_sub, stride=A_CMP_STRIDE), :] for c in range(n_chunk)], axis=1)
          for r in range(A_CMP_STRIDE)]
    pe = pe_ref[...]
    rid = lax.broadcasted_iota(jnp.int32, (n_sub, dh), 0)
    for g in range(ng):
        t = jnp.concatenate([x[:, g * dh:(g + 1) * dh] for x in xs], axis=1)
        za = _dot((t + pe[0:1, :]).astype(BF16), w1_ref[0:half, :])
        zb = _dot((t + pe[1:2, :]).astype(BF16), w1_ref[half:2 * half, :])
        pre = za + pltpu.roll(zb, n_sub - 1, 0)
        hid = (pre * _sigmoid(pre)).astype(BF16)
        out = _dot(hid, w2_ref[...])
        o_ref[g] = jnp.where(rid < n_sub - 1, out, 0.0).astype(o_ref.dtype)


def _compress(p, kc_col, pe, w1, w2, bsz, seq):
    g = A_KV_HEADS
    dh = A_HEAD_DIM
    kvw = g * dh
    n_sub = seq // A_CMP_STRIDE
    half = A_CMP_STRIDE * dh
    hid = w1.shape[-1]
    return pl.pallas_call(
        _compress_kernel,
        grid=(2, bsz),
        in_specs=[
            pl.BlockSpec((seq, kvw), lambda i, b: (b, kc_col // kvw + i)),
            pl.BlockSpec((None, 2, half), lambda i, b: (i, 0, 0)),
            pl.BlockSpec((None, 2 * half, hid), lambda i, b: (i, 0, 0)),
            pl.BlockSpec((None, hid, dh), lambda i, b: (i, 0, 0)),
        ],
        out_specs=pl.BlockSpec((None, None, g, n_sub, dh), lambda i, b: (i, b, 0, 0, 0)),
        out_shape=jax.ShapeDtypeStruct((2, bsz, g, n_sub, dh), BF16),
        scratch_shapes=[pltpu.VMEM((kvw // LANES, seq, LANES), F32)],
        compiler_params=_params("parallel", "parallel"),
        name="compress",
    )(p, pe, w1, w2)


def _mask_heads(x, ok, fill, tq):
    n = x.shape[1] // tq
    return jnp.concatenate([jnp.where(ok, x[:, h * tq:(h + 1) * tq], fill) for h in range(n)], axis=1)


def _with_ones_rows(vt):
    return jnp.concatenate([vt, jnp.ones((BF16_SUBLANES, vt.shape[1]), vt.dtype)], axis=0)


def _nsa_kernel(q_ref, kc_ref, vct_ref, ks_ref, vs_ref, kw_ref, vw_ref, ps_ref, z_ref, o_ref,
                sel_ref, vst_ref, vwt_ref):
    qi = pl.program_id(1)
    tq = q_ref.shape[0]
    dh = A_HEAD_DIM
    hpg = A_GROUP
    ng = A_KV_HEADS
    r = hpg * tq
    n_cmp = kc_ref.shape[1]
    kt = vst_ref.shape[2]
    tk = vwt_ref.shape[2]
    seq = ks_ref.shape[0]
    n_sel = seq // A_SEL_BLOCK
    t0 = qi * tq

    @pl.when(qi == 0)
    def _():
        for j in range(seq // kt):
            vst_ref[j] = vs_ref[j * kt:(j + 1) * kt, :].astype(F32).T.astype(BF16)
        for j in range(seq // tk):
            vwt_ref[j] = vw_ref[j * tk:(j + 1) * tk, :].astype(F32).T.astype(BF16)

    qt = (q_ref[...].astype(F32) * (dh ** -0.5 * LOG2_E)).T.astype(BF16)
    q4t = [jnp.concatenate([qt[(g * hpg + h) * dh:(g * hpg + h + 1) * dh, :] for h in range(hpg)], axis=1)
           for g in range(ng)]
    zpad = jnp.zeros((dh, r), BF16)
    q_pad = [jnp.concatenate([q4t[g] if gg == g else zpad for gg in range(ng)], axis=0) for g in range(ng)]

    cn = lax.broadcasted_iota(jnp.int32, (n_cmp, tq), 0)
    ct = t0 + lax.broadcasted_iota(jnp.int32, (n_cmp, tq), 1)
    valid = (cn * A_CMP_STRIDE + (A_CMP_BLOCK - 1)) <= ct
    jn = lax.broadcasted_iota(jnp.int32, (n_sel, n_cmp), 0)
    nn = lax.broadcasted_iota(jnp.int32, (n_sel, n_cmp), 1)
    ov = ((nn * A_CMP_STRIDE < (jn + 1) * A_SEL_BLOCK)
          & (nn * A_CMP_STRIDE + (A_CMP_BLOCK - 1) >= jn * A_SEL_BLOCK))
    ov_t = jnp.where(ov, 1.0, 0.0).astype(BF16)
    jr = lax.broadcasted_iota(jnp.int32, (n_sel, tq), 0)
    tl = t0 + lax.broadcasted_iota(jnp.int32, (n_sel, tq), 1)
    cur = lax.shift_right_logical(tl, 6)
    forced = (jr == 0) | (jr == cur) | (jr == cur - 1)
    o_c = []
    for g in range(ng):
        s_m = _mask_heads(_dot(kc_ref[g], q4t[g]), valid, NEG_INF, tq)
        mx = jnp.max(s_m, axis=0, keepdims=True)
        e = _mask_heads(jnp.exp2(s_m - mx), valid, 0.0, tq)
        lsum = jnp.sum(e, axis=0, keepdims=True)
        p_c = e * (1.0 / jnp.where(lsum > 0.0, lsum, 1.0))
        o_c.append(_dot(vct_ref[g], p_c.astype(BF16)))
        p_sum = p_c[:, 0:tq]
        for h in range(1, hpg):
            p_sum = p_sum + p_c[:, h * tq:(h + 1) * tq]
        p_hi = p_sum.astype(BF16)
        r1 = p_sum - p_hi.astype(F32)
        p_mid = r1.astype(BF16)
        p_lo = (r1 - p_mid.astype(F32)).astype(BF16)
        imp_t = _dot(ov_t, p_hi) + _dot(ov_t, p_mid) + _dot(ov_t, p_lo)
        score = jnp.where(forced, POS_INF, jnp.where(jr <= cur, imp_t, NEG_INF))
        rank = jnp.zeros((n_sel, tq), F32)
        for jp in range(n_sel):
            sj = score[jp:jp + 1, :]
            beats = (sj > score) | ((sj == score) & (jr > jp))
            rank = rank + jnp.where(beats, 1.0, 0.0)
        sel_ref[g] = jnp.where(rank < float(min(A_SEL_TOPK, n_sel)), 1.0, 0.0)

    sub_s = lax.broadcasted_iota(jnp.int32, (kt, tq), 0)
    t_abs_s = t0 + lax.broadcasted_iota(jnp.int32, (kt, tq), 1)
    blocks_per_iter = kt // A_SEL_BLOCK
    da = dh + BF16_SUBLANES

    def sel_body(j, carry):
        off = pl.multiple_of(j * kt, kt)
        k_all = ks_ref[pl.ds(off, kt), :]
        causal = (off + sub_s) <= t_abs_s
        out = []
        for g in range(ng):
            m, acc = carry[g]
            s = _dot(k_all, q_pad[g])
            flags = jnp.concatenate(
                [jnp.broadcast_to(sel_ref[g, pl.ds(j * blocks_per_iter + i, 1), :], (A_SEL_BLOCK, tq))
                 for i in range(blocks_per_iter)], axis=0)
            s = _mask_heads(s, (flags > 0.5) & causal, NEG_INF, tq)
            m_new = jnp.maximum(m, jnp.max(s, axis=0, keepdims=True))
            p = jnp.exp2(s - m_new).astype(BF16)
            vt = _with_ones_rows(vst_ref[j, g * dh:(g + 1) * dh, :])
            out.append((m_new, jnp.exp2(m - m_new) * acc + _dot(vt, p)))
        return tuple(out)

    init = tuple((jnp.full((1, r), NEG_INF, F32), jnp.zeros((da, r), F32)) for _ in range(ng))
    sel_state = lax.fori_loop(0, (t0 + tq - 1) // kt + 1, sel_body, init)

    n_wb = -(-A_WINDOW // tk)
    wk = (n_wb + 1) * tk
    j0 = jnp.maximum(qi - n_wb, 0)
    off_w = pl.multiple_of(j0 * tk, tk)
    c_abs = off_w + lax.broadcasted_iota(jnp.int32, (wk, tq), 0)
    t_abs_w = t0 + lax.broadcasted_iota(jnp.int32, (wk, tq), 1)
    ok_w = (c_abs <= t_abs_w) & (c_abs > t_abs_w - A_WINDOW)
    kw_all = kw_ref[pl.ds(off_w, wk), :]

    gs = _sigmoid(ps_ref[...])
    g0 = 2 * M_HEADS
    outs = []
    for g in range(ng):
        s_w = _mask_heads(_dot(kw_all, q_pad[g]), ok_w, NEG_INF, tq)
        p_w = jnp.exp2(s_w - jnp.max(s_w, axis=0, keepdims=True)).astype(BF16)
        acc_w = _dot(_with_ones_rows(vwt_ref[j0, g * dh:(g + 1) * dh, :]), p_w[0:tk, :])
        for i in range(1, n_wb + 1):
            acc_w = acc_w + _dot(_with_ones_rows(vwt_ref[j0 + i, g * dh:(g + 1) * dh, :]), p_w[i * tk:(i + 1) * tk, :])
        o_w = acc_w[0:dh, :] * (1.0 / acc_w[dh:dh + 1, :])
        acc_s = sel_state[g][1]
        o_s = acc_s[0:dh, :] * (1.0 / acc_s[dh:dh + 1, :])
        for h in range(hpg):
            sl = slice(h * tq, (h + 1) * tq)
            gr = g0 + 3 * (g * hpg + h)
            outs.append(gs[gr:gr + 1, :] * o_c[g][:, sl] + gs[gr + 1:gr + 2, :] * o_s[:, sl]
                        + gs[gr + 2:gr + 3, :] * o_w[:, sl])
    o_all = jnp.concatenate(outs, axis=0).T
    z = z_ref[...].astype(F32)
    o_ref[...] = (o_all * (z * _sigmoid(z))).astype(o_ref.dtype)


def _nsa(p, ps, kvc, vct, q_col, kv_col, z_col, bsz, seq):
    t = p.shape[0]
    g = A_KV_HEADS
    dh = A_HEAD_DIM
    aw = A_HEADS * dh
    kvw = g * dh
    tq = Q_BLOCK
    nq = seq // tq
    n_cmp = kvc.shape[3]
    kt = SEL_KEYS_PER_STEP
    tk = LANES
    rows = lambda b, qi: b * nq + qi
    kv_piece = lambda i: pl.BlockSpec((seq, kvw), lambda b, qi: (b, kv_col // kvw + i))
    return pl.pallas_call(
        _nsa_kernel,
        grid=(bsz, nq),
        in_specs=[
            pl.BlockSpec((tq, aw), lambda b, qi: (rows(b, qi), q_col // aw)),
            pl.BlockSpec((None, None, g, n_cmp, dh), lambda b, qi: (0, b, 0, 0, 0)),
            pl.BlockSpec((None, g, dh, n_cmp), lambda b, qi: (b, 0, 0, 0)),
            kv_piece(2), kv_piece(3), kv_piece(4), kv_piece(5),
            pl.BlockSpec((LANES, tq), lambda b, qi: (0, rows(b, qi))),
            pl.BlockSpec((tq, aw), lambda b, qi: (rows(b, qi), z_col // aw)),
        ],
        out_specs=pl.BlockSpec((tq, aw), lambda b, qi: (rows(b, qi), 0)),
        out_shape=jax.ShapeDtypeStruct((t, aw), BF16),
        scratch_shapes=[pltpu.VMEM((g, seq // A_SEL_BLOCK, tq), F32),
                        pltpu.VMEM((seq // kt, kvw, kt), BF16),
                        pltpu.VMEM((seq // tk, kvw, tk), BF16)],
        compiler_params=_params("parallel", "arbitrary"),
        name="nsa",
    )(p, kvc, vct, p, p, p, p, ps, p)


def _merge_out_kernel(ym_ref, ya_ref, wm_ref, wa_ref, gm_ref, ga_ref, wo_ref, g_ref, x_ref, o_ref):
    um = _dot(ym_ref[...], wm_ref[...])
    ua = _dot(ya_ref[...], wa_ref[...])
    mg = _sigmoid(gm_ref[...].astype(F32)) * um + _sigmoid(ga_ref[...].astype(F32)) * ua
    out = _dot(mg.astype(BF16), wo_ref[...])
    ms = jnp.mean(out * out, axis=-1, keepdims=True)
    o_ref[...] = x_ref[...] + out * lax.rsqrt(ms + RMS_EPS) * g_ref[...]


def _merge_out(ym, ya, wm, wa, p, gm_col, ga_col, wo, g, x2d, tm=256):
    t, d = x2d.shape
    kdim = ym.shape[1]
    resident = lambda shape: pl.BlockSpec(shape, lambda i: (0, 0), pipeline_mode=pl.Buffered(1))
    return pl.pallas_call(
        _merge_out_kernel,
        grid=(t // tm,),
        in_specs=[
            pl.BlockSpec((tm, kdim), lambda i: (i, 0)),
            pl.BlockSpec((tm, kdim), lambda i: (i, 0)),
            resident((kdim, d)),
            resident((kdim, d)),
            pl.BlockSpec((tm, d), lambda i: (i, gm_col // d)),
            pl.BlockSpec((tm, d), lambda i: (i, ga_col // d)),
            resident((d, d)),
            pl.BlockSpec((1, d), lambda i: (0, 0)),
            pl.BlockSpec((tm, d), lambda i: (i, 0)),
        ],
        out_specs=pl.BlockSpec((tm, d), lambda i: (i, 0)),
        out_shape=jax.ShapeDtypeStruct((t, d), F32),
        compiler_params=_params("parallel"),
        name="merge_out",
    )(ym, ya, wm, wa, p, p, wo, g, x2d)


def _layer(x, pre_g, w_in, conv_w, conv_b, w_q, w_k, b_i, b_f, norm_g, skip,
           pe_k, w1_k, w2_k, pe_v, w1_v, w2_v, w_up_m, w_up_a, w_out, post_g):
    bsz, seq, d = x.shape
    t = bsz * seq
    mw = w_up_m.shape[0]
    aw = w_up_a.shape[0]
    g = A_KV_HEADS
    dh = A_HEAD_DIM
    kvw = g * dh
    n_gate = 3 * A_HEADS

    o_i = 4 * mw
    o_f = o_i + M_HEADS
    o_q = o_f + M_HEADS
    o_g = o_q + aw + 6 * kvw
    o_z = o_g + n_gate
    o_m = o_z + aw
    o_kv = o_q + aw
    main_pieces = ((o_m, w_in.shape[1]), (0, o_i), (o_q, o_kv), (o_z, o_m), (o_kv, o_g))
    small_pieces = ((o_i, o_q), (o_g, o_z))
    w_main, w_small = _repack(w_in, main_pieces, small_pieces)
    w_small_t = w_small.T
    gm_col = 0
    ga_col = d
    mx_col = 2 * d
    q_col = mx_col + 4 * mw
    z_col = q_col + aw
    kv_col = z_col + aw

    x2d = x.reshape(t, d)
    p, ps = _inproj(x2d, pre_g.reshape(1, d), w_main, w_small_t)

    xc, q, k = _conv_qk(p, mx_col, conv_w, conv_b.reshape(1, mw), w_q.astype(BF16), w_k.astype(BF16), bsz, seq)
    nc = seq // M_CHUNK
    bias_m = jnp.concatenate([b_i, b_f]).reshape(2 * M_HEADS, 1)
    y_m = _mlstm(p, mx_col, xc, q, k, ps, bias_m, norm_g.reshape(1, mw), skip.reshape(1, mw), bsz, seq)

    half = A_CMP_STRIDE * dh
    pe = jnp.stack([pe_k.reshape(2, half), pe_v.reshape(2, half)])
    w1 = jnp.stack([w1_k, w1_v]).astype(BF16)
    w2 = jnp.stack([w2_k, w2_v]).astype(BF16)
    kvc = _compress(p, kv_col, pe, w1, w2, bsz, seq)
    vct = jnp.swapaxes(kvc[1], 2, 3)
    y_a = _nsa(p, ps, kvc, vct, q_col, kv_col, z_col, bsz, seq)

    out = _merge_out(y_m, y_a, w_up_m.astype(BF16), w_up_a.astype(BF16), p, gm_col, ga_col,
                     w_out.astype(BF16), post_g.reshape(1, d), x2d)
    return out.reshape(bsz, seq, d)


def kernel(x, pre_norm_g, w_in, m_conv_w, m_conv_b, m_w_q, m_w_k, m_b_i, m_b_f, m_norm_g, m_skip, a_pe_k, a_w1_k, a_w2_k, a_pe_v, a_w1_v, a_w2_v, w_up_m, w_up_a, w_out, post_norm_g):
    depth = w_in.shape[0]
    for l in range(depth):
        x = _layer(x, pre_norm_g[l], w_in[l], m_conv_w[l], m_conv_b[l], m_w_q[l], m_w_k[l], m_b_i[l], m_b_f[l],
                   m_norm_g[l], m_skip[l], a_pe_k[l], a_w1_k[l], a_w2_k[l], a_pe_v[l], a_w1_v[l], a_w2_v[l],
                   w_up_m[l], w_up_a[l], w_out[l], post_norm_g[l])
    return x
```

```python
import functools

import jax
import jax.numpy as jnp
from jax import lax
from jax.experimental import pallas as pl
from jax.experimental.pallas import tpu as pltpu

F32 = jnp.float32
BF16 = jnp.bfloat16

M_HEADS = 4
M_CONV = 4
M_CHUNK = 128
A_HEADS = 16
A_KV_HEADS = 4
A_GROUP = A_HEADS // A_KV_HEADS
A_HEAD_DIM = 64
A_CMP_BLOCK = 32
A_CMP_STRIDE = 16
A_SEL_BLOCK = 64
A_SEL_TOPK = 8
A_WINDOW = 512
Q_BLOCK = 128
SEL_KEYS_PER_STEP = 512
INPROJ_TN = 512
RMS_EPS = 1e-6
LN_EPS = 1e-6
NEG_INF = -1e30
POS_INF = 1e30

LANES = 128
SUBLANES = 8
BF16_SUBLANES = 16
LOG2_E = 1.4426950408889634
VMEM_LIMIT_BYTES = 48 * 1024 * 1024


def _nt_dot(a, b):
    return lax.dot_general(a, b, (((1,), (1,)), ((), ())), preferred_element_type=F32)


def _dot(a, b):
    return jnp.dot(a, b, preferred_element_type=F32)


def _sigmoid(x):
    return 1.0 / (1.0 + jnp.exp(-x))


def _params(*sem):
    return pltpu.CompilerParams(dimension_semantics=sem, vmem_limit_bytes=VMEM_LIMIT_BYTES)


def _inproj_kernel(rows_ref, x_ref, g_ref, wt_ref, *rest):
    del rows_ref
    o_ref, ost_ref, h_scr = rest[-3:]
    small_refs = rest[:-3]

    @pl.when(pl.program_id(1) == 0)
    def _():
        xf = x_ref[...]
        ms = jnp.mean(xf * xf, axis=-1, keepdims=True)
        hb = (xf * lax.rsqrt(ms + RMS_EPS) * g_ref[...]).astype(BF16)
        h_scr[...] = hb
        rows = [r[...] for r in small_refs]
        used = sum(r.shape[0] for r in rows)
        rows.append(jnp.zeros((ost_ref.shape[0] - used, xf.shape[1]), F32))
        ost_ref[...] = _nt_dot(jnp.concatenate(rows, axis=0).astype(BF16), hb)

    o_ref[...] = _nt_dot(h_scr[...], wt_ref[...].astype(BF16)).astype(o_ref.dtype)


def _inproj(x2d, g, w_t, row_starts, small_pieces, tm=1024, tn=512):
    t, d = x2d.shape
    n = row_starts.shape[0] * tn
    ns = LANES
    rows_of = lambda lo, hi: pl.BlockSpec((pl.Element(hi - lo), pl.Element(d)), lambda i, j, rows: (lo, 0))
    grid_spec = pltpu.PrefetchScalarGridSpec(
        num_scalar_prefetch=1,
        grid=(t // tm, n // tn),
        in_specs=[
            pl.BlockSpec((tm, d), lambda i, j, rows: (i, 0)),
            pl.BlockSpec((1, d), lambda i, j, rows: (0, 0)),
            pl.BlockSpec((pl.Element(tn), pl.Element(d)), lambda i, j, rows: (rows[j] * SUBLANES, 0)),
        ] + [rows_of(lo, hi) for lo, hi in small_pieces],
        out_specs=[
            pl.BlockSpec((tm, tn), lambda i, j, rows: (i, j)),
            pl.BlockSpec((ns, tm), lambda i, j, rows: (0, i)),
        ],
        scratch_shapes=[pltpu.VMEM((tm, d), BF16)],
    )
    return pl.pallas_call(
        _inproj_kernel,
        grid_spec=grid_spec,
        out_shape=[jax.ShapeDtypeStruct((t, n), BF16), jax.ShapeDtypeStruct((ns, t), F32)],
        compiler_params=_params("parallel", "arbitrary"),
        name="inproj",
    )(row_starts, x2d, g, w_t, *([w_t] * len(small_pieces)))


def _conv_qk_kernel(x_ref, cw_ref, cb_ref, wq_ref, wk_ref, xc_ref, q_ref, k_ref, pad_ref, *, rows):
    s, c = x_ref.shape
    front = 8
    pad_ref[0:front, :] = jnp.zeros((front, c), F32)
    pad_ref[front:front + s, :] = x_ref[...].astype(F32)
    cw = cw_ref[...]
    cb = cb_ref[...]
    wq = wq_ref[...]
    wk = wk_ref[...]
    k_scale = c ** -0.5
    for r in range(s // rows):
        acc = jnp.broadcast_to(cb, (rows, c))
        for kk in range(M_CONV):
            start = front + r * rows + kk - (M_CONV - 1)
            acc = acc + cw[kk:kk + 1, :] * pad_ref[start:start + rows, :]
        xcb = (acc * _sigmoid(acc)).astype(BF16)
        xc_ref[r * rows:(r + 1) * rows, :] = xcb
        q_ref[r * rows:(r + 1) * rows, :] = _dot(xcb, wq).astype(BF16)
        k_ref[r * rows:(r + 1) * rows, :] = (_dot(xcb, wk) * k_scale).astype(BF16)


def _conv_qk(p, mx_col, conv_w, conv_b, wq, wk, bsz, seq):
    t = p.shape[0]
    heads, hd = wq.shape[0], wq.shape[1]
    width = heads * hd
    blk = pl.BlockSpec((seq, hd), lambda b, h: (b, h))
    out = jax.ShapeDtypeStruct((t, width), BF16)
    return pl.pallas_call(
        functools.partial(_conv_qk_kernel, rows=256),
        grid=(bsz, heads),
        in_specs=[
            pl.BlockSpec((seq, hd), lambda b, h: (b, mx_col // hd + h)),
            pl.BlockSpec((M_CONV, hd), lambda b, h: (0, h)),
            pl.BlockSpec((1, hd), lambda b, h: (0, h)),
            pl.BlockSpec((None, hd, hd), lambda b, h: (h, 0, 0)),
            pl.BlockSpec((None, hd, hd), lambda b, h: (h, 0, 0)),
        ],
        out_specs=[blk, blk, blk],
        out_shape=[out, out, out],
        scratch_shapes=[pltpu.VMEM((seq + 8, hd), F32)],
        compiler_params=_params("parallel", "parallel"),
        name="conv_qk",
    )(p, conv_w, conv_b, wq, wk)


def _log_sigmoid(x):
    return jnp.minimum(x, 0.0) - jnp.log1p(jnp.exp(-jnp.abs(x)))


def _mlstm_kernel(bias_ref, q_ref, k_ref, v_ref, o_ref, z_ref, xc_ref, g_ref, ng_ref, sk_ref,
                  y_ref, c_scr, n_scr, m_scr):
    @pl.when(pl.program_id(1) == 0)
    def _():
        c_scr[...] = jnp.zeros_like(c_scr)
        n_scr[...] = jnp.zeros_like(n_scr)
        m_scr[...] = jnp.zeros_like(m_scr)

    ln = q_ref.shape[0]
    heads, hd = c_scr.shape[0], c_scr.shape[1]
    gates = g_ref[...] + bias_ref[...]
    fb_all = _log_sigmoid(gates[heads:2 * heads, :])
    row = lax.broadcasted_iota(jnp.int32, (ln, ln), 0)
    col = lax.broadcasted_iota(jnp.int32, (ln, ln), 1)
    tril = col <= row
    eye = col == row
    for hh in range(heads):
        sl = slice(hh * hd, (hh + 1) * hd)
        ib = gates[hh:hh + 1, :]
        fb = fb_all[hh:hh + 1, :]
        bcum_c = jnp.sum(jnp.where(tril, fb, 0.0), axis=1, keepdims=True)
        bcum_r = jnp.sum(jnp.where(eye, bcum_c, 0.0), axis=0, keepdims=True)
        g_tot = jnp.sum(fb, axis=1, keepdims=True)
        e_r = ib - bcum_r
        m_st = m_scr[hh]
        d_intra = jnp.where(tril, bcum_c + e_r, NEG_INF)
        d_inter = bcum_c + m_st
        m_t = jnp.maximum(d_inter, jnp.max(d_intra, axis=1, keepdims=True))
        w_intra = jnp.exp(d_intra - m_t)
        w_inter = jnp.exp(d_inter - m_t)

        qb = q_ref[:, sl]
        kb = k_ref[:, sl]
        vb = v_ref[:, sl]
        c_st = c_scr[hh]
        n_st = n_scr[hh]
        sc = _nt_dot(qb, kb) * w_intra
        num = _dot(sc.astype(BF16), vb) + w_inter * _dot(qb, c_st.astype(BF16))
        qn = jnp.sum(qb.astype(F32) * n_st, axis=1, keepdims=True)
        den = jnp.sum(sc, axis=1, keepdims=True) + w_inter * qn
        h = num / jnp.maximum(jnp.abs(den), jnp.exp(-m_t))

        d_state = g_tot + e_r
        m_new = jnp.maximum(g_tot + m_st, jnp.max(d_state, axis=1, keepdims=True))
        w_s_r = jnp.exp(d_state - m_new)
        w_s_c = jnp.sum(jnp.where(eye, w_s_r, 0.0), axis=1, keepdims=True)
        decay = jnp.exp(g_tot + m_st - m_new)
        kw = kb.astype(F32) * w_s_c
        kv = lax.dot_general(kw.astype(BF16), vb, (((0,), (0,)), ((), ())), preferred_element_type=F32)
        c_scr[hh] = decay * c_st + kv
        n_scr[hh] = decay * n_st + jnp.sum(kw, axis=0, keepdims=True)
        m_scr[hh] = m_new

        hg = h * _sigmoid(o_ref[:, sl].astype(F32))
        mu = jnp.mean(hg, axis=-1, keepdims=True)
        hc = hg - mu
        var = jnp.mean(hc * hc, axis=-1, keepdims=True)
        hn = hc * lax.rsqrt(var + LN_EPS)
        z = z_ref[:, sl].astype(F32)
        y = (hn * ng_ref[:, sl] + sk_ref[:, sl] * xc_ref[:, sl].astype(F32)) * (z * _sigmoid(z))
        y_ref[:, sl] = y.astype(y_ref.dtype)


def _mlstm(p, mx_col, xc, q, k, gates, bias, norm_g, skip, bsz, seq):
    t, width = q.shape
    pc = mx_col // width
    heads = M_HEADS
    hd = width // heads
    ln = M_CHUNK
    nc = seq // ln
    rows = lambda b, c: (b * nc + c, 0)
    vec = pl.BlockSpec((1, width), lambda b, c: (0, 0))
    return pl.pallas_call(
        _mlstm_kernel,
        grid=(bsz, nc),
        in_specs=[
            pl.BlockSpec((2 * heads, 1), lambda b, c: (0, 0)),
            pl.BlockSpec((ln, width), rows),
            pl.BlockSpec((ln, width), rows),
            pl.BlockSpec((ln, width), lambda b, c: (b * nc + c, pc + 1)),
            pl.BlockSpec((ln, width), lambda b, c: (b * nc + c, pc + 2)),
            pl.BlockSpec((ln, width), lambda b, c: (b * nc + c, pc + 3)),
            pl.BlockSpec((ln, width), rows),
            pl.BlockSpec((2 * heads, ln), lambda b, c: (0, b * nc + c)),
            vec, vec,
        ],
        out_specs=pl.BlockSpec((ln, width), rows),
        out_shape=jax.ShapeDtypeStruct((t, width), BF16),
        scratch_shapes=[pltpu.VMEM((heads, hd, hd), F32), pltpu.VMEM((heads, 1, hd), F32),
                        pltpu.VMEM((heads, 1, 1), F32)],
        compiler_params=_params("parallel", "arbitrary"),
        name="mlstm",
    )(bias, q, k, p, p, p, xc, gates, norm_g, skip)


def _compress_kernel(x_ref, pe_ref, w1_ref, w2_ref, o_ref, xf_scr):
    ng, n_sub, dh = o_ref.shape
    half = A_CMP_STRIDE * dh
    n_chunk = xf_scr.shape[0]
    for c in range(n_chunk):
        xf_scr[c] = x_ref[:, c * LANES:(c + 1) * LANES].astype(F32)
    xs = [jnp.concatenate([xf_scr[c, pl.ds(r, n_sub, stride=A_CMP_STRIDE), :] for c in range(n_chunk)], axis=1)
          for r in range(A_CMP_STRIDE)]
    pe = pe_ref[...]
    rid = lax.broadcasted_iota(jnp.int32, (n_sub, dh), 0)
    for g in range(ng):
        t = jnp.concatenate([x[:, g * dh:(g + 1) * dh] for x in xs], axis=1)
        za = _dot((t + pe[0:1, :]).astype(BF16), w1_ref[0:half, :])
        zb = _dot((t + pe[1:2, :]).astype(BF16), w1_ref[half:2 * half, :])
        pre = za + pltpu.roll(zb, n_sub - 1, 0)
        hid = (pre * _sigmoid(pre)).astype(BF16)
        out = _dot(hid, w2_ref[...])
        o_ref[g] = jnp.where(rid < n_sub - 1, out, 0.0).astype(o_ref.dtype)


def _compress(p, kc_col, pe, w1, w2, bsz, seq):
    g = A_KV_HEADS
    dh = A_HEAD_DIM
    kvw = g * dh
    n_sub = seq // A_CMP_STRIDE
    half = A_CMP_STRIDE * dh
    hid = w1.shape[-1]
    return pl.pallas_call(
        _compress_kernel,
        grid=(2, bsz),
        in_specs=[
            pl.BlockSpec((seq, kvw), lambda i, b: (b, kc_col // kvw + i)),
            pl.BlockSpec((None, 2, half), lambda i, b: (i, 0, 0)),
            pl.BlockSpec((None, 2 * half, hid), lambda i, b: (i, 0, 0)),
            pl.BlockSpec((None, hid, dh), lambda i, b: (i, 0, 0)),
        ],
        out_specs=pl.BlockSpec((None, None, g, n_sub, dh), lambda i, b: (i, b, 0, 0, 0)),
        out_shape=jax.ShapeDtypeStruct((2, bsz, g, n_sub, dh), BF16),
        scratch_shapes=[pltpu.VMEM((kvw // LANES, seq, LANES), F32)],
        compiler_params=_params("parallel", "parallel"),
        name="compress",
    )(p, pe, w1, w2)


def _mask_heads(x, ok, fill, tq):
    n = x.shape[1] // tq
    return jnp.concatenate([jnp.where(ok, x[:, h * tq:(h + 1) * tq], fill) for h in range(n)], axis=1)


def _with_ones_rows(vt):
    return jnp.concatenate([vt, jnp.ones((BF16_SUBLANES, vt.shape[1]), vt.dtype)], axis=0)


def _nsa_kernel(q_ref, kc_ref, vct_ref, ks_ref, vs_ref, kw_ref, vw_ref, ps_ref, z_ref, o_ref,
                sel_ref, vst_ref, vwt_ref):
    qi = pl.program_id(1)
    tq = q_ref.shape[0]
    dh = A_HEAD_DIM
    hpg = A_GROUP
    ng = A_KV_HEADS
    r = hpg * tq
    n_cmp = kc_ref.shape[1]
    kt = vst_ref.shape[2]
    tk = vwt_ref.shape[2]
    seq = ks_ref.shape[0]
    n_sel = seq // A_SEL_BLOCK
    t0 = qi * tq

    @pl.when(qi == 0)
    def _():
        for j in range(seq // kt):
            vst_ref[j] = vs_ref[j * kt:(j + 1) * kt, :].astype(F32).T.astype(BF16)
        for j in range(seq // tk):
            vwt_ref[j] = vw_ref[j * tk:(j + 1) * tk, :].astype(F32).T.astype(BF16)

    qt = (q_ref[...].astype(F32) * (dh ** -0.5 * LOG2_E)).T.astype(BF16)
    q4t = [jnp.concatenate([qt[(g * hpg + h) * dh:(g * hpg + h + 1) * dh, :] for h in range(hpg)], axis=1)
           for g in range(ng)]
    zpad = jnp.zeros((dh, r), BF16)
    q_pad = [jnp.concatenate([q4t[g] if gg == g else zpad for gg in range(ng)], axis=0) for g in range(ng)]

    cn = lax.broadcasted_iota(jnp.int32, (n_cmp, tq), 0)
    ct = t0 + lax.broadcasted_iota(jnp.int32, (n_cmp, tq), 1)
    valid = (cn * A_CMP_STRIDE + (A_CMP_BLOCK - 1)) <= ct
    jn = lax.broadcasted_iota(jnp.int32, (n_sel, n_cmp), 0)
    nn = lax.broadcasted_iota(jnp.int32, (n_sel, n_cmp), 1)
    ov = ((nn * A_CMP_STRIDE < (jn + 1) * A_SEL_BLOCK)
          & (nn * A_CMP_STRIDE + (A_CMP_BLOCK - 1) >= jn * A_SEL_BLOCK))
    ov_t = jnp.where(ov, 1.0, 0.0).astype(BF16)
    jr = lax.broadcasted_iota(jnp.int32, (n_sel, tq), 0)
    tl = t0 + lax.broadcasted_iota(jnp.int32, (n_sel, tq), 1)
    cur = lax.shift_right_logical(tl, 6)
    forced = (jr == 0) | (jr == cur) | (jr == cur - 1)
    o_c = []
    for g in range(ng):
        s_m = _mask_heads(_dot(kc_ref[g], q4t[g]), valid, NEG_INF, tq)
        mx = jnp.max(s_m, axis=0, keepdims=True)
        e = _mask_heads(jnp.exp2(s_m - mx), valid, 0.0, tq)
        lsum = jnp.sum(e, axis=0, keepdims=True)
        p_c = e * (1.0 / jnp.where(lsum > 0.0, lsum, 1.0))
        o_c.append(_dot(vct_ref[g], p_c.astype(BF16)))
        p_sum = p_c[:, 0:tq]
        for h in range(1, hpg):
            p_sum = p_sum + p_c[:, h * tq:(h + 1) * tq]
        p_hi = p_sum.astype(BF16)
        r1 = p_sum - p_hi.astype(F32)
        p_mid = r1.astype(BF16)
        p_lo = (r1 - p_mid.astype(F32)).astype(BF16)
        imp_t = _dot(ov_t, p_hi) + _dot(ov_t, p_mid) + _dot(ov_t, p_lo)
        score = jnp.where(forced, POS_INF, jnp.where(jr <= cur, imp_t, NEG_INF))
        rank = jnp.zeros((n_sel, tq), F32)
        for jp in range(n_sel):
            sj = score[jp:jp + 1, :]
            beats = (sj > score) | ((sj == score) & (jr > jp))
            rank = rank + jnp.where(beats, 1.0, 0.0)
        sel_ref[g] = jnp.where(rank < float(min(A_SEL_TOPK, n_sel)), 1.0, 0.0)

    sub_s = lax.broadcasted_iota(jnp.int32, (kt, tq), 0)
    t_abs_s = t0 + lax.broadcasted_iota(jnp.int32, (kt, tq), 1)
    blocks_per_iter = kt // A_SEL_BLOCK
    da = dh + BF16_SUBLANES

    def sel_body(j, carry):
        off = pl.multiple_of(j * kt, kt)
        k_all = ks_ref[pl.ds(off, kt), :]
        causal = (off + sub_s) <= t_abs_s
        out = []
        for g in range(ng):
            m, acc = carry[g]
            s = _dot(k_all, q_pad[g])
            flags = jnp.concatenate(
                [jnp.broadcast_to(sel_ref[g, pl.ds(j * blocks_per_iter + i, 1), :], (A_SEL_BLOCK, tq))
                 for i in range(blocks_per_iter)], axis=0)
            s = _mask_heads(s, (flags > 0.5) & causal, NEG_INF, tq)
            m_new = jnp.maximum(m, jnp.max(s, axis=0, keepdims=True))
            p = jnp.exp2(s - m_new).astype(BF16)
            vt = _with_ones_rows(vst_ref[j, g * dh:(g + 1) * dh, :])
            out.append((m_new, jnp.exp2(m - m_new) * acc + _dot(vt, p)))
        return tuple(out)

    init = tuple((jnp.full((1, r), NEG_INF, F32), jnp.zeros((da, r), F32)) for _ in range(ng))
    sel_state = lax.fori_loop(0, (t0 + tq - 1) // kt + 1, sel_body, init)

    n_wb = -(-A_WINDOW // tk)
    wk = (n_wb + 1) * tk
    j0 = jnp.maximum(qi - n_wb, 0)
    off_w = pl.multiple_of(j0 * tk, tk)
    c_abs = off_w + lax.broadcasted_iota(jnp.int32, (wk, tq), 0)
    t_abs_w = t0 + lax.broadcasted_iota(jnp.int32, (wk, tq), 1)
    ok_w = (c_abs <= t_abs_w) & (c_abs > t_abs_w - A_WINDOW)
    kw_all = kw_ref[pl.ds(off_w, wk), :]

    gs = _sigmoid(ps_ref[...])
    g0 = 2 * M_HEADS
    outs = []
    for g in range(ng):
        s_w = _mask_heads(_dot(kw_all, q_pad[g]), ok_w, NEG_INF, tq)
        p_w = jnp.exp2(s_w - jnp.max(s_w, axis=0, keepdims=True)).astype(BF16)
        acc_w = _dot(_with_ones_rows(vwt_ref[j0, g * dh:(g + 1) * dh, :]), p_w[0:tk, :])
        for i in range(1, n_wb + 1):
            acc_w = acc_w + _dot(_with_ones_rows(vwt_ref[j0 + i, g * dh:(g + 1) * dh, :]), p_w[i * tk:(i + 1) * tk, :])
        o_w = acc_w[0:dh, :] * (1.0 / acc_w[dh:dh + 1, :])
        acc_s = sel_state[g][1]
        o_s = acc_s[0:dh, :] * (1.0 / acc_s[dh:dh + 1, :])
        for h in range(hpg):
            sl = slice(h * tq, (h + 1) * tq)
            gr = g0 + 3 * (g * hpg + h)
            outs.append(gs[gr:gr + 1, :] * o_c[g][:, sl] + gs[gr + 1:gr + 2, :] * o_s[:, sl]
                        + gs[gr + 2:gr + 3, :] * o_w[:, sl])
    o_all = jnp.concatenate(outs, axis=0).T
    z = z_ref[...].astype(F32)
    o_ref[...] = (o_all * (z * _sigmoid(z))).astype(o_ref.dtype)


def _nsa(p, ps, kvc, vct, q_col, kv_col, z_col, bsz, seq):
    t = p.shape[0]
    g = A_KV_HEADS
    dh = A_HEAD_DIM
    aw = A_HEADS * dh
    kvw = g * dh
    tq = Q_BLOCK
    nq = seq // tq
    n_cmp = kvc.shape[3]
    kt = SEL_KEYS_PER_STEP
    tk = LANES
    rows = lambda b, qi: b * nq + qi
    kv_piece = lambda i: pl.BlockSpec((seq, kvw), lambda b, qi: (b, kv_col // kvw + i))
    return pl.pallas_call(
        _nsa_kernel,
        grid=(bsz, nq),
        in_specs=[
            pl.BlockSpec((tq, aw), lambda b, qi: (rows(b, qi), q_col // aw)),
            pl.BlockSpec((None, None, g, n_cmp, dh), lambda b, qi: (0, b, 0, 0, 0)),
            pl.BlockSpec((None, g, dh, n_cmp), lambda b, qi: (b, 0, 0, 0)),
            kv_piece(2), kv_piece(3), kv_piece(4), kv_piece(5),
            pl.BlockSpec((LANES, tq), lambda b, qi: (0, rows(b, qi))),
            pl.BlockSpec((tq, aw), lambda b, qi: (rows(b, qi), z_col // aw)),
        ],
        out_specs=pl.BlockSpec((tq, aw), lambda b, qi: (rows(b, qi), 0)),
        out_shape=jax.ShapeDtypeStruct((t, aw), BF16),
        scratch_shapes=[pltpu.VMEM((g, seq // A_SEL_BLOCK, tq), F32),
                        pltpu.VMEM((seq // kt, kvw, kt), BF16),
                        pltpu.VMEM((seq // tk, kvw, tk), BF16)],
        compiler_params=_params("parallel", "arbitrary"),
        name="nsa",
    )(p, kvc, vct, p, p, p, p, ps, p)


def _merge_out_kernel(ym_ref, ya_ref, wm_ref, wa_ref, gm_ref, ga_ref, wo_ref, g_ref, x_ref, o_ref):
    um = _dot(ym_ref[...], wm_ref[...])
    ua = _dot(ya_ref[...], wa_ref[...])
    mg = _sigmoid(gm_ref[...].astype(F32)) * um + _sigmoid(ga_ref[...].astype(F32)) * ua
    out = _dot(mg.astype(BF16), wo_ref[...])
    ms = jnp.mean(out * out, axis=-1, keepdims=True)
    o_ref[...] = x_ref[...] + out * lax.rsqrt(ms + RMS_EPS) * g_ref[...]


def _merge_out(ym, ya, wm, wa, p, gm_col, ga_col, wo, g, x2d, tm=256):
    t, d = x2d.shape
    kdim = ym.shape[1]
    resident = lambda shape: pl.BlockSpec(shape, lambda i: (0, 0), pipeline_mode=pl.Buffered(1))
    return pl.pallas_call(
        _merge_out_kernel,
        grid=(t // tm,),
        in_specs=[
            pl.BlockSpec((tm, kdim), lambda i: (i, 0)),
            pl.BlockSpec((tm, kdim), lambda i: (i, 0)),
            resident((kdim, d)),
            resident((kdim, d)),
            pl.BlockSpec((tm, d), lambda i: (i, gm_col // d)),
            pl.BlockSpec((tm, d), lambda i: (i, ga_col // d)),
            resident((d, d)),
            pl.BlockSpec((1, d), lambda i: (0, 0)),
            pl.BlockSpec((tm, d), lambda i: (i, 0)),
        ],
        out_specs=pl.BlockSpec((tm, d), lambda i: (i, 0)),
        out_shape=jax.ShapeDtypeStruct((t, d), F32),
        compiler_params=_params("parallel"),
        name="merge_out",
    )(ym, ya, wm, wa, p, p, wo, g, x2d)


def _layer(x, pre_g, w_in, conv_w, conv_b, w_q, w_k, b_i, b_f, norm_g, skip,
           pe_k, w1_k, w2_k, pe_v, w1_v, w2_v, w_up_m, w_up_a, w_out, post_g):
    bsz, seq, d = x.shape
    t = bsz * seq
    mw = w_up_m.shape[0]
    aw = w_up_a.shape[0]
    g = A_KV_HEADS
    dh = A_HEAD_DIM
    kvw = g * dh
    n_gate = 3 * A_HEADS

    o_i = 4 * mw
    o_f = o_i + M_HEADS
    o_q = o_f + M_HEADS
    o_g = o_q + aw + 6 * kvw
    o_z = o_g + n_gate
    o_m = o_z + aw
    o_kv = o_q + aw
    main_pieces = ((o_m, w_in.shape[1]), (0, o_i), (o_q, o_kv), (o_z, o_m), (o_kv, o_g))
    w_t = w_in.T
    row_starts = jnp.asarray([r // SUBLANES for lo, hi in main_pieces for r in range(lo, hi, INPROJ_TN)], jnp.int32)
    small_pieces = ((o_i, o_q), (o_g, o_z))
    gm_col = 0
    ga_col = d
    mx_col = 2 * d
    q_col = mx_col + 4 * mw
    z_col = q_col + aw
    kv_col = z_col + aw

    x2d = x.reshape(t, d)
    p, ps = _inproj(x2d, pre_g.reshape(1, d), w_t, row_starts, small_pieces, tn=INPROJ_TN)

    xc, q, k = _conv_qk(p, mx_col, conv_w, conv_b.reshape(1, mw), w_q.astype(BF16), w_k.astype(BF16), bsz, seq)
    nc = seq // M_CHUNK
    bias_m = jnp.concatenate([b_i, b_f]).reshape(2 * M_HEADS, 1)
    y_m = _mlstm(p, mx_col, xc, q, k, ps, bias_m, norm_g.reshape(1, mw), skip.reshape(1, mw), bsz, seq)

    half = A_CMP_STRIDE * dh
    pe = jnp.stack([pe_k.reshape(2, half), pe_v.reshape(2, half)])
    w1 = jnp.stack([w1_k, w1_v]).astype(BF16)
    w2 = jnp.stack([w2_k, w2_v]).astype(BF16)
    kvc = _compress(p, kv_col, pe, w1, w2, bsz, seq)
    vct = jnp.swapaxes(kvc[1], 2, 3)
    y_a = _nsa(p, ps, kvc, vct, q_col, kv_col, z_col, bsz, seq)

    out = _merge_out(y_m, y_a, w_up_m.astype(BF16), w_up_a.astype(BF16), p, gm_col, ga_col,
                     w_out.astype(BF16), post_g.reshape(1, d), x2d)
    return out.reshape(bsz, seq, d)


def kernel(x, pre_norm_g, w_in, m_conv_w, m_conv_b, m_w_q, m_w_k, m_b_i, m_b_f, m_norm_g, m_skip, a_pe_k, a_w1_k, a_w2_k, a_pe_v, a_w1_v, a_w2_v, w_up_m, w_up_a, w_out, post_norm_g):
    depth = w_in.shape[0]
    for l in range(depth):
        x = _layer(x, pre_norm_g[l], w_in[l], m_conv_w[l], m_conv_b[l], m_w_q[l], m_w_k[l], m_b_i[l], m_b_f[l],
                   m_norm_g[l], m_skip[l], a_pe_k[l], a_w1_k[l], a_w2_k[l], a_pe_v[l], a_w1_v[l], a_w2_v[l],
                   w_up_m[l], w_up_a[l], w_out[l], post_norm_g[l])
    return x
```

```python
import functools

import jax
import jax.numpy as jnp
from jax import lax
from jax.experimental import pallas as pl
from jax.experimental.pallas import tpu as pltpu

F32 = jnp.float32
BF16 = jnp.bfloat16

M_HEADS = 4
M_CONV = 4
M_CHUNK = 128
A_HEADS = 16
A_KV_HEADS = 4
A_GROUP = A_HEADS // A_KV_HEADS
A_HEAD_DIM = 64
A_CMP_BLOCK = 32
A_CMP_STRIDE = 16
A_SEL_BLOCK = 64
A_SEL_TOPK = 8
A_WINDOW = 512
Q_BLOCK = 128
SEL_KEYS_PER_STEP = 512
EXCLUDE = 1e4
MAX_FAST_BOUND = 50.0
BOUND_SLACK = 1.01
INPROJ_TN = 512
RMS_EPS = 1e-6
LN_EPS = 1e-6
NEG_INF = -1e30
POS_INF = 1e30

LANES = 128
SUBLANES = 8
BF16_SUBLANES = 16
LOG2_E = 1.4426950408889634
VMEM_LIMIT_BYTES = 48 * 1024 * 1024


def _nt_dot(a, b):
    return lax.dot_general(a, b, (((1,), (1,)), ((), ())), preferred_element_type=F32)


def _dot(a, b):
    return jnp.dot(a, b, preferred_element_type=F32)


def _sigmoid(x):
    return 1.0 / (1.0 + jnp.exp(-x))


def _params(*sem):
    return pltpu.CompilerParams(dimension_semantics=sem, vmem_limit_bytes=VMEM_LIMIT_BYTES)


def _inproj_kernel(rows_ref, x_ref, g_ref, wt_ref, *rest):
    del rows_ref
    o_ref, ost_ref, h_scr = rest[-3:]
    small_refs = rest[:-3]

    @pl.when(pl.program_id(1) == 0)
    def _():
        xf = x_ref[...]
        ms = jnp.mean(xf * xf, axis=-1, keepdims=True)
        hb = (xf * lax.rsqrt(ms + RMS_EPS) * g_ref[...]).astype(BF16)
        h_scr[...] = hb
        rows = [r[...] for r in small_refs]
        used = sum(r.shape[0] for r in rows)
        rows.append(jnp.zeros((ost_ref.shape[0] - used, xf.shape[1]), F32))
        ost_ref[...] = _nt_dot(jnp.concatenate(rows, axis=0).astype(BF16), hb)

    o_ref[...] = _nt_dot(h_scr[...], wt_ref[...].astype(BF16)).astype(o_ref.dtype)


def _inproj(x2d, g, w_t, row_starts, small_pieces, tm=1024, tn=512):
    t, d = x2d.shape
    n = row_starts.shape[0] * tn
    ns = LANES
    rows_of = lambda lo, hi: pl.BlockSpec((pl.Element(hi - lo), pl.Element(d)), lambda i, j, rows: (lo, 0))
    grid_spec = pltpu.PrefetchScalarGridSpec(
        num_scalar_prefetch=1,
        grid=(t // tm, n // tn),
        in_specs=[
            pl.BlockSpec((tm, d), lambda i, j, rows: (i, 0)),
            pl.BlockSpec((1, d), lambda i, j, rows: (0, 0)),
            pl.BlockSpec((pl.Element(tn), pl.Element(d)), lambda i, j, rows: (rows[j] * SUBLANES, 0)),
        ] + [rows_of(lo, hi) for lo, hi in small_pieces],
        out_specs=[
            pl.BlockSpec((tm, tn), lambda i, j, rows: (i, j)),
            pl.BlockSpec((ns, tm), lambda i, j, rows: (0, i)),
        ],
        scratch_shapes=[pltpu.VMEM((tm, d), BF16)],
    )
    return pl.pallas_call(
        _inproj_kernel,
        grid_spec=grid_spec,
        out_shape=[jax.ShapeDtypeStruct((t, n), BF16), jax.ShapeDtypeStruct((ns, t), F32)],
        compiler_params=_params("parallel", "arbitrary"),
        name="inproj",
    )(row_starts, x2d, g, w_t, *([w_t] * len(small_pieces)))


def _conv_qk_kernel(x_ref, cw_ref, cb_ref, wq_ref, wk_ref, xc_ref, q_ref, k_ref, pad_ref, *, rows):
    s, c = x_ref.shape
    front = 8
    pad_ref[0:front, :] = jnp.zeros((front, c), F32)
    pad_ref[front:front + s, :] = x_ref[...].astype(F32)
    cw = cw_ref[...]
    cb = cb_ref[...]
    wq = wq_ref[...]
    wk = wk_ref[...]
    k_scale = c ** -0.5
    for r in range(s // rows):
        acc = jnp.broadcast_to(cb, (rows, c))
        for kk in range(M_CONV):
            start = front + r * rows + kk - (M_CONV - 1)
            acc = acc + cw[kk:kk + 1, :] * pad_ref[start:start + rows, :]
        xcb = (acc * _sigmoid(acc)).astype(BF16)
        xc_ref[r * rows:(r + 1) * rows, :] = xcb
        q_ref[r * rows:(r + 1) * rows, :] = _dot(xcb, wq).astype(BF16)
        k_ref[r * rows:(r + 1) * rows, :] = (_dot(xcb, wk) * k_scale).astype(BF16)


def _conv_qk(p, mx_col, conv_w, conv_b, wq, wk, bsz, seq):
    t = p.shape[0]
    heads, hd = wq.shape[0], wq.shape[1]
    width = heads * hd
    blk = pl.BlockSpec((seq, hd), lambda b, h: (b, h))
    out = jax.ShapeDtypeStruct((t, width), BF16)
    return pl.pallas_call(
        functools.partial(_conv_qk_kernel, rows=256),
        grid=(bsz, heads),
        in_specs=[
            pl.BlockSpec((seq, hd), lambda b, h: (b, mx_col // hd + h)),
            pl.BlockSpec((M_CONV, hd), lambda b, h: (0, h)),
            pl.BlockSpec((1, hd), lambda b, h: (0, h)),
            pl.BlockSpec((None, hd, hd), lambda b, h: (h, 0, 0)),
            pl.BlockSpec((None, hd, hd), lambda b, h: (h, 0, 0)),
        ],
        out_specs=[blk, blk, blk],
        out_shape=[out, out, out],
        scratch_shapes=[pltpu.VMEM((seq + 8, hd), F32)],
        compiler_params=_params("parallel", "parallel"),
        name="conv_qk",
    )(p, conv_w, conv_b, wq, wk)


def _log_sigmoid(x):
    return jnp.minimum(x, 0.0) - jnp.log1p(jnp.exp(-jnp.abs(x)))


def _mlstm_kernel(bias_ref, q_ref, k_ref, v_ref, o_ref, z_ref, xc_ref, g_ref, ng_ref, sk_ref,
                  y_ref, c_scr, n_scr, m_scr):
    @pl.when(pl.program_id(1) == 0)
    def _():
        c_scr[...] = jnp.zeros_like(c_scr)
        n_scr[...] = jnp.zeros_like(n_scr)
        m_scr[...] = jnp.zeros_like(m_scr)

    ln = q_ref.shape[0]
    heads, hd = c_scr.shape[0], c_scr.shape[1]
    gates = g_ref[...] + bias_ref[...]
    fb_all = _log_sigmoid(gates[heads:2 * heads, :])
    row = lax.broadcasted_iota(jnp.int32, (ln, ln), 0)
    col = lax.broadcasted_iota(jnp.int32, (ln, ln), 1)
    tril = col <= row
    eye = col == row
    for hh in range(heads):
        sl = slice(hh * hd, (hh + 1) * hd)
        ib = gates[hh:hh + 1, :]
        fb = fb_all[hh:hh + 1, :]
        bcum_c = jnp.sum(jnp.where(tril, fb, 0.0), axis=1, keepdims=True)
        bcum_r = jnp.sum(jnp.where(eye, bcum_c, 0.0), axis=0, keepdims=True)
        g_tot = jnp.sum(fb, axis=1, keepdims=True)
        e_r = ib - bcum_r
        m_st = m_scr[hh]
        d_intra = jnp.where(tril, bcum_c + e_r, NEG_INF)
        d_inter = bcum_c + m_st
        m_t = jnp.maximum(d_inter, jnp.max(d_intra, axis=1, keepdims=True))
        w_intra = jnp.exp(d_intra - m_t)
        w_inter = jnp.exp(d_inter - m_t)

        qb = q_ref[:, sl]
        kb = k_ref[:, sl]
        vb = v_ref[:, sl]
        c_st = c_scr[hh]
        n_st = n_scr[hh]
        sc = _nt_dot(qb, kb) * w_intra
        num = _dot(sc.astype(BF16), vb) + w_inter * _dot(qb, c_st.astype(BF16))
        qn = jnp.sum(qb.astype(F32) * n_st, axis=1, keepdims=True)
        den = jnp.sum(sc, axis=1, keepdims=True) + w_inter * qn
        h = num / jnp.maximum(jnp.abs(den), jnp.exp(-m_t))

        d_state = g_tot + e_r
        m_new = jnp.maximum(g_tot + m_st, jnp.max(d_state, axis=1, keepdims=True))
        w_s_r = jnp.exp(d_state - m_new)
        w_s_c = jnp.sum(jnp.where(eye, w_s_r, 0.0), axis=1, keepdims=True)
        decay = jnp.exp(g_tot + m_st - m_new)
        kw = kb.astype(F32) * w_s_c
        kv = lax.dot_general(kw.astype(BF16), vb, (((0,), (0,)), ((), ())), preferred_element_type=F32)
        c_scr[hh] = decay * c_st + kv
        n_scr[hh] = decay * n_st + jnp.sum(kw, axis=0, keepdims=True)
        m_scr[hh] = m_new

        hg = h * _sigmoid(o_ref[:, sl].astype(F32))
        mu = jnp.mean(hg, axis=-1, keepdims=True)
        hc = hg - mu
        var = jnp.mean(hc * hc, axis=-1, keepdims=True)
        hn = hc * lax.rsqrt(var + LN_EPS)
        z = z_ref[:, sl].astype(F32)
        y = (hn * ng_ref[:, sl] + sk_ref[:, sl] * xc_ref[:, sl].astype(F32)) * (z * _sigmoid(z))
        y_ref[:, sl] = y.astype(y_ref.dtype)


def _mlstm(p, mx_col, xc, q, k, gates, bias, norm_g, skip, bsz, seq):
    t, width = q.shape
    pc = mx_col // width
    heads = M_HEADS
    hd = width // heads
    ln = M_CHUNK
    nc = seq // ln
    rows = lambda b, c: (b * nc + c, 0)
    vec = pl.BlockSpec((1, width), lambda b, c: (0, 0))
    return pl.pallas_call(
        _mlstm_kernel,
        grid=(bsz, nc),
        in_specs=[
            pl.BlockSpec((2 * heads, 1), lambda b, c: (0, 0)),
            pl.BlockSpec((ln, width), rows),
            pl.BlockSpec((ln, width), rows),
            pl.BlockSpec((ln, width), lambda b, c: (b * nc + c, pc + 1)),
            pl.BlockSpec((ln, width), lambda b, c: (b * nc + c, pc + 2)),
            pl.BlockSpec((ln, width), lambda b, c: (b * nc + c, pc + 3)),
            pl.BlockSpec((ln, width), rows),
            pl.BlockSpec((2 * heads, ln), lambda b, c: (0, b * nc + c)),
            vec, vec,
        ],
        out_specs=pl.BlockSpec((ln, width), rows),
        out_shape=jax.ShapeDtypeStruct((t, width), BF16),
        scratch_shapes=[pltpu.VMEM((heads, hd, hd), F32), pltpu.VMEM((heads, 1, hd), F32),
                        pltpu.VMEM((heads, 1, 1), F32)],
        compiler_params=_params("parallel", "arbitrary"),
        name="mlstm",
    )(bias, q, k, p, p, p, xc, gates, norm_g, skip)


def _compress_kernel(x_ref, pe_ref, w1_ref, w2_ref, o_ref, xf_scr):
    ng, n_sub, dh = o_ref.shape
    half = A_CMP_STRIDE * dh
    n_chunk = xf_scr.shape[0]
    for c in range(n_chunk):
        xf_scr[c] = x_ref[:, c * LANES:(c + 1) * LANES].astype(F32)
    xs = [jnp.concatenate([xf_scr[c, pl.ds(r, n_sub, stride=A_CMP_STRIDE), :] for c in range(n_chunk)], axis=1)
          for r in range(A_CMP_STRIDE)]
    pe = pe_ref[...]
    rid = lax.broadcasted_iota(jnp.int32, (n_sub, dh), 0)
    for g in range(ng):
        t = jnp.concatenate([x[:, g * dh:(g + 1) * dh] for x in xs], axis=1)
        za = _dot((t + pe[0:1, :]).astype(BF16), w1_ref[0:half, :])
        zb = _dot((t + pe[1:2, :]).astype(BF16), w1_ref[half:2 * half, :])
        pre = za + pltpu.roll(zb, n_sub - 1, 0)
        hid = (pre * _sigmoid(pre)).astype(BF16)
        out = _dot(hid, w2_ref[...])
        o_ref[g] = jnp.where(rid < n_sub - 1, out, 0.0).astype(o_ref.dtype)


def _compress(p, kc_col, pe, w1, w2, bsz, seq):
    g = A_KV_HEADS
    dh = A_HEAD_DIM
    kvw = g * dh
    n_sub = seq // A_CMP_STRIDE
    half = A_CMP_STRIDE * dh
    hid = w1.shape[-1]
    return pl.pallas_call(
        _compress_kernel,
        grid=(2, bsz),
        in_specs=[
            pl.BlockSpec((seq, kvw), lambda i, b: (b, kc_col // kvw + i)),
            pl.BlockSpec((None, 2, half), lambda i, b: (i, 0, 0)),
            pl.BlockSpec((None, 2 * half, hid), lambda i, b: (i, 0, 0)),
            pl.BlockSpec((None, hid, dh), lambda i, b: (i, 0, 0)),
        ],
        out_specs=pl.BlockSpec((None, None, g, n_sub, dh), lambda i, b: (i, b, 0, 0, 0)),
        out_shape=jax.ShapeDtypeStruct((2, bsz, g, n_sub, dh), BF16),
        scratch_shapes=[pltpu.VMEM((kvw // LANES, seq, LANES), F32)],
        compiler_params=_params("parallel", "parallel"),
        name="compress",
    )(p, pe, w1, w2)


def _mask_heads(x, ok, fill, tq):
    n = x.shape[1] // tq
    return jnp.concatenate([jnp.where(ok, x[:, h * tq:(h + 1) * tq], fill) for h in range(n)], axis=1)


def _with_ones_rows(vt):
    return jnp.concatenate([vt, jnp.ones((BF16_SUBLANES, vt.shape[1]), vt.dtype)], axis=0)


def _nsa_kernel(q_ref, kc_ref, vct_ref, ks_ref, vs_ref, kw_ref, vw_ref, ps_ref, z_ref, o_ref,
                sel_ref, vst_ref, vwt_ref, kn_ref, osw_ref):
    qi = pl.program_id(1)
    tq = q_ref.shape[0]
    dh = A_HEAD_DIM
    hpg = A_GROUP
    ng = A_KV_HEADS
    r = hpg * tq
    n_cmp = kc_ref.shape[1]
    tk = vwt_ref.shape[2]
    kt = SEL_KEYS_PER_STEP
    tiles_per_iter = kt // tk
    seq = ks_ref.shape[0]
    n_sel = seq // A_SEL_BLOCK
    t0 = qi * tq
    da = dh + BF16_SUBLANES

    @pl.when(qi == 0)
    def _():
        for j in range(seq // tk):
            vst_ref[j] = vs_ref[j * tk:(j + 1) * tk, :].astype(F32).T.astype(BF16)
            vwt_ref[j] = vw_ref[j * tk:(j + 1) * tk, :].astype(F32).T.astype(BF16)
        rows = 512
        for which, k_ref in enumerate((ks_ref, kw_ref)):
            best = [jnp.zeros((1, 1), F32) for _ in range(ng)]
            for c in range(seq // rows):
                kf = k_ref[c * rows:(c + 1) * rows, :].astype(F32)
                ksq = kf * kf
                for g in range(ng):
                    n2 = jnp.sum(ksq[:, g * dh:(g + 1) * dh], axis=1, keepdims=True)
                    best[g] = jnp.maximum(best[g], jnp.max(n2, axis=0, keepdims=True))
            for g in range(ng):
                kn_ref[which * ng + g] = best[g]

    qt = (q_ref[...].astype(F32) * (dh ** -0.5 * LOG2_E)).T.astype(BF16)
    q4t = [jnp.concatenate([qt[(g * hpg + h) * dh:(g * hpg + h + 1) * dh, :] for h in range(hpg)], axis=1)
           for g in range(ng)]

    cn = lax.broadcasted_iota(jnp.int32, (n_cmp, tq), 0)
    ct = t0 + lax.broadcasted_iota(jnp.int32, (n_cmp, tq), 1)
    valid = (cn * A_CMP_STRIDE + (A_CMP_BLOCK - 1)) <= ct
    jn = lax.broadcasted_iota(jnp.int32, (n_sel, n_cmp), 0)
    nn = lax.broadcasted_iota(jnp.int32, (n_sel, n_cmp), 1)
    ov = ((nn * A_CMP_STRIDE < (jn + 1) * A_SEL_BLOCK)
          & (nn * A_CMP_STRIDE + (A_CMP_BLOCK - 1) >= jn * A_SEL_BLOCK))
    ov_t = jnp.where(ov, 1.0, 0.0).astype(BF16)
    jr = lax.broadcasted_iota(jnp.int32, (n_sel, tq), 0)
    tl = t0 + lax.broadcasted_iota(jnp.int32, (n_sel, tq), 1)
    cur = lax.shift_right_logical(tl, 6)
    forced = (jr == 0) | (jr == cur) | (jr == cur - 1)
    o_c = []
    for g in range(ng):
        s_m = _mask_heads(_dot(kc_ref[g], q4t[g]), valid, NEG_INF, tq)
        mx = jnp.max(s_m, axis=0, keepdims=True)
        e = _mask_heads(jnp.exp2(s_m - mx), valid, 0.0, tq)
        lsum = jnp.sum(e, axis=0, keepdims=True)
        p_c = e * (1.0 / jnp.where(lsum > 0.0, lsum, 1.0))
        o_c.append(_dot(vct_ref[g], p_c.astype(BF16)))
        p_sum = p_c[:, 0:tq]
        for h in range(1, hpg):
            p_sum = p_sum + p_c[:, h * tq:(h + 1) * tq]
        p_hi = p_sum.astype(BF16)
        r1 = p_sum - p_hi.astype(F32)
        p_mid = r1.astype(BF16)
        p_lo = (r1 - p_mid.astype(F32)).astype(BF16)
        imp_t = _dot(ov_t, p_hi) + _dot(ov_t, p_mid) + _dot(ov_t, p_lo)
        score = jnp.where(forced, POS_INF, jnp.where(jr <= cur, imp_t, NEG_INF))
        rank = jnp.zeros((n_sel, tq), F32)
        for jp in range(n_sel):
            sj = score[jp:jp + 1, :]
            beats = (sj > score) | ((sj == score) & (jr > jp))
            rank = rank + jnp.where(beats, 1.0, 0.0)
        sel_ref[g] = jnp.where(rank < float(min(A_SEL_TOPK, n_sel)), 1.0, 0.0)

    qn = [jnp.sqrt(jnp.sum(jnp.square(q4t[g].astype(F32)), axis=0, keepdims=True)) for g in range(ng)]
    bound_s = [qn[g] * jnp.sqrt(kn_ref[g]) * BOUND_SLACK for g in range(ng)]
    bound_w = [qn[g] * jnp.sqrt(kn_ref[ng + g]) * BOUND_SLACK for g in range(ng)]
    worst = bound_s[0]
    for b in bound_s[1:] + bound_w:
        worst = jnp.maximum(worst, b)
    fast = jnp.max(worst) <= MAX_FAST_BOUND

    n_wb = -(-A_WINDOW // tk)
    wk = (n_wb + 1) * tk
    j0 = jnp.maximum(qi - n_wb, 0)
    off_w = pl.multiple_of(j0 * tk, tk)
    c_abs = off_w + lax.broadcasted_iota(jnp.int32, (wk, tq), 0)
    t_abs_w = t0 + lax.broadcasted_iota(jnp.int32, (wk, tq), 1)
    ok_w = (c_abs <= t_abs_w) & (c_abs > t_abs_w - A_WINDOW)
    blocks_per_iter = kt // A_SEL_BLOCK

    def normalised(acc):
        return acc[0:dh, :] * (1.0 / acc[dh:dh + 1, :])

    def value_tiles(v_ref, first, count, g):
        vt = jnp.concatenate([v_ref[first + i, g * dh:(g + 1) * dh, :] for i in range(count)], axis=1)
        return _with_ones_rows(vt)

    @pl.when(fast)
    def _():
        kcol = lax.broadcasted_iota(jnp.int32, (kt, dh), 1)
        krow = lax.broadcasted_iota(jnp.int32, (kt, dh), 0)
        aux = jnp.where((kcol == 0) | (kcol == SUBLANES + lax.shift_right_logical(krow, 6)), 1.0, 0.0).astype(BF16)
        aux_w = jnp.where(lax.broadcasted_iota(jnp.int32, (wk, dh), 1) == 0, 1.0, 0.0).astype(BF16)
        aux_d = jnp.where(lax.broadcasted_iota(jnp.int32, (tq, dh), 1) == 0, 1.0, 0.0).astype(BF16)
        zrows = jnp.zeros((LANES - dh - BF16_SUBLANES, r), BF16)

        first_row = lax.broadcasted_iota(jnp.int32, (SUBLANES, r), 0) == 0

        def query_aug(g, shift, block_bias):
            top = jnp.where(first_row, -shift, 0.0)
            low = jnp.zeros((SUBLANES, r), F32) if block_bias is None else jnp.concatenate([block_bias] * hpg, axis=1)
            return jnp.concatenate([q4t[g], jnp.concatenate([top, low], axis=0).astype(BF16), zrows], axis=0)

        before_diag = jr < 2 * qi
        for g in range(ng):
            sel_ref[g] = jnp.where((sel_ref[g] > 0.5) & before_diag, 0.0, -EXCLUDE)

        def body(j, accs):
            off = pl.multiple_of(j * kt, kt)
            k_all = ks_ref[pl.ds(off, kt), :]
            out = []
            for g in range(ng):
                k_aug = jnp.concatenate([k_all[:, g * dh:(g + 1) * dh], aux], axis=1)
                bias = sel_ref[g, pl.ds(pl.multiple_of(j * blocks_per_iter, blocks_per_iter), blocks_per_iter), :]
                p = jnp.exp2(_dot(k_aug, query_aug(g, bound_s[g], bias))).astype(BF16)
                out.append(accs[g] + _dot(value_tiles(vst_ref, j * tiles_per_iter, tiles_per_iter, g), p))
            return tuple(out)

        accs = lax.fori_loop(0, (t0 + kt - 1) // kt, body, tuple(jnp.zeros((da, r), F32) for _ in range(ng)))

        kd_all = ks_ref[pl.ds(pl.multiple_of(t0, tq), tq), :]
        causal = lax.broadcasted_iota(jnp.int32, (tq, tq), 0) <= lax.broadcasted_iota(jnp.int32, (tq, tq), 1)
        kw_all = kw_ref[pl.ds(off_w, wk), :]
        for g in range(ng):
            kd_aug = jnp.concatenate([kd_all[:, g * dh:(g + 1) * dh], aux_d], axis=1)
            s_d = _mask_heads(_dot(kd_aug, query_aug(g, bound_s[g], None)), causal, -EXCLUDE, tq)
            acc = accs[g] + _dot(value_tiles(vst_ref, qi, 1, g), jnp.exp2(s_d).astype(BF16))
            osw_ref[0, g] = normalised(acc)
            kw_aug = jnp.concatenate([kw_all[:, g * dh:(g + 1) * dh], aux_w], axis=1)
            s_w = _mask_heads(_dot(kw_aug, query_aug(g, bound_w[g], None)), ok_w, -EXCLUDE, tq)
            p_w = jnp.exp2(s_w).astype(BF16)
            acc_w = _dot(value_tiles(vwt_ref, j0, 1, g), p_w[0:tk, :])
            for i in range(1, n_wb + 1):
                acc_w = acc_w + _dot(value_tiles(vwt_ref, j0 + i, 1, g), p_w[i * tk:(i + 1) * tk, :])
            osw_ref[1, g] = normalised(acc_w)

    @pl.when(jnp.logical_not(fast))
    def _():
        zpad = jnp.zeros((dh, r), BF16)
        q_pad = [jnp.concatenate([q4t[g] if gg == g else zpad for gg in range(ng)], axis=0) for g in range(ng)]
        sub_s = lax.broadcasted_iota(jnp.int32, (kt, tq), 0)
        t_abs_s = t0 + lax.broadcasted_iota(jnp.int32, (kt, tq), 1)

        def body(j, carry):
            off = pl.multiple_of(j * kt, kt)
            k_all = ks_ref[pl.ds(off, kt), :]
            causal = (off + sub_s) <= t_abs_s
            out = []
            for g in range(ng):
                m, acc = carry[g]
                s = _dot(k_all, q_pad[g])
                flags = jnp.concatenate(
                    [jnp.broadcast_to(sel_ref[g, pl.ds(j * blocks_per_iter + i, 1), :], (A_SEL_BLOCK, tq))
                     for i in range(blocks_per_iter)], axis=0)
                s = _mask_heads(s, (flags > 0.5) & causal, NEG_INF, tq)
                m_new = jnp.maximum(m, jnp.max(s, axis=0, keepdims=True))
                p = jnp.exp2(s - m_new).astype(BF16)
                vt = value_tiles(vst_ref, j * tiles_per_iter, tiles_per_iter, g)
                out.append((m_new, jnp.exp2(m - m_new) * acc + _dot(vt, p)))
            return tuple(out)

        init = tuple((jnp.full((1, r), NEG_INF, F32), jnp.zeros((da, r), F32)) for _ in range(ng))
        state = lax.fori_loop(0, (t0 + tq - 1) // kt + 1, body, init)
        kw_all = kw_ref[pl.ds(off_w, wk), :]
        for g in range(ng):
            osw_ref[0, g] = normalised(state[g][1])
            s_w = _mask_heads(_dot(kw_all, q_pad[g]), ok_w, NEG_INF, tq)
            p_w = jnp.exp2(s_w - jnp.max(s_w, axis=0, keepdims=True)).astype(BF16)
            acc_w = _dot(value_tiles(vwt_ref, j0, 1, g), p_w[0:tk, :])
            for i in range(1, n_wb + 1):
                acc_w = acc_w + _dot(value_tiles(vwt_ref, j0 + i, 1, g), p_w[i * tk:(i + 1) * tk, :])
            osw_ref[1, g] = normalised(acc_w)

    gs = _sigmoid(ps_ref[...])
    g0 = 2 * M_HEADS
    outs = []
    for g in range(ng):
        o_s = osw_ref[0, g]
        o_w = osw_ref[1, g]
        for h in range(hpg):
            sl = slice(h * tq, (h + 1) * tq)
            gr = g0 + 3 * (g * hpg + h)
            outs.append(gs[gr:gr + 1, :] * o_c[g][:, sl] + gs[gr + 1:gr + 2, :] * o_s[:, sl]
                        + gs[gr + 2:gr + 3, :] * o_w[:, sl])
    o_all = jnp.concatenate(outs, axis=0).T
    z = z_ref[...].astype(F32)
    o_ref[...] = (o_all * (z * _sigmoid(z))).astype(o_ref.dtype)


def _nsa(p, ps, kvc, vct, q_col, kv_col, z_col, bsz, seq):
    t = p.shape[0]
    g = A_KV_HEADS
    dh = A_HEAD_DIM
    aw = A_HEADS * dh
    kvw = g * dh
    tq = Q_BLOCK
    nq = seq // tq
    n_cmp = kvc.shape[3]
    tk = LANES
    rows = lambda b, qi: b * nq + qi
    kv_piece = lambda i: pl.BlockSpec((seq, kvw), lambda b, qi: (b, kv_col // kvw + i))
    return pl.pallas_call(
        _nsa_kernel,
        grid=(bsz, nq),
        in_specs=[
            pl.BlockSpec((tq, aw), lambda b, qi: (rows(b, qi), q_col // aw)),
            pl.BlockSpec((None, None, g, n_cmp, dh), lambda b, qi: (0, b, 0, 0, 0)),
            pl.BlockSpec((None, g, dh, n_cmp), lambda b, qi: (b, 0, 0, 0)),
            kv_piece(2), kv_piece(3), kv_piece(4), kv_piece(5),
            pl.BlockSpec((LANES, tq), lambda b, qi: (0, rows(b, qi))),
            pl.BlockSpec((tq, aw), lambda b, qi: (rows(b, qi), z_col // aw)),
        ],
        out_specs=pl.BlockSpec((tq, aw), lambda b, qi: (rows(b, qi), 0)),
        out_shape=jax.ShapeDtypeStruct((t, aw), BF16),
        scratch_shapes=[pltpu.VMEM((g, seq // A_SEL_BLOCK, tq), F32),
                        pltpu.VMEM((seq // tk, kvw, tk), BF16),
                        pltpu.VMEM((seq // tk, kvw, tk), BF16),
                        pltpu.VMEM((2 * g, 1, 1), F32),
                        pltpu.VMEM((2, g, dh, A_GROUP * tq), F32)],
        compiler_params=_params("parallel", "arbitrary"),
        name="nsa",
    )(p, kvc, vct, p, p, p, p, ps, p)


def _merge_out_kernel(ym_ref, ya_ref, wm_ref, wa_ref, gm_ref, ga_ref, wo_ref, g_ref, x_ref, o_ref):
    um = _dot(ym_ref[...], wm_ref[...])
    ua = _dot(ya_ref[...], wa_ref[...])
    mg = _sigmoid(gm_ref[...].astype(F32)) * um + _sigmoid(ga_ref[...].astype(F32)) * ua
    out = _dot(mg.astype(BF16), wo_ref[...])
    ms = jnp.mean(out * out, axis=-1, keepdims=True)
    o_ref[...] = x_ref[...] + out * lax.rsqrt(ms + RMS_EPS) * g_ref[...]


def _merge_out(ym, ya, wm, wa, p, gm_col, ga_col, wo, g, x2d, tm=256):
    t, d = x2d.shape
    kdim = ym.shape[1]
    resident = lambda shape: pl.BlockSpec(shape, lambda i: (0, 0), pipeline_mode=pl.Buffered(1))
    return pl.pallas_call(
        _merge_out_kernel,
        grid=(t // tm,),
        in_specs=[
            pl.BlockSpec((tm, kdim), lambda i: (i, 0)),
            pl.BlockSpec((tm, kdim), lambda i: (i, 0)),
            resident((kdim, d)),
            resident((kdim, d)),
            pl.BlockSpec((tm, d), lambda i: (i, gm_col // d)),
            pl.BlockSpec((tm, d), lambda i: (i, ga_col // d)),
            resident((d, d)),
            pl.BlockSpec((1, d), lambda i: (0, 0)),
            pl.BlockSpec((tm, d), lambda i: (i, 0)),
        ],
        out_specs=pl.BlockSpec((tm, d), lambda i: (i, 0)),
        out_shape=jax.ShapeDtypeStruct((t, d), F32),
        compiler_params=_params("parallel"),
        name="merge_out",
    )(ym, ya, wm, wa, p, p, wo, g, x2d)


def _layer(x, pre_g, w_in, conv_w, conv_b, w_q, w_k, b_i, b_f, norm_g, skip,
           pe_k, w1_k, w2_k, pe_v, w1_v, w2_v, w_up_m, w_up_a, w_out, post_g):
    bsz, seq, d = x.shape
    t = bsz * seq
    mw = w_up_m.shape[0]
    aw = w_up_a.shape[0]
    g = A_KV_HEADS
    dh = A_HEAD_DIM
    kvw = g * dh
    n_gate = 3 * A_HEADS

    o_i = 4 * mw
    o_f = o_i + M_HEADS
    o_q = o_f + M_HEADS
    o_g = o_q + aw + 6 * kvw
    o_z = o_g + n_gate
    o_m = o_z + aw
    o_kv = o_q + aw
    main_pieces = ((o_m, w_in.shape[1]), (0, o_i), (o_q, o_kv), (o_z, o_m), (o_kv, o_g))
    w_t = w_in.T
    row_starts = jnp.asarray([r // SUBLANES for lo, hi in main_pieces for r in range(lo, hi, INPROJ_TN)], jnp.int32)
    small_pieces = ((o_i, o_q), (o_g, o_z))
    gm_col = 0
    ga_col = d
    mx_col = 2 * d
    q_col = mx_col + 4 * mw
    z_col = q_col + aw
    kv_col = z_col + aw

    x2d = x.reshape(t, d)
    p, ps = _inproj(x2d, pre_g.reshape(1, d), w_t, row_starts, small_pieces, tn=INPROJ_TN)

    xc, q, k = _conv_qk(p, mx_col, conv_w, conv_b.reshape(1, mw), w_q.astype(BF16), w_k.astype(BF16), bsz, seq)
    nc = seq // M_CHUNK
    bias_m = jnp.concatenate([b_i, b_f]).reshape(2 * M_HEADS, 1)
    y_m = _mlstm(p, mx_col, xc, q, k, ps, bias_m, norm_g.reshape(1, mw), skip.reshape(1, mw), bsz, seq)

    half = A_CMP_STRIDE * dh
    pe = jnp.stack([pe_k.reshape(2, half), pe_v.reshape(2, half)])
    w1 = jnp.stack([w1_k, w1_v]).astype(BF16)
    w2 = jnp.stack([w2_k, w2_v]).astype(BF16)
    kvc = _compress(p, kv_col, pe, w1, w2, bsz, seq)
    vct = jnp.swapaxes(kvc[1], 2, 3)
    y_a = _nsa(p, ps, kvc, vct, q_col, kv_col, z_col, bsz, seq)

    out = _merge_out(y_m, y_a, w_up_m.astype(BF16), w_up_a.astype(BF16), p, gm_col, ga_col,
                     w_out.astype(BF16), post_g.reshape(1, d), x2d)
    return out.reshape(bsz, seq, d)


def kernel(x, pre_norm_g, w_in, m_conv_w, m_conv_b, m_w_q, m_w_k, m_b_i, m_b_f, m_norm_g, m_skip, a_pe_k, a_w1_k, a_w2_k, a_pe_v, a_w1_v, a_w2_v, w_up_m, w_up_a, w_out, post_norm_g):
    depth = w_in.shape[0]
    for l in range(depth):
        x = _layer(x, pre_norm_g[l], w_in[l], m_conv_w[l], m_conv_b[l], m_w_q[l], m_w_k[l], m_b_i[l], m_b_f[l],
                   m_norm_g[l], m_skip[l], a_pe_k[l], a_w1_k[l], a_w2_k[l], a_pe_v[l], a_w1_v[l], a_w2_v[l],
                   w_up_m[l], w_up_a[l], w_out[l], post_norm_g[l])
    return x
```

```python
import functools

import jax
import jax.numpy as jnp
from jax import lax
from jax.experimental import pallas as pl
from jax.experimental.pallas import tpu as pltpu

F32 = jnp.float32
BF16 = jnp.bfloat16

M_HEADS = 4
M_CONV = 4
M_CHUNK = 128
A_HEADS = 16
A_KV_HEADS = 4
A_GROUP = A_HEADS // A_KV_HEADS
A_HEAD_DIM = 64
A_CMP_BLOCK = 32
A_CMP_STRIDE = 16
A_SEL_BLOCK = 64
A_SEL_TOPK = 8
A_WINDOW = 512
Q_BLOCK = 128
SEL_KEYS_PER_STEP = 512
EXCLUDE = 1e4
MAX_FAST_BOUND = 40.0
SEL_BLOCK_SHIFT = A_SEL_BLOCK.bit_length() - 1
KEY_NORM_ROWS = 512
BOUND_SLACK = 1.01
INPROJ_TN = 512
RMS_EPS = 1e-6
LN_EPS = 1e-6
NEG_INF = -1e30
POS_INF = 1e30

LANES = 128
SUBLANES = 8
BF16_SUBLANES = 16
LOG2_E = 1.4426950408889634
VMEM_LIMIT_BYTES = 48 * 1024 * 1024


def _nt_dot(a, b):
    return lax.dot_general(a, b, (((1,), (1,)), ((), ())), preferred_element_type=F32)


def _dot(a, b):
    return jnp.dot(a, b, preferred_element_type=F32)


def _sigmoid(x):
    return 1.0 / (1.0 + jnp.exp(-x))


def _params(*sem):
    return pltpu.CompilerParams(dimension_semantics=sem, vmem_limit_bytes=VMEM_LIMIT_BYTES)


def _inproj_kernel(rows_ref, x_ref, g_ref, wt_ref, *rest):
    del rows_ref
    o_ref, ost_ref, h_scr = rest[-3:]
    small_refs = rest[:-3]

    @pl.when(pl.program_id(1) == 0)
    def _():
        xf = x_ref[...]
        ms = jnp.mean(xf * xf, axis=-1, keepdims=True)
        hb = (xf * lax.rsqrt(ms + RMS_EPS) * g_ref[...]).astype(BF16)
        h_scr[...] = hb
        rows = [r[...] for r in small_refs]
        used = sum(r.shape[0] for r in rows)
        rows.append(jnp.zeros((ost_ref.shape[0] - used, xf.shape[1]), F32))
        ost_ref[...] = _nt_dot(jnp.concatenate(rows, axis=0).astype(BF16), hb)

    o_ref[...] = _nt_dot(h_scr[...], wt_ref[...].astype(BF16)).astype(o_ref.dtype)


def _inproj(x2d, g, w_t, row_starts, small_pieces, tm=1024, tn=512):
    t, d = x2d.shape
    n = row_starts.shape[0] * tn
    ns = LANES
    rows_of = lambda lo, hi: pl.BlockSpec((pl.Element(hi - lo), pl.Element(d)), lambda i, j, rows: (lo, 0))
    grid_spec = pltpu.PrefetchScalarGridSpec(
        num_scalar_prefetch=1,
        grid=(t // tm, n // tn),
        in_specs=[
            pl.BlockSpec((tm, d), lambda i, j, rows: (i, 0)),
            pl.BlockSpec((1, d), lambda i, j, rows: (0, 0)),
            pl.BlockSpec((pl.Element(tn), pl.Element(d)), lambda i, j, rows: (rows[j] * SUBLANES, 0)),
        ] + [rows_of(lo, hi) for lo, hi in small_pieces],
        out_specs=[
            pl.BlockSpec((tm, tn), lambda i, j, rows: (i, j)),
            pl.BlockSpec((ns, tm), lambda i, j, rows: (0, i)),
        ],
        scratch_shapes=[pltpu.VMEM((tm, d), BF16)],
    )
    return pl.pallas_call(
        _inproj_kernel,
        grid_spec=grid_spec,
        out_shape=[jax.ShapeDtypeStruct((t, n), BF16), jax.ShapeDtypeStruct((ns, t), F32)],
        compiler_params=_params("parallel", "arbitrary"),
        name="inproj",
    )(row_starts, x2d, g, w_t, *([w_t] * len(small_pieces)))


def _conv_qk_kernel(x_ref, cw_ref, cb_ref, wq_ref, wk_ref, xc_ref, q_ref, k_ref, pad_ref, *, rows):
    s, c = x_ref.shape
    front = pad_ref.shape[0] - s
    pad_ref[0:front, :] = jnp.zeros((front, c), F32)
    pad_ref[front:front + s, :] = x_ref[...].astype(F32)
    cw = cw_ref[...]
    cb = cb_ref[...]
    wq = wq_ref[...]
    wk = wk_ref[...]
    k_scale = c ** -0.5
    for r in range(s // rows):
        acc = jnp.broadcast_to(cb, (rows, c))
        for kk in range(M_CONV):
            start = front + r * rows + kk - (M_CONV - 1)
            acc = acc + cw[kk:kk + 1, :] * pad_ref[start:start + rows, :]
        xcb = (acc * _sigmoid(acc)).astype(BF16)
        xc_ref[r * rows:(r + 1) * rows, :] = xcb
        q_ref[r * rows:(r + 1) * rows, :] = _dot(xcb, wq).astype(BF16)
        k_ref[r * rows:(r + 1) * rows, :] = (_dot(xcb, wk) * k_scale).astype(BF16)


def _conv_qk(p, mx_col, conv_w, conv_b, wq, wk, bsz, seq):
    t = p.shape[0]
    heads, hd = wq.shape[0], wq.shape[1]
    width = heads * hd
    blk = pl.BlockSpec((seq, hd), lambda b, h: (b, h))
    out = jax.ShapeDtypeStruct((t, width), BF16)
    return pl.pallas_call(
        functools.partial(_conv_qk_kernel, rows=256),
        grid=(bsz, heads),
        in_specs=[
            pl.BlockSpec((seq, hd), lambda b, h: (b, mx_col // hd + h)),
            pl.BlockSpec((M_CONV, hd), lambda b, h: (0, h)),
            pl.BlockSpec((1, hd), lambda b, h: (0, h)),
            pl.BlockSpec((None, hd, hd), lambda b, h: (h, 0, 0)),
            pl.BlockSpec((None, hd, hd), lambda b, h: (h, 0, 0)),
        ],
        out_specs=[blk, blk, blk],
        out_shape=[out, out, out],
        scratch_shapes=[pltpu.VMEM((seq + SUBLANES, hd), F32)],
        compiler_params=_params("parallel", "parallel"),
        name="conv_qk",
    )(p, conv_w, conv_b, wq, wk)


def _log_sigmoid(x):
    return jnp.minimum(x, 0.0) - jnp.log1p(jnp.exp(-jnp.abs(x)))


def _mlstm_kernel(bias_ref, q_ref, k_ref, v_ref, o_ref, z_ref, xc_ref, g_ref, ng_ref, sk_ref,
                  y_ref, c_scr, n_scr, m_scr):
    @pl.when(pl.program_id(1) == 0)
    def _():
        c_scr[...] = jnp.zeros_like(c_scr)
        n_scr[...] = jnp.zeros_like(n_scr)
        m_scr[...] = jnp.zeros_like(m_scr)

    ln = q_ref.shape[0]
    heads, hd = c_scr.shape[0], c_scr.shape[1]
    gates = g_ref[...] + bias_ref[...]
    fb_all = _log_sigmoid(gates[heads:2 * heads, :])
    row = lax.broadcasted_iota(jnp.int32, (ln, ln), 0)
    col = lax.broadcasted_iota(jnp.int32, (ln, ln), 1)
    tril = col <= row
    eye = col == row
    for hh in range(heads):
        sl = slice(hh * hd, (hh + 1) * hd)
        ib = gates[hh:hh + 1, :]
        fb = fb_all[hh:hh + 1, :]
        bcum_c = jnp.sum(jnp.where(tril, fb, 0.0), axis=1, keepdims=True)
        bcum_r = jnp.sum(jnp.where(eye, bcum_c, 0.0), axis=0, keepdims=True)
        g_tot = jnp.sum(fb, axis=1, keepdims=True)
        e_r = ib - bcum_r
        m_st = m_scr[hh]
        d_intra = jnp.where(tril, bcum_c + e_r, NEG_INF)
        d_inter = bcum_c + m_st
        m_t = jnp.maximum(d_inter, jnp.max(d_intra, axis=1, keepdims=True))
        w_intra = jnp.exp(d_intra - m_t)
        w_inter = jnp.exp(d_inter - m_t)

        qb = q_ref[:, sl]
        kb = k_ref[:, sl]
        vb = v_ref[:, sl]
        c_st = c_scr[hh]
        n_st = n_scr[hh]
        sc = _nt_dot(qb, kb) * w_intra
        num = _dot(sc.astype(BF16), vb) + w_inter * _dot(qb, c_st.astype(BF16))
        qn = jnp.sum(qb.astype(F32) * n_st, axis=1, keepdims=True)
        den = jnp.sum(sc, axis=1, keepdims=True) + w_inter * qn
        h = num / jnp.maximum(jnp.abs(den), jnp.exp(-m_t))

        d_state = g_tot + e_r
        m_new = jnp.maximum(g_tot + m_st, jnp.max(d_state, axis=1, keepdims=True))
        w_s_r = jnp.exp(d_state - m_new)
        w_s_c = jnp.sum(jnp.where(eye, w_s_r, 0.0), axis=1, keepdims=True)
        decay = jnp.exp(g_tot + m_st - m_new)
        kw = kb.astype(F32) * w_s_c
        kv = lax.dot_general(kw.astype(BF16), vb, (((0,), (0,)), ((), ())), preferred_element_type=F32)
        c_scr[hh] = decay * c_st + kv
        n_scr[hh] = decay * n_st + jnp.sum(kw, axis=0, keepdims=True)
        m_scr[hh] = m_new

        hg = h * _sigmoid(o_ref[:, sl].astype(F32))
        mu = jnp.mean(hg, axis=-1, keepdims=True)
        hc = hg - mu
        var = jnp.mean(hc * hc, axis=-1, keepdims=True)
        hn = hc * lax.rsqrt(var + LN_EPS)
        z = z_ref[:, sl].astype(F32)
        y = (hn * ng_ref[:, sl] + sk_ref[:, sl] * xc_ref[:, sl].astype(F32)) * (z * _sigmoid(z))
        y_ref[:, sl] = y.astype(y_ref.dtype)


def _mlstm(p, mx_col, xc, q, k, gates, bias, norm_g, skip, bsz, seq):
    t, width = q.shape
    pc = mx_col // width
    heads = M_HEADS
    hd = width // heads
    ln = M_CHUNK
    nc = seq // ln
    rows = lambda b, c: (b * nc + c, 0)
    vec = pl.BlockSpec((1, width), lambda b, c: (0, 0))
    return pl.pallas_call(
        _mlstm_kernel,
        grid=(bsz, nc),
        in_specs=[
            pl.BlockSpec((2 * heads, 1), lambda b, c: (0, 0)),
            pl.BlockSpec((ln, width), rows),
            pl.BlockSpec((ln, width), rows),
            pl.BlockSpec((ln, width), lambda b, c: (b * nc + c, pc + 1)),
            pl.BlockSpec((ln, width), lambda b, c: (b * nc + c, pc + 2)),
            pl.BlockSpec((ln, width), lambda b, c: (b * nc + c, pc + 3)),
            pl.BlockSpec((ln, width), rows),
            pl.BlockSpec((2 * heads, ln), lambda b, c: (0, b * nc + c)),
            vec, vec,
        ],
        out_specs=pl.BlockSpec((ln, width), rows),
        out_shape=jax.ShapeDtypeStruct((t, width), BF16),
        scratch_shapes=[pltpu.VMEM((heads, hd, hd), F32), pltpu.VMEM((heads, 1, hd), F32),
                        pltpu.VMEM((heads, 1, 1), F32)],
        compiler_params=_params("parallel", "arbitrary"),
        name="mlstm",
    )(bias, q, k, p, p, p, xc, gates, norm_g, skip)


def _compress_kernel(x_ref, pe_ref, w1_ref, w2_ref, o_ref, xf_scr):
    ng, n_sub, dh = o_ref.shape
    half = A_CMP_STRIDE * dh
    n_chunk = xf_scr.shape[0]
    for c in range(n_chunk):
        xf_scr[c] = x_ref[:, c * LANES:(c + 1) * LANES].astype(F32)
    xs = [jnp.concatenate([xf_scr[c, pl.ds(r, n_sub, stride=A_CMP_STRIDE), :] for c in range(n_chunk)], axis=1)
          for r in range(A_CMP_STRIDE)]
    pe = pe_ref[...]
    rid = lax.broadcasted_iota(jnp.int32, (n_sub, dh), 0)
    for g in range(ng):
        t = jnp.concatenate([x[:, g * dh:(g + 1) * dh] for x in xs], axis=1)
        za = _dot((t + pe[0:1, :]).astype(BF16), w1_ref[0:half, :])
        zb = _dot((t + pe[1:2, :]).astype(BF16), w1_ref[half:2 * half, :])
        pre = za + pltpu.roll(zb, n_sub - 1, 0)
        hid = (pre * _sigmoid(pre)).astype(BF16)
        out = _dot(hid, w2_ref[...])
        o_ref[g] = jnp.where(rid < n_sub - 1, out, 0.0).astype(o_ref.dtype)


def _compress(p, kc_col, pe, w1, w2, bsz, seq):
    g = A_KV_HEADS
    dh = A_HEAD_DIM
    kvw = g * dh
    n_sub = seq // A_CMP_STRIDE
    half = A_CMP_STRIDE * dh
    hid = w1.shape[-1]
    return pl.pallas_call(
        _compress_kernel,
        grid=(2, bsz),
        in_specs=[
            pl.BlockSpec((seq, kvw), lambda i, b: (b, kc_col // kvw + i)),
            pl.BlockSpec((None, 2, half), lambda i, b: (i, 0, 0)),
            pl.BlockSpec((None, 2 * half, hid), lambda i, b: (i, 0, 0)),
            pl.BlockSpec((None, hid, dh), lambda i, b: (i, 0, 0)),
        ],
        out_specs=pl.BlockSpec((None, None, g, n_sub, dh), lambda i, b: (i, b, 0, 0, 0)),
        out_shape=jax.ShapeDtypeStruct((2, bsz, g, n_sub, dh), BF16),
        scratch_shapes=[pltpu.VMEM((kvw // LANES, seq, LANES), F32)],
        compiler_params=_params("parallel", "parallel"),
        name="compress",
    )(p, pe, w1, w2)


def _mask_heads(x, ok, fill, tq):
    n = x.shape[1] // tq
    return jnp.concatenate([jnp.where(ok, x[:, h * tq:(h + 1) * tq], fill) for h in range(n)], axis=1)


def _with_ones_rows(vt):
    return jnp.concatenate([vt, jnp.ones((BF16_SUBLANES, vt.shape[1]), vt.dtype)], axis=0)


def _nsa_kernel(q_ref, kc_ref, vct_ref, ks_ref, vs_ref, kw_ref, vw_ref, ps_ref, z_ref, o_ref,
                sel_ref, vst_ref, vwt_ref, kn_ref, osw_ref):
    qi = pl.program_id(1)
    tq = q_ref.shape[0]
    dh = A_HEAD_DIM
    hpg = A_GROUP
    ng = A_KV_HEADS
    r = hpg * tq
    n_cmp = kc_ref.shape[1]
    tk = vwt_ref.shape[2]
    kt = SEL_KEYS_PER_STEP
    tiles_per_iter = kt // tk
    seq = ks_ref.shape[0]
    n_sel = seq // A_SEL_BLOCK
    t0 = qi * tq
    da = dh + BF16_SUBLANES

    @pl.when(qi == 0)
    def _():
        for j in range(seq // tk):
            vst_ref[j] = vs_ref[j * tk:(j + 1) * tk, :].astype(F32).T.astype(BF16)
            vwt_ref[j] = vw_ref[j * tk:(j + 1) * tk, :].astype(F32).T.astype(BF16)
        rows = KEY_NORM_ROWS
        for which, k_ref in enumerate((ks_ref, kw_ref)):
            best = [jnp.zeros((1, 1), F32) for _ in range(ng)]
            for c in range(seq // rows):
                kf = k_ref[c * rows:(c + 1) * rows, :].astype(F32)
                ksq = kf * kf
                for g in range(ng):
                    n2 = jnp.sum(ksq[:, g * dh:(g + 1) * dh], axis=1, keepdims=True)
                    best[g] = jnp.maximum(best[g], jnp.max(n2, axis=0, keepdims=True))
            for g in range(ng):
                kn_ref[which * ng + g] = best[g]

    qt = (q_ref[...].astype(F32) * (dh ** -0.5 * LOG2_E)).T.astype(BF16)
    q4t = [jnp.concatenate([qt[(g * hpg + h) * dh:(g * hpg + h + 1) * dh, :] for h in range(hpg)], axis=1)
           for g in range(ng)]

    cn = lax.broadcasted_iota(jnp.int32, (n_cmp, tq), 0)
    ct = t0 + lax.broadcasted_iota(jnp.int32, (n_cmp, tq), 1)
    valid = (cn * A_CMP_STRIDE + (A_CMP_BLOCK - 1)) <= ct
    jn = lax.broadcasted_iota(jnp.int32, (n_sel, n_cmp), 0)
    nn = lax.broadcasted_iota(jnp.int32, (n_sel, n_cmp), 1)
    ov = ((nn * A_CMP_STRIDE < (jn + 1) * A_SEL_BLOCK)
          & (nn * A_CMP_STRIDE + (A_CMP_BLOCK - 1) >= jn * A_SEL_BLOCK))
    ov_t = jnp.where(ov, 1.0, 0.0).astype(BF16)
    jr = lax.broadcasted_iota(jnp.int32, (n_sel, tq), 0)
    tl = t0 + lax.broadcasted_iota(jnp.int32, (n_sel, tq), 1)
    cur = lax.shift_right_logical(tl, SEL_BLOCK_SHIFT)
    forced = (jr == 0) | (jr == cur) | (jr == cur - 1)
    o_c = []
    for g in range(ng):
        s_m = _mask_heads(_dot(kc_ref[g], q4t[g]), valid, NEG_INF, tq)
        mx = jnp.max(s_m, axis=0, keepdims=True)
        e = _mask_heads(jnp.exp2(s_m - mx), valid, 0.0, tq)
        lsum = jnp.sum(e, axis=0, keepdims=True)
        p_c = e * (1.0 / jnp.where(lsum > 0.0, lsum, 1.0))
        o_c.append(_dot(vct_ref[g], p_c.astype(BF16)))
        p_sum = p_c[:, 0:tq]
        for h in range(1, hpg):
            p_sum = p_sum + p_c[:, h * tq:(h + 1) * tq]
        p_hi = p_sum.astype(BF16)
        r1 = p_sum - p_hi.astype(F32)
        p_mid = r1.astype(BF16)
        p_lo = (r1 - p_mid.astype(F32)).astype(BF16)
        imp_t = _dot(ov_t, p_hi) + _dot(ov_t, p_mid) + _dot(ov_t, p_lo)
        score = jnp.where(forced, POS_INF, jnp.where(jr <= cur, imp_t, NEG_INF))
        rank = jnp.zeros((n_sel, tq), F32)
        for jp in range(n_sel):
            sj = score[jp:jp + 1, :]
            beats = (sj > score) | ((sj == score) & (jr > jp))
            rank = rank + jnp.where(beats, 1.0, 0.0)
        sel_ref[g] = jnp.where(rank < float(min(A_SEL_TOPK, n_sel)), 1.0, 0.0)

    qn = [jnp.sqrt(jnp.sum(jnp.square(q4t[g].astype(F32)), axis=0, keepdims=True)) for g in range(ng)]
    bound_s = [qn[g] * jnp.sqrt(kn_ref[g]) * BOUND_SLACK for g in range(ng)]
    bound_w = [qn[g] * jnp.sqrt(kn_ref[ng + g]) * BOUND_SLACK for g in range(ng)]
    worst = bound_s[0]
    for b in bound_s[1:] + bound_w:
        worst = jnp.maximum(worst, b)
    fast = jnp.max(worst) <= MAX_FAST_BOUND

    n_wb = -(-A_WINDOW // tk)
    wk = (n_wb + 1) * tk
    j0 = jnp.maximum(qi - n_wb, 0)
    off_w = pl.multiple_of(j0 * tk, tk)
    c_abs = off_w + lax.broadcasted_iota(jnp.int32, (wk, tq), 0)
    t_abs_w = t0 + lax.broadcasted_iota(jnp.int32, (wk, tq), 1)
    ok_w = (c_abs <= t_abs_w) & (c_abs > t_abs_w - A_WINDOW)
    blocks_per_iter = kt // A_SEL_BLOCK

    def normalised(acc):
        return acc[0:dh, :] * (1.0 / acc[dh:dh + 1, :])

    def value_tiles(v_ref, first, count, g):
        vt = jnp.concatenate([v_ref[first + i, g * dh:(g + 1) * dh, :] for i in range(count)], axis=1)
        return _with_ones_rows(vt)

    @pl.when(fast)
    def _():
        kcol = lax.broadcasted_iota(jnp.int32, (kt, dh), 1)
        krow = lax.broadcasted_iota(jnp.int32, (kt, dh), 0)
        kblock = lax.shift_right_logical(krow, SEL_BLOCK_SHIFT)
        aux = jnp.where((kcol == 0) | (kcol == SUBLANES + kblock), 1.0, 0.0).astype(BF16)
        aux_w = jnp.where(lax.broadcasted_iota(jnp.int32, (wk, dh), 1) == 0, 1.0, 0.0).astype(BF16)
        aux_d = jnp.where(lax.broadcasted_iota(jnp.int32, (tq, dh), 1) == 0, 1.0, 0.0).astype(BF16)
        zrows = jnp.zeros((LANES - dh - BF16_SUBLANES, r), BF16)

        first_row = lax.broadcasted_iota(jnp.int32, (SUBLANES, r), 0) == 0

        def query_aug(g, shift, block_bias):
            top = jnp.where(first_row, -shift, 0.0)
            low = jnp.zeros((SUBLANES, r), F32) if block_bias is None else jnp.concatenate([block_bias] * hpg, axis=1)
            return jnp.concatenate([q4t[g], jnp.concatenate([top, low], axis=0).astype(BF16), zrows], axis=0)

        before_diag = jr < 2 * qi
        for g in range(ng):
            sel_ref[g] = jnp.where((sel_ref[g] > 0.5) & before_diag, 0.0, -EXCLUDE)

        def step(j, accs):
            k_all = ks_ref[j * kt:(j + 1) * kt, :]
            out = []
            for g in range(ng):
                k_aug = jnp.concatenate([k_all[:, g * dh:(g + 1) * dh], aux], axis=1)
                bias = sel_ref[g, j * blocks_per_iter:(j + 1) * blocks_per_iter, :]
                p = jnp.exp2(_dot(k_aug, query_aug(g, bound_s[g], bias))).astype(BF16)
                out.append(accs[g] + _dot(value_tiles(vst_ref, j * tiles_per_iter, tiles_per_iter, g), p))
            return tuple(out)

        def finish(accs):
            kd_all = ks_ref[pl.ds(pl.multiple_of(t0, tq), tq), :]
            causal = (lax.broadcasted_iota(jnp.int32, (tq, tq), 0)
                      <= lax.broadcasted_iota(jnp.int32, (tq, tq), 1))
            kw_all = kw_ref[pl.ds(off_w, wk), :]
            for g in range(ng):
                kd_aug = jnp.concatenate([kd_all[:, g * dh:(g + 1) * dh], aux_d], axis=1)
                s_d = _mask_heads(_dot(kd_aug, query_aug(g, bound_s[g], None)), causal, -EXCLUDE, tq)
                acc = accs[g] + _dot(value_tiles(vst_ref, qi, 1, g), jnp.exp2(s_d).astype(BF16))
                osw_ref[0, g] = normalised(acc)
                kw_aug = jnp.concatenate([kw_all[:, g * dh:(g + 1) * dh], aux_w], axis=1)
                s_w = _mask_heads(_dot(kw_aug, query_aug(g, bound_w[g], None)), ok_w, -EXCLUDE, tq)
                p_w = jnp.exp2(s_w).astype(BF16)
                acc_w = _dot(value_tiles(vwt_ref, j0, 1, g), p_w[0:tk, :])
                for i in range(1, n_wb + 1):
                    acc_w = acc_w + _dot(value_tiles(vwt_ref, j0 + i, 1, g), p_w[i * tk:(i + 1) * tk, :])
                osw_ref[1, g] = normalised(acc_w)

        steps_needed = (t0 + kt - 1) // kt
        for n_steps in range(seq // kt + 1):
            @pl.when(steps_needed == n_steps)
            def _(n_steps=n_steps):
                accs = tuple(jnp.zeros((da, r), F32) for _ in range(ng))
                for j in range(n_steps):
                    accs = step(j, accs)
                finish(accs)

    @pl.when(jnp.logical_not(fast))
    def _():
        zpad = jnp.zeros((dh, r), BF16)
        q_pad = [jnp.concatenate([q4t[g] if gg == g else zpad for gg in range(ng)], axis=0) for g in range(ng)]
        sub_s = lax.broadcasted_iota(jnp.int32, (kt, tq), 0)
        t_abs_s = t0 + lax.broadcasted_iota(jnp.int32, (kt, tq), 1)

        def body(j, carry):
            off = pl.multiple_of(j * kt, kt)
            k_all = ks_ref[pl.ds(off, kt), :]
            causal = (off + sub_s) <= t_abs_s
            out = []
            for g in range(ng):
                m, acc = carry[g]
                s = _dot(k_all, q_pad[g])
                flags = jnp.concatenate(
                    [jnp.broadcast_to(sel_ref[g, pl.ds(j * blocks_per_iter + i, 1), :], (A_SEL_BLOCK, tq))
                     for i in range(blocks_per_iter)], axis=0)
                s = _mask_heads(s, (flags > 0.5) & causal, NEG_INF, tq)
                m_new = jnp.maximum(m, jnp.max(s, axis=0, keepdims=True))
                p = jnp.exp2(s - m_new).astype(BF16)
                vt = value_tiles(vst_ref, j * tiles_per_iter, tiles_per_iter, g)
                out.append((m_new, jnp.exp2(m - m_new) * acc + _dot(vt, p)))
            return tuple(out)

        init = tuple((jnp.full((1, r), NEG_INF, F32), jnp.zeros((da, r), F32)) for _ in range(ng))
        state = lax.fori_loop(0, (t0 + tq - 1) // kt + 1, body, init)
        kw_all = kw_ref[pl.ds(off_w, wk), :]
        for g in range(ng):
            osw_ref[0, g] = normalised(state[g][1])
            s_w = _mask_heads(_dot(kw_all, q_pad[g]), ok_w, NEG_INF, tq)
            p_w = jnp.exp2(s_w - jnp.max(s_w, axis=0, keepdims=True)).astype(BF16)
            acc_w = _dot(value_tiles(vwt_ref, j0, 1, g), p_w[0:tk, :])
            for i in range(1, n_wb + 1):
                acc_w = acc_w + _dot(value_tiles(vwt_ref, j0 + i, 1, g), p_w[i * tk:(i + 1) * tk, :])
            osw_ref[1, g] = normalised(acc_w)

    gs = _sigmoid(ps_ref[...])
    g0 = 2 * M_HEADS
    outs = []
    for g in range(ng):
        o_s = osw_ref[0, g]
        o_w = osw_ref[1, g]
        for h in range(hpg):
            sl = slice(h * tq, (h + 1) * tq)
            gr = g0 + 3 * (g * hpg + h)
            outs.append(gs[gr:gr + 1, :] * o_c[g][:, sl] + gs[gr + 1:gr + 2, :] * o_s[:, sl]
                        + gs[gr + 2:gr + 3, :] * o_w[:, sl])
    o_all = jnp.concatenate(outs, axis=0).T
    z = z_ref[...].astype(F32)
    o_ref[...] = (o_all * (z * _sigmoid(z))).astype(o_ref.dtype)


def _nsa(p, ps, kvc, vct, q_col, kv_col, z_col, bsz, seq):
    t = p.shape[0]
    g = A_KV_HEADS
    dh = A_HEAD_DIM
    aw = A_HEADS * dh
    kvw = g * dh
    tq = Q_BLOCK
    nq = seq // tq
    n_cmp = kvc.shape[3]
    tk = LANES
    rows = lambda b, qi: b * nq + qi
    kv_piece = lambda i: pl.BlockSpec((seq, kvw), lambda b, qi: (b, kv_col // kvw + i))
    return pl.pallas_call(
        _nsa_kernel,
        grid=(bsz, nq),
        in_specs=[
            pl.BlockSpec((tq, aw), lambda b, qi: (rows(b, qi), q_col // aw)),
            pl.BlockSpec((None, None, g, n_cmp, dh), lambda b, qi: (0, b, 0, 0, 0)),
            pl.BlockSpec((None, g, dh, n_cmp), lambda b, qi: (b, 0, 0, 0)),
            kv_piece(2), kv_piece(3), kv_piece(4), kv_piece(5),
            pl.BlockSpec((LANES, tq), lambda b, qi: (0, rows(b, qi))),
            pl.BlockSpec((tq, aw), lambda b, qi: (rows(b, qi), z_col // aw)),
        ],
        out_specs=pl.BlockSpec((tq, aw), lambda b, qi: (rows(b, qi), 0)),
        out_shape=jax.ShapeDtypeStruct((t, aw), BF16),
        scratch_shapes=[pltpu.VMEM((g, seq // A_SEL_BLOCK, tq), F32),
                        pltpu.VMEM((seq // tk, kvw, tk), BF16),
                        pltpu.VMEM((seq // tk, kvw, tk), BF16),
                        pltpu.VMEM((2 * g, 1, 1), F32),
                        pltpu.VMEM((2, g, dh, A_GROUP * tq), F32)],
        compiler_params=_params("parallel", "arbitrary"),
        name="nsa",
    )(p, kvc, vct, p, p, p, p, ps, p)


def _merge_out_kernel(ym_ref, ya_ref, wm_ref, wa_ref, gm_ref, ga_ref, wo_ref, g_ref, x_ref, o_ref):
    um = _dot(ym_ref[...], wm_ref[...])
    ua = _dot(ya_ref[...], wa_ref[...])
    mg = _sigmoid(gm_ref[...].astype(F32)) * um + _sigmoid(ga_ref[...].astype(F32)) * ua
    out = _dot(mg.astype(BF16), wo_ref[...])
    ms = jnp.mean(out * out, axis=-1, keepdims=True)
    o_ref[...] = x_ref[...] + out * lax.rsqrt(ms + RMS_EPS) * g_ref[...]


def _merge_out(ym, ya, wm, wa, p, gm_col, ga_col, wo, g, x2d, tm=256):
    t, d = x2d.shape
    kdim = ym.shape[1]
    resident = lambda shape: pl.BlockSpec(shape, lambda i: (0, 0), pipeline_mode=pl.Buffered(1))
    return pl.pallas_call(
        _merge_out_kernel,
        grid=(t // tm,),
        in_specs=[
            pl.BlockSpec((tm, kdim), lambda i: (i, 0)),
            pl.BlockSpec((tm, kdim), lambda i: (i, 0)),
            resident((kdim, d)),
            resident((kdim, d)),
            pl.BlockSpec((tm, d), lambda i: (i, gm_col // d)),
            pl.BlockSpec((tm, d), lambda i: (i, ga_col // d)),
            resident((d, d)),
            pl.BlockSpec((1, d), lambda i: (0, 0)),
            pl.BlockSpec((tm, d), lambda i: (i, 0)),
        ],
        out_specs=pl.BlockSpec((tm, d), lambda i: (i, 0)),
        out_shape=jax.ShapeDtypeStruct((t, d), F32),
        compiler_params=_params("parallel"),
        name="merge_out",
    )(ym, ya, wm, wa, p, p, wo, g, x2d)


def _layer(x, pre_g, w_in, conv_w, conv_b, w_q, w_k, b_i, b_f, norm_g, skip,
           pe_k, w1_k, w2_k, pe_v, w1_v, w2_v, w_up_m, w_up_a, w_out, post_g):
    bsz, seq, d = x.shape
    t = bsz * seq
    mw = w_up_m.shape[0]
    aw = w_up_a.shape[0]
    g = A_KV_HEADS
    dh = A_HEAD_DIM
    kvw = g * dh
    n_gate = 3 * A_HEADS

    o_i = 4 * mw
    o_f = o_i + M_HEADS
    o_q = o_f + M_HEADS
    o_g = o_q + aw + 6 * kvw
    o_z = o_g + n_gate
    o_m = o_z + aw
    o_kv = o_q + aw
    main_pieces = ((o_m, w_in.shape[1]), (0, o_i), (o_q, o_kv), (o_z, o_m), (o_kv, o_g))
    w_t = w_in.T
    row_starts = jnp.asarray([r // SUBLANES for lo, hi in main_pieces for r in range(lo, hi, INPROJ_TN)], jnp.int32)
    small_pieces = ((o_i, o_q), (o_g, o_z))
    gm_col = 0
    ga_col = d
    mx_col = 2 * d
    q_col = mx_col + 4 * mw
    z_col = q_col + aw
    kv_col = z_col + aw

    x2d = x.reshape(t, d)
    p, ps = _inproj(x2d, pre_g.reshape(1, d), w_t, row_starts, small_pieces, tn=INPROJ_TN)

    xc, q, k = _conv_qk(p, mx_col, conv_w, conv_b.reshape(1, mw), w_q.astype(BF16), w_k.astype(BF16), bsz, seq)
    bias_m = jnp.concatenate([b_i, b_f]).reshape(2 * M_HEADS, 1)
    y_m = _mlstm(p, mx_col, xc, q, k, ps, bias_m, norm_g.reshape(1, mw), skip.reshape(1, mw), bsz, seq)

    half = A_CMP_STRIDE * dh
    pe = jnp.stack([pe_k.reshape(2, half), pe_v.reshape(2, half)])
    w1 = jnp.stack([w1_k, w1_v]).astype(BF16)
    w2 = jnp.stack([w2_k, w2_v]).astype(BF16)
    kvc = _compress(p, kv_col, pe, w1, w2, bsz, seq)
    vct = jnp.swapaxes(kvc[1], 2, 3)
    y_a = _nsa(p, ps, kvc, vct, q_col, kv_col, z_col, bsz, seq)

    out = _merge_out(y_m, y_a, w_up_m.astype(BF16), w_up_a.astype(BF16), p, gm_col, ga_col,
                     w_out.astype(BF16), post_g.reshape(1, d), x2d)
    return out.reshape(bsz, seq, d)


def kernel(x, pre_norm_g, w_in, m_conv_w, m_conv_b, m_w_q, m_w_k, m_b_i, m_b_f, m_norm_g, m_skip, a_pe_k, a_w1_k, a_w2_k, a_pe_v, a_w1_v, a_w2_v, w_up_m, w_up_a, w_out, post_norm_g):
    depth = w_in.shape[0]
    for l in range(depth):
        x = _layer(x, pre_norm_g[l], w_in[l], m_conv_w[l], m_conv_b[l], m_w_q[l], m_w_k[l], m_b_i[l], m_b_f[l],
                   m_norm_g[l], m_skip[l], a_pe_k[l], a_w1_k[l], a_w2_k[l], a_pe_v[l], a_w1_v[l], a_w2_v[l],
                   w_up_m[l], w_up_a[l], w_out[l], post_norm_g[l])
    return x
```

```python
import functools

import jax
import jax.numpy as jnp
from jax import lax
from jax.experimental import pallas as pl
from jax.experimental.pallas import tpu as pltpu

F32 = jnp.float32
BF16 = jnp.bfloat16

M_HEADS = 4
M_CONV = 4
M_CHUNK = 128
A_HEADS = 16
A_KV_HEADS = 4
A_GROUP = A_HEADS // A_KV_HEADS
A_HEAD_DIM = 64
A_CMP_BLOCK = 32
A_CMP_STRIDE = 16
A_SEL_BLOCK = 64
A_SEL_TOPK = 8
A_WINDOW = 512
Q_BLOCK = 128
SEL_KEYS_PER_STEP = 512
EXCLUDE = 1e4
MAX_FAST_BOUND = 40.0
SEL_BLOCK_SHIFT = A_SEL_BLOCK.bit_length() - 1
KEY_NORM_ROWS = 512
BOUND_SLACK = 1.01
INPROJ_TN = 512
RMS_EPS = 1e-6
LN_EPS = 1e-6
NEG_INF = -1e30
POS_INF = 1e30

LANES = 128
SUBLANES = 8
BF16_SUBLANES = 16
LOG2_E = 1.4426950408889634
VMEM_LIMIT_BYTES = 48 * 1024 * 1024


def _nt_dot(a, b):
    return lax.dot_general(a, b, (((1,), (1,)), ((), ())), preferred_element_type=F32)


def _dot(a, b):
    return jnp.dot(a, b, preferred_element_type=F32)


def _sigmoid(x):
    return 1.0 / (1.0 + jnp.exp(-x))


def _params(*sem):
    return pltpu.CompilerParams(dimension_semantics=sem, vmem_limit_bytes=VMEM_LIMIT_BYTES)


def _inproj_kernel(rows_ref, x_ref, g_ref, wt_ref, *rest):
    del rows_ref
    o_ref, ost_ref, h_scr = rest[-3:]
    small_refs = rest[:-3]

    @pl.when(pl.program_id(1) == 0)
    def _():
        xf = x_ref[...]
        ms = jnp.mean(xf * xf, axis=-1, keepdims=True)
        hb = (xf * lax.rsqrt(ms + RMS_EPS) * g_ref[...]).astype(BF16)
        h_scr[...] = hb
        rows = [r[...] for r in small_refs]
        used = sum(r.shape[0] for r in rows)
        rows.append(jnp.zeros((ost_ref.shape[0] - used, xf.shape[1]), F32))
        ost_ref[...] = _nt_dot(jnp.concatenate(rows, axis=0).astype(BF16), hb)

    o_ref[...] = _nt_dot(h_scr[...], wt_ref[...].astype(BF16)).astype(o_ref.dtype)


def _inproj(x2d, g, w_t, row_starts, small_pieces, tm=1024, tn=512):
    t, d = x2d.shape
    n = row_starts.shape[0] * tn
    ns = LANES
    rows_of = lambda lo, hi: pl.BlockSpec((pl.Element(hi - lo), pl.Element(d)), lambda i, j, rows: (lo, 0))
    grid_spec = pltpu.PrefetchScalarGridSpec(
        num_scalar_prefetch=1,
        grid=(t // tm, n // tn),
        in_specs=[
            pl.BlockSpec((tm, d), lambda i, j, rows: (i, 0)),
            pl.BlockSpec((1, d), lambda i, j, rows: (0, 0)),
            pl.BlockSpec((pl.Element(tn), pl.Element(d)), lambda i, j, rows: (rows[j] * SUBLANES, 0)),
        ] + [rows_of(lo, hi) for lo, hi in small_pieces],
        out_specs=[
            pl.BlockSpec((tm, tn), lambda i, j, rows: (i, j)),
            pl.BlockSpec((ns, tm), lambda i, j, rows: (0, i)),
        ],
        scratch_shapes=[pltpu.VMEM((tm, d), BF16)],
    )
    return pl.pallas_call(
        _inproj_kernel,
        grid_spec=grid_spec,
        out_shape=[jax.ShapeDtypeStruct((t, n), BF16), jax.ShapeDtypeStruct((ns, t), F32)],
        compiler_params=_params("parallel", "arbitrary"),
        name="inproj",
    )(row_starts, x2d, g, w_t, *([w_t] * len(small_pieces)))


def _conv_qk_kernel(x_ref, cw_ref, cb_ref, wq_ref, wk_ref, xc_ref, q_ref, k_ref, pad_ref, *, rows):
    s, c = x_ref.shape
    front = pad_ref.shape[0] - s
    pad_ref[0:front, :] = jnp.zeros((front, c), F32)
    pad_ref[front:front + s, :] = x_ref[...].astype(F32)
    cw = cw_ref[...]
    cb = cb_ref[...]
    wq = wq_ref[...]
    wk = wk_ref[...]
    k_scale = c ** -0.5
    for r in range(s // rows):
        acc = jnp.broadcast_to(cb, (rows, c))
        for kk in range(M_CONV):
            start = front + r * rows + kk - (M_CONV - 1)
            acc = acc + cw[kk:kk + 1, :] * pad_ref[start:start + rows, :]
        xcb = (acc * _sigmoid(acc)).astype(BF16)
        xc_ref[r * rows:(r + 1) * rows, :] = xcb
        q_ref[r * rows:(r + 1) * rows, :] = _dot(xcb, wq).astype(BF16)
        k_ref[r * rows:(r + 1) * rows, :] = (_dot(xcb, wk) * k_scale).astype(BF16)


def _conv_qk(p, mx_col, conv_w, conv_b, wq, wk, bsz, seq):
    t = p.shape[0]
    heads, hd = wq.shape[0], wq.shape[1]
    width = heads * hd
    blk = pl.BlockSpec((seq, hd), lambda b, h: (b, h))
    out = jax.ShapeDtypeStruct((t, width), BF16)
    return pl.pallas_call(
        functools.partial(_conv_qk_kernel, rows=256),
        grid=(bsz, heads),
        in_specs=[
            pl.BlockSpec((seq, hd), lambda b, h: (b, mx_col // hd + h)),
            pl.BlockSpec((M_CONV, hd), lambda b, h: (0, h)),
            pl.BlockSpec((1, hd), lambda b, h: (0, h)),
            pl.BlockSpec((None, hd, hd), lambda b, h: (h, 0, 0)),
            pl.BlockSpec((None, hd, hd), lambda b, h: (h, 0, 0)),
        ],
        out_specs=[blk, blk, blk],
        out_shape=[out, out, out],
        scratch_shapes=[pltpu.VMEM((seq + SUBLANES, hd), F32)],
        compiler_params=_params("parallel", "parallel"),
        name="conv_qk",
    )(p, conv_w, conv_b, wq, wk)


def _log_sigmoid(x):
    return jnp.minimum(x, 0.0) - jnp.log1p(jnp.exp(-jnp.abs(x)))


def _mlstm_kernel(bias_ref, q_ref, k_ref, v_ref, o_ref, z_ref, xc_ref, g_ref, ng_ref, sk_ref,
                  y_ref, c_scr, n_scr, m_scr):
    @pl.when(pl.program_id(1) == 0)
    def _():
        c_scr[...] = jnp.zeros_like(c_scr)
        n_scr[...] = jnp.zeros_like(n_scr)
        m_scr[...] = jnp.zeros_like(m_scr)

    ln = q_ref.shape[0]
    heads, hd = c_scr.shape[0], c_scr.shape[1]
    gates = g_ref[...] + bias_ref[...]
    fb_all = _log_sigmoid(gates[heads:2 * heads, :])
    row = lax.broadcasted_iota(jnp.int32, (ln, ln), 0)
    col = lax.broadcasted_iota(jnp.int32, (ln, ln), 1)
    tril = col <= row
    eye = col == row
    for hh in range(heads):
        sl = slice(hh * hd, (hh + 1) * hd)
        ib = gates[hh:hh + 1, :]
        fb = fb_all[hh:hh + 1, :]
        bcum_c = jnp.sum(jnp.where(tril, fb, 0.0), axis=1, keepdims=True)
        bcum_r = jnp.sum(jnp.where(eye, bcum_c, 0.0), axis=0, keepdims=True)
        g_tot = jnp.sum(fb, axis=1, keepdims=True)
        e_r = ib - bcum_r
        m_st = m_scr[hh]
        d_intra = jnp.where(tril, bcum_c + e_r, NEG_INF)
        d_inter = bcum_c + m_st
        m_t = jnp.maximum(d_inter, jnp.max(d_intra, axis=1, keepdims=True))
        w_intra = jnp.exp(d_intra - m_t)
        w_inter = jnp.exp(d_inter - m_t)

        qb = q_ref[:, sl]
        kb = k_ref[:, sl]
        vb = v_ref[:, sl]
        c_st = c_scr[hh]
        n_st = n_scr[hh]
        sc = _nt_dot(qb, kb) * w_intra
        num = _dot(sc.astype(BF16), vb) + w_inter * _dot(qb, c_st.astype(BF16))
        qn = jnp.sum(qb.astype(F32) * n_st, axis=1, keepdims=True)
        den = jnp.sum(sc, axis=1, keepdims=True) + w_inter * qn
        h = num / jnp.maximum(jnp.abs(den), jnp.exp(-m_t))

        d_state = g_tot + e_r
        m_new = jnp.maximum(g_tot + m_st, jnp.max(d_state, axis=1, keepdims=True))
        w_s_r = jnp.exp(d_state - m_new)
        w_s_c = jnp.sum(jnp.where(eye, w_s_r, 0.0), axis=1, keepdims=True)
        decay = jnp.exp(g_tot + m_st - m_new)
        kw = kb.astype(F32) * w_s_c
        kv = lax.dot_general(kw.astype(BF16), vb, (((0,), (0,)), ((), ())), preferred_element_type=F32)
        c_scr[hh] = decay * c_st + kv
        n_scr[hh] = decay * n_st + jnp.sum(kw, axis=0, keepdims=True)
        m_scr[hh] = m_new

        hg = h * _sigmoid(o_ref[:, sl].astype(F32))
        mu = jnp.mean(hg, axis=-1, keepdims=True)
        hc = hg - mu
        var = jnp.mean(hc * hc, axis=-1, keepdims=True)
        hn = hc * lax.rsqrt(var + LN_EPS)
        z = z_ref[:, sl].astype(F32)
        y = (hn * ng_ref[:, sl] + sk_ref[:, sl] * xc_ref[:, sl].astype(F32)) * (z * _sigmoid(z))
        y_ref[:, sl] = y.astype(y_ref.dtype)


def _mlstm(p, mx_col, xc, q, k, gates, bias, norm_g, skip, bsz, seq):
    t, width = q.shape
    pc = mx_col // width
    heads = M_HEADS
    hd = width // heads
    ln = M_CHUNK
    nc = seq // ln
    rows = lambda b, c: (b * nc + c, 0)
    vec = pl.BlockSpec((1, width), lambda b, c: (0, 0))
    return pl.pallas_call(
        _mlstm_kernel,
        grid=(bsz, nc),
        in_specs=[
            pl.BlockSpec((2 * heads, 1), lambda b, c: (0, 0)),
            pl.BlockSpec((ln, width), rows),
            pl.BlockSpec((ln, width), rows),
            pl.BlockSpec((ln, width), lambda b, c: (b * nc + c, pc + 1)),
            pl.BlockSpec((ln, width), lambda b, c: (b * nc + c, pc + 2)),
            pl.BlockSpec((ln, width), lambda b, c: (b * nc + c, pc + 3)),
            pl.BlockSpec((ln, width), rows),
            pl.BlockSpec((2 * heads, ln), lambda b, c: (0, b * nc + c)),
            vec, vec,
        ],
        out_specs=pl.BlockSpec((ln, width), rows),
        out_shape=jax.ShapeDtypeStruct((t, width), BF16),
        scratch_shapes=[pltpu.VMEM((heads, hd, hd), F32), pltpu.VMEM((heads, 1, hd), F32),
                        pltpu.VMEM((heads, 1, 1), F32)],
        compiler_params=_params("parallel", "arbitrary"),
        name="mlstm",
    )(bias, q, k, p, p, p, xc, gates, norm_g, skip)


def _compress_kernel(x_ref, pe_ref, w1_ref, w2_ref, o_ref, xf_scr):
    ng, n_sub, dh = o_ref.shape
    half = A_CMP_STRIDE * dh
    n_chunk = xf_scr.shape[0]
    for c in range(n_chunk):
        xf_scr[c] = x_ref[:, c * LANES:(c + 1) * LANES].astype(F32)
    xs = [jnp.concatenate([xf_scr[c, pl.ds(r, n_sub, stride=A_CMP_STRIDE), :] for c in range(n_chunk)], axis=1)
          for r in range(A_CMP_STRIDE)]
    pe = pe_ref[...]
    rid = lax.broadcasted_iota(jnp.int32, (n_sub, dh), 0)
    for g in range(ng):
        t = jnp.concatenate([x[:, g * dh:(g + 1) * dh] for x in xs], axis=1)
        za = _dot((t + pe[0:1, :]).astype(BF16), w1_ref[0:half, :])
        zb = _dot((t + pe[1:2, :]).astype(BF16), w1_ref[half:2 * half, :])
        pre = za + pltpu.roll(zb, n_sub - 1, 0)
        hid = (pre * _sigmoid(pre)).astype(BF16)
        out = _dot(hid, w2_ref[...])
        o_ref[g] = jnp.where(rid < n_sub - 1, out, 0.0).astype(o_ref.dtype)


def _compress(p, kc_col, pe, w1, w2, bsz, seq):
    g = A_KV_HEADS
    dh = A_HEAD_DIM
    kvw = g * dh
    n_sub = seq // A_CMP_STRIDE
    half = A_CMP_STRIDE * dh
    hid = w1.shape[-1]
    return pl.pallas_call(
        _compress_kernel,
        grid=(2, bsz),
        in_specs=[
            pl.BlockSpec((seq, kvw), lambda i, b: (b, kc_col // kvw + i)),
            pl.BlockSpec((None, 2, half), lambda i, b: (i, 0, 0)),
            pl.BlockSpec((None, 2 * half, hid), lambda i, b: (i, 0, 0)),
            pl.BlockSpec((None, hid, dh), lambda i, b: (i, 0, 0)),
        ],
        out_specs=pl.BlockSpec((None, None, g, n_sub, dh), lambda i, b: (i, b, 0, 0, 0)),
        out_shape=jax.ShapeDtypeStruct((2, bsz, g, n_sub, dh), BF16),
        scratch_shapes=[pltpu.VMEM((kvw // LANES, seq, LANES), F32)],
        compiler_params=_params("parallel", "parallel"),
        name="compress",
    )(p, pe, w1, w2)


def _mask_heads(x, ok, fill, tq):
    n = x.shape[1] // tq
    return jnp.concatenate([jnp.where(ok, x[:, h * tq:(h + 1) * tq], fill) for h in range(n)], axis=1)


def _with_ones_rows(vt):
    return jnp.concatenate([vt, jnp.ones((BF16_SUBLANES, vt.shape[1]), vt.dtype)], axis=0)


def _nsa_kernel(q_ref, kc_ref, vct_ref, ks_ref, vs_ref, kw_ref, vw_ref, ps_ref, z_ref, o_ref,
                sel_ref, vst_ref, vwt_ref, kn_ref, osw_ref):
    qi = pl.program_id(1)
    tq = q_ref.shape[0]
    dh = A_HEAD_DIM
    hpg = A_GROUP
    ng = A_KV_HEADS
    r = hpg * tq
    n_cmp = kc_ref.shape[1]
    tk = vwt_ref.shape[2]
    kt = SEL_KEYS_PER_STEP
    tiles_per_iter = kt // tk
    seq = ks_ref.shape[0]
    n_sel = seq // A_SEL_BLOCK
    t0 = qi * tq
    da = dh + BF16_SUBLANES

    @pl.when(qi == 0)
    def _():
        for j in range(seq // tk):
            vst_ref[j] = vs_ref[j * tk:(j + 1) * tk, :].astype(F32).T.astype(BF16)
            vwt_ref[j] = vw_ref[j * tk:(j + 1) * tk, :].astype(F32).T.astype(BF16)
        rows = KEY_NORM_ROWS
        for which, k_ref in enumerate((ks_ref, kw_ref)):
            best = [jnp.zeros((1, 1), F32) for _ in range(ng)]
            for c in range(seq // rows):
                kf = k_ref[c * rows:(c + 1) * rows, :].astype(F32)
                ksq = kf * kf
                for g in range(ng):
                    n2 = jnp.sum(ksq[:, g * dh:(g + 1) * dh], axis=1, keepdims=True)
                    best[g] = jnp.maximum(best[g], jnp.max(n2, axis=0, keepdims=True))
            for g in range(ng):
                kn_ref[which * ng + g] = best[g]

    qt = (q_ref[...].astype(F32) * (dh ** -0.5 * LOG2_E)).T.astype(BF16)
    q4t = [jnp.concatenate([qt[(g * hpg + h) * dh:(g * hpg + h + 1) * dh, :] for h in range(hpg)], axis=1)
           for g in range(ng)]

    cn = lax.broadcasted_iota(jnp.int32, (n_cmp, tq), 0)
    ct = t0 + lax.broadcasted_iota(jnp.int32, (n_cmp, tq), 1)
    valid = (cn * A_CMP_STRIDE + (A_CMP_BLOCK - 1)) <= ct
    jn = lax.broadcasted_iota(jnp.int32, (n_sel, n_cmp), 0)
    nn = lax.broadcasted_iota(jnp.int32, (n_sel, n_cmp), 1)
    ov = ((nn * A_CMP_STRIDE < (jn + 1) * A_SEL_BLOCK)
          & (nn * A_CMP_STRIDE + (A_CMP_BLOCK - 1) >= jn * A_SEL_BLOCK))
    ov_t = jnp.where(ov, 1.0, 0.0).astype(BF16)
    jr = lax.broadcasted_iota(jnp.int32, (n_sel, tq), 0)
    tl = t0 + lax.broadcasted_iota(jnp.int32, (n_sel, tq), 1)
    cur = lax.shift_right_logical(tl, SEL_BLOCK_SHIFT)
    forced = (jr == 0) | (jr == cur) | (jr == cur - 1)
    def compressed_and_select():
        for g in range(ng):
            s_m = _mask_heads(_dot(kc_ref[g], q4t[g]), valid, NEG_INF, tq)
            mx = jnp.max(s_m, axis=0, keepdims=True)
            e = _mask_heads(jnp.exp2(s_m - mx), valid, 0.0, tq)
            lsum = jnp.sum(e, axis=0, keepdims=True)
            p_c = e * (1.0 / jnp.where(lsum > 0.0, lsum, 1.0))
            osw_ref[2, g] = _dot(vct_ref[g], p_c.astype(BF16))
            p_sum = p_c[:, 0:tq]
            for h in range(1, hpg):
                p_sum = p_sum + p_c[:, h * tq:(h + 1) * tq]
            p_hi = p_sum.astype(BF16)
            r1 = p_sum - p_hi.astype(F32)
            p_mid = r1.astype(BF16)
            p_lo = (r1 - p_mid.astype(F32)).astype(BF16)
            imp_t = _dot(ov_t, p_hi) + _dot(ov_t, p_mid) + _dot(ov_t, p_lo)
            score = jnp.where(forced, POS_INF, jnp.where(jr <= cur, imp_t, NEG_INF))
            rank = jnp.zeros((n_sel, tq), F32)
            for jp in range(n_sel):
                sj = score[jp:jp + 1, :]
                beats = (sj > score) | ((sj == score) & (jr > jp))
                rank = rank + jnp.where(beats, 1.0, 0.0)
            sel_ref[g] = jnp.where(rank < float(min(A_SEL_TOPK, n_sel)), 1.0, 0.0)

    qn = [jnp.sqrt(jnp.sum(jnp.square(q4t[g].astype(F32)), axis=0, keepdims=True)) for g in range(ng)]
    bound_s = [qn[g] * jnp.sqrt(kn_ref[g]) * BOUND_SLACK for g in range(ng)]
    bound_w = [qn[g] * jnp.sqrt(kn_ref[ng + g]) * BOUND_SLACK for g in range(ng)]
    worst = bound_s[0]
    for b in bound_s[1:] + bound_w:
        worst = jnp.maximum(worst, b)
    fast = jnp.max(worst) <= MAX_FAST_BOUND

    n_wb = -(-A_WINDOW // tk)
    wk = (n_wb + 1) * tk
    j0 = jnp.maximum(qi - n_wb, 0)
    off_w = pl.multiple_of(j0 * tk, tk)
    c_abs = off_w + lax.broadcasted_iota(jnp.int32, (wk, tq), 0)
    t_abs_w = t0 + lax.broadcasted_iota(jnp.int32, (wk, tq), 1)
    ok_w = (c_abs <= t_abs_w) & (c_abs > t_abs_w - A_WINDOW)
    blocks_per_iter = kt // A_SEL_BLOCK

    def normalised(acc):
        return acc[0:dh, :] * (1.0 / acc[dh:dh + 1, :])

    def value_tiles(v_ref, first, count, g):
        vt = jnp.concatenate([v_ref[first + i, g * dh:(g + 1) * dh, :] for i in range(count)], axis=1)
        return _with_ones_rows(vt)

    @pl.when(fast)
    def _():
        kcol = lax.broadcasted_iota(jnp.int32, (kt, dh), 1)
        krow = lax.broadcasted_iota(jnp.int32, (kt, dh), 0)
        kblock = lax.shift_right_logical(krow, SEL_BLOCK_SHIFT)
        aux = jnp.where((kcol == 0) | (kcol == SUBLANES + kblock), 1.0, 0.0).astype(BF16)
        aux_w = jnp.where(lax.broadcasted_iota(jnp.int32, (wk, dh), 1) == 0, 1.0, 0.0).astype(BF16)
        aux_d = jnp.where(lax.broadcasted_iota(jnp.int32, (tq, dh), 1) == 0, 1.0, 0.0).astype(BF16)
        zrows = jnp.zeros((LANES - dh - BF16_SUBLANES, r), BF16)

        first_row = lax.broadcasted_iota(jnp.int32, (SUBLANES, r), 0) == 0

        def query_aug(g, shift, block_bias):
            top = jnp.where(first_row, -shift, 0.0)
            low = jnp.zeros((SUBLANES, r), F32) if block_bias is None else jnp.concatenate([block_bias] * hpg, axis=1)
            return jnp.concatenate([q4t[g], jnp.concatenate([top, low], axis=0).astype(BF16), zrows], axis=0)

        def step(j, accs):
            k_all = ks_ref[j * kt:(j + 1) * kt, :]
            out = []
            for g in range(ng):
                k_aug = jnp.concatenate([k_all[:, g * dh:(g + 1) * dh], aux], axis=1)
                bias = sel_ref[g, j * blocks_per_iter:(j + 1) * blocks_per_iter, :]
                p = jnp.exp2(_dot(k_aug, query_aug(g, bound_s[g], bias))).astype(BF16)
                out.append(accs[g] + _dot(value_tiles(vst_ref, j * tiles_per_iter, tiles_per_iter, g), p))
            return tuple(out)

        def finish(accs):
            kd_all = ks_ref[pl.ds(pl.multiple_of(t0, tq), tq), :]
            causal = (lax.broadcasted_iota(jnp.int32, (tq, tq), 0)
                      <= lax.broadcasted_iota(jnp.int32, (tq, tq), 1))
            kw_all = kw_ref[pl.ds(off_w, wk), :]
            for g in range(ng):
                kd_aug = jnp.concatenate([kd_all[:, g * dh:(g + 1) * dh], aux_d], axis=1)
                s_d = _mask_heads(_dot(kd_aug, query_aug(g, bound_s[g], None)), causal, -EXCLUDE, tq)
                acc = accs[g] + _dot(value_tiles(vst_ref, qi, 1, g), jnp.exp2(s_d).astype(BF16))
                osw_ref[0, g] = normalised(acc)
                kw_aug = jnp.concatenate([kw_all[:, g * dh:(g + 1) * dh], aux_w], axis=1)
                s_w = _mask_heads(_dot(kw_aug, query_aug(g, bound_w[g], None)), ok_w, -EXCLUDE, tq)
                p_w = jnp.exp2(s_w).astype(BF16)
                acc_w = _dot(value_tiles(vwt_ref, j0, 1, g), p_w[0:tk, :])
                for i in range(1, n_wb + 1):
                    acc_w = acc_w + _dot(value_tiles(vwt_ref, j0 + i, 1, g), p_w[i * tk:(i + 1) * tk, :])
                osw_ref[1, g] = normalised(acc_w)

        steps_needed = (t0 + kt - 1) // kt
        for n_steps in range(seq // kt + 1):
            @pl.when(steps_needed == n_steps)
            def _(n_steps=n_steps):
                compressed_and_select()
                before_diag = jr < 2 * qi
                for g in range(ng):
                    sel_ref[g] = jnp.where((sel_ref[g] > 0.5) & before_diag, 0.0, -EXCLUDE)
                accs = tuple(jnp.zeros((da, r), F32) for _ in range(ng))
                for j in range(n_steps):
                    accs = step(j, accs)
                finish(accs)

    @pl.when(jnp.logical_not(fast))
    def _():
        compressed_and_select()
        zpad = jnp.zeros((dh, r), BF16)
        q_pad = [jnp.concatenate([q4t[g] if gg == g else zpad for gg in range(ng)], axis=0) for g in range(ng)]
        sub_s = lax.broadcasted_iota(jnp.int32, (kt, tq), 0)
        t_abs_s = t0 + lax.broadcasted_iota(jnp.int32, (kt, tq), 1)

        def body(j, carry):
            off = pl.multiple_of(j * kt, kt)
            k_all = ks_ref[pl.ds(off, kt), :]
            causal = (off + sub_s) <= t_abs_s
            out = []
            for g in range(ng):
                m, acc = carry[g]
                s = _dot(k_all, q_pad[g])
                flags = jnp.concatenate(
                    [jnp.broadcast_to(sel_ref[g, pl.ds(j * blocks_per_iter + i, 1), :], (A_SEL_BLOCK, tq))
                     for i in range(blocks_per_iter)], axis=0)
                s = _mask_heads(s, (flags > 0.5) & causal, NEG_INF, tq)
                m_new = jnp.maximum(m, jnp.max(s, axis=0, keepdims=True))
                p = jnp.exp2(s - m_new).astype(BF16)
                vt = value_tiles(vst_ref, j * tiles_per_iter, tiles_per_iter, g)
                out.append((m_new, jnp.exp2(m - m_new) * acc + _dot(vt, p)))
            return tuple(out)

        init = tuple((jnp.full((1, r), NEG_INF, F32), jnp.zeros((da, r), F32)) for _ in range(ng))
        state = lax.fori_loop(0, (t0 + tq - 1) // kt + 1, body, init)
        kw_all = kw_ref[pl.ds(off_w, wk), :]
        for g in range(ng):
            osw_ref[0, g] = normalised(state[g][1])
            s_w = _mask_heads(_dot(kw_all, q_pad[g]), ok_w, NEG_INF, tq)
            p_w = jnp.exp2(s_w - jnp.max(s_w, axis=0, keepdims=True)).astype(BF16)
            acc_w = _dot(value_tiles(vwt_ref, j0, 1, g), p_w[0:tk, :])
            for i in range(1, n_wb + 1):
                acc_w = acc_w + _dot(value_tiles(vwt_ref, j0 + i, 1, g), p_w[i * tk:(i + 1) * tk, :])
            osw_ref[1, g] = normalised(acc_w)

    gs = _sigmoid(ps_ref[...])
    g0 = 2 * M_HEADS
    outs = []
    for g in range(ng):
        o_s = osw_ref[0, g]
        o_w = osw_ref[1, g]
        o_c = osw_ref[2, g]
        for h in range(hpg):
            sl = slice(h * tq, (h + 1) * tq)
            gr = g0 + 3 * (g * hpg + h)
            outs.append(gs[gr:gr + 1, :] * o_c[:, sl] + gs[gr + 1:gr + 2, :] * o_s[:, sl]
                        + gs[gr + 2:gr + 3, :] * o_w[:, sl])
    o_all = jnp.concatenate(outs, axis=0).T
    z = z_ref[...].astype(F32)
    o_ref[...] = (o_all * (z * _sigmoid(z))).astype(o_ref.dtype)


def _nsa(p, ps, kvc, vct, q_col, kv_col, z_col, bsz, seq):
    t = p.shape[0]
    g = A_KV_HEADS
    dh = A_HEAD_DIM
    aw = A_HEADS * dh
    kvw = g * dh
    tq = Q_BLOCK
    nq = seq // tq
    n_cmp = kvc.shape[3]
    tk = LANES
    rows = lambda b, qi: b * nq + qi
    kv_piece = lambda i: pl.BlockSpec((seq, kvw), lambda b, qi: (b, kv_col // kvw + i))
    return pl.pallas_call(
        _nsa_kernel,
        grid=(bsz, nq),
        in_specs=[
            pl.BlockSpec((tq, aw), lambda b, qi: (rows(b, qi), q_col // aw)),
            pl.BlockSpec((None, None, g, n_cmp, dh), lambda b, qi: (0, b, 0, 0, 0)),
            pl.BlockSpec((None, g, dh, n_cmp), lambda b, qi: (b, 0, 0, 0)),
            kv_piece(2), kv_piece(3), kv_piece(4), kv_piece(5),
            pl.BlockSpec((LANES, tq), lambda b, qi: (0, rows(b, qi))),
            pl.BlockSpec((tq, aw), lambda b, qi: (rows(b, qi), z_col // aw)),
        ],
        out_specs=pl.BlockSpec((tq, aw), lambda b, qi: (rows(b, qi), 0)),
        out_shape=jax.ShapeDtypeStruct((t, aw), BF16),
        scratch_shapes=[pltpu.VMEM((g, seq // A_SEL_BLOCK, tq), F32),
                        pltpu.VMEM((seq // tk, kvw, tk), BF16),
                        pltpu.VMEM((seq // tk, kvw, tk), BF16),
                        pltpu.VMEM((2 * g, 1, 1), F32),
                        pltpu.VMEM((3, g, dh, A_GROUP * tq), F32)],
        compiler_params=_params("parallel", "arbitrary"),
        name="nsa",
    )(p, kvc, vct, p, p, p, p, ps, p)


def _merge_out_kernel(ym_ref, ya_ref, wm_ref, wa_ref, gm_ref, ga_ref, wo_ref, g_ref, x_ref, o_ref):
    um = _dot(ym_ref[...], wm_ref[...])
    ua = _dot(ya_ref[...], wa_ref[...])
    mg = _sigmoid(gm_ref[...].astype(F32)) * um + _sigmoid(ga_ref[...].astype(F32)) * ua
    out = _dot(mg.astype(BF16), wo_ref[...])
    ms = jnp.mean(out * out, axis=-1, keepdims=True)
    o_ref[...] = x_ref[...] + out * lax.rsqrt(ms + RMS_EPS) * g_ref[...]


def _merge_out(ym, ya, wm, wa, p, gm_col, ga_col, wo, g, x2d, tm=256):
    t, d = x2d.shape
    kdim = ym.shape[1]
    resident = lambda shape: pl.BlockSpec(shape, lambda i: (0, 0), pipeline_mode=pl.Buffered(1))
    return pl.pallas_call(
        _merge_out_kernel,
        grid=(t // tm,),
        in_specs=[
            pl.BlockSpec((tm, kdim), lambda i: (i, 0)),
            pl.BlockSpec((tm, kdim), lambda i: (i, 0)),
            resident((kdim, d)),
            resident((kdim, d)),
            pl.BlockSpec((tm, d), lambda i: (i, gm_col // d)),
            pl.BlockSpec((tm, d), lambda i: (i, ga_col // d)),
            resident((d, d)),
            pl.BlockSpec((1, d), lambda i: (0, 0)),
            pl.BlockSpec((tm, d), lambda i: (i, 0)),
        ],
        out_specs=pl.BlockSpec((tm, d), lambda i: (i, 0)),
        out_shape=jax.ShapeDtypeStruct((t, d), F32),
        compiler_params=_params("parallel"),
        name="merge_out",
    )(ym, ya, wm, wa, p, p, wo, g, x2d)


def _layer(x, pre_g, w_in, conv_w, conv_b, w_q, w_k, b_i, b_f, norm_g, skip,
           pe_k, w1_k, w2_k, pe_v, w1_v, w2_v, w_up_m, w_up_a, w_out, post_g):
    bsz, seq, d = x.shape
    t = bsz * seq
    mw = w_up_m.shape[0]
    aw = w_up_a.shape[0]
    g = A_KV_HEADS
    dh = A_HEAD_DIM
    kvw = g * dh
    n_gate = 3 * A_HEADS

    o_i = 4 * mw
    o_f = o_i + M_HEADS
    o_q = o_f + M_HEADS
    o_g = o_q + aw + 6 * kvw
    o_z = o_g + n_gate
    o_m = o_z + aw
    o_kv = o_q + aw
    main_pieces = ((o_m, w_in.shape[1]), (0, o_i), (o_q, o_kv), (o_z, o_m), (o_kv, o_g))
    w_t = w_in.T
    row_starts = jnp.asarray([r // SUBLANES for lo, hi in main_pieces for r in range(lo, hi, INPROJ_TN)], jnp.int32)
    small_pieces = ((o_i, o_q), (o_g, o_z))
    gm_col = 0
    ga_col = d
    mx_col = 2 * d
    q_col = mx_col + 4 * mw
    z_col = q_col + aw
    kv_col = z_col + aw

    x2d = x.reshape(t, d)
    p, ps = _inproj(x2d, pre_g.reshape(1, d), w_t, row_starts, small_pieces, tn=INPROJ_TN)

    xc, q, k = _conv_qk(p, mx_col, conv_w, conv_b.reshape(1, mw), w_q.astype(BF16), w_k.astype(BF16), bsz, seq)
    bias_m = jnp.concatenate([b_i, b_f]).reshape(2 * M_HEADS, 1)
    y_m = _mlstm(p, mx_col, xc, q, k, ps, bias_m, norm_g.reshape(1, mw), skip.reshape(1, mw), bsz, seq)

    half = A_CMP_STRIDE * dh
    pe = jnp.stack([pe_k.reshape(2, half), pe_v.reshape(2, half)])
    w1 = jnp.stack([w1_k, w1_v]).astype(BF16)
    w2 = jnp.stack([w2_k, w2_v]).astype(BF16)
    kvc = _compress(p, kv_col, pe, w1, w2, bsz, seq)
    vct = jnp.swapaxes(kvc[1], 2, 3)
    y_a = _nsa(p, ps, kvc, vct, q_col, kv_col, z_col, bsz, seq)

    out = _merge_out(y_m, y_a, w_up_m.astype(BF16), w_up_a.astype(BF16), p, gm_col, ga_col,
                     w_out.astype(BF16), post_g.reshape(1, d), x2d)
    return out.reshape(bsz, seq, d)


def kernel(x, pre_norm_g, w_in, m_conv_w, m_conv_b, m_w_q, m_w_k, m_b_i, m_b_f, m_norm_g, m_skip, a_pe_k, a_w1_k, a_w2_k, a_pe_v, a_w1_v, a_w2_v, w_up_m, w_up_a, w_out, post_norm_g):
    depth = w_in.shape[0]
    for l in range(depth):
        x = _layer(x, pre_norm_g[l], w_in[l], m_conv_w[l], m_conv_b[l], m_w_q[l], m_w_k[l], m_b_i[l], m_b_f[l],
                   m_norm_g[l], m_skip[l], a_pe_k[l], a_w1_k[l], a_w2_k[l], a_pe_v[l], a_w1_v[l], a_w2_v[l],
                   w_up_m[l], w_up_a[l], w_out[l], post_norm_g[l])
    return x
```

```python
import functools

import jax
import jax.numpy as jnp
from jax import lax
from jax.experimental import pallas as pl
from jax.experimental.pallas import tpu as pltpu

F32 = jnp.float32
BF16 = jnp.bfloat16

M_HEADS = 4
M_CONV = 4
M_CHUNK = 128
A_HEADS = 16
A_KV_HEADS = 4
A_GROUP = A_HEADS // A_KV_HEADS
A_HEAD_DIM = 64
A_CMP_BLOCK = 32
A_CMP_STRIDE = 16
A_SEL_BLOCK = 64
A_SEL_TOPK = 8
A_WINDOW = 512
Q_BLOCK = 128
SEL_KEYS_PER_STEP = 512
EXCLUDE = 1e4
MAX_FAST_BOUND = 40.0
SEL_BLOCK_SHIFT = A_SEL_BLOCK.bit_length() - 1
KEY_NORM_ROWS = 512
BOUND_SLACK = 1.01
INPROJ_TN = 512
RMS_EPS = 1e-6
LN_EPS = 1e-6
NEG_INF = -1e30
POS_INF = 1e30

LANES = 128
SUBLANES = 8
BF16_SUBLANES = 16
LOG2_E = 1.4426950408889634
VMEM_LIMIT_BYTES = 48 * 1024 * 1024


def _nt_dot(a, b):
    return lax.dot_general(a, b, (((1,), (1,)), ((), ())), preferred_element_type=F32)


def _dot(a, b):
    return jnp.dot(a, b, preferred_element_type=F32)


def _sigmoid(x):
    return 1.0 / (1.0 + jnp.exp(-x))


def _params(*sem):
    return pltpu.CompilerParams(dimension_semantics=sem, vmem_limit_bytes=VMEM_LIMIT_BYTES)


def _inproj_kernel(rows_ref, x_ref, g_ref, wt_ref, *rest):
    del rows_ref
    o_ref, ost_ref, h_scr = rest[-3:]
    small_refs = rest[:-3]

    @pl.when(pl.program_id(1) == 0)
    def _():
        xf = x_ref[...]
        ms = jnp.mean(xf * xf, axis=-1, keepdims=True)
        hb = (xf * lax.rsqrt(ms + RMS_EPS) * g_ref[...]).astype(BF16)
        h_scr[...] = hb
        rows = [r[...] for r in small_refs]
        used = sum(r.shape[0] for r in rows)
        rows.append(jnp.zeros((ost_ref.shape[0] - used, xf.shape[1]), F32))
        ost_ref[...] = _nt_dot(jnp.concatenate(rows, axis=0).astype(BF16), hb)

    o_ref[...] = _nt_dot(h_scr[...], wt_ref[...].astype(BF16)).astype(o_ref.dtype)


def _inproj(x2d, g, w_t, row_starts, small_pieces, tm=1024, tn=512):
    t, d = x2d.shape
    n = row_starts.shape[0] * tn
    ns = LANES
    rows_of = lambda lo, hi: pl.BlockSpec((pl.Element(hi - lo), pl.Element(d)), lambda i, j, rows: (lo, 0))
    grid_spec = pltpu.PrefetchScalarGridSpec(
        num_scalar_prefetch=1,
        grid=(t // tm, n // tn),
        in_specs=[
            pl.BlockSpec((tm, d), lambda i, j, rows: (i, 0)),
            pl.BlockSpec((1, d), lambda i, j, rows: (0, 0)),
            pl.BlockSpec((pl.Element(tn), pl.Element(d)), lambda i, j, rows: (rows[j] * SUBLANES, 0)),
        ] + [rows_of(lo, hi) for lo, hi in small_pieces],
        out_specs=[
            pl.BlockSpec((tm, tn), lambda i, j, rows: (i, j)),
            pl.BlockSpec((ns, tm), lambda i, j, rows: (0, i)),
        ],
        scratch_shapes=[pltpu.VMEM((tm, d), BF16)],
    )
    return pl.pallas_call(
        _inproj_kernel,
        grid_spec=grid_spec,
        out_shape=[jax.ShapeDtypeStruct((t, n), BF16), jax.ShapeDtypeStruct((ns, t), F32)],
        compiler_params=_params("parallel", "arbitrary"),
        name="inproj",
    )(row_starts, x2d, g, w_t, *([w_t] * len(small_pieces)))


def _conv_qk_kernel(x_ref, cw_ref, cb_ref, wq_ref, wk_ref, xc_ref, q_ref, k_ref, *, rows):
    s, c = x_ref.shape
    cw = cw_ref[...]
    cb = cb_ref[...]
    wq = wq_ref[...]
    wk = wk_ref[...]
    k_scale = c ** -0.5
    first_row = lax.broadcasted_iota(jnp.int32, (rows, c), 0) == 0
    carry = [jnp.zeros((1, c), F32) for _ in range(M_CONV - 1)]
    for r in range(s // rows):
        xr = x_ref[r * rows:(r + 1) * rows, :].astype(F32)
        a = cw[0:1, :] * xr
        last = []
        for kk in range(1, M_CONV):
            last.append(a[rows - 1:rows, :])
            a = jnp.where(first_row, carry[kk - 1], pltpu.roll(a, 1, 0)) + cw[kk:kk + 1, :] * xr
        carry = last
        acc = a + cb
        xcb = (acc * _sigmoid(acc)).astype(BF16)
        xc_ref[r * rows:(r + 1) * rows, :] = xcb
        q_ref[r * rows:(r + 1) * rows, :] = _dot(xcb, wq).astype(BF16)
        k_ref[r * rows:(r + 1) * rows, :] = (_dot(xcb, wk) * k_scale).astype(BF16)


def _conv_qk(p, mx_col, conv_w, conv_b, wq, wk, bsz, seq):
    t = p.shape[0]
    heads, hd = wq.shape[0], wq.shape[1]
    width = heads * hd
    blk = pl.BlockSpec((seq, hd), lambda b, h: (b, h))
    out = jax.ShapeDtypeStruct((t, width), BF16)
    return pl.pallas_call(
        functools.partial(_conv_qk_kernel, rows=256),
        grid=(bsz, heads),
        in_specs=[
            pl.BlockSpec((seq, hd), lambda b, h: (b, mx_col // hd + h)),
            pl.BlockSpec((M_CONV, hd), lambda b, h: (0, h)),
            pl.BlockSpec((1, hd), lambda b, h: (0, h)),
            pl.BlockSpec((None, hd, hd), lambda b, h: (h, 0, 0)),
            pl.BlockSpec((None, hd, hd), lambda b, h: (h, 0, 0)),
        ],
        out_specs=[blk, blk, blk],
        out_shape=[out, out, out],
        compiler_params=_params("parallel", "parallel"),
        name="conv_qk",
    )(p, conv_w, conv_b, wq, wk)


def _log_sigmoid(x):
    return jnp.minimum(x, 0.0) - jnp.log1p(jnp.exp(-jnp.abs(x)))


def _mlstm_kernel(bias_ref, q_ref, k_ref, v_ref, o_ref, z_ref, xc_ref, g_ref, ng_ref, sk_ref,
                  y_ref, c_scr, n_scr, m_scr):
    @pl.when(pl.program_id(1) == 0)
    def _():
        c_scr[...] = jnp.zeros_like(c_scr)
        n_scr[...] = jnp.zeros_like(n_scr)
        m_scr[...] = jnp.zeros_like(m_scr)

    ln = q_ref.shape[0]
    heads, hd = c_scr.shape[0], c_scr.shape[1]
    gates = g_ref[...] + bias_ref[...]
    fb_all = _log_sigmoid(gates[heads:2 * heads, :])
    row = lax.broadcasted_iota(jnp.int32, (ln, ln), 0)
    col = lax.broadcasted_iota(jnp.int32, (ln, ln), 1)
    tril = col <= row
    eye = col == row
    for hh in range(heads):
        sl = slice(hh * hd, (hh + 1) * hd)
        ib = gates[hh:hh + 1, :]
        fb = fb_all[hh:hh + 1, :]
        bcum_c = jnp.sum(jnp.where(tril, fb, 0.0), axis=1, keepdims=True)
        bcum_r = jnp.sum(jnp.where(eye, bcum_c, 0.0), axis=0, keepdims=True)
        g_tot = jnp.sum(fb, axis=1, keepdims=True)
        e_r = ib - bcum_r
        m_st = m_scr[hh]
        d_intra = jnp.where(tril, bcum_c + e_r, NEG_INF)
        d_inter = bcum_c + m_st
        m_t = jnp.maximum(d_inter, jnp.max(d_intra, axis=1, keepdims=True))
        w_intra = jnp.exp(d_intra - m_t)
        w_inter = jnp.exp(d_inter - m_t)

        qb = q_ref[:, sl]
        kb = k_ref[:, sl]
        vb = v_ref[:, sl]
        c_st = c_scr[hh]
        n_st = n_scr[hh]
        sc = _nt_dot(qb, kb) * w_intra
        num = _dot(sc.astype(BF16), vb) + w_inter * _dot(qb, c_st.astype(BF16))
        n_rows = jnp.broadcast_to(n_st, (SUBLANES, hd)).astype(BF16)
        qn = _nt_dot(qb, n_rows)[:, 0:1]
        den = jnp.sum(sc, axis=1, keepdims=True) + w_inter * qn
        h = num / jnp.maximum(jnp.abs(den), jnp.exp(-m_t))

        d_state = g_tot + e_r
        m_new = jnp.maximum(g_tot + m_st, jnp.max(d_state, axis=1, keepdims=True))
        w_s_r = jnp.exp(d_state - m_new)
        w_s_c = jnp.sum(jnp.where(eye, w_s_r, 0.0), axis=1, keepdims=True)
        decay = jnp.exp(g_tot + m_st - m_new)
        kw = kb * w_s_c.astype(BF16)
        kv = lax.dot_general(kw, vb, (((0,), (0,)), ((), ())), preferred_element_type=F32)
        c_scr[hh] = decay * c_st + kv
        w_rows = jnp.broadcast_to(w_s_r, (SUBLANES, ln)).astype(BF16)
        n_scr[hh] = decay * n_st + _dot(w_rows, kb)[0:1, :]
        m_scr[hh] = m_new

        hg = h * _sigmoid(o_ref[:, sl].astype(F32))
        mu = jnp.mean(hg, axis=-1, keepdims=True)
        hc = hg - mu
        var = jnp.mean(hc * hc, axis=-1, keepdims=True)
        hn = hc * lax.rsqrt(var + LN_EPS)
        z = z_ref[:, sl].astype(F32)
        y = (hn * ng_ref[:, sl] + sk_ref[:, sl] * xc_ref[:, sl].astype(F32)) * (z * _sigmoid(z))
        y_ref[:, sl] = y.astype(y_ref.dtype)


def _mlstm(p, mx_col, xc, q, k, gates, bias, norm_g, skip, bsz, seq):
    t, width = q.shape
    pc = mx_col // width
    heads = M_HEADS
    hd = width // heads
    ln = M_CHUNK
    nc = seq // ln
    rows = lambda b, c: (b * nc + c, 0)
    vec = pl.BlockSpec((1, width), lambda b, c: (0, 0))
    return pl.pallas_call(
        _mlstm_kernel,
        grid=(bsz, nc),
        in_specs=[
            pl.BlockSpec((2 * heads, 1), lambda b, c: (0, 0)),
            pl.BlockSpec((ln, width), rows),
            pl.BlockSpec((ln, width), rows),
            pl.BlockSpec((ln, width), lambda b, c: (b * nc + c, pc + 1)),
            pl.BlockSpec((ln, width), lambda b, c: (b * nc + c, pc + 2)),
            pl.BlockSpec((ln, width), lambda b, c: (b * nc + c, pc + 3)),
            pl.BlockSpec((ln, width), rows),
            pl.BlockSpec((2 * heads, ln), lambda b, c: (0, b * nc + c)),
            vec, vec,
        ],
        out_specs=pl.BlockSpec((ln, width), rows),
        out_shape=jax.ShapeDtypeStruct((t, width), BF16),
        scratch_shapes=[pltpu.VMEM((heads, hd, hd), F32), pltpu.VMEM((heads, 1, hd), F32),
                        pltpu.VMEM((heads, 1, 1), F32)],
        compiler_params=_params("parallel", "arbitrary"),
        name="mlstm",
    )(bias, q, k, p, p, p, xc, gates, norm_g, skip)


def _compress_kernel(x_ref, pe_ref, w1_ref, w2_ref, o_ref, xf_scr):
    ng, n_sub, dh = o_ref.shape
    half = A_CMP_STRIDE * dh
    n_chunk = xf_scr.shape[0]
    for c in range(n_chunk):
        xf_scr[c] = x_ref[:, c * LANES:(c + 1) * LANES].astype(F32)
    xs = [jnp.concatenate([xf_scr[c, pl.ds(r, n_sub, stride=A_CMP_STRIDE), :] for c in range(n_chunk)], axis=1)
          for r in range(A_CMP_STRIDE)]
    pe = pe_ref[...]
    rid = lax.broadcasted_iota(jnp.int32, (n_sub, dh), 0)
    for g in range(ng):
        t = jnp.concatenate([x[:, g * dh:(g + 1) * dh] for x in xs], axis=1)
        za = _dot((t + pe[0:1, :]).astype(BF16), w1_ref[0:half, :])
        zb = _dot((t + pe[1:2, :]).astype(BF16), w1_ref[half:2 * half, :])
        pre = za + pltpu.roll(zb, n_sub - 1, 0)
        hid = (pre * _sigmoid(pre)).astype(BF16)
        out = _dot(hid, w2_ref[...])
        o_ref[g] = jnp.where(rid < n_sub - 1, out, 0.0).astype(o_ref.dtype)


def _compress(p, kc_col, pe, w1, w2, bsz, seq):
    g = A_KV_HEADS
    dh = A_HEAD_DIM
    kvw = g * dh
    n_sub = seq // A_CMP_STRIDE
    half = A_CMP_STRIDE * dh
    hid = w1.shape[-1]
    return pl.pallas_call(
        _compress_kernel,
        grid=(2, bsz),
        in_specs=[
            pl.BlockSpec((seq, kvw), lambda i, b: (b, kc_col // kvw + i)),
            pl.BlockSpec((None, 2, half), lambda i, b: (i, 0, 0)),
            pl.BlockSpec((None, 2 * half, hid), lambda i, b: (i, 0, 0)),
            pl.BlockSpec((None, hid, dh), lambda i, b: (i, 0, 0)),
        ],
        out_specs=pl.BlockSpec((None, None, g, n_sub, dh), lambda i, b: (i, b, 0, 0, 0)),
        out_shape=jax.ShapeDtypeStruct((2, bsz, g, n_sub, dh), BF16),
        scratch_shapes=[pltpu.VMEM((kvw // LANES, seq, LANES), F32)],
        compiler_params=_params("parallel", "parallel"),
        name="compress",
    )(p, pe, w1, w2)


def _mask_heads(x, ok, fill, tq):
    n = x.shape[1] // tq
    return jnp.concatenate([jnp.where(ok, x[:, h * tq:(h + 1) * tq], fill) for h in range(n)], axis=1)


def _with_ones_rows(vt):
    return jnp.concatenate([vt, jnp.ones((BF16_SUBLANES, vt.shape[1]), vt.dtype)], axis=0)


def _nsa_kernel(q_ref, kc_ref, vct_ref, ks_ref, vs_ref, kw_ref, vw_ref, ps_ref, z_ref, o_ref,
                sel_ref, vst_ref, vwt_ref, kn_ref, osw_ref):
    qi = pl.program_id(1)
    tq = q_ref.shape[0]
    dh = A_HEAD_DIM
    hpg = A_GROUP
    ng = A_KV_HEADS
    r = hpg * tq
    n_cmp = kc_ref.shape[1]
    tk = vwt_ref.shape[2]
    kt = SEL_KEYS_PER_STEP
    tiles_per_iter = kt // tk
    seq = ks_ref.shape[0]
    n_sel = seq // A_SEL_BLOCK
    t0 = qi * tq
    da = dh + BF16_SUBLANES

    @pl.when(qi == 0)
    def _():
        for j in range(seq // tk):
            vst_ref[j] = vs_ref[j * tk:(j + 1) * tk, :].astype(F32).T.astype(BF16)
            vwt_ref[j] = vw_ref[j * tk:(j + 1) * tk, :].astype(F32).T.astype(BF16)
        rows = KEY_NORM_ROWS
        for which, k_ref in enumerate((ks_ref, kw_ref)):
            best = [jnp.zeros((1, 1), F32) for _ in range(ng)]
            for c in range(seq // rows):
                kf = k_ref[c * rows:(c + 1) * rows, :].astype(F32)
                ksq = kf * kf
                for g in range(ng):
                    n2 = jnp.sum(ksq[:, g * dh:(g + 1) * dh], axis=1, keepdims=True)
                    best[g] = jnp.maximum(best[g], jnp.max(n2, axis=0, keepdims=True))
            for g in range(ng):
                kn_ref[which * ng + g] = best[g]

    qt = (q_ref[...].astype(F32) * (dh ** -0.5 * LOG2_E)).T.astype(BF16)
    q4t = [jnp.concatenate([qt[(g * hpg + h) * dh:(g * hpg + h + 1) * dh, :] for h in range(hpg)], axis=1)
           for g in range(ng)]

    cn = lax.broadcasted_iota(jnp.int32, (n_cmp, tq), 0)
    ct = t0 + lax.broadcasted_iota(jnp.int32, (n_cmp, tq), 1)
    valid = (cn * A_CMP_STRIDE + (A_CMP_BLOCK - 1)) <= ct
    jn = lax.broadcasted_iota(jnp.int32, (n_sel, n_cmp), 0)
    nn = lax.broadcasted_iota(jnp.int32, (n_sel, n_cmp), 1)
    ov = ((nn * A_CMP_STRIDE < (jn + 1) * A_SEL_BLOCK)
          & (nn * A_CMP_STRIDE + (A_CMP_BLOCK - 1) >= jn * A_SEL_BLOCK))
    ov_t = jnp.where(ov, 1.0, 0.0).astype(BF16)
    jr = lax.broadcasted_iota(jnp.int32, (n_sel, tq), 0)
    tl = t0 + lax.broadcasted_iota(jnp.int32, (n_sel, tq), 1)
    cur = lax.shift_right_logical(tl, SEL_BLOCK_SHIFT)
    forced = (jr == 0) | (jr == cur) | (jr == cur - 1)
    def compressed_and_select():
        for g in range(ng):
            s_m = _mask_heads(_dot(kc_ref[g], q4t[g]), valid, NEG_INF, tq)
            mx = jnp.max(s_m, axis=0, keepdims=True)
            e = _mask_heads(jnp.exp2(s_m - mx), valid, 0.0, tq)
            lsum = jnp.sum(e, axis=0, keepdims=True)
            p_c = e * (1.0 / jnp.where(lsum > 0.0, lsum, 1.0))
            osw_ref[2, g] = _dot(vct_ref[g], p_c.astype(BF16))
            p_sum = p_c[:, 0:tq]
            for h in range(1, hpg):
                p_sum = p_sum + p_c[:, h * tq:(h + 1) * tq]
            p_hi = p_sum.astype(BF16)
            r1 = p_sum - p_hi.astype(F32)
            p_mid = r1.astype(BF16)
            p_lo = (r1 - p_mid.astype(F32)).astype(BF16)
            imp_t = _dot(ov_t, p_hi) + _dot(ov_t, p_mid) + _dot(ov_t, p_lo)
            score = jnp.where(forced, POS_INF, jnp.where(jr <= cur, imp_t, NEG_INF))
            rank = jnp.zeros((n_sel, tq), F32)
            for jp in range(n_sel):
                sj = score[jp:jp + 1, :]
                beats = (sj > score) | ((sj == score) & (jr > jp))
                rank = rank + jnp.where(beats, 1.0, 0.0)
            sel_ref[g] = jnp.where(rank < float(min(A_SEL_TOPK, n_sel)), 1.0, 0.0)

    qn = [jnp.sqrt(jnp.sum(jnp.square(q4t[g].astype(F32)), axis=0, keepdims=True)) for g in range(ng)]
    bound_s = [qn[g] * jnp.sqrt(kn_ref[g]) * BOUND_SLACK for g in range(ng)]
    bound_w = [qn[g] * jnp.sqrt(kn_ref[ng + g]) * BOUND_SLACK for g in range(ng)]
    worst = bound_s[0]
    for b in bound_s[1:] + bound_w:
        worst = jnp.maximum(worst, b)
    fast = jnp.max(worst) <= MAX_FAST_BOUND

    n_wb = -(-A_WINDOW // tk)
    wk = (n_wb + 1) * tk
    j0 = jnp.maximum(qi - n_wb, 0)
    off_w = pl.multiple_of(j0 * tk, tk)
    c_abs = off_w + lax.broadcasted_iota(jnp.int32, (wk, tq), 0)
    t_abs_w = t0 + lax.broadcasted_iota(jnp.int32, (wk, tq), 1)
    ok_w = (c_abs <= t_abs_w) & (c_abs > t_abs_w - A_WINDOW)
    blocks_per_iter = kt // A_SEL_BLOCK

    def normalised(acc):
        return acc[0:dh, :] * (1.0 / acc[dh:dh + 1, :])

    def value_tiles(v_ref, first, count, g):
        vt = jnp.concatenate([v_ref[first + i, g * dh:(g + 1) * dh, :] for i in range(count)], axis=1)
        return _with_ones_rows(vt)

    def emit_output():
        gs = _sigmoid(ps_ref[...])
        g0 = 2 * M_HEADS
        outs = []
        for g in range(ng):
            o_s = osw_ref[0, g]
            o_w = osw_ref[1, g]
            o_c = osw_ref[2, g]
            for h in range(hpg):
                sl = slice(h * tq, (h + 1) * tq)
                gr = g0 + 3 * (g * hpg + h)
                outs.append(gs[gr:gr + 1, :] * o_c[:, sl] + gs[gr + 1:gr + 2, :] * o_s[:, sl]
                            + gs[gr + 2:gr + 3, :] * o_w[:, sl])
        o_all = jnp.concatenate(outs, axis=0).T
        z = z_ref[...].astype(F32)
        o_ref[...] = (o_all * (z * _sigmoid(z))).astype(o_ref.dtype)

    @pl.when(fast)
    def _():
        kcol = lax.broadcasted_iota(jnp.int32, (kt, dh), 1)
        krow = lax.broadcasted_iota(jnp.int32, (kt, dh), 0)
        kblock = lax.shift_right_logical(krow, SEL_BLOCK_SHIFT)
        aux = jnp.where((kcol == 0) | (kcol == SUBLANES + kblock), 1.0, 0.0).astype(BF16)
        aux_w = jnp.where(lax.broadcasted_iota(jnp.int32, (wk, dh), 1) == 0, 1.0, 0.0).astype(BF16)
        aux_d = jnp.where(lax.broadcasted_iota(jnp.int32, (tq, dh), 1) == 0, 1.0, 0.0).astype(BF16)
        zrows = jnp.zeros((LANES - dh - BF16_SUBLANES, r), BF16)

        first_row = lax.broadcasted_iota(jnp.int32, (SUBLANES, r), 0) == 0

        def query_aug(g, shift, block_bias):
            top = jnp.where(first_row, -shift, 0.0)
            low = jnp.zeros((SUBLANES, r), F32) if block_bias is None else jnp.concatenate([block_bias] * hpg, axis=1)
            return jnp.concatenate([q4t[g], jnp.concatenate([top, low], axis=0).astype(BF16), zrows], axis=0)

        def step(j, accs):
            k_all = ks_ref[j * kt:(j + 1) * kt, :]
            out = []
            for g in range(ng):
                k_aug = jnp.concatenate([k_all[:, g * dh:(g + 1) * dh], aux], axis=1)
                bias = sel_ref[g, j * blocks_per_iter:(j + 1) * blocks_per_iter, :]
                p = jnp.exp2(_dot(k_aug, query_aug(g, bound_s[g], bias))).astype(BF16)
                out.append(accs[g] + _dot(value_tiles(vst_ref, j * tiles_per_iter, tiles_per_iter, g), p))
            return tuple(out)

        def finish(accs):
            kd_all = ks_ref[pl.ds(pl.multiple_of(t0, tq), tq), :]
            causal = (lax.broadcasted_iota(jnp.int32, (tq, tq), 0)
                      <= lax.broadcasted_iota(jnp.int32, (tq, tq), 1))
            kw_all = kw_ref[pl.ds(off_w, wk), :]
            for g in range(ng):
                kd_aug = jnp.concatenate([kd_all[:, g * dh:(g + 1) * dh], aux_d], axis=1)
                s_d = _mask_heads(_dot(kd_aug, query_aug(g, bound_s[g], None)), causal, -EXCLUDE, tq)
                acc = accs[g] + _dot(value_tiles(vst_ref, qi, 1, g), jnp.exp2(s_d).astype(BF16))
                osw_ref[0, g] = normalised(acc)
                kw_aug = jnp.concatenate([kw_all[:, g * dh:(g + 1) * dh], aux_w], axis=1)
                s_w = _mask_heads(_dot(kw_aug, query_aug(g, bound_w[g], None)), ok_w, -EXCLUDE, tq)
                p_w = jnp.exp2(s_w).astype(BF16)
                acc_w = _dot(value_tiles(vwt_ref, j0, 1, g), p_w[0:tk, :])
                for i in range(1, n_wb + 1):
                    acc_w = acc_w + _dot(value_tiles(vwt_ref, j0 + i, 1, g), p_w[i * tk:(i + 1) * tk, :])
                osw_ref[1, g] = normalised(acc_w)

        steps_needed = (t0 + kt - 1) // kt
        for n_steps in range(seq // kt + 1):
            @pl.when(steps_needed == n_steps)
            def _(n_steps=n_steps):
                compressed_and_select()
                before_diag = jr < 2 * qi
                for g in range(ng):
                    sel_ref[g] = jnp.where((sel_ref[g] > 0.5) & before_diag, 0.0, -EXCLUDE)
                accs = tuple(jnp.zeros((da, r), F32) for _ in range(ng))
                for j in range(n_steps):
                    accs = step(j, accs)
                finish(accs)
                emit_output()

    @pl.when(jnp.logical_not(fast))
    def _():
        compressed_and_select()
        zpad = jnp.zeros((dh, r), BF16)
        q_pad = [jnp.concatenate([q4t[g] if gg == g else zpad for gg in range(ng)], axis=0) for g in range(ng)]
        sub_s = lax.broadcasted_iota(jnp.int32, (kt, tq), 0)
        t_abs_s = t0 + lax.broadcasted_iota(jnp.int32, (kt, tq), 1)

        def body(j, carry):
            off = pl.multiple_of(j * kt, kt)
            k_all = ks_ref[pl.ds(off, kt), :]
            causal = (off + sub_s) <= t_abs_s
            out = []
            for g in range(ng):
                m, acc = carry[g]
                s = _dot(k_all, q_pad[g])
                flags = jnp.concatenate(
                    [jnp.broadcast_to(sel_ref[g, pl.ds(j * blocks_per_iter + i, 1), :], (A_SEL_BLOCK, tq))
                     for i in range(blocks_per_iter)], axis=0)
                s = _mask_heads(s, (flags > 0.5) & causal, NEG_INF, tq)
                m_new = jnp.maximum(m, jnp.max(s, axis=0, keepdims=True))
                p = jnp.exp2(s - m_new).astype(BF16)
                vt = value_tiles(vst_ref, j * tiles_per_iter, tiles_per_iter, g)
                out.append((m_new, jnp.exp2(m - m_new) * acc + _dot(vt, p)))
            return tuple(out)

        init = tuple((jnp.full((1, r), NEG_INF, F32), jnp.zeros((da, r), F32)) for _ in range(ng))
        state = lax.fori_loop(0, (t0 + tq - 1) // kt + 1, body, init)
        kw_all = kw_ref[pl.ds(off_w, wk), :]
        for g in range(ng):
            osw_ref[0, g] = normalised(state[g][1])
            s_w = _mask_heads(_dot(kw_all, q_pad[g]), ok_w, NEG_INF, tq)
            p_w = jnp.exp2(s_w - jnp.max(s_w, axis=0, keepdims=True)).astype(BF16)
            acc_w = _dot(value_tiles(vwt_ref, j0, 1, g), p_w[0:tk, :])
            for i in range(1, n_wb + 1):
                acc_w = acc_w + _dot(value_tiles(vwt_ref, j0 + i, 1, g), p_w[i * tk:(i + 1) * tk, :])
            osw_ref[1, g] = normalised(acc_w)
        emit_output()


def _nsa(p, ps, kvc, vct, q_col, kv_col, z_col, bsz, seq):
    t = p.shape[0]
    g = A_KV_HEADS
    dh = A_HEAD_DIM
    aw = A_HEADS * dh
    kvw = g * dh
    tq = Q_BLOCK
    nq = seq // tq
    n_cmp = kvc.shape[3]
    tk = LANES
    rows = lambda b, qi: b * nq + qi
    kv_piece = lambda i: pl.BlockSpec((seq, kvw), lambda b, qi: (b, kv_col // kvw + i))
    return pl.pallas_call(
        _nsa_kernel,
        grid=(bsz, nq),
        in_specs=[
            pl.BlockSpec((tq, aw), lambda b, qi: (rows(b, qi), q_col // aw)),
            pl.BlockSpec((None, None, g, n_cmp, dh), lambda b, qi: (0, b, 0, 0, 0)),
            pl.BlockSpec((None, g, dh, n_cmp), lambda b, qi: (b, 0, 0, 0)),
            kv_piece(2), kv_piece(3), kv_piece(4), kv_piece(5),
            pl.BlockSpec((LANES, tq), lambda b, qi: (0, rows(b, qi))),
            pl.BlockSpec((tq, aw), lambda b, qi: (rows(b, qi), z_col // aw)),
        ],
        out_specs=pl.BlockSpec((tq, aw), lambda b, qi: (rows(b, qi), 0)),
        out_shape=jax.ShapeDtypeStruct((t, aw), BF16),
        scratch_shapes=[pltpu.VMEM((g, seq // A_SEL_BLOCK, tq), F32),
                        pltpu.VMEM((seq // tk, kvw, tk), BF16),
                        pltpu.VMEM((seq // tk, kvw, tk), BF16),
                        pltpu.VMEM((2 * g, 1, 1), F32),
                        pltpu.VMEM((3, g, dh, A_GROUP * tq), F32)],
        compiler_params=_params("parallel", "arbitrary"),
        name="nsa",
    )(p, kvc, vct, p, p, p, p, ps, p)


def _merge_out_kernel(ym_ref, ya_ref, wm_ref, wa_ref, gm_ref, ga_ref, wo_ref, g_ref, x_ref, o_ref):
    um = _dot(ym_ref[...], wm_ref[...])
    ua = _dot(ya_ref[...], wa_ref[...])
    mg = _sigmoid(gm_ref[...].astype(F32)) * um + _sigmoid(ga_ref[...].astype(F32)) * ua
    out = _dot(mg.astype(BF16), wo_ref[...])
    ms = jnp.mean(out * out, axis=-1, keepdims=True)
    o_ref[...] = x_ref[...] + out * lax.rsqrt(ms + RMS_EPS) * g_ref[...]


def _merge_out(ym, ya, wm, wa, p, gm_col, ga_col, wo, g, x2d, tm=256):
    t, d = x2d.shape
    kdim = ym.shape[1]
    resident = lambda shape: pl.BlockSpec(shape, lambda i: (0, 0), pipeline_mode=pl.Buffered(1))
    return pl.pallas_call(
        _merge_out_kernel,
        grid=(t // tm,),
        in_specs=[
            pl.BlockSpec((tm, kdim), lambda i: (i, 0)),
            pl.BlockSpec((tm, kdim), lambda i: (i, 0)),
            resident((kdim, d)),
            resident((kdim, d)),
            pl.BlockSpec((tm, d), lambda i: (i, gm_col // d)),
            pl.BlockSpec((tm, d), lambda i: (i, ga_col // d)),
            resident((d, d)),
            pl.BlockSpec((1, d), lambda i: (0, 0)),
            pl.BlockSpec((tm, d), lambda i: (i, 0)),
        ],
        out_specs=pl.BlockSpec((tm, d), lambda i: (i, 0)),
        out_shape=jax.ShapeDtypeStruct((t, d), F32),
        compiler_params=_params("parallel"),
        name="merge_out",
    )(ym, ya, wm, wa, p, p, wo, g, x2d)


def _layer(x, pre_g, w_in, conv_w, conv_b, w_q, w_k, b_i, b_f, norm_g, skip,
           pe_k, w1_k, w2_k, pe_v, w1_v, w2_v, w_up_m, w_up_a, w_out, post_g):
    bsz, seq, d = x.shape
    t = bsz * seq
    mw = w_up_m.shape[0]
    aw = w_up_a.shape[0]
    g = A_KV_HEADS
    dh = A_HEAD_DIM
    kvw = g * dh
    n_gate = 3 * A_HEADS

    o_i = 4 * mw
    o_f = o_i + M_HEADS
    o_q = o_f + M_HEADS
    o_g = o_q + aw + 6 * kvw
    o_z = o_g + n_gate
    o_m = o_z + aw
    o_kv = o_q + aw
    main_pieces = ((o_m, w_in.shape[1]), (0, o_i), (o_q, o_kv), (o_z, o_m), (o_kv, o_g))
    w_t = w_in.T
    row_starts = jnp.asarray([r // SUBLANES for lo, hi in main_pieces for r in range(lo, hi, INPROJ_TN)], jnp.int32)
    small_pieces = ((o_i, o_q), (o_g, o_z))
    gm_col = 0
    ga_col = d
    mx_col = 2 * d
    q_col = mx_col + 4 * mw
    z_col = q_col + aw
    kv_col = z_col + aw

    x2d = x.reshape(t, d)
    p, ps = _inproj(x2d, pre_g.reshape(1, d), w_t, row_starts, small_pieces, tn=INPROJ_TN)

    xc, q, k = _conv_qk(p, mx_col, conv_w, conv_b.reshape(1, mw), w_q.astype(BF16), w_k.astype(BF16), bsz, seq)
    bias_m = jnp.concatenate([b_i, b_f]).reshape(2 * M_HEADS, 1)
    y_m = _mlstm(p, mx_col, xc, q, k, ps, bias_m, norm_g.reshape(1, mw), skip.reshape(1, mw), bsz, seq)

    half = A_CMP_STRIDE * dh
    pe = jnp.stack([pe_k.reshape(2, half), pe_v.reshape(2, half)])
    w1 = jnp.stack([w1_k, w1_v]).astype(BF16)
    w2 = jnp.stack([w2_k, w2_v]).astype(BF16)
    kvc = _compress(p, kv_col, pe, w1, w2, bsz, seq)
    vct = jnp.swapaxes(kvc[1], 2, 3)
    y_a = _nsa(p, ps, kvc, vct, q_col, kv_col, z_col, bsz, seq)

    out = _merge_out(y_m, y_a, w_up_m.astype(BF16), w_up_a.astype(BF16), p, gm_col, ga_col,
                     w_out.astype(BF16), post_g.reshape(1, d), x2d)
    return out.reshape(bsz, seq, d)


def kernel(x, pre_norm_g, w_in, m_conv_w, m_conv_b, m_w_q, m_w_k, m_b_i, m_b_f, m_norm_g, m_skip, a_pe_k, a_w1_k, a_w2_k, a_pe_v, a_w1_v, a_w2_v, w_up_m, w_up_a, w_out, post_norm_g):
    depth = w_in.shape[0]
    for l in range(depth):
        x = _layer(x, pre_norm_g[l], w_in[l], m_conv_w[l], m_conv_b[l], m_w_q[l], m_w_k[l], m_b_i[l], m_b_f[l],
                   m_norm_g[l], m_skip[l], a_pe_k[l], a_w1_k[l], a_w2_k[l], a_pe_v[l], a_w1_v[l], a_w2_v[l],
                   w_up_m[l], w_up_a[l], w_out[l], post_norm_g[l])
    return x
```

```python
import functools

import jax
import jax.numpy as jnp
from jax import lax
from jax.experimental import pallas as pl
from jax.experimental.pallas import tpu as pltpu

F32 = jnp.float32
BF16 = jnp.bfloat16

M_HEADS = 4
M_CONV = 4
M_CHUNK = 128
A_HEADS = 16
A_KV_HEADS = 4
A_GROUP = A_HEADS // A_KV_HEADS
A_HEAD_DIM = 64
A_CMP_BLOCK = 32
A_CMP_STRIDE = 16
A_SEL_BLOCK = 64
A_SEL_TOPK = 8
A_WINDOW = 512
Q_BLOCK = 128
SEL_KEYS_PER_STEP = 512
EXCLUDE = 1e4
MAX_FAST_BOUND = 40.0
SEL_BLOCK_SHIFT = A_SEL_BLOCK.bit_length() - 1
KEY_NORM_ROWS = 512
BOUND_SLACK = 1.01
INPROJ_TN = 512
RMS_EPS = 1e-6
LN_EPS = 1e-6
NEG_INF = -1e30
POS_INF = 1e30

LANES = 128
SUBLANES = 8
BF16_SUBLANES = 16
LOG2_E = 1.4426950408889634
VMEM_LIMIT_BYTES = 48 * 1024 * 1024


def _nt_dot(a, b):
    return lax.dot_general(a, b, (((1,), (1,)), ((), ())), preferred_element_type=F32)


def _dot(a, b):
    return jnp.dot(a, b, preferred_element_type=F32)


def _sigmoid(x):
    return 1.0 / (1.0 + jnp.exp(-x))


def _params(*sem):
    return pltpu.CompilerParams(dimension_semantics=sem, vmem_limit_bytes=VMEM_LIMIT_BYTES)


def _inproj_kernel(rows_ref, x_ref, g_ref, wt_ref, *rest):
    del rows_ref
    o_ref, ost_ref, h_scr = rest[-3:]
    small_refs = rest[:-3]

    @pl.when(pl.program_id(1) == 0)
    def _():
        xf = x_ref[...]
        ms = jnp.mean(xf * xf, axis=-1, keepdims=True)
        hb = (xf * lax.rsqrt(ms + RMS_EPS) * g_ref[...]).astype(BF16)
        h_scr[...] = hb
        rows = [r[...] for r in small_refs]
        used = sum(r.shape[0] for r in rows)
        rows.append(jnp.zeros((ost_ref.shape[0] - used, xf.shape[1]), F32))
        ost_ref[...] = _nt_dot(jnp.concatenate(rows, axis=0).astype(BF16), hb)

    o_ref[...] = _nt_dot(h_scr[...], wt_ref[...].astype(BF16)).astype(o_ref.dtype)


def _inproj(x2d, g, w_t, row_starts, small_pieces, tm=1024, tn=512):
    t, d = x2d.shape
    n = row_starts.shape[0] * tn
    ns = LANES
    rows_of = lambda lo, hi: pl.BlockSpec((pl.Element(hi - lo), pl.Element(d)), lambda i, j, rows: (lo, 0))
    grid_spec = pltpu.PrefetchScalarGridSpec(
        num_scalar_prefetch=1,
        grid=(t // tm, n // tn),
        in_specs=[
            pl.BlockSpec((tm, d), lambda i, j, rows: (i, 0)),
            pl.BlockSpec((1, d), lambda i, j, rows: (0, 0)),
            pl.BlockSpec((pl.Element(tn), pl.Element(d)), lambda i, j, rows: (rows[j] * SUBLANES, 0)),
        ] + [rows_of(lo, hi) for lo, hi in small_pieces],
        out_specs=[
            pl.BlockSpec((tm, tn), lambda i, j, rows: (i, j)),
            pl.BlockSpec((ns, tm), lambda i, j, rows: (0, i)),
        ],
        scratch_shapes=[pltpu.VMEM((tm, d), BF16)],
    )
    return pl.pallas_call(
        _inproj_kernel,
        grid_spec=grid_spec,
        out_shape=[jax.ShapeDtypeStruct((t, n), BF16), jax.ShapeDtypeStruct((ns, t), F32)],
        compiler_params=_params("parallel", "arbitrary"),
        name="inproj",
    )(row_starts, x2d, g, w_t, *([w_t] * len(small_pieces)))


def _conv_qk_kernel(x_ref, cw_ref, cb_ref, wq_ref, wk_ref, xc_ref, q_ref, k_ref, *, rows):
    s, c = x_ref.shape
    cw = cw_ref[...]
    cb = cb_ref[...]
    wq = wq_ref[...]
    wk = wk_ref[...]
    k_scale = c ** -0.5
    first_row = lax.broadcasted_iota(jnp.int32, (rows, c), 0) == 0
    carry = [jnp.zeros((1, c), F32) for _ in range(M_CONV - 1)]
    for r in range(s // rows):
        xr = x_ref[r * rows:(r + 1) * rows, :].astype(F32)
        a = cw[0:1, :] * xr
        last = []
        for kk in range(1, M_CONV):
            last.append(a[rows - 1:rows, :])
            a = jnp.where(first_row, carry[kk - 1], pltpu.roll(a, 1, 0)) + cw[kk:kk + 1, :] * xr
        carry = last
        acc = a + cb
        xcb = (acc * _sigmoid(acc)).astype(BF16)
        xc_ref[r * rows:(r + 1) * rows, :] = xcb
        q_ref[r * rows:(r + 1) * rows, :] = _dot(xcb, wq).astype(BF16)
        k_ref[r * rows:(r + 1) * rows, :] = (_dot(xcb, wk) * k_scale).astype(BF16)


def _conv_qk(p, mx_col, conv_w, conv_b, wq, wk, bsz, seq):
    t = p.shape[0]
    heads, hd = wq.shape[0], wq.shape[1]
    width = heads * hd
    blk = pl.BlockSpec((seq, hd), lambda b, h: (b, h))
    out = jax.ShapeDtypeStruct((t, width), BF16)
    return pl.pallas_call(
        functools.partial(_conv_qk_kernel, rows=256),
        grid=(bsz, heads),
        in_specs=[
            pl.BlockSpec((seq, hd), lambda b, h: (b, mx_col // hd + h)),
            pl.BlockSpec((M_CONV, hd), lambda b, h: (0, h)),
            pl.BlockSpec((1, hd), lambda b, h: (0, h)),
            pl.BlockSpec((None, hd, hd), lambda b, h: (h, 0, 0)),
            pl.BlockSpec((None, hd, hd), lambda b, h: (h, 0, 0)),
        ],
        out_specs=[blk, blk, blk],
        out_shape=[out, out, out],
        compiler_params=_params("parallel", "parallel"),
        name="conv_qk",
    )(p, conv_w, conv_b, wq, wk)


def _log_sigmoid(x):
    return jnp.minimum(x, 0.0) - jnp.log1p(jnp.exp(-jnp.abs(x)))


def _mlstm_kernel(bias_ref, q_ref, k_ref, v_ref, o_ref, z_ref, xc_ref, g_ref, ng_ref, sk_ref,
                  y_ref, c_scr, n_scr, m_scr):
    @pl.when(pl.program_id(1) == 0)
    def _():
        c_scr[...] = jnp.zeros_like(c_scr)
        n_scr[...] = jnp.zeros_like(n_scr)
        m_scr[...] = jnp.zeros_like(m_scr)

    ln = q_ref.shape[0]
    heads, hd = c_scr.shape[0], c_scr.shape[1]
    gates = g_ref[...] + bias_ref[...]
    fb_all = _log_sigmoid(gates[heads:2 * heads, :])
    row = lax.broadcasted_iota(jnp.int32, (ln, ln), 0)
    col = lax.broadcasted_iota(jnp.int32, (ln, ln), 1)
    tril = col <= row
    eye = col == row
    for hh in range(heads):
        sl = slice(hh * hd, (hh + 1) * hd)
        ib = gates[hh:hh + 1, :]
        fb = fb_all[hh:hh + 1, :]
        bcum_c = jnp.sum(jnp.where(tril, fb, 0.0), axis=1, keepdims=True)
        bcum_r = jnp.sum(jnp.where(eye, bcum_c, 0.0), axis=0, keepdims=True)
        g_tot = jnp.sum(fb, axis=1, keepdims=True)
        e_r = ib - bcum_r
        m_st = m_scr[hh]
        d_intra = jnp.where(tril, bcum_c + e_r, NEG_INF)
        d_inter = bcum_c + m_st
        m_t = jnp.maximum(d_inter, jnp.max(d_intra, axis=1, keepdims=True))
        w_intra = jnp.exp(d_intra - m_t)
        w_inter = jnp.exp(d_inter - m_t)

        qb = q_ref[:, sl]
        kb = k_ref[:, sl]
        vb = v_ref[:, sl]
        c_st = c_scr[hh]
        n_st = n_scr[hh]
        sc = _nt_dot(qb, kb) * w_intra
        num = _dot(sc.astype(BF16), vb) + w_inter * _dot(qb, c_st.astype(BF16))
        n_rows = jnp.broadcast_to(n_st, (SUBLANES, hd)).astype(BF16)
        qn = _nt_dot(qb, n_rows)[:, 0:1]
        den = jnp.sum(sc, axis=1, keepdims=True) + w_inter * qn
        h = num / jnp.maximum(jnp.abs(den), jnp.exp(-m_t))

        d_state = g_tot + e_r
        m_new = jnp.maximum(g_tot + m_st, jnp.max(d_state, axis=1, keepdims=True))
        w_s_r = jnp.exp(d_state - m_new)
        w_s_c = jnp.sum(jnp.where(eye, w_s_r, 0.0), axis=1, keepdims=True)
        decay = jnp.exp(g_tot + m_st - m_new)
        kw = kb * w_s_c.astype(BF16)
        kv = lax.dot_general(kw, vb, (((0,), (0,)), ((), ())), preferred_element_type=F32)
        c_scr[hh] = decay * c_st + kv
        w_rows = jnp.broadcast_to(w_s_r, (SUBLANES, ln)).astype(BF16)
        n_scr[hh] = decay * n_st + _dot(w_rows, kb)[0:1, :]
        m_scr[hh] = m_new

        hg = h * _sigmoid(o_ref[:, sl].astype(F32))
        mu = jnp.mean(hg, axis=-1, keepdims=True)
        hc = hg - mu
        var = jnp.mean(hc * hc, axis=-1, keepdims=True)
        hn = hc * lax.rsqrt(var + LN_EPS)
        z = z_ref[:, sl].astype(F32)
        y = (hn * ng_ref[:, sl] + sk_ref[:, sl] * xc_ref[:, sl].astype(F32)) * (z * _sigmoid(z))
        y_ref[:, sl] = y.astype(y_ref.dtype)


def _mlstm(p, mx_col, xc, q, k, gates, bias, norm_g, skip, bsz, seq):
    t, width = q.shape
    pc = mx_col // width
    heads = M_HEADS
    hd = width // heads
    ln = M_CHUNK
    nc = seq // ln
    rows = lambda b, c: (b * nc + c, 0)
    vec = pl.BlockSpec((1, width), lambda b, c: (0, 0))
    return pl.pallas_call(
        _mlstm_kernel,
        grid=(bsz, nc),
        in_specs=[
            pl.BlockSpec((2 * heads, 1), lambda b, c: (0, 0)),
            pl.BlockSpec((ln, width), rows),
            pl.BlockSpec((ln, width), rows),
            pl.BlockSpec((ln, width), lambda b, c: (b * nc + c, pc + 1)),
            pl.BlockSpec((ln, width), lambda b, c: (b * nc + c, pc + 2)),
            pl.BlockSpec((ln, width), lambda b, c: (b * nc + c, pc + 3)),
            pl.BlockSpec((ln, width), rows),
            pl.BlockSpec((2 * heads, ln), lambda b, c: (0, b * nc + c)),
            vec, vec,
        ],
        out_specs=pl.BlockSpec((ln, width), rows),
        out_shape=jax.ShapeDtypeStruct((t, width), BF16),
        scratch_shapes=[pltpu.VMEM((heads, hd, hd), F32), pltpu.VMEM((heads, 1, hd), F32),
                        pltpu.VMEM((heads, 1, 1), F32)],
        compiler_params=_params("parallel", "arbitrary"),
        name="mlstm",
    )(bias, q, k, p, p, p, xc, gates, norm_g, skip)


def _compress_kernel(x_ref, pe_ref, w1_ref, w2_ref, o_ref, xf_scr):
    ng, n_sub, dh = o_ref.shape
    half = A_CMP_STRIDE * dh
    n_chunk = xf_scr.shape[0]
    for c in range(n_chunk):
        xf_scr[c] = x_ref[:, c * LANES:(c + 1) * LANES].astype(F32)
    xs = [jnp.concatenate([xf_scr[c, pl.ds(r, n_sub, stride=A_CMP_STRIDE), :] for c in range(n_chunk)], axis=1)
          for r in range(A_CMP_STRIDE)]
    pe = pe_ref[...]
    rid = lax.broadcasted_iota(jnp.int32, (n_sub, dh), 0)
    for g in range(ng):
        t = jnp.concatenate([x[:, g * dh:(g + 1) * dh] for x in xs], axis=1)
        za = _dot((t + pe[0:1, :]).astype(BF16), w1_ref[0:half, :])
        zb = _dot((t + pe[1:2, :]).astype(BF16), w1_ref[half:2 * half, :])
        pre = za + pltpu.roll(zb, n_sub - 1, 0)
        hid = (pre * _sigmoid(pre)).astype(BF16)
        out = _dot(hid, w2_ref[...])
        o_ref[g] = jnp.where(rid < n_sub - 1, out, 0.0).astype(o_ref.dtype)


def _compress(p, kc_col, pe, w1, w2, bsz, seq):
    g = A_KV_HEADS
    dh = A_HEAD_DIM
    kvw = g * dh
    n_sub = seq // A_CMP_STRIDE
    half = A_CMP_STRIDE * dh
    hid = w1.shape[-1]
    return pl.pallas_call(
        _compress_kernel,
        grid=(2, bsz),
        in_specs=[
            pl.BlockSpec((seq, kvw), lambda i, b: (b, kc_col // kvw + i)),
            pl.BlockSpec((None, 2, half), lambda i, b: (i, 0, 0)),
            pl.BlockSpec((None, 2 * half, hid), lambda i, b: (i, 0, 0)),
            pl.BlockSpec((None, hid, dh), lambda i, b: (i, 0, 0)),
        ],
        out_specs=pl.BlockSpec((None, None, g, n_sub, dh), lambda i, b: (i, b, 0, 0, 0)),
        out_shape=jax.ShapeDtypeStruct((2, bsz, g, n_sub, dh), BF16),
        scratch_shapes=[pltpu.VMEM((kvw // LANES, seq, LANES), F32)],
        compiler_params=_params("parallel", "parallel"),
        name="compress",
    )(p, pe, w1, w2)


def _mask_heads(x, ok, fill, tq):
    n = x.shape[1] // tq
    return jnp.concatenate([jnp.where(ok, x[:, h * tq:(h + 1) * tq], fill) for h in range(n)], axis=1)


def _with_ones_rows(vt):
    return jnp.concatenate([vt, jnp.ones((BF16_SUBLANES, vt.shape[1]), vt.dtype)], axis=0)


def _nsa_kernel(q_ref, kc_ref, vct_ref, ks_ref, vs_ref, kw_ref, vw_ref, ps_ref, z_ref, o_ref,
                sel_ref, vst_ref, vwt_ref, kn_ref, osw_ref):
    qi = pl.program_id(1)
    tq = q_ref.shape[0]
    dh = A_HEAD_DIM
    hpg = A_GROUP
    ng = A_KV_HEADS
    r = hpg * tq
    n_cmp = kc_ref.shape[1]
    tk = vwt_ref.shape[2]
    kt = SEL_KEYS_PER_STEP
    tiles_per_iter = kt // tk
    seq = ks_ref.shape[0]
    n_sel = seq // A_SEL_BLOCK
    t0 = qi * tq
    da = dh + BF16_SUBLANES

    @pl.when(qi == 0)
    def _():
        for j in range(seq // tk):
            vst_ref[j] = vs_ref[j * tk:(j + 1) * tk, :].astype(F32).T.astype(BF16)
            vwt_ref[j] = vw_ref[j * tk:(j + 1) * tk, :].astype(F32).T.astype(BF16)
        rows = KEY_NORM_ROWS
        for which, k_ref in enumerate((ks_ref, kw_ref)):
            best = [jnp.zeros((1, 1), F32) for _ in range(ng)]
            for c in range(seq // rows):
                kf = k_ref[c * rows:(c + 1) * rows, :].astype(F32)
                ksq = kf * kf
                for g in range(ng):
                    n2 = jnp.sum(ksq[:, g * dh:(g + 1) * dh], axis=1, keepdims=True)
                    best[g] = jnp.maximum(best[g], jnp.max(n2, axis=0, keepdims=True))
            for g in range(ng):
                kn_ref[which * ng + g] = best[g]

    qt = (q_ref[...].astype(F32) * (dh ** -0.5 * LOG2_E)).T.astype(BF16)
    q4t = [jnp.concatenate([qt[(g * hpg + h) * dh:(g * hpg + h + 1) * dh, :] for h in range(hpg)], axis=1)
           for g in range(ng)]

    cn = lax.broadcasted_iota(jnp.int32, (n_cmp, tq), 0)
    ct = t0 + lax.broadcasted_iota(jnp.int32, (n_cmp, tq), 1)
    valid = (cn * A_CMP_STRIDE + (A_CMP_BLOCK - 1)) <= ct
    jn = lax.broadcasted_iota(jnp.int32, (n_sel, n_cmp), 0)
    nn = lax.broadcasted_iota(jnp.int32, (n_sel, n_cmp), 1)
    ov = ((nn * A_CMP_STRIDE < (jn + 1) * A_SEL_BLOCK)
          & (nn * A_CMP_STRIDE + (A_CMP_BLOCK - 1) >= jn * A_SEL_BLOCK))
    ov_t = jnp.where(ov, 1.0, 0.0).astype(BF16)
    jr = lax.broadcasted_iota(jnp.int32, (n_sel, tq), 0)
    tl = t0 + lax.broadcasted_iota(jnp.int32, (n_sel, tq), 1)
    cur = lax.shift_right_logical(tl, SEL_BLOCK_SHIFT)
    forced = (jr == 0) | (jr == cur) | (jr == cur - 1)
    def compressed_and_select():
        for g in range(ng):
            s_m = _mask_heads(_dot(kc_ref[g], q4t[g]), valid, NEG_INF, tq)
            mx = jnp.max(s_m, axis=0, keepdims=True)
            e = _mask_heads(jnp.exp2(s_m - mx), valid, 0.0, tq)
            lsum = jnp.sum(e, axis=0, keepdims=True)
            p_c = e * (1.0 / jnp.where(lsum > 0.0, lsum, 1.0))
            osw_ref[2, g] = _dot(vct_ref[g], p_c.astype(BF16))
            p_sum = p_c[:, 0:tq]
            for h in range(1, hpg):
                p_sum = p_sum + p_c[:, h * tq:(h + 1) * tq]
            p_hi = p_sum.astype(BF16)
            r1 = p_sum - p_hi.astype(F32)
            p_mid = r1.astype(BF16)
            p_lo = (r1 - p_mid.astype(F32)).astype(BF16)
            imp_t = _dot(ov_t, p_hi) + _dot(ov_t, p_mid) + _dot(ov_t, p_lo)
            score = jnp.where(forced, POS_INF, jnp.where(jr <= cur, imp_t, NEG_INF))
            rank = jnp.zeros((n_sel, tq), F32)
            for jp in range(n_sel):
                sj = score[jp:jp + 1, :]
                beats = (sj > score) | ((sj == score) & (jr > jp))
                rank = rank + jnp.where(beats, 1.0, 0.0)
            sel_ref[g] = jnp.where(rank < float(min(A_SEL_TOPK, n_sel)), 1.0, 0.0)

    qn = [jnp.sqrt(jnp.sum(jnp.square(q4t[g].astype(F32)), axis=0, keepdims=True)) for g in range(ng)]
    bound_s = [qn[g] * jnp.sqrt(kn_ref[g]) * BOUND_SLACK for g in range(ng)]
    bound_w = [qn[g] * jnp.sqrt(kn_ref[ng + g]) * BOUND_SLACK for g in range(ng)]
    worst = bound_s[0]
    for b in bound_s[1:] + bound_w:
        worst = jnp.maximum(worst, b)
    fast = jnp.max(worst) <= MAX_FAST_BOUND

    n_wb = -(-A_WINDOW // tk)
    wk = (n_wb + 1) * tk
    j0 = jnp.maximum(qi - n_wb, 0)
    off_w = pl.multiple_of(j0 * tk, tk)
    c_abs = off_w + lax.broadcasted_iota(jnp.int32, (wk, tq), 0)
    t_abs_w = t0 + lax.broadcasted_iota(jnp.int32, (wk, tq), 1)
    ok_w = (c_abs <= t_abs_w) & (c_abs > t_abs_w - A_WINDOW)
    blocks_per_iter = kt // A_SEL_BLOCK

    def normalised(acc):
        return acc[0:dh, :] * (1.0 / acc[dh:dh + 1, :])

    def value_tiles(v_ref, first, count, g):
        vt = jnp.concatenate([v_ref[first + i, g * dh:(g + 1) * dh, :] for i in range(count)], axis=1)
        return _with_ones_rows(vt)

    def emit_output():
        gs = _sigmoid(ps_ref[...])
        g0 = 2 * M_HEADS
        outs = []
        for g in range(ng):
            o_s = osw_ref[0, g]
            o_w = osw_ref[1, g]
            o_c = osw_ref[2, g]
            for h in range(hpg):
                sl = slice(h * tq, (h + 1) * tq)
                gr = g0 + 3 * (g * hpg + h)
                outs.append(gs[gr:gr + 1, :] * o_c[:, sl] + gs[gr + 1:gr + 2, :] * o_s[:, sl]
                            + gs[gr + 2:gr + 3, :] * o_w[:, sl])
        o_all = jnp.concatenate(outs, axis=0).T
        z = z_ref[...].astype(F32)
        o_ref[...] = (o_all * (z * _sigmoid(z))).astype(o_ref.dtype)

    @pl.when(fast)
    def _():
        kcol = lax.broadcasted_iota(jnp.int32, (kt, dh), 1)
        krow = lax.broadcasted_iota(jnp.int32, (kt, dh), 0)
        kblock = lax.shift_right_logical(krow, SEL_BLOCK_SHIFT)
        aux = jnp.where((kcol == 0) | (kcol == SUBLANES + kblock), 1.0, 0.0).astype(BF16)
        aux_w = jnp.where(lax.broadcasted_iota(jnp.int32, (wk, dh), 1) == 0, 1.0, 0.0).astype(BF16)
        aux_d = jnp.where(lax.broadcasted_iota(jnp.int32, (tq, dh), 1) == 0, 1.0, 0.0).astype(BF16)
        zrows = jnp.zeros((LANES - dh - BF16_SUBLANES, r), BF16)

        first_row = lax.broadcasted_iota(jnp.int32, (SUBLANES, r), 0) == 0

        def query_aug(g, shift, block_bias):
            top = jnp.where(first_row, -shift, 0.0)
            low = jnp.zeros((SUBLANES, r), F32) if block_bias is None else jnp.concatenate([block_bias] * hpg, axis=1)
            return jnp.concatenate([q4t[g], jnp.concatenate([top, low], axis=0).astype(BF16), zrows], axis=0)

        def step(j, accs):
            k_all = ks_ref[j * kt:(j + 1) * kt, :]
            out = []
            for g in range(ng):
                k_aug = jnp.concatenate([k_all[:, g * dh:(g + 1) * dh], aux], axis=1)
                bias = sel_ref[g, j * blocks_per_iter:(j + 1) * blocks_per_iter, :]
                p = jnp.exp2(_dot(k_aug, query_aug(g, bound_s[g], bias))).astype(BF16)
                out.append(accs[g] + _dot(value_tiles(vst_ref, j * tiles_per_iter, tiles_per_iter, g), p))
            return tuple(out)

        def finish(accs):
            kd_all = ks_ref[pl.ds(pl.multiple_of(t0, tq), tq), :]
            causal = (lax.broadcasted_iota(jnp.int32, (tq, tq), 0)
                      <= lax.broadcasted_iota(jnp.int32, (tq, tq), 1))
            kw_all = kw_ref[pl.ds(off_w, wk), :]
            for g in range(ng):
                kd_aug = jnp.concatenate([kd_all[:, g * dh:(g + 1) * dh], aux_d], axis=1)
                s_d = _mask_heads(_dot(kd_aug, query_aug(g, bound_s[g], None)), causal, -EXCLUDE, tq)
                acc = accs[g] + _dot(value_tiles(vst_ref, qi, 1, g), jnp.exp2(s_d).astype(BF16))
                osw_ref[0, g] = normalised(acc)
                kw_aug = jnp.concatenate([kw_all[:, g * dh:(g + 1) * dh], aux_w], axis=1)
                s_w = _mask_heads(_dot(kw_aug, query_aug(g, bound_w[g], None)), ok_w, -EXCLUDE, tq)
                p_w = jnp.exp2(s_w).astype(BF16)
                acc_w = _dot(value_tiles(vwt_ref, j0, 1, g), p_w[0:tk, :])
                for i in range(1, n_wb + 1):
                    acc_w = acc_w + _dot(value_tiles(vwt_ref, j0 + i, 1, g), p_w[i * tk:(i + 1) * tk, :])
                osw_ref[1, g] = normalised(acc_w)

        steps_needed = (t0 + kt - 1) // kt
        for n_steps in range(seq // kt + 1):
            @pl.when(steps_needed == n_steps)
            def _(n_steps=n_steps):
                compressed_and_select()
                before_diag = jr < qi * (tq // A_SEL_BLOCK)
                for g in range(ng):
                    sel_ref[g] = jnp.where((sel_ref[g] > 0.5) & before_diag, 0.0, -EXCLUDE)
                accs = tuple(jnp.zeros((da, r), F32) for _ in range(ng))
                for j in range(n_steps):
                    accs = step(j, accs)
                finish(accs)
                emit_output()

    @pl.when(jnp.logical_not(fast))
    def _():
        compressed_and_select()
        zpad = jnp.zeros((dh, r), BF16)
        q_pad = [jnp.concatenate([q4t[g] if gg == g else zpad for gg in range(ng)], axis=0) for g in range(ng)]
        sub_s = lax.broadcasted_iota(jnp.int32, (kt, tq), 0)
        t_abs_s = t0 + lax.broadcasted_iota(jnp.int32, (kt, tq), 1)

        def body(j, carry):
            off = pl.multiple_of(j * kt, kt)
            k_all = ks_ref[pl.ds(off, kt), :]
            causal = (off + sub_s) <= t_abs_s
            out = []
            for g in range(ng):
                m, acc = carry[g]
                s = _dot(k_all, q_pad[g])
                flags = jnp.concatenate(
                    [jnp.broadcast_to(sel_ref[g, pl.ds(j * blocks_per_iter + i, 1), :], (A_SEL_BLOCK, tq))
                     for i in range(blocks_per_iter)], axis=0)
                s = _mask_heads(s, (flags > 0.5) & causal, NEG_INF, tq)
                m_new = jnp.maximum(m, jnp.max(s, axis=0, keepdims=True))
                p = jnp.exp2(s - m_new).astype(BF16)
                vt = value_tiles(vst_ref, j * tiles_per_iter, tiles_per_iter, g)
                out.append((m_new, jnp.exp2(m - m_new) * acc + _dot(vt, p)))
            return tuple(out)

        init = tuple((jnp.full((1, r), NEG_INF, F32), jnp.zeros((da, r), F32)) for _ in range(ng))
        state = lax.fori_loop(0, (t0 + tq - 1) // kt + 1, body, init)
        kw_all = kw_ref[pl.ds(off_w, wk), :]
        for g in range(ng):
            osw_ref[0, g] = normalised(state[g][1])
            s_w = _mask_heads(_dot(kw_all, q_pad[g]), ok_w, NEG_INF, tq)
            p_w = jnp.exp2(s_w - jnp.max(s_w, axis=0, keepdims=True)).astype(BF16)
            acc_w = _dot(value_tiles(vwt_ref, j0, 1, g), p_w[0:tk, :])
            for i in range(1, n_wb + 1):
                acc_w = acc_w + _dot(value_tiles(vwt_ref, j0 + i, 1, g), p_w[i * tk:(i + 1) * tk, :])
            osw_ref[1, g] = normalised(acc_w)
        emit_output()


def _nsa(p, ps, kvc, vct, q_col, kv_col, z_col, bsz, seq):
    t = p.shape[0]
    g = A_KV_HEADS
    dh = A_HEAD_DIM
    aw = A_HEADS * dh
    kvw = g * dh
    tq = Q_BLOCK
    nq = seq // tq
    n_cmp = kvc.shape[3]
    tk = LANES
    rows = lambda b, qi: b * nq + qi
    kv_piece = lambda i: pl.BlockSpec((seq, kvw), lambda b, qi: (b, kv_col // kvw + i))
    return pl.pallas_call(
        _nsa_kernel,
        grid=(bsz, nq),
        in_specs=[
            pl.BlockSpec((tq, aw), lambda b, qi: (rows(b, qi), q_col // aw)),
            pl.BlockSpec((None, None, g, n_cmp, dh), lambda b, qi: (0, b, 0, 0, 0)),
            pl.BlockSpec((None, g, dh, n_cmp), lambda b, qi: (b, 0, 0, 0)),
            kv_piece(2), kv_piece(3), kv_piece(4), kv_piece(5),
            pl.BlockSpec((LANES, tq), lambda b, qi: (0, rows(b, qi))),
            pl.BlockSpec((tq, aw), lambda b, qi: (rows(b, qi), z_col // aw)),
        ],
        out_specs=pl.BlockSpec((tq, aw), lambda b, qi: (rows(b, qi), 0)),
        out_shape=jax.ShapeDtypeStruct((t, aw), BF16),
        scratch_shapes=[pltpu.VMEM((g, seq // A_SEL_BLOCK, tq), F32),
                        pltpu.VMEM((seq // tk, kvw, tk), BF16),
                        pltpu.VMEM((seq // tk, kvw, tk), BF16),
                        pltpu.VMEM((2 * g, 1, 1), F32),
                        pltpu.VMEM((3, g, dh, A_GROUP * tq), F32)],
        compiler_params=_params("parallel", "arbitrary"),
        name="nsa",
    )(p, kvc, vct, p, p, p, p, ps, p)


def _merge_out_kernel(ym_ref, ya_ref, wm_ref, wa_ref, gm_ref, ga_ref, wo_ref, g_ref, x_ref, o_ref):
    um = _dot(ym_ref[...], wm_ref[...])
    ua = _dot(ya_ref[...], wa_ref[...])
    mg = _sigmoid(gm_ref[...].astype(F32)) * um + _sigmoid(ga_ref[...].astype(F32)) * ua
    out = _dot(mg.astype(BF16), wo_ref[...])
    ms = jnp.mean(out * out, axis=-1, keepdims=True)
    o_ref[...] = x_ref[...] + out * lax.rsqrt(ms + RMS_EPS) * g_ref[...]


def _merge_out(ym, ya, wm, wa, p, gm_col, ga_col, wo, g, x2d, tm=256):
    t, d = x2d.shape
    kdim = ym.shape[1]
    resident = lambda shape: pl.BlockSpec(shape, lambda i: (0, 0), pipeline_mode=pl.Buffered(1))
    return pl.pallas_call(
        _merge_out_kernel,
        grid=(t // tm,),
        in_specs=[
            pl.BlockSpec((tm, kdim), lambda i: (i, 0)),
            pl.BlockSpec((tm, kdim), lambda i: (i, 0)),
            resident((kdim, d)),
            resident((kdim, d)),
            pl.BlockSpec((tm, d), lambda i: (i, gm_col // d)),
            pl.BlockSpec((tm, d), lambda i: (i, ga_col // d)),
            resident((d, d)),
            pl.BlockSpec((1, d), lambda i: (0, 0)),
            pl.BlockSpec((tm, d), lambda i: (i, 0)),
        ],
        out_specs=pl.BlockSpec((tm, d), lambda i: (i, 0)),
        out_shape=jax.ShapeDtypeStruct((t, d), F32),
        compiler_params=_params("parallel"),
        name="merge_out",
    )(ym, ya, wm, wa, p, p, wo, g, x2d)


def _layer(x, pre_g, w_in, conv_w, conv_b, w_q, w_k, b_i, b_f, norm_g, skip,
           pe_k, w1_k, w2_k, pe_v, w1_v, w2_v, w_up_m, w_up_a, w_out, post_g):
    bsz, seq, d = x.shape
    t = bsz * seq
    mw = w_up_m.shape[0]
    aw = w_up_a.shape[0]
    g = A_KV_HEADS
    dh = A_HEAD_DIM
    kvw = g * dh
    n_gate = 3 * A_HEADS

    o_i = 4 * mw
    o_f = o_i + M_HEADS
    o_q = o_f + M_HEADS
    o_g = o_q + aw + 6 * kvw
    o_z = o_g + n_gate
    o_m = o_z + aw
    o_kv = o_q + aw
    main_pieces = ((o_m, w_in.shape[1]), (0, o_i), (o_q, o_kv), (o_z, o_m), (o_kv, o_g))
    w_t = w_in.T
    row_starts = jnp.asarray([r // SUBLANES for lo, hi in main_pieces for r in range(lo, hi, INPROJ_TN)], jnp.int32)
    small_pieces = ((o_i, o_q), (o_g, o_z))
    gm_col = 0
    ga_col = d
    mx_col = 2 * d
    q_col = mx_col + 4 * mw
    z_col = q_col + aw
    kv_col = z_col + aw

    x2d = x.reshape(t, d)
    p, ps = _inproj(x2d, pre_g.reshape(1, d), w_t, row_starts, small_pieces, tn=INPROJ_TN)

    xc, q, k = _conv_qk(p, mx_col, conv_w, conv_b.reshape(1, mw), w_q.astype(BF16), w_k.astype(BF16), bsz, seq)
    bias_m = jnp.concatenate([b_i, b_f]).reshape(2 * M_HEADS, 1)
    y_m = _mlstm(p, mx_col, xc, q, k, ps, bias_m, norm_g.reshape(1, mw), skip.reshape(1, mw), bsz, seq)

    half = A_CMP_STRIDE * dh
    pe = jnp.stack([pe_k.reshape(2, half), pe_v.reshape(2, half)])
    w1 = jnp.stack([w1_k, w1_v]).astype(BF16)
    w2 = jnp.stack([w2_k, w2_v]).astype(BF16)
    kvc = _compress(p, kv_col, pe, w1, w2, bsz, seq)
    vct = jnp.swapaxes(kvc[1], 2, 3)
    y_a = _nsa(p, ps, kvc, vct, q_col, kv_col, z_col, bsz, seq)

    out = _merge_out(y_m, y_a, w_up_m.astype(BF16), w_up_a.astype(BF16), p, gm_col, ga_col,
                     w_out.astype(BF16), post_g.reshape(1, d), x2d)
    return out.reshape(bsz, seq, d)


def kernel(x, pre_norm_g, w_in, m_conv_w, m_conv_b, m_w_q, m_w_k, m_b_i, m_b_f, m_norm_g, m_skip, a_pe_k, a_w1_k, a_w2_k, a_pe_v, a_w1_v, a_w2_v, w_up_m, w_up_a, w_out, post_norm_g):
    depth = w_in.shape[0]
    for l in range(depth):
        x = _layer(x, pre_norm_g[l], w_in[l], m_conv_w[l], m_conv_b[l], m_w_q[l], m_w_k[l], m_b_i[l], m_b_f[l],
                   m_norm_g[l], m_skip[l], a_pe_k[l], a_w1_k[l], a_w2_k[l], a_pe_v[l], a_w1_v[l], a_w2_v[l],
                   w_up_m[l], w_up_a[l], w_out[l], post_norm_g[l])
    return x
```

```python
import functools

import jax
import jax.numpy as jnp
from jax import lax
from jax.experimental import pallas as pl
from jax.experimental.pallas import tpu as pltpu

F32 = jnp.float32
BF16 = jnp.bfloat16

M_HEADS = 4
M_CONV = 4
M_CHUNK = 128
A_HEADS = 16
A_KV_HEADS = 4
A_GROUP = A_HEADS // A_KV_HEADS
A_HEAD_DIM = 64
A_CMP_BLOCK = 32
A_CMP_STRIDE = 16
A_SEL_BLOCK = 64
A_SEL_TOPK = 8
A_WINDOW = 512
Q_BLOCK = 128
SEL_KEYS_PER_STEP = 512
EXCLUDE = 1e4
MAX_FAST_BOUND = 40.0
SEL_BLOCK_SHIFT = A_SEL_BLOCK.bit_length() - 1
KEY_NORM_ROWS = 512
BOUND_SLACK = 1.01
INPROJ_TN = 512
RMS_EPS = 1e-6
LN_EPS = 1e-6
NEG_INF = -1e30
POS_INF = 1e30

LANES = 128
SUBLANES = 8
BF16_SUBLANES = 16
LOG2_E = 1.4426950408889634
VMEM_LIMIT_BYTES = 48 * 1024 * 1024


def _nt_dot(a, b):
    return lax.dot_general(a, b, (((1,), (1,)), ((), ())), preferred_element_type=F32)


def _dot(a, b):
    return jnp.dot(a, b, preferred_element_type=F32)


def _sigmoid(x):
    return 1.0 / (1.0 + jnp.exp(-x))


def _params(*sem):
    return pltpu.CompilerParams(dimension_semantics=sem, vmem_limit_bytes=VMEM_LIMIT_BYTES)


def _inproj_kernel(rows_ref, x_ref, g_ref, wt_ref, *rest):
    del rows_ref
    o_ref, ost_ref, h_scr = rest[-3:]
    small_refs = rest[:-3]

    @pl.when(pl.program_id(1) == 0)
    def _():
        xf = x_ref[...]
        ms = jnp.mean(xf * xf, axis=-1, keepdims=True)
        hb = (xf * lax.rsqrt(ms + RMS_EPS) * g_ref[...]).astype(BF16)
        h_scr[...] = hb
        rows = [r[...] for r in small_refs]
        used = sum(r.shape[0] for r in rows)
        rows.append(jnp.zeros((ost_ref.shape[0] - used, xf.shape[1]), F32))
        ost_ref[...] = _nt_dot(jnp.concatenate(rows, axis=0).astype(BF16), hb)

    o_ref[...] = _nt_dot(h_scr[...], wt_ref[...].astype(BF16)).astype(o_ref.dtype)


def _inproj(x2d, g, w_t, row_starts, small_pieces, tm=1024, tn=512):
    t, d = x2d.shape
    n = row_starts.shape[0] * tn
    ns = LANES
    rows_of = lambda lo, hi: pl.BlockSpec((pl.Element(hi - lo), pl.Element(d)), lambda i, j, rows: (lo, 0))
    grid_spec = pltpu.PrefetchScalarGridSpec(
        num_scalar_prefetch=1,
        grid=(t // tm, n // tn),
        in_specs=[
            pl.BlockSpec((tm, d), lambda i, j, rows: (i, 0)),
            pl.BlockSpec((1, d), lambda i, j, rows: (0, 0)),
            pl.BlockSpec((pl.Element(tn), pl.Element(d)), lambda i, j, rows: (rows[j] * SUBLANES, 0)),
        ] + [rows_of(lo, hi) for lo, hi in small_pieces],
        out_specs=[
            pl.BlockSpec((tm, tn), lambda i, j, rows: (i, j)),
            pl.BlockSpec((ns, tm), lambda i, j, rows: (0, i)),
        ],
        scratch_shapes=[pltpu.VMEM((tm, d), BF16)],
    )
    return pl.pallas_call(
        _inproj_kernel,
        grid_spec=grid_spec,
        out_shape=[jax.ShapeDtypeStruct((t, n), BF16), jax.ShapeDtypeStruct((ns, t), F32)],
        compiler_params=_params("parallel", "arbitrary"),
        name="inproj",
    )(row_starts, x2d, g, w_t, *([w_t] * len(small_pieces)))


def _conv_qk_kernel(x_ref, cw_ref, cb_ref, wq_ref, wk_ref, xc_ref, q_ref, k_ref, *, rows):
    s, c = x_ref.shape
    cw = cw_ref[...]
    cb = cb_ref[...]
    wq = wq_ref[...]
    wk = wk_ref[...]
    k_scale = c ** -0.5
    first_row = lax.broadcasted_iota(jnp.int32, (rows, c), 0) == 0
    carry = [jnp.zeros((1, c), F32) for _ in range(M_CONV - 1)]
    for r in range(s // rows):
        xr = x_ref[r * rows:(r + 1) * rows, :].astype(F32)
        a = cw[0:1, :] * xr
        last = []
        for kk in range(1, M_CONV):
            last.append(a[rows - 1:rows, :])
            a = jnp.where(first_row, carry[kk - 1], pltpu.roll(a, 1, 0)) + cw[kk:kk + 1, :] * xr
        carry = last
        acc = a + cb
        xcb = (acc * _sigmoid(acc)).astype(BF16)
        xc_ref[r * rows:(r + 1) * rows, :] = xcb
        q_ref[r * rows:(r + 1) * rows, :] = _dot(xcb, wq).astype(BF16)
        k_ref[r * rows:(r + 1) * rows, :] = (_dot(xcb, wk) * k_scale).astype(BF16)


def _conv_qk(p, mx_col, conv_w, conv_b, wq, wk, bsz, seq):
    t = p.shape[0]
    heads, hd = wq.shape[0], wq.shape[1]
    width = heads * hd
    blk = pl.BlockSpec((seq, hd), lambda b, h: (b, h))
    out = jax.ShapeDtypeStruct((t, width), BF16)
    return pl.pallas_call(
        functools.partial(_conv_qk_kernel, rows=256),
        grid=(bsz, heads),
        in_specs=[
            pl.BlockSpec((seq, hd), lambda b, h: (b, mx_col // hd + h)),
            pl.BlockSpec((M_CONV, hd), lambda b, h: (0, h)),
            pl.BlockSpec((1, hd), lambda b, h: (0, h)),
            pl.BlockSpec((None, hd, hd), lambda b, h: (h, 0, 0)),
            pl.BlockSpec((None, hd, hd), lambda b, h: (h, 0, 0)),
        ],
        out_specs=[blk, blk, blk],
        out_shape=[out, out, out],
        compiler_params=_params("parallel", "parallel"),
        name="conv_qk",
    )(p, conv_w, conv_b, wq, wk)


def _log_sigmoid(x):
    return jnp.minimum(x, 0.0) - jnp.log1p(jnp.exp(-jnp.abs(x)))


def _mlstm_kernel(bias_ref, q_ref, k_ref, v_ref, o_ref, z_ref, xc_ref, g_ref, ng_ref, sk_ref,
                  y_ref, c_scr, n_scr, m_scr):
    @pl.when(pl.program_id(1) == 0)
    def _():
        c_scr[...] = jnp.zeros_like(c_scr)
        n_scr[...] = jnp.zeros_like(n_scr)
        m_scr[...] = jnp.zeros_like(m_scr)

    ln = q_ref.shape[0]
    heads, hd = c_scr.shape[0], c_scr.shape[1]
    gates = g_ref[...] + bias_ref[...]
    fb_all = _log_sigmoid(gates[heads:2 * heads, :])
    row = lax.broadcasted_iota(jnp.int32, (ln, ln), 0)
    col = lax.broadcasted_iota(jnp.int32, (ln, ln), 1)
    tril = col <= row
    eye = col == row
    for hh in range(heads):
        sl = slice(hh * hd, (hh + 1) * hd)
        ib = gates[hh:hh + 1, :]
        fb = fb_all[hh:hh + 1, :]
        bcum_c = jnp.sum(jnp.where(tril, fb, 0.0), axis=1, keepdims=True)
        bcum_r = jnp.sum(jnp.where(eye, bcum_c, 0.0), axis=0, keepdims=True)
        g_tot = jnp.sum(fb, axis=1, keepdims=True)
        e_r = ib - bcum_r
        m_st = m_scr[hh]
        d_intra = jnp.where(tril, bcum_c + e_r, NEG_INF)
        d_inter = bcum_c + m_st
        m_t = jnp.maximum(d_inter, jnp.max(d_intra, axis=1, keepdims=True))
        w_intra = jnp.exp(d_intra - m_t)
        w_inter = jnp.exp(d_inter - m_t)

        qb = q_ref[:, sl]
        kb = k_ref[:, sl]
        vb = v_ref[:, sl]
        c_st = c_scr[hh]
        n_st = n_scr[hh]
        sc = _nt_dot(qb, kb) * w_intra
        num = _dot(sc.astype(BF16), vb) + w_inter * _dot(qb, c_st.astype(BF16))
        n_rows = jnp.broadcast_to(n_st, (SUBLANES, hd)).astype(BF16)
        qn = _nt_dot(qb, n_rows)[:, 0:1]
        den = jnp.sum(sc, axis=1, keepdims=True) + w_inter * qn
        h = num / jnp.maximum(jnp.abs(den), jnp.exp(-m_t))

        d_state = g_tot + e_r
        m_new = jnp.maximum(g_tot + m_st, jnp.max(d_state, axis=1, keepdims=True))
        w_s_r = jnp.exp(d_state - m_new)
        w_s_c = jnp.sum(jnp.where(eye, w_s_r, 0.0), axis=1, keepdims=True)
        decay = jnp.exp(g_tot + m_st - m_new)
        kw = kb * w_s_c.astype(BF16)
        kv = lax.dot_general(kw, vb, (((0,), (0,)), ((), ())), preferred_element_type=F32)
        c_scr[hh] = decay * c_st + kv
        w_rows = jnp.broadcast_to(w_s_r, (SUBLANES, ln)).astype(BF16)
        n_scr[hh] = decay * n_st + _dot(w_rows, kb)[0:1, :]
        m_scr[hh] = m_new

        hg = h * _sigmoid(o_ref[:, sl].astype(F32))
        mu = jnp.mean(hg, axis=-1, keepdims=True)
        hc = hg - mu
        var = jnp.mean(hc * hc, axis=-1, keepdims=True)
        hn = hc * lax.rsqrt(var + LN_EPS)
        z = z_ref[:, sl].astype(F32)
        y = (hn * ng_ref[:, sl] + sk_ref[:, sl] * xc_ref[:, sl].astype(F32)) * (z * _sigmoid(z))
        y_ref[:, sl] = y.astype(y_ref.dtype)


def _mlstm(p, mx_col, xc, q, k, gates, bias, norm_g, skip, bsz, seq):
    t, width = q.shape
    pc = mx_col // width
    heads = M_HEADS
    hd = width // heads
    ln = M_CHUNK
    nc = seq // ln
    rows = lambda b, c: (b * nc + c, 0)
    vec = pl.BlockSpec((1, width), lambda b, c: (0, 0))
    return pl.pallas_call(
        _mlstm_kernel,
        grid=(bsz, nc),
        in_specs=[
            pl.BlockSpec((2 * heads, 1), lambda b, c: (0, 0)),
            pl.BlockSpec((ln, width), rows),
            pl.BlockSpec((ln, width), rows),
            pl.BlockSpec((ln, width), lambda b, c: (b * nc + c, pc + 1)),
            pl.BlockSpec((ln, width), lambda b, c: (b * nc + c, pc + 2)),
            pl.BlockSpec((ln, width), lambda b, c: (b * nc + c, pc + 3)),
            pl.BlockSpec((ln, width), rows),
            pl.BlockSpec((2 * heads, ln), lambda b, c: (0, b * nc + c)),
            vec, vec,
        ],
        out_specs=pl.BlockSpec((ln, width), rows),
        out_shape=jax.ShapeDtypeStruct((t, width), BF16),
        scratch_shapes=[pltpu.VMEM((heads, hd, hd), F32), pltpu.VMEM((heads, 1, hd), F32),
                        pltpu.VMEM((heads, 1, 1), F32)],
        compiler_params=_params("parallel", "arbitrary"),
        name="mlstm",
    )(bias, q, k, p, p, p, xc, gates, norm_g, skip)


def _compress_kernel(x_ref, pe_ref, w1_ref, w2_ref, o_ref, xf_scr):
    ng, n_sub, dh = o_ref.shape
    half = A_CMP_STRIDE * dh
    n_chunk = xf_scr.shape[0]
    for c in range(n_chunk):
        xf_scr[c] = x_ref[:, c * LANES:(c + 1) * LANES].astype(F32)
    xs = [jnp.concatenate([xf_scr[c, pl.ds(r, n_sub, stride=A_CMP_STRIDE), :] for c in range(n_chunk)], axis=1)
          for r in range(A_CMP_STRIDE)]
    pe = pe_ref[...]
    rid = lax.broadcasted_iota(jnp.int32, (n_sub, dh), 0)
    for g in range(ng):
        t = jnp.concatenate([x[:, g * dh:(g + 1) * dh] for x in xs], axis=1)
        za = _dot((t + pe[0:1, :]).astype(BF16), w1_ref[0:half, :])
        zb = _dot((t + pe[1:2, :]).astype(BF16), w1_ref[half:2 * half, :])
        pre = za + pltpu.roll(zb, n_sub - 1, 0)
        hid = (pre * _sigmoid(pre)).astype(BF16)
        out = _dot(hid, w2_ref[...])
        o_ref[g] = jnp.where(rid < n_sub - 1, out, 0.0).astype(o_ref.dtype)


def _compress(p, kc_col, pe, w1, w2, bsz, seq):
    g = A_KV_HEADS
    dh = A_HEAD_DIM
    kvw = g * dh
    n_sub = seq // A_CMP_STRIDE
    half = A_CMP_STRIDE * dh
    hid = w1.shape[-1]
    return pl.pallas_call(
        _compress_kernel,
        grid=(2, bsz),
        in_specs=[
            pl.BlockSpec((seq, kvw), lambda i, b: (b, kc_col // kvw + i)),
            pl.BlockSpec((None, 2, half), lambda i, b: (i, 0, 0)),
            pl.BlockSpec((None, 2 * half, hid), lambda i, b: (i, 0, 0)),
            pl.BlockSpec((None, hid, dh), lambda i, b: (i, 0, 0)),
        ],
        out_specs=pl.BlockSpec((None, None, g, n_sub, dh), lambda i, b: (i, b, 0, 0, 0)),
        out_shape=jax.ShapeDtypeStruct((2, bsz, g, n_sub, dh), BF16),
        scratch_shapes=[pltpu.VMEM((kvw // LANES, seq, LANES), F32)],
        compiler_params=_params("parallel", "parallel"),
        name="compress",
    )(p, pe, w1, w2)


def _mask_heads(x, ok, fill, tq):
    n = x.shape[1] // tq
    return jnp.concatenate([jnp.where(ok, x[:, h * tq:(h + 1) * tq], fill) for h in range(n)], axis=1)


def _with_ones_rows(vt):
    return jnp.concatenate([vt, jnp.ones((BF16_SUBLANES, vt.shape[1]), vt.dtype)], axis=0)


def _nsa_kernel(q_ref, kc_ref, vct_ref, ks_ref, vs_ref, kw_ref, vw_ref, ps_ref, z_ref, o_ref,
                sel_ref, vst_ref, vwt_ref, kn_ref, osw_ref):
    qi = pl.program_id(1)
    tq = q_ref.shape[0]
    dh = A_HEAD_DIM
    hpg = A_GROUP
    ng = A_KV_HEADS
    r = hpg * tq
    n_cmp = kc_ref.shape[1]
    tk = vwt_ref.shape[2]
    kt = SEL_KEYS_PER_STEP
    tiles_per_iter = kt // tk
    seq = ks_ref.shape[0]
    n_sel = seq // A_SEL_BLOCK
    t0 = qi * tq
    da = dh + BF16_SUBLANES

    @pl.when(qi == 0)
    def _():
        for j in range(seq // tk):
            vst_ref[j] = vs_ref[j * tk:(j + 1) * tk, :].astype(F32).T.astype(BF16)
            vwt_ref[j] = vw_ref[j * tk:(j + 1) * tk, :].astype(F32).T.astype(BF16)
        rows = KEY_NORM_ROWS
        dim_group = lax.broadcasted_iota(jnp.int32, (ng * dh, LANES), 0) // dh
        member = jnp.where(dim_group == lax.broadcasted_iota(jnp.int32, (ng * dh, LANES), 1), 1.0, 0.0).astype(BF16)
        for which, k_ref in enumerate((ks_ref, kw_ref)):
            best = jnp.zeros((1, LANES), F32)
            for c in range(seq // rows):
                kb = k_ref[c * rows:(c + 1) * rows, :]
                best = jnp.maximum(best, jnp.max(_dot(kb * kb, member), axis=0, keepdims=True))
            for g in range(ng):
                kn_ref[which * ng + g] = best[:, g:g + 1]

    qt = (q_ref[...].astype(F32) * (dh ** -0.5 * LOG2_E)).T.astype(BF16)
    q4t = [jnp.concatenate([qt[(g * hpg + h) * dh:(g * hpg + h + 1) * dh, :] for h in range(hpg)], axis=1)
           for g in range(ng)]

    cn = lax.broadcasted_iota(jnp.int32, (n_cmp, tq), 0)
    ct = t0 + lax.broadcasted_iota(jnp.int32, (n_cmp, tq), 1)
    valid = (cn * A_CMP_STRIDE + (A_CMP_BLOCK - 1)) <= ct
    jn = lax.broadcasted_iota(jnp.int32, (n_sel, n_cmp), 0)
    nn = lax.broadcasted_iota(jnp.int32, (n_sel, n_cmp), 1)
    ov = ((nn * A_CMP_STRIDE < (jn + 1) * A_SEL_BLOCK)
          & (nn * A_CMP_STRIDE + (A_CMP_BLOCK - 1) >= jn * A_SEL_BLOCK))
    ov_t = jnp.where(ov, 1.0, 0.0).astype(BF16)
    jr = lax.broadcasted_iota(jnp.int32, (n_sel, tq), 0)
    tl = t0 + lax.broadcasted_iota(jnp.int32, (n_sel, tq), 1)
    cur = lax.shift_right_logical(tl, SEL_BLOCK_SHIFT)
    forced = (jr == 0) | (jr == cur) | (jr == cur - 1)
    def compressed_and_select():
        for g in range(ng):
            s_m = _mask_heads(_dot(kc_ref[g], q4t[g]), valid, NEG_INF, tq)
            mx = jnp.max(s_m, axis=0, keepdims=True)
            e = _mask_heads(jnp.exp2(s_m - mx), valid, 0.0, tq)
            lsum = jnp.sum(e, axis=0, keepdims=True)
            p_c = e * (1.0 / jnp.where(lsum > 0.0, lsum, 1.0))
            osw_ref[2, g] = _dot(vct_ref[g], p_c.astype(BF16))
            p_sum = p_c[:, 0:tq]
            for h in range(1, hpg):
                p_sum = p_sum + p_c[:, h * tq:(h + 1) * tq]
            p_hi = p_sum.astype(BF16)
            r1 = p_sum - p_hi.astype(F32)
            p_mid = r1.astype(BF16)
            p_lo = (r1 - p_mid.astype(F32)).astype(BF16)
            imp_t = _dot(ov_t, p_hi) + _dot(ov_t, p_mid) + _dot(ov_t, p_lo)
            score = jnp.where(forced, POS_INF, jnp.where(jr <= cur, imp_t, NEG_INF))
            rank = jnp.zeros((n_sel, tq), F32)
            for jp in range(n_sel):
                sj = score[jp:jp + 1, :]
                beats = (sj > score) | ((sj == score) & (jr > jp))
                rank = rank + jnp.where(beats, 1.0, 0.0)
            sel_ref[g] = jnp.where(rank < float(min(A_SEL_TOPK, n_sel)), 1.0, 0.0)

    qn = [jnp.sqrt(jnp.sum(jnp.square(q4t[g].astype(F32)), axis=0, keepdims=True)) for g in range(ng)]
    bound_s = [qn[g] * jnp.sqrt(kn_ref[g]) * BOUND_SLACK for g in range(ng)]
    bound_w = [qn[g] * jnp.sqrt(kn_ref[ng + g]) * BOUND_SLACK for g in range(ng)]
    worst = bound_s[0]
    for b in bound_s[1:] + bound_w:
        worst = jnp.maximum(worst, b)
    fast = jnp.max(worst) <= MAX_FAST_BOUND

    n_wb = -(-A_WINDOW // tk)
    wk = (n_wb + 1) * tk
    j0 = jnp.maximum(qi - n_wb, 0)
    off_w = pl.multiple_of(j0 * tk, tk)
    c_abs = off_w + lax.broadcasted_iota(jnp.int32, (wk, tq), 0)
    t_abs_w = t0 + lax.broadcasted_iota(jnp.int32, (wk, tq), 1)
    ok_w = (c_abs <= t_abs_w) & (c_abs > t_abs_w - A_WINDOW)
    blocks_per_iter = kt // A_SEL_BLOCK

    def normalised(acc):
        return acc[0:dh, :] * (1.0 / acc[dh:dh + 1, :])

    def value_tiles(v_ref, first, count, g):
        vt = jnp.concatenate([v_ref[first + i, g * dh:(g + 1) * dh, :] for i in range(count)], axis=1)
        return _with_ones_rows(vt)

    def emit_output():
        gs = _sigmoid(ps_ref[...])
        g0 = 2 * M_HEADS
        outs = []
        for g in range(ng):
            o_s = osw_ref[0, g]
            o_w = osw_ref[1, g]
            o_c = osw_ref[2, g]
            for h in range(hpg):
                sl = slice(h * tq, (h + 1) * tq)
                gr = g0 + 3 * (g * hpg + h)
                outs.append(gs[gr:gr + 1, :] * o_c[:, sl] + gs[gr + 1:gr + 2, :] * o_s[:, sl]
                            + gs[gr + 2:gr + 3, :] * o_w[:, sl])
        o_all = jnp.concatenate(outs, axis=0).T
        z = z_ref[...].astype(F32)
        o_ref[...] = (o_all * (z * _sigmoid(z))).astype(o_ref.dtype)

    @pl.when(fast)
    def _():
        kcol = lax.broadcasted_iota(jnp.int32, (kt, dh), 1)
        krow = lax.broadcasted_iota(jnp.int32, (kt, dh), 0)
        kblock = lax.shift_right_logical(krow, SEL_BLOCK_SHIFT)
        aux = jnp.where((kcol == 0) | (kcol == SUBLANES + kblock), 1.0, 0.0).astype(BF16)
        aux_w = jnp.where(lax.broadcasted_iota(jnp.int32, (wk, dh), 1) == 0, 1.0, 0.0).astype(BF16)
        aux_d = jnp.where(lax.broadcasted_iota(jnp.int32, (tq, dh), 1) == 0, 1.0, 0.0).astype(BF16)
        zrows = jnp.zeros((LANES - dh - BF16_SUBLANES, r), BF16)

        first_row = lax.broadcasted_iota(jnp.int32, (SUBLANES, r), 0) == 0

        def query_aug(g, shift, block_bias):
            top = jnp.where(first_row, -shift, 0.0)
            low = jnp.zeros((SUBLANES, r), F32) if block_bias is None else jnp.concatenate([block_bias] * hpg, axis=1)
            return jnp.concatenate([q4t[g], jnp.concatenate([top, low], axis=0).astype(BF16), zrows], axis=0)

        def step(j, accs):
            k_all = ks_ref[j * kt:(j + 1) * kt, :]
            out = []
            for g in range(ng):
                k_aug = jnp.concatenate([k_all[:, g * dh:(g + 1) * dh], aux], axis=1)
                bias = sel_ref[g, j * blocks_per_iter:(j + 1) * blocks_per_iter, :]
                p = jnp.exp2(_dot(k_aug, query_aug(g, bound_s[g], bias))).astype(BF16)
                out.append(accs[g] + _dot(value_tiles(vst_ref, j * tiles_per_iter, tiles_per_iter, g), p))
            return tuple(out)

        def finish(accs):
            kd_all = ks_ref[pl.ds(pl.multiple_of(t0, tq), tq), :]
            causal = (lax.broadcasted_iota(jnp.int32, (tq, tq), 0)
                      <= lax.broadcasted_iota(jnp.int32, (tq, tq), 1))
            kw_all = kw_ref[pl.ds(off_w, wk), :]
            for g in range(ng):
                kd_aug = jnp.concatenate([kd_all[:, g * dh:(g + 1) * dh], aux_d], axis=1)
                s_d = _mask_heads(_dot(kd_aug, query_aug(g, bound_s[g], None)), causal, -EXCLUDE, tq)
                acc = accs[g] + _dot(value_tiles(vst_ref, qi, 1, g), jnp.exp2(s_d).astype(BF16))
                osw_ref[0, g] = normalised(acc)
                kw_aug = jnp.concatenate([kw_all[:, g * dh:(g + 1) * dh], aux_w], axis=1)
                s_w = _mask_heads(_dot(kw_aug, query_aug(g, bound_w[g], None)), ok_w, -EXCLUDE, tq)
                p_w = jnp.exp2(s_w).astype(BF16)
                acc_w = _dot(value_tiles(vwt_ref, j0, 1, g), p_w[0:tk, :])
                for i in range(1, n_wb + 1):
                    acc_w = acc_w + _dot(value_tiles(vwt_ref, j0 + i, 1, g), p_w[i * tk:(i + 1) * tk, :])
                osw_ref[1, g] = normalised(acc_w)

        steps_needed = (t0 + kt - 1) // kt
        for n_steps in range(seq // kt + 1):
            @pl.when(steps_needed == n_steps)
            def _(n_steps=n_steps):
                compressed_and_select()
                before_diag = jr < qi * (tq // A_SEL_BLOCK)
                for g in range(ng):
                    sel_ref[g] = jnp.where((sel_ref[g] > 0.5) & before_diag, 0.0, -EXCLUDE)
                accs = tuple(jnp.zeros((da, r), F32) for _ in range(ng))
                for j in range(n_steps):
                    accs = step(j, accs)
                finish(accs)
                emit_output()

    @pl.when(jnp.logical_not(fast))
    def _():
        compressed_and_select()
        zpad = jnp.zeros((dh, r), BF16)
        q_pad = [jnp.concatenate([q4t[g] if gg == g else zpad for gg in range(ng)], axis=0) for g in range(ng)]
        sub_s = lax.broadcasted_iota(jnp.int32, (kt, tq), 0)
        t_abs_s = t0 + lax.broadcasted_iota(jnp.int32, (kt, tq), 1)

        def body(j, carry):
            off = pl.multiple_of(j * kt, kt)
            k_all = ks_ref[pl.ds(off, kt), :]
            causal = (off + sub_s) <= t_abs_s
            out = []
            for g in range(ng):
                m, acc = carry[g]
                s = _dot(k_all, q_pad[g])
                flags = jnp.concatenate(
                    [jnp.broadcast_to(sel_ref[g, pl.ds(j * blocks_per_iter + i, 1), :], (A_SEL_BLOCK, tq))
                     for i in range(blocks_per_iter)], axis=0)
                s = _mask_heads(s, (flags > 0.5) & causal, NEG_INF, tq)
                m_new = jnp.maximum(m, jnp.max(s, axis=0, keepdims=True))
                p = jnp.exp2(s - m_new).astype(BF16)
                vt = value_tiles(vst_ref, j * tiles_per_iter, tiles_per_iter, g)
                out.append((m_new, jnp.exp2(m - m_new) * acc + _dot(vt, p)))
            return tuple(out)

        init = tuple((jnp.full((1, r), NEG_INF, F32), jnp.zeros((da, r), F32)) for _ in range(ng))
        state = lax.fori_loop(0, (t0 + tq - 1) // kt + 1, body, init)
        kw_all = kw_ref[pl.ds(off_w, wk), :]
        for g in range(ng):
            osw_ref[0, g] = normalised(state[g][1])
            s_w = _mask_heads(_dot(kw_all, q_pad[g]), ok_w, NEG_INF, tq)
            p_w = jnp.exp2(s_w - jnp.max(s_w, axis=0, keepdims=True)).astype(BF16)
            acc_w = _dot(value_tiles(vwt_ref, j0, 1, g), p_w[0:tk, :])
            for i in range(1, n_wb + 1):
                acc_w = acc_w + _dot(value_tiles(vwt_ref, j0 + i, 1, g), p_w[i * tk:(i + 1) * tk, :])
            osw_ref[1, g] = normalised(acc_w)
        emit_output()


def _nsa(p, ps, kvc, vct, q_col, kv_col, z_col, bsz, seq):
    t = p.shape[0]
    g = A_KV_HEADS
    dh = A_HEAD_DIM
    aw = A_HEADS * dh
    kvw = g * dh
    tq = Q_BLOCK
    nq = seq // tq
    n_cmp = kvc.shape[3]
    tk = LANES
    rows = lambda b, qi: b * nq + qi
    kv_piece = lambda i: pl.BlockSpec((seq, kvw), lambda b, qi: (b, kv_col // kvw + i))
    return pl.pallas_call(
        _nsa_kernel,
        grid=(bsz, nq),
        in_specs=[
            pl.BlockSpec((tq, aw), lambda b, qi: (rows(b, qi), q_col // aw)),
            pl.BlockSpec((None, None, g, n_cmp, dh), lambda b, qi: (0, b, 0, 0, 0)),
            pl.BlockSpec((None, g, dh, n_cmp), lambda b, qi: (b, 0, 0, 0)),
            kv_piece(2), kv_piece(3), kv_piece(4), kv_piece(5),
            pl.BlockSpec((LANES, tq), lambda b, qi: (0, rows(b, qi))),
            pl.BlockSpec((tq, aw), lambda b, qi: (rows(b, qi), z_col // aw)),
        ],
        out_specs=pl.BlockSpec((tq, aw), lambda b, qi: (rows(b, qi), 0)),
        out_shape=jax.ShapeDtypeStruct((t, aw), BF16),
        scratch_shapes=[pltpu.VMEM((g, seq // A_SEL_BLOCK, tq), F32),
                        pltpu.VMEM((seq // tk, kvw, tk), BF16),
                        pltpu.VMEM((seq // tk, kvw, tk), BF16),
                        pltpu.VMEM((2 * g, 1, 1), F32),
                        pltpu.VMEM((3, g, dh, A_GROUP * tq), F32)],
        compiler_params=_params("parallel", "arbitrary"),
        name="nsa",
    )(p, kvc, vct, p, p, p, p, ps, p)


def _merge_out_kernel(ym_ref, ya_ref, wm_ref, wa_ref, gm_ref, ga_ref, wo_ref, g_ref, x_ref, o_ref):
    um = _dot(ym_ref[...], wm_ref[...])
    ua = _dot(ya_ref[...], wa_ref[...])
    mg = _sigmoid(gm_ref[...].astype(F32)) * um + _sigmoid(ga_ref[...].astype(F32)) * ua
    out = _dot(mg.astype(BF16), wo_ref[...])
    ms = jnp.mean(out * out, axis=-1, keepdims=True)
    o_ref[...] = x_ref[...] + out * lax.rsqrt(ms + RMS_EPS) * g_ref[...]


def _merge_out(ym, ya, wm, wa, p, gm_col, ga_col, wo, g, x2d, tm=256):
    t, d = x2d.shape
    kdim = ym.shape[1]
    resident = lambda shape: pl.BlockSpec(shape, lambda i: (0, 0), pipeline_mode=pl.Buffered(1))
    return pl.pallas_call(
        _merge_out_kernel,
        grid=(t // tm,),
        in_specs=[
            pl.BlockSpec((tm, kdim), lambda i: (i, 0)),
            pl.BlockSpec((tm, kdim), lambda i: (i, 0)),
            resident((kdim, d)),
            resident((kdim, d)),
            pl.BlockSpec((tm, d), lambda i: (i, gm_col // d)),
            pl.BlockSpec((tm, d), lambda i: (i, ga_col // d)),
            resident((d, d)),
            pl.BlockSpec((1, d), lambda i: (0, 0)),
            pl.BlockSpec((tm, d), lambda i: (i, 0)),
        ],
        out_specs=pl.BlockSpec((tm, d), lambda i: (i, 0)),
        out_shape=jax.ShapeDtypeStruct((t, d), F32),
        compiler_params=_params("parallel"),
        name="merge_out",
    )(ym, ya, wm, wa, p, p, wo, g, x2d)


def _layer(x, pre_g, w_in, conv_w, conv_b, w_q, w_k, b_i, b_f, norm_g, skip,
           pe_k, w1_k, w2_k, pe_v, w1_v, w2_v, w_up_m, w_up_a, w_out, post_g):
    bsz, seq, d = x.shape
    t = bsz * seq
    mw = w_up_m.shape[0]
    aw = w_up_a.shape[0]
    g = A_KV_HEADS
    dh = A_HEAD_DIM
    kvw = g * dh
    n_gate = 3 * A_HEADS

    o_i = 4 * mw
    o_f = o_i + M_HEADS
    o_q = o_f + M_HEADS
    o_g = o_q + aw + 6 * kvw
    o_z = o_g + n_gate
    o_m = o_z + aw
    o_kv = o_q + aw
    main_pieces = ((o_m, w_in.shape[1]), (0, o_i), (o_q, o_kv), (o_z, o_m), (o_kv, o_g))
    w_t = w_in.T
    row_starts = jnp.asarray([r // SUBLANES for lo, hi in main_pieces for r in range(lo, hi, INPROJ_TN)], jnp.int32)
    small_pieces = ((o_i, o_q), (o_g, o_z))
    gm_col = 0
    ga_col = d
    mx_col = 2 * d
    q_col = mx_col + 4 * mw
    z_col = q_col + aw
    kv_col = z_col + aw

    x2d = x.reshape(t, d)
    p, ps = _inproj(x2d, pre_g.reshape(1, d), w_t, row_starts, small_pieces, tn=INPROJ_TN)

    xc, q, k = _conv_qk(p, mx_col, conv_w, conv_b.reshape(1, mw), w_q.astype(BF16), w_k.astype(BF16), bsz, seq)
    bias_m = jnp.concatenate([b_i, b_f]).reshape(2 * M_HEADS, 1)
    y_m = _mlstm(p, mx_col, xc, q, k, ps, bias_m, norm_g.reshape(1, mw), skip.reshape(1, mw), bsz, seq)

    half = A_CMP_STRIDE * dh
    pe = jnp.stack([pe_k.reshape(2, half), pe_v.reshape(2, half)])
    w1 = jnp.stack([w1_k, w1_v]).astype(BF16)
    w2 = jnp.stack([w2_k, w2_v]).astype(BF16)
    kvc = _compress(p, kv_col, pe, w1, w2, bsz, seq)
    vct = jnp.swapaxes(kvc[1], 2, 3)
    y_a = _nsa(p, ps, kvc, vct, q_col, kv_col, z_col, bsz, seq)

    out = _merge_out(y_m, y_a, w_up_m.astype(BF16), w_up_a.astype(BF16), p, gm_col, ga_col,
                     w_out.astype(BF16), post_g.reshape(1, d), x2d)
    return out.reshape(bsz, seq, d)


def kernel(x, pre_norm_g, w_in, m_conv_w, m_conv_b, m_w_q, m_w_k, m_b_i, m_b_f, m_norm_g, m_skip, a_pe_k, a_w1_k, a_w2_k, a_pe_v, a_w1_v, a_w2_v, w_up_m, w_up_a, w_out, post_norm_g):
    depth = w_in.shape[0]
    for l in range(depth):
        x = _layer(x, pre_norm_g[l], w_in[l], m_conv_w[l], m_conv_b[l], m_w_q[l], m_w_k[l], m_b_i[l], m_b_f[l],
                   m_norm_g[l], m_skip[l], a_pe_k[l], a_w1_k[l], a_w2_k[l], a_pe_v[l], a_w1_v[l], a_w2_v[l],
                   w_up_m[l], w_up_a[l], w_out[l], post_norm_g[l])
    return x
```

```python
import functools

import jax
import jax.numpy as jnp
from jax import lax
from jax.experimental import pallas as pl
from jax.experimental.pallas import tpu as pltpu

F32 = jnp.float32
BF16 = jnp.bfloat16

M_HEADS = 4
M_CONV = 4
M_CHUNK = 128
A_HEADS = 16
A_KV_HEADS = 4
A_GROUP = A_HEADS // A_KV_HEADS
A_HEAD_DIM = 64
A_CMP_BLOCK = 32
A_CMP_STRIDE = 16
A_SEL_BLOCK = 64
A_SEL_TOPK = 8
A_WINDOW = 512
Q_BLOCK = 128
SEL_KEYS_PER_STEP = 512
EXCLUDE = 1e4
MAX_FAST_BOUND = 40.0
SEL_BLOCK_SHIFT = A_SEL_BLOCK.bit_length() - 1
KEY_NORM_ROWS = 512
BOUND_SLACK = 1.01
INPROJ_TN = 1024
INPROJ_TAIL_TN = 512
RMS_EPS = 1e-6
LN_EPS = 1e-6
NEG_INF = -1e30
POS_INF = 1e30

LANES = 128
SUBLANES = 8
BF16_SUBLANES = 16
LOG2_E = 1.4426950408889634
VMEM_LIMIT_BYTES = 48 * 1024 * 1024
INPROJ_VMEM_LIMIT_BYTES = 56 * 1024 * 1024


def _nt_dot(a, b):
    return lax.dot_general(a, b, (((1,), (1,)), ((), ())), preferred_element_type=F32)


def _dot(a, b):
    return jnp.dot(a, b, preferred_element_type=F32)


def _sigmoid(x):
    return 1.0 / (1.0 + jnp.exp(-x))


def _params(*sem):
    return pltpu.CompilerParams(dimension_semantics=sem, vmem_limit_bytes=VMEM_LIMIT_BYTES)


def _inproj_kernel(rows_ref, x_ref, g_ref, wt_ref, *rest):
    del rows_ref
    o_ref, ost_ref, h_ref, h_scr = rest[-4:]
    small_refs = rest[:-4]

    @pl.when(pl.program_id(1) == 0)
    def _():
        xf = x_ref[...]
        ms = jnp.mean(xf * xf, axis=-1, keepdims=True)
        hb = (xf * lax.rsqrt(ms + RMS_EPS) * g_ref[...]).astype(BF16)
        h_scr[...] = hb
        h_ref[...] = hb
        rows = [r[...] for r in small_refs]
        used = sum(r.shape[0] for r in rows)
        rows.append(jnp.zeros((ost_ref.shape[0] - used, xf.shape[1]), F32))
        ost_ref[...] = _nt_dot(jnp.concatenate(rows, axis=0).astype(BF16), hb)

    o_ref[...] = _nt_dot(h_scr[...], wt_ref[...].astype(BF16)).astype(o_ref.dtype)


def _inproj(x2d, g, w_t, row_starts, small_pieces, tm=1024, tn=512):
    t, d = x2d.shape
    n = row_starts.shape[0] * tn
    ns = LANES
    rows_of = lambda lo, hi: pl.BlockSpec((pl.Element(hi - lo), pl.Element(d)), lambda i, j, rows: (lo, 0))
    grid_spec = pltpu.PrefetchScalarGridSpec(
        num_scalar_prefetch=1,
        grid=(t // tm, n // tn),
        in_specs=[
            pl.BlockSpec((tm, d), lambda i, j, rows: (i, 0)),
            pl.BlockSpec((1, d), lambda i, j, rows: (0, 0)),
            pl.BlockSpec((pl.Element(tn), pl.Element(d)), lambda i, j, rows: (rows[j] * SUBLANES, 0)),
        ] + [rows_of(lo, hi) for lo, hi in small_pieces],
        out_specs=[
            pl.BlockSpec((tm, tn), lambda i, j, rows: (i, j)),
            pl.BlockSpec((ns, tm), lambda i, j, rows: (0, i)),
            pl.BlockSpec((tm, d), lambda i, j, rows: (i, 0)),
        ],
        scratch_shapes=[pltpu.VMEM((tm, d), BF16)],
    )
    return pl.pallas_call(
        _inproj_kernel,
        grid_spec=grid_spec,
        out_shape=[jax.ShapeDtypeStruct((t, n), BF16), jax.ShapeDtypeStruct((ns, t), F32),
                   jax.ShapeDtypeStruct((t, d), BF16)],
        compiler_params=pltpu.CompilerParams(dimension_semantics=("parallel", "arbitrary"),
                                             vmem_limit_bytes=INPROJ_VMEM_LIMIT_BYTES),
        name="inproj",
    )(row_starts, x2d, g, w_t, *([w_t] * len(small_pieces)))


def _inproj_tail_kernel(rows_ref, h_ref, wt_ref, o_ref):
    del rows_ref
    o_ref[...] = _nt_dot(h_ref[...], wt_ref[...].astype(BF16)).astype(o_ref.dtype)


def _inproj_tail(h, w_t, row_starts, tm=2048, tn=512):
    t, d = h.shape
    nj = row_starts.shape[0]
    grid_spec = pltpu.PrefetchScalarGridSpec(
        num_scalar_prefetch=1,
        grid=(t // tm, nj),
        in_specs=[
            pl.BlockSpec((tm, d), lambda i, j, rows: (i, 0)),
            pl.BlockSpec((pl.Element(tn), pl.Element(d)), lambda i, j, rows: (rows[j] * SUBLANES, 0)),
        ],
        out_specs=pl.BlockSpec((tm, tn), lambda i, j, rows: (i, j)),
    )
    return pl.pallas_call(
        _inproj_tail_kernel,
        grid_spec=grid_spec,
        out_shape=jax.ShapeDtypeStruct((t, nj * tn), BF16),
        compiler_params=_params("parallel", "arbitrary"),
        name="inproj_tail",
    )(row_starts, h, w_t)


def _conv_qk_kernel(x_ref, cw_ref, cb_ref, wq_ref, wk_ref, xc_ref, q_ref, k_ref, *, rows):
    s, c = x_ref.shape
    cw = cw_ref[...]
    cb = cb_ref[...]
    wq = wq_ref[...]
    wk = wk_ref[...]
    k_scale = c ** -0.5
    first_row = lax.broadcasted_iota(jnp.int32, (rows, c), 0) == 0
    carry = [jnp.zeros((1, c), F32) for _ in range(M_CONV - 1)]
    for r in range(s // rows):
        xr = x_ref[r * rows:(r + 1) * rows, :].astype(F32)
        a = cw[0:1, :] * xr
        last = []
        for kk in range(1, M_CONV):
            last.append(a[rows - 1:rows, :])
            a = jnp.where(first_row, carry[kk - 1], pltpu.roll(a, 1, 0)) + cw[kk:kk + 1, :] * xr
        carry = last
        acc = a + cb
        xcb = (acc * _sigmoid(acc)).astype(BF16)
        xc_ref[r * rows:(r + 1) * rows, :] = xcb
        q_ref[r * rows:(r + 1) * rows, :] = _dot(xcb, wq).astype(BF16)
        k_ref[r * rows:(r + 1) * rows, :] = (_dot(xcb, wk) * k_scale).astype(BF16)


def _conv_qk(p, mx_col, conv_w, conv_b, wq, wk, bsz, seq):
    t = p.shape[0]
    heads, hd = wq.shape[0], wq.shape[1]
    width = heads * hd
    blk = pl.BlockSpec((seq, hd), lambda b, h: (b, h))
    out = jax.ShapeDtypeStruct((t, width), BF16)
    return pl.pallas_call(
        functools.partial(_conv_qk_kernel, rows=256),
        grid=(bsz, heads),
        in_specs=[
            pl.BlockSpec((seq, hd), lambda b, h: (b, mx_col // hd + h)),
            pl.BlockSpec((M_CONV, hd), lambda b, h: (0, h)),
            pl.BlockSpec((1, hd), lambda b, h: (0, h)),
            pl.BlockSpec((None, hd, hd), lambda b, h: (h, 0, 0)),
            pl.BlockSpec((None, hd, hd), lambda b, h: (h, 0, 0)),
        ],
        out_specs=[blk, blk, blk],
        out_shape=[out, out, out],
        compiler_params=_params("parallel", "parallel"),
        name="conv_qk",
    )(p, conv_w, conv_b, wq, wk)


def _log_sigmoid(x):
    return jnp.minimum(x, 0.0) - jnp.log1p(jnp.exp(-jnp.abs(x)))


def _mlstm_kernel(bias_ref, q_ref, k_ref, v_ref, o_ref, z_ref, xc_ref, g_ref, ng_ref, sk_ref,
                  y_ref, c_scr, n_scr, m_scr):
    @pl.when(pl.program_id(1) == 0)
    def _():
        c_scr[...] = jnp.zeros_like(c_scr)
        n_scr[...] = jnp.zeros_like(n_scr)
        m_scr[...] = jnp.zeros_like(m_scr)

    ln = q_ref.shape[0]
    heads, hd = c_scr.shape[0], c_scr.shape[1]
    gates = g_ref[...] + bias_ref[...]
    fb_all = _log_sigmoid(gates[heads:2 * heads, :])
    row = lax.broadcasted_iota(jnp.int32, (ln, ln), 0)
    col = lax.broadcasted_iota(jnp.int32, (ln, ln), 1)
    tril = col <= row
    eye = col == row
    for hh in range(heads):
        sl = slice(hh * hd, (hh + 1) * hd)
        ib = gates[hh:hh + 1, :]
        fb = fb_all[hh:hh + 1, :]
        bcum_c = jnp.sum(jnp.where(tril, fb, 0.0), axis=1, keepdims=True)
        bcum_r = jnp.sum(jnp.where(eye, bcum_c, 0.0), axis=0, keepdims=True)
        g_tot = jnp.sum(fb, axis=1, keepdims=True)
        e_r = ib - bcum_r
        m_st = m_scr[hh]
        d_intra = jnp.where(tril, bcum_c + e_r, NEG_INF)
        d_inter = bcum_c + m_st
        m_t = jnp.maximum(d_inter, jnp.max(d_intra, axis=1, keepdims=True))
        w_intra = jnp.exp(d_intra - m_t)
        w_inter = jnp.exp(d_inter - m_t)

        qb = q_ref[:, sl]
        kb = k_ref[:, sl]
        vb = v_ref[:, sl]
        c_st = c_scr[hh]
        n_st = n_scr[hh]
        sc = _nt_dot(qb, kb) * w_intra
        num = _dot(sc.astype(BF16), vb) + w_inter * _dot(qb, c_st.astype(BF16))
        n_rows = jnp.broadcast_to(n_st, (SUBLANES, hd)).astype(BF16)
        qn = _nt_dot(qb, n_rows)[:, 0:1]
        den = jnp.sum(sc, axis=1, keepdims=True) + w_inter * qn
        h = num / jnp.maximum(jnp.abs(den), jnp.exp(-m_t))

        d_state = g_tot + e_r
        m_new = jnp.maximum(g_tot + m_st, jnp.max(d_state, axis=1, keepdims=True))
        w_s_r = jnp.exp(d_state - m_new)
        w_s_c = jnp.sum(jnp.where(eye, w_s_r, 0.0), axis=1, keepdims=True)
        decay = jnp.exp(g_tot + m_st - m_new)
        kw = kb * w_s_c.astype(BF16)
        kv = lax.dot_general(kw, vb, (((0,), (0,)), ((), ())), preferred_element_type=F32)
        c_scr[hh] = decay * c_st + kv
        w_rows = jnp.broadcast_to(w_s_r, (SUBLANES, ln)).astype(BF16)
        n_scr[hh] = decay * n_st + _dot(w_rows, kb)[0:1, :]
        m_scr[hh] = m_new

        hg = h * _sigmoid(o_ref[:, sl].astype(F32))
        mu = jnp.mean(hg, axis=-1, keepdims=True)
        hc = hg - mu
        var = jnp.mean(hc * hc, axis=-1, keepdims=True)
        hn = hc * lax.rsqrt(var + LN_EPS)
        z = z_ref[:, sl].astype(F32)
        y = (hn * ng_ref[:, sl] + sk_ref[:, sl] * xc_ref[:, sl].astype(F32)) * (z * _sigmoid(z))
        y_ref[:, sl] = y.astype(y_ref.dtype)


def _mlstm(p, mx_col, xc, q, k, gates, bias, norm_g, skip, bsz, seq):
    t, width = q.shape
    pc = mx_col // width
    heads = M_HEADS
    hd = width // heads
    ln = M_CHUNK
    nc = seq // ln
    rows = lambda b, c: (b * nc + c, 0)
    vec = pl.BlockSpec((1, width), lambda b, c: (0, 0))
    return pl.pallas_call(
        _mlstm_kernel,
        grid=(bsz, nc),
        in_specs=[
            pl.BlockSpec((2 * heads, 1), lambda b, c: (0, 0)),
            pl.BlockSpec((ln, width), rows),
            pl.BlockSpec((ln, width), rows),
            pl.BlockSpec((ln, width), lambda b, c: (b * nc + c, pc + 1)),
            pl.BlockSpec((ln, width), lambda b, c: (b * nc + c, pc + 2)),
            pl.BlockSpec((ln, width), lambda b, c: (b * nc + c, pc + 3)),
            pl.BlockSpec((ln, width), rows),
            pl.BlockSpec((2 * heads, ln), lambda b, c: (0, b * nc + c)),
            vec, vec,
        ],
        out_specs=pl.BlockSpec((ln, width), rows),
        out_shape=jax.ShapeDtypeStruct((t, width), BF16),
        scratch_shapes=[pltpu.VMEM((heads, hd, hd), F32), pltpu.VMEM((heads, 1, hd), F32),
                        pltpu.VMEM((heads, 1, 1), F32)],
        compiler_params=_params("parallel", "arbitrary"),
        name="mlstm",
    )(bias, q, k, p, p, p, xc, gates, norm_g, skip)


def _compress_kernel(x_ref, pe_ref, w1_ref, w2_ref, o_ref, xf_scr):
    ng, n_sub, dh = o_ref.shape
    half = A_CMP_STRIDE * dh
    n_chunk = xf_scr.shape[0]
    for c in range(n_chunk):
        xf_scr[c] = x_ref[:, c * LANES:(c + 1) * LANES].astype(F32)
    xs = [jnp.concatenate([xf_scr[c, pl.ds(r, n_sub, stride=A_CMP_STRIDE), :] for c in range(n_chunk)], axis=1)
          for r in range(A_CMP_STRIDE)]
    pe = pe_ref[...]
    rid = lax.broadcasted_iota(jnp.int32, (n_sub, dh), 0)
    for g in range(ng):
        t = jnp.concatenate([x[:, g * dh:(g + 1) * dh] for x in xs], axis=1)
        za = _dot((t + pe[0:1, :]).astype(BF16), w1_ref[0:half, :])
        zb = _dot((t + pe[1:2, :]).astype(BF16), w1_ref[half:2 * half, :])
        pre = za + pltpu.roll(zb, n_sub - 1, 0)
        hid = (pre * _sigmoid(pre)).astype(BF16)
        out = _dot(hid, w2_ref[...])
        o_ref[g] = jnp.where(rid < n_sub - 1, out, 0.0).astype(o_ref.dtype)


def _compress(p, kc_col, pe, w1, w2, bsz, seq):
    g = A_KV_HEADS
    dh = A_HEAD_DIM
    kvw = g * dh
    n_sub = seq // A_CMP_STRIDE
    half = A_CMP_STRIDE * dh
    hid = w1.shape[-1]
    return pl.pallas_call(
        _compress_kernel,
        grid=(2, bsz),
        in_specs=[
            pl.BlockSpec((seq, kvw), lambda i, b: (b, kc_col // kvw + i)),
            pl.BlockSpec((None, 2, half), lambda i, b: (i, 0, 0)),
            pl.BlockSpec((None, 2 * half, hid), lambda i, b: (i, 0, 0)),
            pl.BlockSpec((None, hid, dh), lambda i, b: (i, 0, 0)),
        ],
        out_specs=pl.BlockSpec((None, None, g, n_sub, dh), lambda i, b: (i, b, 0, 0, 0)),
        out_shape=jax.ShapeDtypeStruct((2, bsz, g, n_sub, dh), BF16),
        scratch_shapes=[pltpu.VMEM((kvw // LANES, seq, LANES), F32)],
        compiler_params=_params("parallel", "parallel"),
        name="compress",
    )(p, pe, w1, w2)


def _mask_heads(x, ok, fill, tq):
    n = x.shape[1] // tq
    return jnp.concatenate([jnp.where(ok, x[:, h * tq:(h + 1) * tq], fill) for h in range(n)], axis=1)


def _with_ones_rows(vt):
    return jnp.concatenate([vt, jnp.ones((BF16_SUBLANES, vt.shape[1]), vt.dtype)], axis=0)


def _nsa_kernel(q_ref, kc_ref, vct_ref, ks_ref, vs_ref, kw_ref, vw_ref, ps_ref, z_ref, o_ref,
                sel_ref, vst_ref, vwt_ref, kn_ref, osw_ref):
    qi = pl.program_id(1)
    tq = q_ref.shape[0]
    dh = A_HEAD_DIM
    hpg = A_GROUP
    ng = A_KV_HEADS
    r = hpg * tq
    n_cmp = kc_ref.shape[1]
    tk = vwt_ref.shape[2]
    kt = SEL_KEYS_PER_STEP
    tiles_per_iter = kt // tk
    seq = ks_ref.shape[0]
    n_sel = seq // A_SEL_BLOCK
    t0 = qi * tq
    da = dh + BF16_SUBLANES

    @pl.when(qi == 0)
    def _():
        for j in range(seq // tk):
            vst_ref[j] = vs_ref[j * tk:(j + 1) * tk, :].astype(F32).T.astype(BF16)
            vwt_ref[j] = vw_ref[j * tk:(j + 1) * tk, :].astype(F32).T.astype(BF16)
        rows = KEY_NORM_ROWS
        dim_group = lax.broadcasted_iota(jnp.int32, (ng * dh, LANES), 0) // dh
        member = jnp.where(dim_group == lax.broadcasted_iota(jnp.int32, (ng * dh, LANES), 1), 1.0, 0.0).astype(BF16)
        for which, k_ref in enumerate((ks_ref, kw_ref)):
            best = jnp.zeros((1, LANES), F32)
            for c in range(seq // rows):
                kb = k_ref[c * rows:(c + 1) * rows, :]
                best = jnp.maximum(best, jnp.max(_dot(kb * kb, member), axis=0, keepdims=True))
            for g in range(ng):
                kn_ref[which * ng + g] = best[:, g:g + 1]

    qt = (q_ref[...].astype(F32) * (dh ** -0.5 * LOG2_E)).T.astype(BF16)
    q4t = [jnp.concatenate([qt[(g * hpg + h) * dh:(g * hpg + h + 1) * dh, :] for h in range(hpg)], axis=1)
           for g in range(ng)]

    cn = lax.broadcasted_iota(jnp.int32, (n_cmp, tq), 0)
    ct = t0 + lax.broadcasted_iota(jnp.int32, (n_cmp, tq), 1)
    valid = (cn * A_CMP_STRIDE + (A_CMP_BLOCK - 1)) <= ct
    jn = lax.broadcasted_iota(jnp.int32, (n_sel, n_cmp), 0)
    nn = lax.broadcasted_iota(jnp.int32, (n_sel, n_cmp), 1)
    ov = ((nn * A_CMP_STRIDE < (jn + 1) * A_SEL_BLOCK)
          & (nn * A_CMP_STRIDE + (A_CMP_BLOCK - 1) >= jn * A_SEL_BLOCK))
    ov_t = jnp.where(ov, 1.0, 0.0).astype(BF16)
    jr = lax.broadcasted_iota(jnp.int32, (n_sel, tq), 0)
    tl = t0 + lax.broadcasted_iota(jnp.int32, (n_sel, tq), 1)
    cur = lax.shift_right_logical(tl, SEL_BLOCK_SHIFT)
    forced = (jr == 0) | (jr == cur) | (jr == cur - 1)
    def compressed_and_select():
        for g in range(ng):
            s_m = _mask_heads(_dot(kc_ref[g], q4t[g]), valid, NEG_INF, tq)
            mx = jnp.max(s_m, axis=0, keepdims=True)
            e = _mask_heads(jnp.exp2(s_m - mx), valid, 0.0, tq)
            lsum = jnp.sum(e, axis=0, keepdims=True)
            p_c = e * (1.0 / jnp.where(lsum > 0.0, lsum, 1.0))
            osw_ref[2, g] = _dot(vct_ref[g], p_c.astype(BF16))
            p_sum = p_c[:, 0:tq]
            for h in range(1, hpg):
                p_sum = p_sum + p_c[:, h * tq:(h + 1) * tq]
            p_hi = p_sum.astype(BF16)
            r1 = p_sum - p_hi.astype(F32)
            p_mid = r1.astype(BF16)
            p_lo = (r1 - p_mid.astype(F32)).astype(BF16)
            imp_t = _dot(ov_t, p_hi) + _dot(ov_t, p_mid) + _dot(ov_t, p_lo)
            score = jnp.where(forced, POS_INF, jnp.where(jr <= cur, imp_t, NEG_INF))
            rank = jnp.zeros((n_sel, tq), F32)
            for jp in range(n_sel):
                sj = score[jp:jp + 1, :]
                beats = (sj > score) | ((sj == score) & (jr > jp))
                rank = rank + jnp.where(beats, 1.0, 0.0)
            sel_ref[g] = jnp.where(rank < float(min(A_SEL_TOPK, n_sel)), 1.0, 0.0)

    qn = [jnp.sqrt(jnp.sum(jnp.square(q4t[g].astype(F32)), axis=0, keepdims=True)) for g in range(ng)]
    bound_s = [qn[g] * jnp.sqrt(kn_ref[g]) * BOUND_SLACK for g in range(ng)]
    bound_w = [qn[g] * jnp.sqrt(kn_ref[ng + g]) * BOUND_SLACK for g in range(ng)]
    worst = bound_s[0]
    for b in bound_s[1:] + bound_w:
        worst = jnp.maximum(worst, b)
    fast = jnp.max(worst) <= MAX_FAST_BOUND

    n_wb = -(-A_WINDOW // tk)
    wk = (n_wb + 1) * tk
    j0 = jnp.maximum(qi - n_wb, 0)
    off_w = pl.multiple_of(j0 * tk, tk)
    c_abs = off_w + lax.broadcasted_iota(jnp.int32, (wk, tq), 0)
    t_abs_w = t0 + lax.broadcasted_iota(jnp.int32, (wk, tq), 1)
    ok_w = (c_abs <= t_abs_w) & (c_abs > t_abs_w - A_WINDOW)
    blocks_per_iter = kt // A_SEL_BLOCK

    def normalised(acc):
        return acc[0:dh, :] * (1.0 / acc[dh:dh + 1, :])

    def value_tiles(v_ref, first, count, g):
        vt = jnp.concatenate([v_ref[first + i, g * dh:(g + 1) * dh, :] for i in range(count)], axis=1)
        return _with_ones_rows(vt)

    def emit_output():
        gs = _sigmoid(ps_ref[...])
        g0 = 2 * M_HEADS
        outs = []
        for g in range(ng):
            o_s = osw_ref[0, g]
            o_w = osw_ref[1, g]
            o_c = osw_ref[2, g]
            for h in range(hpg):
                sl = slice(h * tq, (h + 1) * tq)
                gr = g0 + 3 * (g * hpg + h)
                outs.append(gs[gr:gr + 1, :] * o_c[:, sl] + gs[gr + 1:gr + 2, :] * o_s[:, sl]
                            + gs[gr + 2:gr + 3, :] * o_w[:, sl])
        o_all = jnp.concatenate(outs, axis=0).T
        z = z_ref[...].astype(F32)
        o_ref[...] = (o_all * (z * _sigmoid(z))).astype(o_ref.dtype)

    @pl.when(fast)
    def _():
        kcol = lax.broadcasted_iota(jnp.int32, (kt, dh), 1)
        krow = lax.broadcasted_iota(jnp.int32, (kt, dh), 0)
        kblock = lax.shift_right_logical(krow, SEL_BLOCK_SHIFT)
        aux = jnp.where((kcol == 0) | (kcol == SUBLANES + kblock), 1.0, 0.0).astype(BF16)
        aux_w = jnp.where(lax.broadcasted_iota(jnp.int32, (wk, dh), 1) == 0, 1.0, 0.0).astype(BF16)
        aux_d = jnp.where(lax.broadcasted_iota(jnp.int32, (tq, dh), 1) == 0, 1.0, 0.0).astype(BF16)
        zrows = jnp.zeros((LANES - dh - BF16_SUBLANES, r), BF16)

        first_row = lax.broadcasted_iota(jnp.int32, (SUBLANES, r), 0) == 0

        def query_aug(g, shift, block_bias):
            top = jnp.where(first_row, -shift, 0.0)
            low = jnp.zeros((SUBLANES, r), F32) if block_bias is None else jnp.concatenate([block_bias] * hpg, axis=1)
            return jnp.concatenate([q4t[g], jnp.concatenate([top, low], axis=0).astype(BF16), zrows], axis=0)

        def step(j, accs):
            k_all = ks_ref[j * kt:(j + 1) * kt, :]
            out = []
            for g in range(ng):
                k_aug = jnp.concatenate([k_all[:, g * dh:(g + 1) * dh], aux], axis=1)
                bias = sel_ref[g, j * blocks_per_iter:(j + 1) * blocks_per_iter, :]
                p = jnp.exp2(_dot(k_aug, query_aug(g, bound_s[g], bias))).astype(BF16)
                out.append(accs[g] + _dot(value_tiles(vst_ref, j * tiles_per_iter, tiles_per_iter, g), p))
            return tuple(out)

        def finish(accs):
            kd_all = ks_ref[pl.ds(pl.multiple_of(t0, tq), tq), :]
            causal = (lax.broadcasted_iota(jnp.int32, (tq, tq), 0)
                      <= lax.broadcasted_iota(jnp.int32, (tq, tq), 1))
            kw_all = kw_ref[pl.ds(off_w, wk), :]
            for g in range(ng):
                kd_aug = jnp.concatenate([kd_all[:, g * dh:(g + 1) * dh], aux_d], axis=1)
                s_d = _mask_heads(_dot(kd_aug, query_aug(g, bound_s[g], None)), causal, -EXCLUDE, tq)
                acc = accs[g] + _dot(value_tiles(vst_ref, qi, 1, g), jnp.exp2(s_d).astype(BF16))
                osw_ref[0, g] = normalised(acc)
                kw_aug = jnp.concatenate([kw_all[:, g * dh:(g + 1) * dh], aux_w], axis=1)
                s_w = _mask_heads(_dot(kw_aug, query_aug(g, bound_w[g], None)), ok_w, -EXCLUDE, tq)
                p_w = jnp.exp2(s_w).astype(BF16)
                acc_w = _dot(value_tiles(vwt_ref, j0, 1, g), p_w[0:tk, :])
                for i in range(1, n_wb + 1):
                    acc_w = acc_w + _dot(value_tiles(vwt_ref, j0 + i, 1, g), p_w[i * tk:(i + 1) * tk, :])
                osw_ref[1, g] = normalised(acc_w)

        steps_needed = (t0 + kt - 1) // kt
        for n_steps in range(seq // kt + 1):
            @pl.when(steps_needed == n_steps)
            def _(n_steps=n_steps):
                compressed_and_select()
                before_diag = jr < qi * (tq // A_SEL_BLOCK)
                for g in range(ng):
                    sel_ref[g] = jnp.where((sel_ref[g] > 0.5) & before_diag, 0.0, -EXCLUDE)
                accs = tuple(jnp.zeros((da, r), F32) for _ in range(ng))
                for j in range(n_steps):
                    accs = step(j, accs)
                finish(accs)
                emit_output()

    @pl.when(jnp.logical_not(fast))
    def _():
        compressed_and_select()
        zpad = jnp.zeros((dh, r), BF16)
        q_pad = [jnp.concatenate([q4t[g] if gg == g else zpad for gg in range(ng)], axis=0) for g in range(ng)]
        sub_s = lax.broadcasted_iota(jnp.int32, (kt, tq), 0)
        t_abs_s = t0 + lax.broadcasted_iota(jnp.int32, (kt, tq), 1)

        def body(j, carry):
            off = pl.multiple_of(j * kt, kt)
            k_all = ks_ref[pl.ds(off, kt), :]
            causal = (off + sub_s) <= t_abs_s
            out = []
            for g in range(ng):
                m, acc = carry[g]
                s = _dot(k_all, q_pad[g])
                flags = jnp.concatenate(
                    [jnp.broadcast_to(sel_ref[g, pl.ds(j * blocks_per_iter + i, 1), :], (A_SEL_BLOCK, tq))
                     for i in range(blocks_per_iter)], axis=0)
                s = _mask_heads(s, (flags > 0.5) & causal, NEG_INF, tq)
                m_new = jnp.maximum(m, jnp.max(s, axis=0, keepdims=True))
                p = jnp.exp2(s - m_new).astype(BF16)
                vt = value_tiles(vst_ref, j * tiles_per_iter, tiles_per_iter, g)
                out.append((m_new, jnp.exp2(m - m_new) * acc + _dot(vt, p)))
            return tuple(out)

        init = tuple((jnp.full((1, r), NEG_INF, F32), jnp.zeros((da, r), F32)) for _ in range(ng))
        state = lax.fori_loop(0, (t0 + tq - 1) // kt + 1, body, init)
        kw_all = kw_ref[pl.ds(off_w, wk), :]
        for g in range(ng):
            osw_ref[0, g] = normalised(state[g][1])
            s_w = _mask_heads(_dot(kw_all, q_pad[g]), ok_w, NEG_INF, tq)
            p_w = jnp.exp2(s_w - jnp.max(s_w, axis=0, keepdims=True)).astype(BF16)
            acc_w = _dot(value_tiles(vwt_ref, j0, 1, g), p_w[0:tk, :])
            for i in range(1, n_wb + 1):
                acc_w = acc_w + _dot(value_tiles(vwt_ref, j0 + i, 1, g), p_w[i * tk:(i + 1) * tk, :])
            osw_ref[1, g] = normalised(acc_w)
        emit_output()


def _nsa(p, pkv, ps, kvc, vct, q_col, z_col, bsz, seq):
    t = p.shape[0]
    g = A_KV_HEADS
    dh = A_HEAD_DIM
    aw = A_HEADS * dh
    kvw = g * dh
    tq = Q_BLOCK
    nq = seq // tq
    n_cmp = kvc.shape[3]
    tk = LANES
    rows = lambda b, qi: b * nq + qi
    kv_piece = lambda i: pl.BlockSpec((seq, kvw), lambda b, qi: (b, i))
    return pl.pallas_call(
        _nsa_kernel,
        grid=(bsz, nq),
        in_specs=[
            pl.BlockSpec((tq, aw), lambda b, qi: (rows(b, qi), q_col // aw)),
            pl.BlockSpec((None, None, g, n_cmp, dh), lambda b, qi: (0, b, 0, 0, 0)),
            pl.BlockSpec((None, g, dh, n_cmp), lambda b, qi: (b, 0, 0, 0)),
            kv_piece(2), kv_piece(3), kv_piece(4), kv_piece(5),
            pl.BlockSpec((LANES, tq), lambda b, qi: (0, rows(b, qi))),
            pl.BlockSpec((tq, aw), lambda b, qi: (rows(b, qi), z_col // aw)),
        ],
        out_specs=pl.BlockSpec((tq, aw), lambda b, qi: (rows(b, qi), 0)),
        out_shape=jax.ShapeDtypeStruct((t, aw), BF16),
        scratch_shapes=[pltpu.VMEM((g, seq // A_SEL_BLOCK, tq), F32),
                        pltpu.VMEM((seq // tk, kvw, tk), BF16),
                        pltpu.VMEM((seq // tk, kvw, tk), BF16),
                        pltpu.VMEM((2 * g, 1, 1), F32),
                        pltpu.VMEM((3, g, dh, A_GROUP * tq), F32)],
        compiler_params=_params("parallel", "arbitrary"),
        name="nsa",
    )(p, kvc, vct, pkv, pkv, pkv, pkv, ps, p)


def _merge_out_kernel(ym_ref, ya_ref, wm_ref, wa_ref, gm_ref, ga_ref, wo_ref, g_ref, x_ref, o_ref):
    um = _dot(ym_ref[...], wm_ref[...])
    ua = _dot(ya_ref[...], wa_ref[...])
    mg = _sigmoid(gm_ref[...].astype(F32)) * um + _sigmoid(ga_ref[...].astype(F32)) * ua
    out = _dot(mg.astype(BF16), wo_ref[...])
    ms = jnp.mean(out * out, axis=-1, keepdims=True)
    o_ref[...] = x_ref[...] + out * lax.rsqrt(ms + RMS_EPS) * g_ref[...]


def _merge_out(ym, ya, wm, wa, p, gm_col, ga_col, wo, g, x2d, tm=256):
    t, d = x2d.shape
    kdim = ym.shape[1]
    resident = lambda shape: pl.BlockSpec(shape, lambda i: (0, 0), pipeline_mode=pl.Buffered(1))
    return pl.pallas_call(
        _merge_out_kernel,
        grid=(t // tm,),
        in_specs=[
            pl.BlockSpec((tm, kdim), lambda i: (i, 0)),
            pl.BlockSpec((tm, kdim), lambda i: (i, 0)),
            resident((kdim, d)),
            resident((kdim, d)),
            pl.BlockSpec((tm, d), lambda i: (i, gm_col // d)),
            pl.BlockSpec((tm, d), lambda i: (i, ga_col // d)),
            resident((d, d)),
            pl.BlockSpec((1, d), lambda i: (0, 0)),
            pl.BlockSpec((tm, d), lambda i: (i, 0)),
        ],
        out_specs=pl.BlockSpec((tm, d), lambda i: (i, 0)),
        out_shape=jax.ShapeDtypeStruct((t, d), F32),
        compiler_params=_params("parallel"),
        name="merge_out",
    )(ym, ya, wm, wa, p, p, wo, g, x2d)


def _layer(x, pre_g, w_in, conv_w, conv_b, w_q, w_k, b_i, b_f, norm_g, skip,
           pe_k, w1_k, w2_k, pe_v, w1_v, w2_v, w_up_m, w_up_a, w_out, post_g):
    bsz, seq, d = x.shape
    t = bsz * seq
    mw = w_up_m.shape[0]
    aw = w_up_a.shape[0]
    g = A_KV_HEADS
    dh = A_HEAD_DIM
    kvw = g * dh
    n_gate = 3 * A_HEADS

    o_i = 4 * mw
    o_f = o_i + M_HEADS
    o_q = o_f + M_HEADS
    o_g = o_q + aw + 6 * kvw
    o_z = o_g + n_gate
    o_m = o_z + aw
    o_kv = o_q + aw
    main_pieces = ((o_m, w_in.shape[1]), (0, o_i), (o_q, o_kv), (o_z, o_m))
    tail_pieces = ((o_kv, o_g),)
    w_t = w_in.T
    tile_rows = lambda pieces, tn: jnp.asarray(
        [r // SUBLANES for lo, hi in pieces for r in range(lo, hi, tn)], jnp.int32)
    row_starts = tile_rows(main_pieces, INPROJ_TN)
    small_pieces = ((o_i, o_q), (o_g, o_z))
    gm_col = 0
    ga_col = d
    mx_col = 2 * d
    q_col = mx_col + 4 * mw
    z_col = q_col + aw

    x2d = x.reshape(t, d)
    p, ps, h = _inproj(x2d, pre_g.reshape(1, d), w_t, row_starts, small_pieces, tn=INPROJ_TN)
    pkv = _inproj_tail(h, w_t, tile_rows(tail_pieces, INPROJ_TAIL_TN), tn=INPROJ_TAIL_TN)

    xc, q, k = _conv_qk(p, mx_col, conv_w, conv_b.reshape(1, mw), w_q.astype(BF16), w_k.astype(BF16), bsz, seq)
    bias_m = jnp.concatenate([b_i, b_f]).reshape(2 * M_HEADS, 1)
    y_m = _mlstm(p, mx_col, xc, q, k, ps, bias_m, norm_g.reshape(1, mw), skip.reshape(1, mw), bsz, seq)

    half = A_CMP_STRIDE * dh
    pe = jnp.stack([pe_k.reshape(2, half), pe_v.reshape(2, half)])
    w1 = jnp.stack([w1_k, w1_v]).astype(BF16)
    w2 = jnp.stack([w2_k, w2_v]).astype(BF16)
    kvc = _compress(pkv, 0, pe, w1, w2, bsz, seq)
    vct = jnp.swapaxes(kvc[1], 2, 3)
    y_a = _nsa(p, pkv, ps, kvc, vct, q_col, z_col, bsz, seq)

    out = _merge_out(y_m, y_a, w_up_m.astype(BF16), w_up_a.astype(BF16), p, gm_col, ga_col,
                     w_out.astype(BF16), post_g.reshape(1, d), x2d)
    return out.reshape(bsz, seq, d)


def kernel(x, pre_norm_g, w_in, m_conv_w, m_conv_b, m_w_q, m_w_k, m_b_i, m_b_f, m_norm_g, m_skip, a_pe_k, a_w1_k, a_w2_k, a_pe_v, a_w1_v, a_w2_v, w_up_m, w_up_a, w_out, post_norm_g):
    depth = w_in.shape[0]
    for l in range(depth):
        x = _layer(x, pre_norm_g[l], w_in[l], m_conv_w[l], m_conv_b[l], m_w_q[l], m_w_k[l], m_b_i[l], m_b_f[l],
                   m_norm_g[l], m_skip[l], a_pe_k[l], a_w1_k[l], a_w2_k[l], a_pe_v[l], a_w1_v[l], a_w2_v[l],
                   w_up_m[l], w_up_a[l], w_out[l], post_norm_g[l])
    return x
```

```python
import functools

import jax
import jax.numpy as jnp
from jax import lax
from jax.experimental import pallas as pl
from jax.experimental.pallas import tpu as pltpu

F32 = jnp.float32
BF16 = jnp.bfloat16

M_HEADS = 4
M_CONV = 4
M_CHUNK = 128
A_HEADS = 16
A_KV_HEADS = 4
A_GROUP = A_HEADS // A_KV_HEADS
A_HEAD_DIM = 64
A_CMP_BLOCK = 32
A_CMP_STRIDE = 16
A_SEL_BLOCK = 64
A_SEL_TOPK = 8
A_WINDOW = 512
Q_BLOCK = 128
SEL_KEYS_PER_STEP = 512
EXCLUDE = 1e4
MAX_FAST_BOUND = 40.0
SEL_BLOCK_SHIFT = A_SEL_BLOCK.bit_length() - 1
KEY_NORM_ROWS = 512
BOUND_SLACK = 1.01
INPROJ_TN = 1024
INPROJ_TAIL_TN = 512
RMS_EPS = 1e-6
LN_EPS = 1e-6
NEG_INF = -1e30
POS_INF = 1e30

LANES = 128
SUBLANES = 8
BF16_SUBLANES = 16
LOG2_E = 1.4426950408889634
VMEM_LIMIT_BYTES = 48 * 1024 * 1024
INPROJ_VMEM_LIMIT_BYTES = 56 * 1024 * 1024


def _nt_dot(a, b):
    return lax.dot_general(a, b, (((1,), (1,)), ((), ())), preferred_element_type=F32)


def _dot(a, b):
    return jnp.dot(a, b, preferred_element_type=F32)


def _sigmoid(x):
    return 1.0 / (1.0 + jnp.exp(-x))


def _params(*sem):
    return pltpu.CompilerParams(dimension_semantics=sem, vmem_limit_bytes=VMEM_LIMIT_BYTES)


def _inproj_kernel(rows_ref, x_ref, g_ref, wt_ref, *rest):
    del rows_ref
    o_ref, ost_ref, h_ref, h_scr = rest[-4:]
    small_refs = rest[:-4]

    @pl.when(pl.program_id(1) == 0)
    def _():
        xf = x_ref[...]
        ms = jnp.mean(xf * xf, axis=-1, keepdims=True)
        hb = (xf * lax.rsqrt(ms + RMS_EPS) * g_ref[...]).astype(BF16)
        h_scr[...] = hb
        h_ref[...] = hb
        rows = [r[...] for r in small_refs]
        used = sum(r.shape[0] for r in rows)
        rows.append(jnp.zeros((ost_ref.shape[0] - used, xf.shape[1]), F32))
        ost_ref[...] = _nt_dot(jnp.concatenate(rows, axis=0).astype(BF16), hb)

    o_ref[...] = _nt_dot(h_scr[...], wt_ref[...].astype(BF16)).astype(o_ref.dtype)


def _inproj(x2d, g, w_t, row_starts, small_pieces, tm=1024, tn=512):
    t, d = x2d.shape
    n = row_starts.shape[0] * tn
    ns = LANES
    rows_of = lambda lo, hi: pl.BlockSpec((pl.Element(hi - lo), pl.Element(d)), lambda i, j, rows: (lo, 0))
    grid_spec = pltpu.PrefetchScalarGridSpec(
        num_scalar_prefetch=1,
        grid=(t // tm, n // tn),
        in_specs=[
            pl.BlockSpec((tm, d), lambda i, j, rows: (i, 0)),
            pl.BlockSpec((1, d), lambda i, j, rows: (0, 0)),
            pl.BlockSpec((pl.Element(tn), pl.Element(d)), lambda i, j, rows: (rows[j] * SUBLANES, 0)),
        ] + [rows_of(lo, hi) for lo, hi in small_pieces],
        out_specs=[
            pl.BlockSpec((tm, tn), lambda i, j, rows: (i, j)),
            pl.BlockSpec((ns, tm), lambda i, j, rows: (0, i)),
            pl.BlockSpec((tm, d), lambda i, j, rows: (i, 0)),
        ],
        scratch_shapes=[pltpu.VMEM((tm, d), BF16)],
    )
    return pl.pallas_call(
        _inproj_kernel,
        grid_spec=grid_spec,
        out_shape=[jax.ShapeDtypeStruct((t, n), BF16), jax.ShapeDtypeStruct((ns, t), F32),
                   jax.ShapeDtypeStruct((t, d), BF16)],
        compiler_params=pltpu.CompilerParams(dimension_semantics=("parallel", "arbitrary"),
                                             vmem_limit_bytes=INPROJ_VMEM_LIMIT_BYTES),
        name="inproj",
    )(row_starts, x2d, g, w_t, *([w_t] * len(small_pieces)))


def _inproj_tail_kernel(rows_ref, h_ref, wt_ref, *rest):
    del rows_ref
    n_cast = (len(rest) - 1) // 2
    cast_in, o_ref, cast_out = rest[:n_cast], rest[n_cast], rest[n_cast + 1:]
    o_ref[...] = _nt_dot(h_ref[...], wt_ref[...].astype(BF16)).astype(o_ref.dtype)

    @pl.when(pl.program_id(1) == 0)
    def _():
        for src, dst in zip(cast_in, cast_out):
            dst[...] = src[...].astype(dst.dtype)


def _inproj_tail(h, w_t, row_starts, cast_weights, tm=2048, tn=512):
    t, d = h.shape
    nj = row_starts.shape[0]
    ni = t // tm
    slab = lambda w: pl.BlockSpec((w.shape[0] // ni, w.shape[1]), lambda i, j, rows: (i, 0))
    grid_spec = pltpu.PrefetchScalarGridSpec(
        num_scalar_prefetch=1,
        grid=(ni, nj),
        in_specs=[
            pl.BlockSpec((tm, d), lambda i, j, rows: (i, 0)),
            pl.BlockSpec((pl.Element(tn), pl.Element(d)), lambda i, j, rows: (rows[j] * SUBLANES, 0)),
        ] + [slab(w) for w in cast_weights],
        out_specs=[pl.BlockSpec((tm, tn), lambda i, j, rows: (i, j))] + [slab(w) for w in cast_weights],
    )
    outs = pl.pallas_call(
        _inproj_tail_kernel,
        grid_spec=grid_spec,
        out_shape=[jax.ShapeDtypeStruct((t, nj * tn), BF16)]
                  + [jax.ShapeDtypeStruct(w.shape, BF16) for w in cast_weights],
        compiler_params=pltpu.CompilerParams(dimension_semantics=("arbitrary", "arbitrary"),
                                             vmem_limit_bytes=INPROJ_VMEM_LIMIT_BYTES),
        name="inproj_tail",
    )(row_starts, h, w_t, *cast_weights)
    return outs[0], outs[1:]


def _conv_qk_kernel(x_ref, cw_ref, cb_ref, wq_ref, wk_ref, xc_ref, q_ref, k_ref, *, rows):
    s, c = x_ref.shape
    cw = cw_ref[...]
    cb = cb_ref[...]
    wq = wq_ref[...]
    wk = wk_ref[...]
    k_scale = c ** -0.5
    first_row = lax.broadcasted_iota(jnp.int32, (rows, c), 0) == 0
    carry = [jnp.zeros((1, c), F32) for _ in range(M_CONV - 1)]
    for r in range(s // rows):
        xr = x_ref[r * rows:(r + 1) * rows, :].astype(F32)
        a = cw[0:1, :] * xr
        last = []
        for kk in range(1, M_CONV):
            last.append(a[rows - 1:rows, :])
            a = jnp.where(first_row, carry[kk - 1], pltpu.roll(a, 1, 0)) + cw[kk:kk + 1, :] * xr
        carry = last
        acc = a + cb
        xcb = (acc * _sigmoid(acc)).astype(BF16)
        xc_ref[r * rows:(r + 1) * rows, :] = xcb
        q_ref[r * rows:(r + 1) * rows, :] = _dot(xcb, wq).astype(BF16)
        k_ref[r * rows:(r + 1) * rows, :] = (_dot(xcb, wk) * k_scale).astype(BF16)


def _conv_qk(p, mx_col, conv_w, conv_b, wq, wk, bsz, seq):
    t = p.shape[0]
    heads, hd = wq.shape[0], wq.shape[1]
    width = heads * hd
    blk = pl.BlockSpec((seq, hd), lambda b, h: (b, h))
    out = jax.ShapeDtypeStruct((t, width), BF16)
    return pl.pallas_call(
        functools.partial(_conv_qk_kernel, rows=256),
        grid=(bsz, heads),
        in_specs=[
            pl.BlockSpec((seq, hd), lambda b, h: (b, mx_col // hd + h)),
            pl.BlockSpec((M_CONV, hd), lambda b, h: (0, h)),
            pl.BlockSpec((1, hd), lambda b, h: (0, h)),
            pl.BlockSpec((None, hd, hd), lambda b, h: (h, 0, 0)),
            pl.BlockSpec((None, hd, hd), lambda b, h: (h, 0, 0)),
        ],
        out_specs=[blk, blk, blk],
        out_shape=[out, out, out],
        compiler_params=_params("parallel", "parallel"),
        name="conv_qk",
    )(p, conv_w, conv_b, wq, wk)


def _log_sigmoid(x):
    return jnp.minimum(x, 0.0) - jnp.log1p(jnp.exp(-jnp.abs(x)))


def _mlstm_kernel(bias_ref, q_ref, k_ref, v_ref, o_ref, z_ref, xc_ref, g_ref, ng_ref, sk_ref,
                  y_ref, c_scr, n_scr, m_scr):
    @pl.when(pl.program_id(1) == 0)
    def _():
        c_scr[...] = jnp.zeros_like(c_scr)
        n_scr[...] = jnp.zeros_like(n_scr)
        m_scr[...] = jnp.zeros_like(m_scr)

    ln = q_ref.shape[0]
    heads, hd = c_scr.shape[0], c_scr.shape[1]
    gates = g_ref[...] + bias_ref[...]
    fb_all = _log_sigmoid(gates[heads:2 * heads, :])
    row = lax.broadcasted_iota(jnp.int32, (ln, ln), 0)
    col = lax.broadcasted_iota(jnp.int32, (ln, ln), 1)
    tril = col <= row
    eye = col == row
    for hh in range(heads):
        sl = slice(hh * hd, (hh + 1) * hd)
        ib = gates[hh:hh + 1, :]
        fb = fb_all[hh:hh + 1, :]
        bcum_c = jnp.sum(jnp.where(tril, fb, 0.0), axis=1, keepdims=True)
        bcum_r = jnp.sum(jnp.where(eye, bcum_c, 0.0), axis=0, keepdims=True)
        g_tot = jnp.sum(fb, axis=1, keepdims=True)
        e_r = ib - bcum_r
        m_st = m_scr[hh]
        d_intra = jnp.where(tril, bcum_c + e_r, NEG_INF)
        d_inter = bcum_c + m_st
        m_t = jnp.maximum(d_inter, jnp.max(d_intra, axis=1, keepdims=True))
        w_intra = jnp.exp(d_intra - m_t)
        w_inter = jnp.exp(d_inter - m_t)

        qb = q_ref[:, sl]
        kb = k_ref[:, sl]
        vb = v_ref[:, sl]
        c_st = c_scr[hh]
        n_st = n_scr[hh]
        sc = _nt_dot(qb, kb) * w_intra
        num = _dot(sc.astype(BF16), vb) + w_inter * _dot(qb, c_st.astype(BF16))
        n_rows = jnp.broadcast_to(n_st, (SUBLANES, hd)).astype(BF16)
        qn = _nt_dot(qb, n_rows)[:, 0:1]
        den = jnp.sum(sc, axis=1, keepdims=True) + w_inter * qn
        h = num / jnp.maximum(jnp.abs(den), jnp.exp(-m_t))

        d_state = g_tot + e_r
        m_new = jnp.maximum(g_tot + m_st, jnp.max(d_state, axis=1, keepdims=True))
        w_s_r = jnp.exp(d_state - m_new)
        w_s_c = jnp.sum(jnp.where(eye, w_s_r, 0.0), axis=1, keepdims=True)
        decay = jnp.exp(g_tot + m_st - m_new)
        kw = kb * w_s_c.astype(BF16)
        kv = lax.dot_general(kw, vb, (((0,), (0,)), ((), ())), preferred_element_type=F32)
        c_scr[hh] = decay * c_st + kv
        w_rows = jnp.broadcast_to(w_s_r, (SUBLANES, ln)).astype(BF16)
        n_scr[hh] = decay * n_st + _dot(w_rows, kb)[0:1, :]
        m_scr[hh] = m_new

        hg = h * _sigmoid(o_ref[:, sl].astype(F32))
        mu = jnp.mean(hg, axis=-1, keepdims=True)
        hc = hg - mu
        var = jnp.mean(hc * hc, axis=-1, keepdims=True)
        hn = hc * lax.rsqrt(var + LN_EPS)
        z = z_ref[:, sl].astype(F32)
        y = (hn * ng_ref[:, sl] + sk_ref[:, sl] * xc_ref[:, sl].astype(F32)) * (z * _sigmoid(z))
        y_ref[:, sl] = y.astype(y_ref.dtype)


def _mlstm(p, mx_col, xc, q, k, gates, bias, norm_g, skip, bsz, seq):
    t, width = q.shape
    pc = mx_col // width
    heads = M_HEADS
    hd = width // heads
    ln = M_CHUNK
    nc = seq // ln
    rows = lambda b, c: (b * nc + c, 0)
    vec = pl.BlockSpec((1, width), lambda b, c: (0, 0))
    return pl.pallas_call(
        _mlstm_kernel,
        grid=(bsz, nc),
        in_specs=[
            pl.BlockSpec((2 * heads, 1), lambda b, c: (0, 0)),
            pl.BlockSpec((ln, width), rows),
            pl.BlockSpec((ln, width), rows),
            pl.BlockSpec((ln, width), lambda b, c: (b * nc + c, pc + 1)),
            pl.BlockSpec((ln, width), lambda b, c: (b * nc + c, pc + 2)),
            pl.BlockSpec((ln, width), lambda b, c: (b * nc + c, pc + 3)),
            pl.BlockSpec((ln, width), rows),
            pl.BlockSpec((2 * heads, ln), lambda b, c: (0, b * nc + c)),
            vec, vec,
        ],
        out_specs=pl.BlockSpec((ln, width), rows),
        out_shape=jax.ShapeDtypeStruct((t, width), BF16),
        scratch_shapes=[pltpu.VMEM((heads, hd, hd), F32), pltpu.VMEM((heads, 1, hd), F32),
                        pltpu.VMEM((heads, 1, 1), F32)],
        compiler_params=_params("parallel", "arbitrary"),
        name="mlstm",
    )(bias, q, k, p, p, p, xc, gates, norm_g, skip)


def _compress_kernel(x_ref, pe_ref, w1_ref, w2_ref, o_ref, xf_scr):
    ng, n_sub, dh = o_ref.shape
    half = A_CMP_STRIDE * dh
    n_chunk = xf_scr.shape[0]
    for c in range(n_chunk):
        xf_scr[c] = x_ref[:, c * LANES:(c + 1) * LANES].astype(F32)
    xs = [jnp.concatenate([xf_scr[c, pl.ds(r, n_sub, stride=A_CMP_STRIDE), :] for c in range(n_chunk)], axis=1)
          for r in range(A_CMP_STRIDE)]
    pe = pe_ref[...]
    rid = lax.broadcasted_iota(jnp.int32, (n_sub, dh), 0)
    for g in range(ng):
        t = jnp.concatenate([x[:, g * dh:(g + 1) * dh] for x in xs], axis=1)
        za = _dot((t + pe[0:1, :]).astype(BF16), w1_ref[0:half, :])
        zb = _dot((t + pe[1:2, :]).astype(BF16), w1_ref[half:2 * half, :])
        pre = za + pltpu.roll(zb, n_sub - 1, 0)
        hid = (pre * _sigmoid(pre)).astype(BF16)
        out = _dot(hid, w2_ref[...])
        o_ref[g] = jnp.where(rid < n_sub - 1, out, 0.0).astype(o_ref.dtype)


def _compress(p, kc_col, pe, w1, w2, bsz, seq):
    g = A_KV_HEADS
    dh = A_HEAD_DIM
    kvw = g * dh
    n_sub = seq // A_CMP_STRIDE
    half = A_CMP_STRIDE * dh
    hid = w1.shape[-1]
    return pl.pallas_call(
        _compress_kernel,
        grid=(2, bsz),
        in_specs=[
            pl.BlockSpec((seq, kvw), lambda i, b: (b, kc_col // kvw + i)),
            pl.BlockSpec((None, 2, half), lambda i, b: (i, 0, 0)),
            pl.BlockSpec((None, 2 * half, hid), lambda i, b: (i, 0, 0)),
            pl.BlockSpec((None, hid, dh), lambda i, b: (i, 0, 0)),
        ],
        out_specs=pl.BlockSpec((None, None, g, n_sub, dh), lambda i, b: (i, b, 0, 0, 0)),
        out_shape=jax.ShapeDtypeStruct((2, bsz, g, n_sub, dh), BF16),
        scratch_shapes=[pltpu.VMEM((kvw // LANES, seq, LANES), F32)],
        compiler_params=_params("parallel", "parallel"),
        name="compress",
    )(p, pe, w1, w2)


def _mask_heads(x, ok, fill, tq):
    n = x.shape[1] // tq
    return jnp.concatenate([jnp.where(ok, x[:, h * tq:(h + 1) * tq], fill) for h in range(n)], axis=1)


def _with_ones_rows(vt):
    return jnp.concatenate([vt, jnp.ones((BF16_SUBLANES, vt.shape[1]), vt.dtype)], axis=0)


def _nsa_kernel(q_ref, kc_ref, vct_ref, ks_ref, vs_ref, kw_ref, vw_ref, ps_ref, z_ref, o_ref,
                sel_ref, vst_ref, vwt_ref, kn_ref, osw_ref):
    qi = pl.program_id(1)
    tq = q_ref.shape[0]
    dh = A_HEAD_DIM
    hpg = A_GROUP
    ng = A_KV_HEADS
    r = hpg * tq
    n_cmp = kc_ref.shape[1]
    tk = vwt_ref.shape[2]
    kt = SEL_KEYS_PER_STEP
    tiles_per_iter = kt // tk
    seq = ks_ref.shape[0]
    n_sel = seq // A_SEL_BLOCK
    t0 = qi * tq
    da = dh + BF16_SUBLANES

    @pl.when(qi == 0)
    def _():
        for j in range(seq // tk):
            vst_ref[j] = vs_ref[j * tk:(j + 1) * tk, :].astype(F32).T.astype(BF16)
            vwt_ref[j] = vw_ref[j * tk:(j + 1) * tk, :].astype(F32).T.astype(BF16)
        rows = KEY_NORM_ROWS
        dim_group = lax.broadcasted_iota(jnp.int32, (ng * dh, LANES), 0) // dh
        member = jnp.where(dim_group == lax.broadcasted_iota(jnp.int32, (ng * dh, LANES), 1), 1.0, 0.0).astype(BF16)
        for which, k_ref in enumerate((ks_ref, kw_ref)):
            best = jnp.zeros((1, LANES), F32)
            for c in range(seq // rows):
                kb = k_ref[c * rows:(c + 1) * rows, :]
                best = jnp.maximum(best, jnp.max(_dot(kb * kb, member), axis=0, keepdims=True))
            for g in range(ng):
                kn_ref[which * ng + g] = best[:, g:g + 1]

    qt = (q_ref[...].astype(F32) * (dh ** -0.5 * LOG2_E)).T.astype(BF16)
    q4t = [jnp.concatenate([qt[(g * hpg + h) * dh:(g * hpg + h + 1) * dh, :] for h in range(hpg)], axis=1)
           for g in range(ng)]

    cn = lax.broadcasted_iota(jnp.int32, (n_cmp, tq), 0)
    ct = t0 + lax.broadcasted_iota(jnp.int32, (n_cmp, tq), 1)
    valid = (cn * A_CMP_STRIDE + (A_CMP_BLOCK - 1)) <= ct
    jn = lax.broadcasted_iota(jnp.int32, (n_sel, n_cmp), 0)
    nn = lax.broadcasted_iota(jnp.int32, (n_sel, n_cmp), 1)
    ov = ((nn * A_CMP_STRIDE < (jn + 1) * A_SEL_BLOCK)
          & (nn * A_CMP_STRIDE + (A_CMP_BLOCK - 1) >= jn * A_SEL_BLOCK))
    ov_t = jnp.where(ov, 1.0, 0.0).astype(BF16)
    jr = lax.broadcasted_iota(jnp.int32, (n_sel, tq), 0)
    tl = t0 + lax.broadcasted_iota(jnp.int32, (n_sel, tq), 1)
    cur = lax.shift_right_logical(tl, SEL_BLOCK_SHIFT)
    forced = (jr == 0) | (jr == cur) | (jr == cur - 1)
    def compressed_and_select():
        for g in range(ng):
            s_m = _mask_heads(_dot(kc_ref[g], q4t[g]), valid, NEG_INF, tq)
            mx = jnp.max(s_m, axis=0, keepdims=True)
            e = _mask_heads(jnp.exp2(s_m - mx), valid, 0.0, tq)
            lsum = jnp.sum(e, axis=0, keepdims=True)
            p_c = e * (1.0 / jnp.where(lsum > 0.0, lsum, 1.0))
            osw_ref[2, g] = _dot(vct_ref[g], p_c.astype(BF16))
            p_sum = p_c[:, 0:tq]
            for h in range(1, hpg):
                p_sum = p_sum + p_c[:, h * tq:(h + 1) * tq]
            p_hi = p_sum.astype(BF16)
            r1 = p_sum - p_hi.astype(F32)
            p_mid = r1.astype(BF16)
            p_lo = (r1 - p_mid.astype(F32)).astype(BF16)
            imp_t = _dot(ov_t, p_hi) + _dot(ov_t, p_mid) + _dot(ov_t, p_lo)
            score = jnp.where(forced, POS_INF, jnp.where(jr <= cur, imp_t, NEG_INF))
            rank = jnp.zeros((n_sel, tq), F32)
            for jp in range(n_sel):
                sj = score[jp:jp + 1, :]
                beats = (sj > score) | ((sj == score) & (jr > jp))
                rank = rank + jnp.where(beats, 1.0, 0.0)
            sel_ref[g] = jnp.where(rank < float(min(A_SEL_TOPK, n_sel)), 1.0, 0.0)

    qn = [jnp.sqrt(jnp.sum(jnp.square(q4t[g].astype(F32)), axis=0, keepdims=True)) for g in range(ng)]
    bound_s = [qn[g] * jnp.sqrt(kn_ref[g]) * BOUND_SLACK for g in range(ng)]
    bound_w = [qn[g] * jnp.sqrt(kn_ref[ng + g]) * BOUND_SLACK for g in range(ng)]
    worst = bound_s[0]
    for b in bound_s[1:] + bound_w:
        worst = jnp.maximum(worst, b)
    fast = jnp.max(worst) <= MAX_FAST_BOUND

    n_wb = -(-A_WINDOW // tk)
    wk = (n_wb + 1) * tk
    j0 = jnp.maximum(qi - n_wb, 0)
    off_w = pl.multiple_of(j0 * tk, tk)
    c_abs = off_w + lax.broadcasted_iota(jnp.int32, (wk, tq), 0)
    t_abs_w = t0 + lax.broadcasted_iota(jnp.int32, (wk, tq), 1)
    ok_w = (c_abs <= t_abs_w) & (c_abs > t_abs_w - A_WINDOW)
    blocks_per_iter = kt // A_SEL_BLOCK

    def normalised(acc):
        return acc[0:dh, :] * (1.0 / acc[dh:dh + 1, :])

    def value_tiles(v_ref, first, count, g):
        vt = jnp.concatenate([v_ref[first + i, g * dh:(g + 1) * dh, :] for i in range(count)], axis=1)
        return _with_ones_rows(vt)

    def emit_output():
        gs = _sigmoid(ps_ref[...])
        g0 = 2 * M_HEADS
        outs = []
        for g in range(ng):
            o_s = osw_ref[0, g]
            o_w = osw_ref[1, g]
            o_c = osw_ref[2, g]
            for h in range(hpg):
                sl = slice(h * tq, (h + 1) * tq)
                gr = g0 + 3 * (g * hpg + h)
                outs.append(gs[gr:gr + 1, :] * o_c[:, sl] + gs[gr + 1:gr + 2, :] * o_s[:, sl]
                            + gs[gr + 2:gr + 3, :] * o_w[:, sl])
        o_all = jnp.concatenate(outs, axis=0).T
        z = z_ref[...].astype(F32)
        o_ref[...] = (o_all * (z * _sigmoid(z))).astype(o_ref.dtype)

    @pl.when(fast)
    def _():
        kcol = lax.broadcasted_iota(jnp.int32, (kt, dh), 1)
        krow = lax.broadcasted_iota(jnp.int32, (kt, dh), 0)
        kblock = lax.shift_right_logical(krow, SEL_BLOCK_SHIFT)
        aux = jnp.where((kcol == 0) | (kcol == SUBLANES + kblock), 1.0, 0.0).astype(BF16)
        aux_w = jnp.where(lax.broadcasted_iota(jnp.int32, (wk, dh), 1) == 0, 1.0, 0.0).astype(BF16)
        aux_d = jnp.where(lax.broadcasted_iota(jnp.int32, (tq, dh), 1) == 0, 1.0, 0.0).astype(BF16)
        zrows = jnp.zeros((LANES - dh - BF16_SUBLANES, r), BF16)

        first_row = lax.broadcasted_iota(jnp.int32, (SUBLANES, r), 0) == 0

        def query_aug(g, shift, block_bias):
            top = jnp.where(first_row, -shift, 0.0)
            low = jnp.zeros((SUBLANES, r), F32) if block_bias is None else jnp.concatenate([block_bias] * hpg, axis=1)
            return jnp.concatenate([q4t[g], jnp.concatenate([top, low], axis=0).astype(BF16), zrows], axis=0)

        def step(j, accs):
            k_all = ks_ref[j * kt:(j + 1) * kt, :]
            out = []
            for g in range(ng):
                k_aug = jnp.concatenate([k_all[:, g * dh:(g + 1) * dh], aux], axis=1)
                bias = sel_ref[g, j * blocks_per_iter:(j + 1) * blocks_per_iter, :]
                p = jnp.exp2(_dot(k_aug, query_aug(g, bound_s[g], bias))).astype(BF16)
                out.append(accs[g] + _dot(value_tiles(vst_ref, j * tiles_per_iter, tiles_per_iter, g), p))
            return tuple(out)

        def finish(accs):
            kd_all = ks_ref[pl.ds(pl.multiple_of(t0, tq), tq), :]
            causal = (lax.broadcasted_iota(jnp.int32, (tq, tq), 0)
                      <= lax.broadcasted_iota(jnp.int32, (tq, tq), 1))
            kw_all = kw_ref[pl.ds(off_w, wk), :]
            for g in range(ng):
                kd_aug = jnp.concatenate([kd_all[:, g * dh:(g + 1) * dh], aux_d], axis=1)
                s_d = _mask_heads(_dot(kd_aug, query_aug(g, bound_s[g], None)), causal, -EXCLUDE, tq)
                acc = accs[g] + _dot(value_tiles(vst_ref, qi, 1, g), jnp.exp2(s_d).astype(BF16))
                osw_ref[0, g] = normalised(acc)
                kw_aug = jnp.concatenate([kw_all[:, g * dh:(g + 1) * dh], aux_w], axis=1)
                s_w = _mask_heads(_dot(kw_aug, query_aug(g, bound_w[g], None)), ok_w, -EXCLUDE, tq)
                p_w = jnp.exp2(s_w).astype(BF16)
                acc_w = _dot(value_tiles(vwt_ref, j0, 1, g), p_w[0:tk, :])
                for i in range(1, n_wb + 1):
                    acc_w = acc_w + _dot(value_tiles(vwt_ref, j0 + i, 1, g), p_w[i * tk:(i + 1) * tk, :])
                osw_ref[1, g] = normalised(acc_w)

        steps_needed = (t0 + kt - 1) // kt
        for n_steps in range(seq // kt + 1):
            @pl.when(steps_needed == n_steps)
            def _(n_steps=n_steps):
                compressed_and_select()
                before_diag = jr < qi * (tq // A_SEL_BLOCK)
                for g in range(ng):
                    sel_ref[g] = jnp.where((sel_ref[g] > 0.5) & before_diag, 0.0, -EXCLUDE)
                accs = tuple(jnp.zeros((da, r), F32) for _ in range(ng))
                for j in range(n_steps):
                    accs = step(j, accs)
                finish(accs)
                emit_output()

    @pl.when(jnp.logical_not(fast))
    def _():
        compressed_and_select()
        zpad = jnp.zeros((dh, r), BF16)
        q_pad = [jnp.concatenate([q4t[g] if gg == g else zpad for gg in range(ng)], axis=0) for g in range(ng)]
        sub_s = lax.broadcasted_iota(jnp.int32, (kt, tq), 0)
        t_abs_s = t0 + lax.broadcasted_iota(jnp.int32, (kt, tq), 1)

        def body(j, carry):
            off = pl.multiple_of(j * kt, kt)
            k_all = ks_ref[pl.ds(off, kt), :]
            causal = (off + sub_s) <= t_abs_s
            out = []
            for g in range(ng):
                m, acc = carry[g]
                s = _dot(k_all, q_pad[g])
                flags = jnp.concatenate(
                    [jnp.broadcast_to(sel_ref[g, pl.ds(j * blocks_per_iter + i, 1), :], (A_SEL_BLOCK, tq))
                     for i in range(blocks_per_iter)], axis=0)
                s = _mask_heads(s, (flags > 0.5) & causal, NEG_INF, tq)
                m_new = jnp.maximum(m, jnp.max(s, axis=0, keepdims=True))
                p = jnp.exp2(s - m_new).astype(BF16)
                vt = value_tiles(vst_ref, j * tiles_per_iter, tiles_per_iter, g)
                out.append((m_new, jnp.exp2(m - m_new) * acc + _dot(vt, p)))
            return tuple(out)

        init = tuple((jnp.full((1, r), NEG_INF, F32), jnp.zeros((da, r), F32)) for _ in range(ng))
        state = lax.fori_loop(0, (t0 + tq - 1) // kt + 1, body, init)
        kw_all = kw_ref[pl.ds(off_w, wk), :]
        for g in range(ng):
            osw_ref[0, g] = normalised(state[g][1])
            s_w = _mask_heads(_dot(kw_all, q_pad[g]), ok_w, NEG_INF, tq)
            p_w = jnp.exp2(s_w - jnp.max(s_w, axis=0, keepdims=True)).astype(BF16)
            acc_w = _dot(value_tiles(vwt_ref, j0, 1, g), p_w[0:tk, :])
            for i in range(1, n_wb + 1):
                acc_w = acc_w + _dot(value_tiles(vwt_ref, j0 + i, 1, g), p_w[i * tk:(i + 1) * tk, :])
            osw_ref[1, g] = normalised(acc_w)
        emit_output()


def _nsa(p, pkv, ps, kvc, vct, q_col, z_col, bsz, seq):
    t = p.shape[0]
    g = A_KV_HEADS
    dh = A_HEAD_DIM
    aw = A_HEADS * dh
    kvw = g * dh
    tq = Q_BLOCK
    nq = seq // tq
    n_cmp = kvc.shape[3]
    tk = LANES
    rows = lambda b, qi: b * nq + qi
    kv_piece = lambda i: pl.BlockSpec((seq, kvw), lambda b, qi: (b, i))
    return pl.pallas_call(
        _nsa_kernel,
        grid=(bsz, nq),
        in_specs=[
            pl.BlockSpec((tq, aw), lambda b, qi: (rows(b, qi), q_col // aw)),
            pl.BlockSpec((None, None, g, n_cmp, dh), lambda b, qi: (0, b, 0, 0, 0)),
            pl.BlockSpec((None, g, dh, n_cmp), lambda b, qi: (b, 0, 0, 0)),
            kv_piece(2), kv_piece(3), kv_piece(4), kv_piece(5),
            pl.BlockSpec((LANES, tq), lambda b, qi: (0, rows(b, qi))),
            pl.BlockSpec((tq, aw), lambda b, qi: (rows(b, qi), z_col // aw)),
        ],
        out_specs=pl.BlockSpec((tq, aw), lambda b, qi: (rows(b, qi), 0)),
        out_shape=jax.ShapeDtypeStruct((t, aw), BF16),
        scratch_shapes=[pltpu.VMEM((g, seq // A_SEL_BLOCK, tq), F32),
                        pltpu.VMEM((seq // tk, kvw, tk), BF16),
                        pltpu.VMEM((seq // tk, kvw, tk), BF16),
                        pltpu.VMEM((2 * g, 1, 1), F32),
                        pltpu.VMEM((3, g, dh, A_GROUP * tq), F32)],
        compiler_params=_params("parallel", "arbitrary"),
        name="nsa",
    )(p, kvc, vct, pkv, pkv, pkv, pkv, ps, p)


def _merge_out_kernel(ym_ref, ya_ref, wm_ref, wa_ref, gm_ref, ga_ref, wo_ref, g_ref, x_ref, o_ref):
    um = _dot(ym_ref[...], wm_ref[...])
    ua = _dot(ya_ref[...], wa_ref[...])
    mg = _sigmoid(gm_ref[...].astype(F32)) * um + _sigmoid(ga_ref[...].astype(F32)) * ua
    out = _dot(mg.astype(BF16), wo_ref[...])
    ms = jnp.mean(out * out, axis=-1, keepdims=True)
    o_ref[...] = x_ref[...] + out * lax.rsqrt(ms + RMS_EPS) * g_ref[...]


def _merge_out(ym, ya, wm, wa, p, gm_col, ga_col, wo, g, x2d, tm=256):
    t, d = x2d.shape
    kdim = ym.shape[1]
    resident = lambda shape: pl.BlockSpec(shape, lambda i: (0, 0), pipeline_mode=pl.Buffered(1))
    return pl.pallas_call(
        _merge_out_kernel,
        grid=(t // tm,),
        in_specs=[
            pl.BlockSpec((tm, kdim), lambda i: (i, 0)),
            pl.BlockSpec((tm, kdim), lambda i: (i, 0)),
            resident((kdim, d)),
            resident((kdim, d)),
            pl.BlockSpec((tm, d), lambda i: (i, gm_col // d)),
            pl.BlockSpec((tm, d), lambda i: (i, ga_col // d)),
            resident((d, d)),
            pl.BlockSpec((1, d), lambda i: (0, 0)),
            pl.BlockSpec((tm, d), lambda i: (i, 0)),
        ],
        out_specs=pl.BlockSpec((tm, d), lambda i: (i, 0)),
        out_shape=jax.ShapeDtypeStruct((t, d), F32),
        compiler_params=_params("parallel"),
        name="merge_out",
    )(ym, ya, wm, wa, p, p, wo, g, x2d)


def _layer(x, pre_g, w_in, conv_w, conv_b, w_q, w_k, b_i, b_f, norm_g, skip,
           pe_k, w1_k, w2_k, pe_v, w1_v, w2_v, w_up_m, w_up_a, w_out, post_g):
    bsz, seq, d = x.shape
    t = bsz * seq
    mw = w_up_m.shape[0]
    aw = w_up_a.shape[0]
    g = A_KV_HEADS
    dh = A_HEAD_DIM
    kvw = g * dh
    n_gate = 3 * A_HEADS

    o_i = 4 * mw
    o_f = o_i + M_HEADS
    o_q = o_f + M_HEADS
    o_g = o_q + aw + 6 * kvw
    o_z = o_g + n_gate
    o_m = o_z + aw
    o_kv = o_q + aw
    main_pieces = ((o_m, w_in.shape[1]), (0, o_i), (o_q, o_kv), (o_z, o_m))
    tail_pieces = ((o_kv, o_g),)
    w_t = w_in.T
    tile_rows = lambda pieces, tn: jnp.asarray(
        [r // SUBLANES for lo, hi in pieces for r in range(lo, hi, tn)], jnp.int32)
    row_starts = tile_rows(main_pieces, INPROJ_TN)
    small_pieces = ((o_i, o_q), (o_g, o_z))
    gm_col = 0
    ga_col = d
    mx_col = 2 * d
    q_col = mx_col + 4 * mw
    z_col = q_col + aw

    x2d = x.reshape(t, d)
    p, ps, h = _inproj(x2d, pre_g.reshape(1, d), w_t, row_starts, small_pieces, tn=INPROJ_TN)
    pkv, (w_up_m_b, w_up_a_b, w_out_b) = _inproj_tail(h, w_t, tile_rows(tail_pieces, INPROJ_TAIL_TN),
                                                      (w_up_m, w_up_a, w_out), tn=INPROJ_TAIL_TN)

    xc, q, k = _conv_qk(p, mx_col, conv_w, conv_b.reshape(1, mw), w_q.astype(BF16), w_k.astype(BF16), bsz, seq)
    bias_m = jnp.concatenate([b_i, b_f]).reshape(2 * M_HEADS, 1)
    y_m = _mlstm(p, mx_col, xc, q, k, ps, bias_m, norm_g.reshape(1, mw), skip.reshape(1, mw), bsz, seq)

    half = A_CMP_STRIDE * dh
    pe = jnp.stack([pe_k.reshape(2, half), pe_v.reshape(2, half)])
    w1 = jnp.stack([w1_k, w1_v]).astype(BF16)
    w2 = jnp.stack([w2_k, w2_v]).astype(BF16)
    kvc = _compress(pkv, 0, pe, w1, w2, bsz, seq)
    vct = jnp.swapaxes(kvc[1], 2, 3)
    y_a = _nsa(p, pkv, ps, kvc, vct, q_col, z_col, bsz, seq)

    out = _merge_out(y_m, y_a, w_up_m_b, w_up_a_b, p, gm_col, ga_col, w_out_b, post_g.reshape(1, d), x2d)
    return out.reshape(bsz, seq, d)


def kernel(x, pre_norm_g, w_in, m_conv_w, m_conv_b, m_w_q, m_w_k, m_b_i, m_b_f, m_norm_g, m_skip, a_pe_k, a_w1_k, a_w2_k, a_pe_v, a_w1_v, a_w2_v, w_up_m, w_up_a, w_out, post_norm_g):
    depth = w_in.shape[0]
    for l in range(depth):
        x = _layer(x, pre_norm_g[l], w_in[l], m_conv_w[l], m_conv_b[l], m_w_q[l], m_w_k[l], m_b_i[l], m_b_f[l],
                   m_norm_g[l], m_skip[l], a_pe_k[l], a_w1_k[l], a_w2_k[l], a_pe_v[l], a_w1_v[l], a_w2_v[l],
                   w_up_m[l], w_up_a[l], w_out[l], post_norm_g[l])
    return x
```

```python
import functools

import jax
import jax.numpy as jnp
from jax import lax
from jax.experimental import pallas as pl
from jax.experimental.pallas import tpu as pltpu

F32 = jnp.float32
BF16 = jnp.bfloat16

M_HEADS = 4
M_CONV = 4
M_CHUNK = 128
A_HEADS = 16
A_KV_HEADS = 4
A_GROUP = A_HEADS // A_KV_HEADS
A_HEAD_DIM = 64
A_CMP_BLOCK = 32
A_CMP_STRIDE = 16
A_SEL_BLOCK = 64
A_SEL_TOPK = 8
A_WINDOW = 512
Q_BLOCK = 128
SEL_KEYS_PER_STEP = 256
EXCLUDE = 1e4
MAX_FAST_BOUND = 40.0
SEL_BLOCK_SHIFT = A_SEL_BLOCK.bit_length() - 1
KEY_NORM_ROWS = 512
BOUND_SLACK = 1.01
INPROJ_TN = 1024
INPROJ_TAIL_TN = 512
RMS_EPS = 1e-6
LN_EPS = 1e-6
NEG_INF = -1e30
POS_INF = 1e30

LANES = 128
SUBLANES = 8
BF16_SUBLANES = 16
LOG2_E = 1.4426950408889634
VMEM_LIMIT_BYTES = 48 * 1024 * 1024
INPROJ_VMEM_LIMIT_BYTES = 56 * 1024 * 1024


def _nt_dot(a, b):
    return lax.dot_general(a, b, (((1,), (1,)), ((), ())), preferred_element_type=F32)


def _dot(a, b):
    return jnp.dot(a, b, preferred_element_type=F32)


def _sigmoid(x):
    return 1.0 / (1.0 + jnp.exp(-x))


def _params(*sem):
    return pltpu.CompilerParams(dimension_semantics=sem, vmem_limit_bytes=VMEM_LIMIT_BYTES)


def _inproj_kernel(rows_ref, x_ref, g_ref, wt_ref, *rest):
    del rows_ref
    o_ref, ost_ref, h_ref, h_scr = rest[-4:]
    small_refs = rest[:-4]

    @pl.when(pl.program_id(1) == 0)
    def _():
        xf = x_ref[...]
        ms = jnp.mean(xf * xf, axis=-1, keepdims=True)
        hb = (xf * lax.rsqrt(ms + RMS_EPS) * g_ref[...]).astype(BF16)
        h_scr[...] = hb
        h_ref[...] = hb
        rows = [r[...] for r in small_refs]
        used = sum(r.shape[0] for r in rows)
        rows.append(jnp.zeros((ost_ref.shape[0] - used, xf.shape[1]), F32))
        ost_ref[...] = _nt_dot(jnp.concatenate(rows, axis=0).astype(BF16), hb)

    o_ref[...] = _nt_dot(h_scr[...], wt_ref[...].astype(BF16)).astype(o_ref.dtype)


def _inproj(x2d, g, w_t, row_starts, small_pieces, tm=1024, tn=512):
    t, d = x2d.shape
    n = row_starts.shape[0] * tn
    ns = LANES
    rows_of = lambda lo, hi: pl.BlockSpec((pl.Element(hi - lo), pl.Element(d)), lambda i, j, rows: (lo, 0))
    grid_spec = pltpu.PrefetchScalarGridSpec(
        num_scalar_prefetch=1,
        grid=(t // tm, n // tn),
        in_specs=[
            pl.BlockSpec((tm, d), lambda i, j, rows: (i, 0)),
            pl.BlockSpec((1, d), lambda i, j, rows: (0, 0)),
            pl.BlockSpec((pl.Element(tn), pl.Element(d)), lambda i, j, rows: (rows[j] * SUBLANES, 0)),
        ] + [rows_of(lo, hi) for lo, hi in small_pieces],
        out_specs=[
            pl.BlockSpec((tm, tn), lambda i, j, rows: (i, j)),
            pl.BlockSpec((ns, tm), lambda i, j, rows: (0, i)),
            pl.BlockSpec((tm, d), lambda i, j, rows: (i, 0)),
        ],
        scratch_shapes=[pltpu.VMEM((tm, d), BF16)],
    )
    return pl.pallas_call(
        _inproj_kernel,
        grid_spec=grid_spec,
        out_shape=[jax.ShapeDtypeStruct((t, n), BF16), jax.ShapeDtypeStruct((ns, t), F32),
                   jax.ShapeDtypeStruct((t, d), BF16)],
        compiler_params=pltpu.CompilerParams(dimension_semantics=("parallel", "arbitrary"),
                                             vmem_limit_bytes=INPROJ_VMEM_LIMIT_BYTES),
        name="inproj",
    )(row_starts, x2d, g, w_t, *([w_t] * len(small_pieces)))


def _inproj_tail_kernel(rows_ref, h_ref, wt_ref, o_ref):
    del rows_ref
    o_ref[...] = _nt_dot(h_ref[...], wt_ref[...].astype(BF16)).astype(o_ref.dtype)


def _inproj_tail(h, w_t, row_starts, tm=2048, tn=512):
    t, d = h.shape
    nj = row_starts.shape[0]
    grid_spec = pltpu.PrefetchScalarGridSpec(
        num_scalar_prefetch=1,
        grid=(t // tm, nj),
        in_specs=[
            pl.BlockSpec((tm, d), lambda i, j, rows: (i, 0)),
            pl.BlockSpec((pl.Element(tn), pl.Element(d)), lambda i, j, rows: (rows[j] * SUBLANES, 0)),
        ],
        out_specs=pl.BlockSpec((tm, tn), lambda i, j, rows: (i, j)),
    )
    return pl.pallas_call(
        _inproj_tail_kernel,
        grid_spec=grid_spec,
        out_shape=jax.ShapeDtypeStruct((t, nj * tn), BF16),
        compiler_params=_params("parallel", "arbitrary"),
        name="inproj_tail",
    )(row_starts, h, w_t)


def _conv_qk_kernel(x_ref, cw_ref, cb_ref, wq_ref, wk_ref, xc_ref, q_ref, k_ref, *, rows):
    s, c = x_ref.shape
    cw = cw_ref[...]
    cb = cb_ref[...]
    wq = wq_ref[...]
    wk = wk_ref[...]
    k_scale = c ** -0.5
    first_row = lax.broadcasted_iota(jnp.int32, (rows, c), 0) == 0
    carry = [jnp.zeros((1, c), F32) for _ in range(M_CONV - 1)]
    for r in range(s // rows):
        xr = x_ref[r * rows:(r + 1) * rows, :].astype(F32)
        a = cw[0:1, :] * xr
        last = []
        for kk in range(1, M_CONV):
            last.append(a[rows - 1:rows, :])
            a = jnp.where(first_row, carry[kk - 1], pltpu.roll(a, 1, 0)) + cw[kk:kk + 1, :] * xr
        carry = last
        acc = a + cb
        xcb = (acc * _sigmoid(acc)).astype(BF16)
        xc_ref[r * rows:(r + 1) * rows, :] = xcb
        q_ref[r * rows:(r + 1) * rows, :] = _dot(xcb, wq).astype(BF16)
        k_ref[r * rows:(r + 1) * rows, :] = (_dot(xcb, wk) * k_scale).astype(BF16)


def _conv_qk(p, mx_col, conv_w, conv_b, wq, wk, bsz, seq):
    t = p.shape[0]
    heads, hd = wq.shape[0], wq.shape[1]
    width = heads * hd
    blk = pl.BlockSpec((seq, hd), lambda b, h: (b, h))
    out = jax.ShapeDtypeStruct((t, width), BF16)
    return pl.pallas_call(
        functools.partial(_conv_qk_kernel, rows=256),
        grid=(bsz, heads),
        in_specs=[
            pl.BlockSpec((seq, hd), lambda b, h: (b, mx_col // hd + h)),
            pl.BlockSpec((M_CONV, hd), lambda b, h: (0, h)),
            pl.BlockSpec((1, hd), lambda b, h: (0, h)),
            pl.BlockSpec((None, hd, hd), lambda b, h: (h, 0, 0)),
            pl.BlockSpec((None, hd, hd), lambda b, h: (h, 0, 0)),
        ],
        out_specs=[blk, blk, blk],
        out_shape=[out, out, out],
        compiler_params=_params("parallel", "parallel"),
        name="conv_qk",
    )(p, conv_w, conv_b, wq, wk)


def _log_sigmoid(x):
    return jnp.minimum(x, 0.0) - jnp.log1p(jnp.exp(-jnp.abs(x)))


def _mlstm_kernel(bias_ref, q_ref, k_ref, v_ref, o_ref, z_ref, xc_ref, g_ref, ng_ref, sk_ref,
                  y_ref, c_scr, n_scr, m_scr):
    @pl.when(pl.program_id(1) == 0)
    def _():
        c_scr[...] = jnp.zeros_like(c_scr)
        n_scr[...] = jnp.zeros_like(n_scr)
        m_scr[...] = jnp.zeros_like(m_scr)

    ln = q_ref.shape[0]
    heads, hd = c_scr.shape[0], c_scr.shape[1]
    gates = g_ref[...] + bias_ref[...]
    fb_all = _log_sigmoid(gates[heads:2 * heads, :])
    row = lax.broadcasted_iota(jnp.int32, (ln, ln), 0)
    col = lax.broadcasted_iota(jnp.int32, (ln, ln), 1)
    tril = col <= row
    eye = col == row
    for hh in range(heads):
        sl = slice(hh * hd, (hh + 1) * hd)
        ib = gates[hh:hh + 1, :]
        fb = fb_all[hh:hh + 1, :]
        bcum_c = jnp.sum(jnp.where(tril, fb, 0.0), axis=1, keepdims=True)
        bcum_r = jnp.sum(jnp.where(eye, bcum_c, 0.0), axis=0, keepdims=True)
        g_tot = jnp.sum(fb, axis=1, keepdims=True)
        e_r = ib - bcum_r
        m_st = m_scr[hh]
        d_intra = jnp.where(tril, bcum_c + e_r, NEG_INF)
        d_inter = bcum_c + m_st
        m_t = jnp.maximum(d_inter, jnp.max(d_intra, axis=1, keepdims=True))
        w_intra = jnp.exp(d_intra - m_t)
        w_inter = jnp.exp(d_inter - m_t)

        qb = q_ref[:, sl]
        kb = k_ref[:, sl]
        vb = v_ref[:, sl]
        c_st = c_scr[hh]
        n_st = n_scr[hh]
        sc = _nt_dot(qb, kb) * w_intra
        num = _dot(sc.astype(BF16), vb) + w_inter * _dot(qb, c_st.astype(BF16))
        n_rows = jnp.broadcast_to(n_st, (SUBLANES, hd)).astype(BF16)
        qn = _nt_dot(qb, n_rows)[:, 0:1]
        den = jnp.sum(sc, axis=1, keepdims=True) + w_inter * qn
        h = num / jnp.maximum(jnp.abs(den), jnp.exp(-m_t))

        d_state = g_tot + e_r
        m_new = jnp.maximum(g_tot + m_st, jnp.max(d_state, axis=1, keepdims=True))
        w_s_r = jnp.exp(d_state - m_new)
        w_s_c = jnp.sum(jnp.where(eye, w_s_r, 0.0), axis=1, keepdims=True)
        decay = jnp.exp(g_tot + m_st - m_new)
        kw = kb * w_s_c.astype(BF16)
        kv = lax.dot_general(kw, vb, (((0,), (0,)), ((), ())), preferred_element_type=F32)
        c_scr[hh] = decay * c_st + kv
        w_rows = jnp.broadcast_to(w_s_r, (SUBLANES, ln)).astype(BF16)
        n_scr[hh] = decay * n_st + _dot(w_rows, kb)[0:1, :]
        m_scr[hh] = m_new

        hg = h * _sigmoid(o_ref[:, sl].astype(F32))
        mu = jnp.mean(hg, axis=-1, keepdims=True)
        hc = hg - mu
        var = jnp.mean(hc * hc, axis=-1, keepdims=True)
        hn = hc * lax.rsqrt(var + LN_EPS)
        z = z_ref[:, sl].astype(F32)
        y = (hn * ng_ref[:, sl] + sk_ref[:, sl] * xc_ref[:, sl].astype(F32)) * (z * _sigmoid(z))
        y_ref[:, sl] = y.astype(y_ref.dtype)


def _mlstm(p, mx_col, xc, q, k, gates, bias, norm_g, skip, bsz, seq):
    t, width = q.shape
    pc = mx_col // width
    heads = M_HEADS
    hd = width // heads
    ln = M_CHUNK
    nc = seq // ln
    rows = lambda b, c: (b * nc + c, 0)
    vec = pl.BlockSpec((1, width), lambda b, c: (0, 0))
    return pl.pallas_call(
        _mlstm_kernel,
        grid=(bsz, nc),
        in_specs=[
            pl.BlockSpec((2 * heads, 1), lambda b, c: (0, 0)),
            pl.BlockSpec((ln, width), rows),
            pl.BlockSpec((ln, width), rows),
            pl.BlockSpec((ln, width), lambda b, c: (b * nc + c, pc + 1)),
            pl.BlockSpec((ln, width), lambda b, c: (b * nc + c, pc + 2)),
            pl.BlockSpec((ln, width), lambda b, c: (b * nc + c, pc + 3)),
            pl.BlockSpec((ln, width), rows),
            pl.BlockSpec((2 * heads, ln), lambda b, c: (0, b * nc + c)),
            vec, vec,
        ],
        out_specs=pl.BlockSpec((ln, width), rows),
        out_shape=jax.ShapeDtypeStruct((t, width), BF16),
        scratch_shapes=[pltpu.VMEM((heads, hd, hd), F32), pltpu.VMEM((heads, 1, hd), F32),
                        pltpu.VMEM((heads, 1, 1), F32)],
        compiler_params=_params("parallel", "arbitrary"),
        name="mlstm",
    )(bias, q, k, p, p, p, xc, gates, norm_g, skip)


def _compress_kernel(x_ref, pe_ref, w1_ref, w2_ref, o_ref, xf_scr):
    ng, n_sub, dh = o_ref.shape
    half = A_CMP_STRIDE * dh
    n_chunk = xf_scr.shape[0]
    for c in range(n_chunk):
        xf_scr[c] = x_ref[:, c * LANES:(c + 1) * LANES].astype(F32)
    xs = [jnp.concatenate([xf_scr[c, pl.ds(r, n_sub, stride=A_CMP_STRIDE), :] for c in range(n_chunk)], axis=1)
          for r in range(A_CMP_STRIDE)]
    pe = pe_ref[...]
    rid = lax.broadcasted_iota(jnp.int32, (n_sub, dh), 0)
    for g in range(ng):
        t = jnp.concatenate([x[:, g * dh:(g + 1) * dh] for x in xs], axis=1)
        za = _dot((t + pe[0:1, :]).astype(BF16), w1_ref[0:half, :])
        zb = _dot((t + pe[1:2, :]).astype(BF16), w1_ref[half:2 * half, :])
        pre = za + pltpu.roll(zb, n_sub - 1, 0)
        hid = (pre * _sigmoid(pre)).astype(BF16)
        out = _dot(hid, w2_ref[...])
        o_ref[g] = jnp.where(rid < n_sub - 1, out, 0.0).astype(o_ref.dtype)


def _compress(p, kc_col, pe, w1, w2, bsz, seq):
    g = A_KV_HEADS
    dh = A_HEAD_DIM
    kvw = g * dh
    n_sub = seq // A_CMP_STRIDE
    half = A_CMP_STRIDE * dh
    hid = w1.shape[-1]
    return pl.pallas_call(
        _compress_kernel,
        grid=(2, bsz),
        in_specs=[
            pl.BlockSpec((seq, kvw), lambda i, b: (b, kc_col // kvw + i)),
            pl.BlockSpec((None, 2, half), lambda i, b: (i, 0, 0)),
            pl.BlockSpec((None, 2 * half, hid), lambda i, b: (i, 0, 0)),
            pl.BlockSpec((None, hid, dh), lambda i, b: (i, 0, 0)),
        ],
        out_specs=pl.BlockSpec((None, None, g, n_sub, dh), lambda i, b: (i, b, 0, 0, 0)),
        out_shape=jax.ShapeDtypeStruct((2, bsz, g, n_sub, dh), BF16),
        scratch_shapes=[pltpu.VMEM((kvw // LANES, seq, LANES), F32)],
        compiler_params=_params("parallel", "parallel"),
        name="compress",
    )(p, pe, w1, w2)


def _mask_heads(x, ok, fill, tq):
    n = x.shape[1] // tq
    return jnp.concatenate([jnp.where(ok, x[:, h * tq:(h + 1) * tq], fill) for h in range(n)], axis=1)


def _with_ones_rows(vt):
    return jnp.concatenate([vt, jnp.ones((BF16_SUBLANES, vt.shape[1]), vt.dtype)], axis=0)


def _nsa_kernel(q_ref, kc_ref, vct_ref, ks_ref, vs_ref, kw_ref, vw_ref, ps_ref, z_ref, o_ref,
                sel_ref, vst_ref, vwt_ref, kn_ref, osw_ref):
    qi = pl.program_id(1)
    tq = q_ref.shape[0]
    dh = A_HEAD_DIM
    hpg = A_GROUP
    ng = A_KV_HEADS
    r = hpg * tq
    n_cmp = kc_ref.shape[1]
    tk = vwt_ref.shape[2]
    kt = SEL_KEYS_PER_STEP
    tiles_per_iter = kt // tk
    seq = ks_ref.shape[0]
    n_sel = seq // A_SEL_BLOCK
    t0 = qi * tq
    da = dh + BF16_SUBLANES

    @pl.when(qi == 0)
    def _():
        for j in range(seq // tk):
            vst_ref[j] = vs_ref[j * tk:(j + 1) * tk, :].astype(F32).T.astype(BF16)
            vwt_ref[j] = vw_ref[j * tk:(j + 1) * tk, :].astype(F32).T.astype(BF16)
        rows = KEY_NORM_ROWS
        dim_group = lax.broadcasted_iota(jnp.int32, (ng * dh, LANES), 0) // dh
        member = jnp.where(dim_group == lax.broadcasted_iota(jnp.int32, (ng * dh, LANES), 1), 1.0, 0.0).astype(BF16)
        for which, k_ref in enumerate((ks_ref, kw_ref)):
            best = jnp.zeros((1, LANES), F32)
            for c in range(seq // rows):
                kb = k_ref[c * rows:(c + 1) * rows, :]
                best = jnp.maximum(best, jnp.max(_dot(kb * kb, member), axis=0, keepdims=True))
            for g in range(ng):
                kn_ref[which * ng + g] = best[:, g:g + 1]

    qt = (q_ref[...].astype(F32) * (dh ** -0.5 * LOG2_E)).T.astype(BF16)
    q4t = [jnp.concatenate([qt[(g * hpg + h) * dh:(g * hpg + h + 1) * dh, :] for h in range(hpg)], axis=1)
           for g in range(ng)]

    cn = lax.broadcasted_iota(jnp.int32, (n_cmp, tq), 0)
    ct = t0 + lax.broadcasted_iota(jnp.int32, (n_cmp, tq), 1)
    valid = (cn * A_CMP_STRIDE + (A_CMP_BLOCK - 1)) <= ct
    jn = lax.broadcasted_iota(jnp.int32, (n_sel, n_cmp), 0)
    nn = lax.broadcasted_iota(jnp.int32, (n_sel, n_cmp), 1)
    ov = ((nn * A_CMP_STRIDE < (jn + 1) * A_SEL_BLOCK)
          & (nn * A_CMP_STRIDE + (A_CMP_BLOCK - 1) >= jn * A_SEL_BLOCK))
    ov_t = jnp.where(ov, 1.0, 0.0).astype(BF16)
    jr = lax.broadcasted_iota(jnp.int32, (n_sel, tq), 0)
    tl = t0 + lax.broadcasted_iota(jnp.int32, (n_sel, tq), 1)
    cur = lax.shift_right_logical(tl, SEL_BLOCK_SHIFT)
    forced = (jr == 0) | (jr == cur) | (jr == cur - 1)
    def compressed_and_select():
        for g in range(ng):
            s_m = _mask_heads(_dot(kc_ref[g], q4t[g]), valid, NEG_INF, tq)
            mx = jnp.max(s_m, axis=0, keepdims=True)
            e = _mask_heads(jnp.exp2(s_m - mx), valid, 0.0, tq)
            lsum = jnp.sum(e, axis=0, keepdims=True)
            p_c = e * (1.0 / jnp.where(lsum > 0.0, lsum, 1.0))
            osw_ref[2, g] = _dot(vct_ref[g], p_c.astype(BF16))
            p_sum = p_c[:, 0:tq]
            for h in range(1, hpg):
                p_sum = p_sum + p_c[:, h * tq:(h + 1) * tq]
            p_hi = p_sum.astype(BF16)
            r1 = p_sum - p_hi.astype(F32)
            p_mid = r1.astype(BF16)
            p_lo = (r1 - p_mid.astype(F32)).astype(BF16)
            imp_t = _dot(ov_t, p_hi) + _dot(ov_t, p_mid) + _dot(ov_t, p_lo)
            score = jnp.where(forced, POS_INF, jnp.where(jr <= cur, imp_t, NEG_INF))
            rank = jnp.zeros((n_sel, tq), F32)
            for jp in range(n_sel):
                sj = score[jp:jp + 1, :]
                beats = (sj > score) | ((sj == score) & (jr > jp))
                rank = rank + jnp.where(beats, 1.0, 0.0)
            sel_ref[g] = jnp.where(rank < float(min(A_SEL_TOPK, n_sel)), 1.0, 0.0)

    qn = [jnp.sqrt(jnp.sum(jnp.square(q4t[g].astype(F32)), axis=0, keepdims=True)) for g in range(ng)]
    bound_s = [qn[g] * jnp.sqrt(kn_ref[g]) * BOUND_SLACK for g in range(ng)]
    bound_w = [qn[g] * jnp.sqrt(kn_ref[ng + g]) * BOUND_SLACK for g in range(ng)]
    worst = bound_s[0]
    for b in bound_s[1:] + bound_w:
        worst = jnp.maximum(worst, b)
    fast = jnp.max(worst) <= MAX_FAST_BOUND

    n_wb = -(-A_WINDOW // tk)
    wk = (n_wb + 1) * tk
    j0 = jnp.maximum(qi - n_wb, 0)
    off_w = pl.multiple_of(j0 * tk, tk)
    c_abs = off_w + lax.broadcasted_iota(jnp.int32, (wk, tq), 0)
    t_abs_w = t0 + lax.broadcasted_iota(jnp.int32, (wk, tq), 1)
    ok_w = (c_abs <= t_abs_w) & (c_abs > t_abs_w - A_WINDOW)
    blocks_per_iter = kt // A_SEL_BLOCK

    def normalised(acc):
        return acc[0:dh, :] * (1.0 / acc[dh:dh + 1, :])

    def value_tiles(v_ref, first, count, g):
        vt = jnp.concatenate([v_ref[first + i, g * dh:(g + 1) * dh, :] for i in range(count)], axis=1)
        return _with_ones_rows(vt)

    def emit_output():
        gs = _sigmoid(ps_ref[...])
        g0 = 2 * M_HEADS
        outs = []
        for g in range(ng):
            o_s = osw_ref[0, g]
            o_w = osw_ref[1, g]
            o_c = osw_ref[2, g]
            for h in range(hpg):
                sl = slice(h * tq, (h + 1) * tq)
                gr = g0 + 3 * (g * hpg + h)
                outs.append(gs[gr:gr + 1, :] * o_c[:, sl] + gs[gr + 1:gr + 2, :] * o_s[:, sl]
                            + gs[gr + 2:gr + 3, :] * o_w[:, sl])
        o_all = jnp.concatenate(outs, axis=0).T
        z = z_ref[...].astype(F32)
        o_ref[...] = (o_all * (z * _sigmoid(z))).astype(o_ref.dtype)

    @pl.when(fast)
    def _():
        kcol = lax.broadcasted_iota(jnp.int32, (kt, dh), 1)
        krow = lax.broadcasted_iota(jnp.int32, (kt, dh), 0)
        kblock = lax.shift_right_logical(krow, SEL_BLOCK_SHIFT)
        aux = jnp.where((kcol == 0) | (kcol == SUBLANES + kblock), 1.0, 0.0).astype(BF16)
        aux_w = jnp.where(lax.broadcasted_iota(jnp.int32, (wk, dh), 1) == 0, 1.0, 0.0).astype(BF16)
        aux_d = jnp.where(lax.broadcasted_iota(jnp.int32, (tq, dh), 1) == 0, 1.0, 0.0).astype(BF16)
        zrows = jnp.zeros((LANES - dh - BF16_SUBLANES, r), BF16)

        first_row = lax.broadcasted_iota(jnp.int32, (SUBLANES, r), 0) == 0

        def query_aug(g, shift, block_bias):
            top = jnp.where(first_row, -shift, 0.0)
            if block_bias is None:
                low = jnp.zeros((SUBLANES, r), F32)
            else:
                low = jnp.concatenate([block_bias] * hpg, axis=1)
                if low.shape[0] < SUBLANES:
                    low = jnp.concatenate([low, jnp.zeros((SUBLANES - low.shape[0], r), F32)], axis=0)
            return jnp.concatenate([q4t[g], jnp.concatenate([top, low], axis=0).astype(BF16), zrows], axis=0)

        def step(j, accs):
            k_all = ks_ref[j * kt:(j + 1) * kt, :]
            out = []
            for g in range(ng):
                k_aug = jnp.concatenate([k_all[:, g * dh:(g + 1) * dh], aux], axis=1)
                bias = sel_ref[g, j * blocks_per_iter:(j + 1) * blocks_per_iter, :]
                p = jnp.exp2(_dot(k_aug, query_aug(g, bound_s[g], bias))).astype(BF16)
                out.append(accs[g] + _dot(value_tiles(vst_ref, j * tiles_per_iter, tiles_per_iter, g), p))
            return tuple(out)

        def finish(accs):
            kd_all = ks_ref[pl.ds(pl.multiple_of(t0, tq), tq), :]
            causal = (lax.broadcasted_iota(jnp.int32, (tq, tq), 0)
                      <= lax.broadcasted_iota(jnp.int32, (tq, tq), 1))
            kw_all = kw_ref[pl.ds(off_w, wk), :]
            for g in range(ng):
                kd_aug = jnp.concatenate([kd_all[:, g * dh:(g + 1) * dh], aux_d], axis=1)
                s_d = _mask_heads(_dot(kd_aug, query_aug(g, bound_s[g], None)), causal, -EXCLUDE, tq)
                acc = accs[g] + _dot(value_tiles(vst_ref, qi, 1, g), jnp.exp2(s_d).astype(BF16))
                osw_ref[0, g] = normalised(acc)
                kw_aug = jnp.concatenate([kw_all[:, g * dh:(g + 1) * dh], aux_w], axis=1)
                s_w = _mask_heads(_dot(kw_aug, query_aug(g, bound_w[g], None)), ok_w, -EXCLUDE, tq)
                p_w = jnp.exp2(s_w).astype(BF16)
                acc_w = _dot(value_tiles(vwt_ref, j0, 1, g), p_w[0:tk, :])
                for i in range(1, n_wb + 1):
                    acc_w = acc_w + _dot(value_tiles(vwt_ref, j0 + i, 1, g), p_w[i * tk:(i + 1) * tk, :])
                osw_ref[1, g] = normalised(acc_w)

        steps_needed = (t0 + kt - 1) // kt
        for n_steps in range(seq // kt + 1):
            @pl.when(steps_needed == n_steps)
            def _(n_steps=n_steps):
                compressed_and_select()
                before_diag = jr < qi * (tq // A_SEL_BLOCK)
                for g in range(ng):
                    sel_ref[g] = jnp.where((sel_ref[g] > 0.5) & before_diag, 0.0, -EXCLUDE)
                accs = tuple(jnp.zeros((da, r), F32) for _ in range(ng))
                for j in range(n_steps):
                    accs = step(j, accs)
                finish(accs)
                emit_output()

    @pl.when(jnp.logical_not(fast))
    def _():
        compressed_and_select()
        zpad = jnp.zeros((dh, r), BF16)
        q_pad = [jnp.concatenate([q4t[g] if gg == g else zpad for gg in range(ng)], axis=0) for g in range(ng)]
        sub_s = lax.broadcasted_iota(jnp.int32, (kt, tq), 0)
        t_abs_s = t0 + lax.broadcasted_iota(jnp.int32, (kt, tq), 1)

        def body(j, carry):
            off = pl.multiple_of(j * kt, kt)
            k_all = ks_ref[pl.ds(off, kt), :]
            causal = (off + sub_s) <= t_abs_s
            out = []
            for g in range(ng):
                m, acc = carry[g]
                s = _dot(k_all, q_pad[g])
                flags = jnp.concatenate(
                    [jnp.broadcast_to(sel_ref[g, pl.ds(j * blocks_per_iter + i, 1), :], (A_SEL_BLOCK, tq))
                     for i in range(blocks_per_iter)], axis=0)
                s = _mask_heads(s, (flags > 0.5) & causal, NEG_INF, tq)
                m_new = jnp.maximum(m, jnp.max(s, axis=0, keepdims=True))
                p = jnp.exp2(s - m_new).astype(BF16)
                vt = value_tiles(vst_ref, j * tiles_per_iter, tiles_per_iter, g)
                out.append((m_new, jnp.exp2(m - m_new) * acc + _dot(vt, p)))
            return tuple(out)

        init = tuple((jnp.full((1, r), NEG_INF, F32), jnp.zeros((da, r), F32)) for _ in range(ng))
        state = lax.fori_loop(0, (t0 + tq - 1) // kt + 1, body, init)
        kw_all = kw_ref[pl.ds(off_w, wk), :]
        for g in range(ng):
            osw_ref[0, g] = normalised(state[g][1])
            s_w = _mask_heads(_dot(kw_all, q_pad[g]), ok_w, NEG_INF, tq)
            p_w = jnp.exp2(s_w - jnp.max(s_w, axis=0, keepdims=True)).astype(BF16)
            acc_w = _dot(value_tiles(vwt_ref, j0, 1, g), p_w[0:tk, :])
            for i in range(1, n_wb + 1):
                acc_w = acc_w + _dot(value_tiles(vwt_ref, j0 + i, 1, g), p_w[i * tk:(i + 1) * tk, :])
            osw_ref[1, g] = normalised(acc_w)
        emit_output()


def _nsa(p, pkv, ps, kvc, vct, q_col, z_col, bsz, seq):
    t = p.shape[0]
    g = A_KV_HEADS
    dh = A_HEAD_DIM
    aw = A_HEADS * dh
    kvw = g * dh
    tq = Q_BLOCK
    nq = seq // tq
    n_cmp = kvc.shape[3]
    tk = LANES
    rows = lambda b, qi: b * nq + qi
    kv_piece = lambda i: pl.BlockSpec((seq, kvw), lambda b, qi: (b, i))
    return pl.pallas_call(
        _nsa_kernel,
        grid=(bsz, nq),
        in_specs=[
            pl.BlockSpec((tq, aw), lambda b, qi: (rows(b, qi), q_col // aw)),
            pl.BlockSpec((None, None, g, n_cmp, dh), lambda b, qi: (0, b, 0, 0, 0)),
            pl.BlockSpec((None, g, dh, n_cmp), lambda b, qi: (b, 0, 0, 0)),
            kv_piece(2), kv_piece(3), kv_piece(4), kv_piece(5),
            pl.BlockSpec((LANES, tq), lambda b, qi: (0, rows(b, qi))),
            pl.BlockSpec((tq, aw), lambda b, qi: (rows(b, qi), z_col // aw)),
        ],
        out_specs=pl.BlockSpec((tq, aw), lambda b, qi: (rows(b, qi), 0)),
        out_shape=jax.ShapeDtypeStruct((t, aw), BF16),
        scratch_shapes=[pltpu.VMEM((g, seq // A_SEL_BLOCK, tq), F32),
                        pltpu.VMEM((seq // tk, kvw, tk), BF16),
                        pltpu.VMEM((seq // tk, kvw, tk), BF16),
                        pltpu.VMEM((2 * g, 1, 1), F32),
                        pltpu.VMEM((3, g, dh, A_GROUP * tq), F32)],
        compiler_params=_params("parallel", "arbitrary"),
        name="nsa",
    )(p, kvc, vct, pkv, pkv, pkv, pkv, ps, p)


def _merge_out_kernel(ym_ref, ya_ref, wm_ref, wa_ref, gm_ref, ga_ref, wo_ref, g_ref, x_ref, o_ref):
    um = _dot(ym_ref[...], wm_ref[...])
    ua = _dot(ya_ref[...], wa_ref[...])
    mg = _sigmoid(gm_ref[...].astype(F32)) * um + _sigmoid(ga_ref[...].astype(F32)) * ua
    out = _dot(mg.astype(BF16), wo_ref[...])
    ms = jnp.mean(out * out, axis=-1, keepdims=True)
    o_ref[...] = x_ref[...] + out * lax.rsqrt(ms + RMS_EPS) * g_ref[...]


def _merge_out(ym, ya, wm, wa, p, gm_col, ga_col, wo, g, x2d, tm=256):
    t, d = x2d.shape
    kdim = ym.shape[1]
    resident = lambda shape: pl.BlockSpec(shape, lambda i: (0, 0), pipeline_mode=pl.Buffered(1))
    return pl.pallas_call(
        _merge_out_kernel,
        grid=(t // tm,),
        in_specs=[
            pl.BlockSpec((tm, kdim), lambda i: (i, 0)),
            pl.BlockSpec((tm, kdim), lambda i: (i, 0)),
            resident((kdim, d)),
            resident((kdim, d)),
            pl.BlockSpec((tm, d), lambda i: (i, gm_col // d)),
            pl.BlockSpec((tm, d), lambda i: (i, ga_col // d)),
            resident((d, d)),
            pl.BlockSpec((1, d), lambda i: (0, 0)),
            pl.BlockSpec((tm, d), lambda i: (i, 0)),
        ],
        out_specs=pl.BlockSpec((tm, d), lambda i: (i, 0)),
        out_shape=jax.ShapeDtypeStruct((t, d), F32),
        compiler_params=_params("parallel"),
        name="merge_out",
    )(ym, ya, wm, wa, p, p, wo, g, x2d)


def _layer(x, pre_g, w_in, conv_w, conv_b, w_q, w_k, b_i, b_f, norm_g, skip,
           pe_k, w1_k, w2_k, pe_v, w1_v, w2_v, w_up_m, w_up_a, w_out, post_g):
    bsz, seq, d = x.shape
    t = bsz * seq
    mw = w_up_m.shape[0]
    aw = w_up_a.shape[0]
    g = A_KV_HEADS
    dh = A_HEAD_DIM
    kvw = g * dh
    n_gate = 3 * A_HEADS

    o_i = 4 * mw
    o_f = o_i + M_HEADS
    o_q = o_f + M_HEADS
    o_g = o_q + aw + 6 * kvw
    o_z = o_g + n_gate
    o_m = o_z + aw
    o_kv = o_q + aw
    main_pieces = ((o_m, w_in.shape[1]), (0, o_i), (o_q, o_kv), (o_z, o_m))
    tail_pieces = ((o_kv, o_g),)
    w_t = w_in.T
    tile_rows = lambda pieces, tn: jnp.asarray(
        [r // SUBLANES for lo, hi in pieces for r in range(lo, hi, tn)], jnp.int32)
    row_starts = tile_rows(main_pieces, INPROJ_TN)
    small_pieces = ((o_i, o_q), (o_g, o_z))
    gm_col = 0
    ga_col = d
    mx_col = 2 * d
    q_col = mx_col + 4 * mw
    z_col = q_col + aw

    x2d = x.reshape(t, d)
    p, ps, h = _inproj(x2d, pre_g.reshape(1, d), w_t, row_starts, small_pieces, tn=INPROJ_TN)
    pkv = _inproj_tail(h, w_t, tile_rows(tail_pieces, INPROJ_TAIL_TN), tn=INPROJ_TAIL_TN)

    xc, q, k = _conv_qk(p, mx_col, conv_w, conv_b.reshape(1, mw), w_q.astype(BF16), w_k.astype(BF16), bsz, seq)
    bias_m = jnp.concatenate([b_i, b_f]).reshape(2 * M_HEADS, 1)
    y_m = _mlstm(p, mx_col, xc, q, k, ps, bias_m, norm_g.reshape(1, mw), skip.reshape(1, mw), bsz, seq)

    half = A_CMP_STRIDE * dh
    pe = jnp.stack([pe_k.reshape(2, half), pe_v.reshape(2, half)])
    w1 = jnp.stack([w1_k, w1_v]).astype(BF16)
    w2 = jnp.stack([w2_k, w2_v]).astype(BF16)
    kvc = _compress(pkv, 0, pe, w1, w2, bsz, seq)
    vct = jnp.swapaxes(kvc[1], 2, 3)
    y_a = _nsa(p, pkv, ps, kvc, vct, q_col, z_col, bsz, seq)

    out = _merge_out(y_m, y_a, w_up_m.astype(BF16), w_up_a.astype(BF16), p, gm_col, ga_col,
                     w_out.astype(BF16), post_g.reshape(1, d), x2d)
    return out.reshape(bsz, seq, d)


def kernel(x, pre_norm_g, w_in, m_conv_w, m_conv_b, m_w_q, m_w_k, m_b_i, m_b_f, m_norm_g, m_skip, a_pe_k, a_w1_k, a_w2_k, a_pe_v, a_w1_v, a_w2_v, w_up_m, w_up_a, w_out, post_norm_g):
    depth = w_in.shape[0]
    for l in range(depth):
        x = _layer(x, pre_norm_g[l], w_in[l], m_conv_w[l], m_conv_b[l], m_w_q[l], m_w_k[l], m_b_i[l], m_b_f[l],
                   m_norm_g[l], m_skip[l], a_pe_k[l], a_w1_k[l], a_w2_k[l], a_pe_v[l], a_w1_v[l], a_w2_v[l],
                   w_up_m[l], w_up_a[l], w_out[l], post_norm_g[l])
    return x
```

```python
import functools

import jax
import jax.numpy as jnp
from jax import lax
from jax.experimental import pallas as pl
from jax.experimental.pallas import tpu as pltpu

F32 = jnp.float32
BF16 = jnp.bfloat16

M_HEADS = 4
M_CONV = 4
M_CHUNK = 128
A_HEADS = 16
A_KV_HEADS = 4
A_GROUP = A_HEADS // A_KV_HEADS
A_HEAD_DIM = 64
A_CMP_BLOCK = 32
A_CMP_STRIDE = 16
A_SEL_BLOCK = 64
A_SEL_TOPK = 8
A_WINDOW = 512
Q_BLOCK = 128
SEL_KEYS_PER_STEP = 512
EXCLUDE = 1e4
MAX_FAST_BOUND = 40.0
SEL_BLOCK_SHIFT = A_SEL_BLOCK.bit_length() - 1
KEY_NORM_ROWS = 512
BOUND_SLACK = 1.01
INPROJ_TN = 1024
INPROJ_TAIL_TN = 512
RMS_EPS = 1e-6
LN_EPS = 1e-6
NEG_INF = -1e30
POS_INF = 1e30

LANES = 128
SUBLANES = 8
BF16_SUBLANES = 16
LOG2_E = 1.4426950408889634
VMEM_LIMIT_BYTES = 48 * 1024 * 1024
INPROJ_VMEM_LIMIT_BYTES = 56 * 1024 * 1024


def _nt_dot(a, b):
    return lax.dot_general(a, b, (((1,), (1,)), ((), ())), preferred_element_type=F32)


def _dot(a, b):
    return jnp.dot(a, b, preferred_element_type=F32)


def _sigmoid(x):
    return 1.0 / (1.0 + jnp.exp(-x))


def _params(*sem):
    return pltpu.CompilerParams(dimension_semantics=sem, vmem_limit_bytes=VMEM_LIMIT_BYTES)


def _inproj_kernel(rows_ref, x_ref, g_ref, wt_ref, *rest):
    del rows_ref
    o_ref, ost_ref, h_ref, h_scr = rest[-4:]
    small_refs = rest[:-4]

    @pl.when(pl.program_id(1) == 0)
    def _():
        xf = x_ref[...]
        ms = jnp.mean(xf * xf, axis=-1, keepdims=True)
        hb = (xf * lax.rsqrt(ms + RMS_EPS) * g_ref[...]).astype(BF16)
        h_scr[...] = hb
        h_ref[...] = hb
        rows = [r[...] for r in small_refs]
        used = sum(r.shape[0] for r in rows)
        rows.append(jnp.zeros((ost_ref.shape[0] - used, xf.shape[1]), F32))
        ost_ref[...] = _nt_dot(jnp.concatenate(rows, axis=0).astype(BF16), hb)

    o_ref[...] = _nt_dot(h_scr[...], wt_ref[...].astype(BF16)).astype(o_ref.dtype)


def _inproj(x2d, g, w_t, row_starts, small_pieces, tm=1024, tn=512):
    t, d = x2d.shape
    n = row_starts.shape[0] * tn
    ns = LANES
    rows_of = lambda lo, hi: pl.BlockSpec((pl.Element(hi - lo), pl.Element(d)), lambda i, j, rows: (lo, 0))
    grid_spec = pltpu.PrefetchScalarGridSpec(
        num_scalar_prefetch=1,
        grid=(t // tm, n // tn),
        in_specs=[
            pl.BlockSpec((tm, d), lambda i, j, rows: (i, 0)),
            pl.BlockSpec((1, d), lambda i, j, rows: (0, 0)),
            pl.BlockSpec((pl.Element(tn), pl.Element(d)), lambda i, j, rows: (rows[j] * SUBLANES, 0)),
        ] + [rows_of(lo, hi) for lo, hi in small_pieces],
        out_specs=[
            pl.BlockSpec((tm, tn), lambda i, j, rows: (i, j)),
            pl.BlockSpec((ns, tm), lambda i, j, rows: (0, i)),
            pl.BlockSpec((tm, d), lambda i, j, rows: (i, 0)),
        ],
        scratch_shapes=[pltpu.VMEM((tm, d), BF16)],
    )
    return pl.pallas_call(
        _inproj_kernel,
        grid_spec=grid_spec,
        out_shape=[jax.ShapeDtypeStruct((t, n), BF16), jax.ShapeDtypeStruct((ns, t), F32),
                   jax.ShapeDtypeStruct((t, d), BF16)],
        compiler_params=pltpu.CompilerParams(dimension_semantics=("parallel", "arbitrary"),
                                             vmem_limit_bytes=INPROJ_VMEM_LIMIT_BYTES),
        name="inproj",
    )(row_starts, x2d, g, w_t, *([w_t] * len(small_pieces)))


def _inproj_tail_kernel(rows_ref, h_ref, wt_ref, o_ref):
    del rows_ref
    o_ref[...] = _nt_dot(h_ref[...], wt_ref[...].astype(BF16)).astype(o_ref.dtype)


def _inproj_tail(h, w_t, row_starts, tm=2048, tn=512):
    t, d = h.shape
    nj = row_starts.shape[0]
    grid_spec = pltpu.PrefetchScalarGridSpec(
        num_scalar_prefetch=1,
        grid=(t // tm, nj),
        in_specs=[
            pl.BlockSpec((tm, d), lambda i, j, rows: (i, 0)),
            pl.BlockSpec((pl.Element(tn), pl.Element(d)), lambda i, j, rows: (rows[j] * SUBLANES, 0)),
        ],
        out_specs=pl.BlockSpec((tm, tn), lambda i, j, rows: (i, j)),
    )
    return pl.pallas_call(
        _inproj_tail_kernel,
        grid_spec=grid_spec,
        out_shape=jax.ShapeDtypeStruct((t, nj * tn), BF16),
        compiler_params=_params("parallel", "arbitrary"),
        name="inproj_tail",
    )(row_starts, h, w_t)


def _conv_qk_kernel(x_ref, cw_ref, cb_ref, wq_ref, wk_ref, *rest, rows):
    n_cast = (len(rest) - 3) // 2
    cast_in = rest[:n_cast]
    xc_ref, q_ref, k_ref = rest[n_cast:n_cast + 3]
    cast_out = rest[n_cast + 3:]
    for src, dst in zip(cast_in, cast_out):
        dst[...] = src[...].astype(dst.dtype)
    s, c = x_ref.shape
    cw = cw_ref[...]
    cb = cb_ref[...]
    wq = wq_ref[...]
    wk = wk_ref[...]
    k_scale = c ** -0.5
    first_row = lax.broadcasted_iota(jnp.int32, (rows, c), 0) == 0
    carry = [jnp.zeros((1, c), F32) for _ in range(M_CONV - 1)]
    for r in range(s // rows):
        xr = x_ref[r * rows:(r + 1) * rows, :].astype(F32)
        a = cw[0:1, :] * xr
        last = []
        for kk in range(1, M_CONV):
            last.append(a[rows - 1:rows, :])
            a = jnp.where(first_row, carry[kk - 1], pltpu.roll(a, 1, 0)) + cw[kk:kk + 1, :] * xr
        carry = last
        acc = a + cb
        xcb = (acc * _sigmoid(acc)).astype(BF16)
        xc_ref[r * rows:(r + 1) * rows, :] = xcb
        q_ref[r * rows:(r + 1) * rows, :] = _dot(xcb, wq).astype(BF16)
        k_ref[r * rows:(r + 1) * rows, :] = (_dot(xcb, wk) * k_scale).astype(BF16)


def _conv_qk(p, mx_col, conv_w, conv_b, wq, wk, cast_weights, bsz, seq):
    t = p.shape[0]
    heads, hd = wq.shape[0], wq.shape[1]
    width = heads * hd
    n_steps = bsz * heads
    blk = pl.BlockSpec((seq, hd), lambda b, h: (b, h))
    slab = lambda w: pl.BlockSpec((w.shape[0] // n_steps, w.shape[1]), lambda b, h: (b * heads + h, 0))
    out = jax.ShapeDtypeStruct((t, width), BF16)
    outs = pl.pallas_call(
        functools.partial(_conv_qk_kernel, rows=256),
        grid=(bsz, heads),
        in_specs=[
            pl.BlockSpec((seq, hd), lambda b, h: (b, mx_col // hd + h)),
            pl.BlockSpec((M_CONV, hd), lambda b, h: (0, h)),
            pl.BlockSpec((1, hd), lambda b, h: (0, h)),
            pl.BlockSpec((None, hd, hd), lambda b, h: (h, 0, 0)),
            pl.BlockSpec((None, hd, hd), lambda b, h: (h, 0, 0)),
        ] + [slab(w) for w in cast_weights],
        out_specs=[blk, blk, blk] + [slab(w) for w in cast_weights],
        out_shape=[out, out, out] + [jax.ShapeDtypeStruct(w.shape, BF16) for w in cast_weights],
        compiler_params=_params("arbitrary", "arbitrary"),
        name="conv_qk",
    )(p, conv_w, conv_b, wq, wk, *cast_weights)
    return outs[0], outs[1], outs[2], outs[3:]


def _log_sigmoid(x):
    return jnp.minimum(x, 0.0) - jnp.log1p(jnp.exp(-jnp.abs(x)))


def _mlstm_kernel(bias_ref, q_ref, k_ref, v_ref, o_ref, z_ref, xc_ref, g_ref, ng_ref, sk_ref,
                  y_ref, c_scr, n_scr, m_scr):
    @pl.when(pl.program_id(1) == 0)
    def _():
        c_scr[...] = jnp.zeros_like(c_scr)
        n_scr[...] = jnp.zeros_like(n_scr)
        m_scr[...] = jnp.zeros_like(m_scr)

    ln = q_ref.shape[0]
    heads, hd = c_scr.shape[0], c_scr.shape[1]
    gates = g_ref[...] + bias_ref[...]
    fb_all = _log_sigmoid(gates[heads:2 * heads, :])
    row = lax.broadcasted_iota(jnp.int32, (ln, ln), 0)
    col = lax.broadcasted_iota(jnp.int32, (ln, ln), 1)
    tril = col <= row
    eye = col == row
    for hh in range(heads):
        sl = slice(hh * hd, (hh + 1) * hd)
        ib = gates[hh:hh + 1, :]
        fb = fb_all[hh:hh + 1, :]
        bcum_c = jnp.sum(jnp.where(tril, fb, 0.0), axis=1, keepdims=True)
        bcum_r = jnp.sum(jnp.where(eye, bcum_c, 0.0), axis=0, keepdims=True)
        g_tot = jnp.sum(fb, axis=1, keepdims=True)
        e_r = ib - bcum_r
        m_st = m_scr[hh]
        d_intra = jnp.where(tril, bcum_c + e_r, NEG_INF)
        d_inter = bcum_c + m_st
        m_t = jnp.maximum(d_inter, jnp.max(d_intra, axis=1, keepdims=True))
        w_intra = jnp.exp(d_intra - m_t)
        w_inter = jnp.exp(d_inter - m_t)

        qb = q_ref[:, sl]
        kb = k_ref[:, sl]
        vb = v_ref[:, sl]
        c_st = c_scr[hh]
        n_st = n_scr[hh]
        sc = _nt_dot(qb, kb) * w_intra
        num = _dot(sc.astype(BF16), vb) + w_inter * _dot(qb, c_st.astype(BF16))
        n_rows = jnp.broadcast_to(n_st, (SUBLANES, hd)).astype(BF16)
        qn = _nt_dot(qb, n_rows)[:, 0:1]
        den = jnp.sum(sc, axis=1, keepdims=True) + w_inter * qn
        h = num / jnp.maximum(jnp.abs(den), jnp.exp(-m_t))

        d_state = g_tot + e_r
        m_new = jnp.maximum(g_tot + m_st, jnp.max(d_state, axis=1, keepdims=True))
        w_s_r = jnp.exp(d_state - m_new)
        w_s_c = jnp.sum(jnp.where(eye, w_s_r, 0.0), axis=1, keepdims=True)
        decay = jnp.exp(g_tot + m_st - m_new)
        kw = kb * w_s_c.astype(BF16)
        kv = lax.dot_general(kw, vb, (((0,), (0,)), ((), ())), preferred_element_type=F32)
        c_scr[hh] = decay * c_st + kv
        w_rows = jnp.broadcast_to(w_s_r, (SUBLANES, ln)).astype(BF16)
        n_scr[hh] = decay * n_st + _dot(w_rows, kb)[0:1, :]
        m_scr[hh] = m_new

        hg = h * _sigmoid(o_ref[:, sl].astype(F32))
        mu = jnp.mean(hg, axis=-1, keepdims=True)
        hc = hg - mu
        var = jnp.mean(hc * hc, axis=-1, keepdims=True)
        hn = hc * lax.rsqrt(var + LN_EPS)
        z = z_ref[:, sl].astype(F32)
        y = (hn * ng_ref[:, sl] + sk_ref[:, sl] * xc_ref[:, sl].astype(F32)) * (z * _sigmoid(z))
        y_ref[:, sl] = y.astype(y_ref.dtype)


def _mlstm(p, mx_col, xc, q, k, gates, bias, norm_g, skip, bsz, seq):
    t, width = q.shape
    pc = mx_col // width
    heads = M_HEADS
    hd = width // heads
    ln = M_CHUNK
    nc = seq // ln
    rows = lambda b, c: (b * nc + c, 0)
    vec = pl.BlockSpec((1, width), lambda b, c: (0, 0))
    return pl.pallas_call(
        _mlstm_kernel,
        grid=(bsz, nc),
        in_specs=[
            pl.BlockSpec((2 * heads, 1), lambda b, c: (0, 0)),
            pl.BlockSpec((ln, width), rows),
            pl.BlockSpec((ln, width), rows),
            pl.BlockSpec((ln, width), lambda b, c: (b * nc + c, pc + 1)),
            pl.BlockSpec((ln, width), lambda b, c: (b * nc + c, pc + 2)),
            pl.BlockSpec((ln, width), lambda b, c: (b * nc + c, pc + 3)),
            pl.BlockSpec((ln, width), rows),
            pl.BlockSpec((2 * heads, ln), lambda b, c: (0, b * nc + c)),
            vec, vec,
        ],
        out_specs=pl.BlockSpec((ln, width), rows),
        out_shape=jax.ShapeDtypeStruct((t, width), BF16),
        scratch_shapes=[pltpu.VMEM((heads, hd, hd), F32), pltpu.VMEM((heads, 1, hd), F32),
                        pltpu.VMEM((heads, 1, 1), F32)],
        compiler_params=_params("parallel", "arbitrary"),
        name="mlstm",
    )(bias, q, k, p, p, p, xc, gates, norm_g, skip)


def _compress_kernel(x_ref, pe_ref, w1_ref, w2_ref, o_ref, xf_scr):
    ng, n_sub, dh = o_ref.shape
    half = A_CMP_STRIDE * dh
    n_chunk = xf_scr.shape[0]
    for c in range(n_chunk):
        xf_scr[c] = x_ref[:, c * LANES:(c + 1) * LANES].astype(F32)
    xs = [jnp.concatenate([xf_scr[c, pl.ds(r, n_sub, stride=A_CMP_STRIDE), :] for c in range(n_chunk)], axis=1)
          for r in range(A_CMP_STRIDE)]
    pe = pe_ref[...]
    rid = lax.broadcasted_iota(jnp.int32, (n_sub, dh), 0)
    for g in range(ng):
        t = jnp.concatenate([x[:, g * dh:(g + 1) * dh] for x in xs], axis=1)
        za = _dot((t + pe[0:1, :]).astype(BF16), w1_ref[0:half, :])
        zb = _dot((t + pe[1:2, :]).astype(BF16), w1_ref[half:2 * half, :])
        pre = za + pltpu.roll(zb, n_sub - 1, 0)
        hid = (pre * _sigmoid(pre)).astype(BF16)
        out = _dot(hid, w2_ref[...])
        o_ref[g] = jnp.where(rid < n_sub - 1, out, 0.0).astype(o_ref.dtype)


def _compress(p, kc_col, pe, w1, w2, bsz, seq):
    g = A_KV_HEADS
    dh = A_HEAD_DIM
    kvw = g * dh
    n_sub = seq // A_CMP_STRIDE
    half = A_CMP_STRIDE * dh
    hid = w1.shape[-1]
    return pl.pallas_call(
        _compress_kernel,
        grid=(2, bsz),
        in_specs=[
            pl.BlockSpec((seq, kvw), lambda i, b: (b, kc_col // kvw + i)),
            pl.BlockSpec((None, 2, half), lambda i, b: (i, 0, 0)),
            pl.BlockSpec((None, 2 * half, hid), lambda i, b: (i, 0, 0)),
            pl.BlockSpec((None, hid, dh), lambda i, b: (i, 0, 0)),
        ],
        out_specs=pl.BlockSpec((None, None, g, n_sub, dh), lambda i, b: (i, b, 0, 0, 0)),
        out_shape=jax.ShapeDtypeStruct((2, bsz, g, n_sub, dh), BF16),
        scratch_shapes=[pltpu.VMEM((kvw // LANES, seq, LANES), F32)],
        compiler_params=_params("parallel", "parallel"),
        name="compress",
    )(p, pe, w1, w2)


def _mask_heads(x, ok, fill, tq):
    n = x.shape[1] // tq
    return jnp.concatenate([jnp.where(ok, x[:, h * tq:(h + 1) * tq], fill) for h in range(n)], axis=1)


def _with_ones_rows(vt):
    return jnp.concatenate([vt, jnp.ones((BF16_SUBLANES, vt.shape[1]), vt.dtype)], axis=0)


def _nsa_kernel(q_ref, kc_ref, vct_ref, ks_ref, vs_ref, kw_ref, vw_ref, ps_ref, z_ref, o_ref,
                sel_ref, vst_ref, vwt_ref, kn_ref, osw_ref):
    qi = pl.program_id(1)
    tq = q_ref.shape[0]
    dh = A_HEAD_DIM
    hpg = A_GROUP
    ng = A_KV_HEADS
    r = hpg * tq
    n_cmp = kc_ref.shape[1]
    tk = vwt_ref.shape[2]
    kt = SEL_KEYS_PER_STEP
    tiles_per_iter = kt // tk
    seq = ks_ref.shape[0]
    n_sel = seq // A_SEL_BLOCK
    t0 = qi * tq
    da = dh + BF16_SUBLANES

    @pl.when(qi == 0)
    def _():
        for j in range(seq // tk):
            vst_ref[j] = vs_ref[j * tk:(j + 1) * tk, :].astype(F32).T.astype(BF16)
            vwt_ref[j] = vw_ref[j * tk:(j + 1) * tk, :].astype(F32).T.astype(BF16)
        rows = KEY_NORM_ROWS
        dim_group = lax.broadcasted_iota(jnp.int32, (ng * dh, LANES), 0) // dh
        member = jnp.where(dim_group == lax.broadcasted_iota(jnp.int32, (ng * dh, LANES), 1), 1.0, 0.0).astype(BF16)
        for which, k_ref in enumerate((ks_ref, kw_ref)):
            best = jnp.zeros((1, LANES), F32)
            for c in range(seq // rows):
                kb = k_ref[c * rows:(c + 1) * rows, :]
                best = jnp.maximum(best, jnp.max(_dot(kb * kb, member), axis=0, keepdims=True))
            for g in range(ng):
                kn_ref[which * ng + g] = best[:, g:g + 1]

    qt = (q_ref[...].astype(F32) * (dh ** -0.5 * LOG2_E)).T.astype(BF16)
    q4t = [jnp.concatenate([qt[(g * hpg + h) * dh:(g * hpg + h + 1) * dh, :] for h in range(hpg)], axis=1)
           for g in range(ng)]

    cn = lax.broadcasted_iota(jnp.int32, (n_cmp, tq), 0)
    ct = t0 + lax.broadcasted_iota(jnp.int32, (n_cmp, tq), 1)
    valid = (cn * A_CMP_STRIDE + (A_CMP_BLOCK - 1)) <= ct
    jn = lax.broadcasted_iota(jnp.int32, (n_sel, n_cmp), 0)
    nn = lax.broadcasted_iota(jnp.int32, (n_sel, n_cmp), 1)
    ov = ((nn * A_CMP_STRIDE < (jn + 1) * A_SEL_BLOCK)
          & (nn * A_CMP_STRIDE + (A_CMP_BLOCK - 1) >= jn * A_SEL_BLOCK))
    ov_t = jnp.where(ov, 1.0, 0.0).astype(BF16)
    jr = lax.broadcasted_iota(jnp.int32, (n_sel, tq), 0)
    tl = t0 + lax.broadcasted_iota(jnp.int32, (n_sel, tq), 1)
    cur = lax.shift_right_logical(tl, SEL_BLOCK_SHIFT)
    forced = (jr == 0) | (jr == cur) | (jr == cur - 1)
    def compressed_and_select():
        for g in range(ng):
            s_m = _mask_heads(_dot(kc_ref[g], q4t[g]), valid, NEG_INF, tq)
            mx = jnp.max(s_m, axis=0, keepdims=True)
            e = _mask_heads(jnp.exp2(s_m - mx), valid, 0.0, tq)
            lsum = jnp.sum(e, axis=0, keepdims=True)
            p_c = e * (1.0 / jnp.where(lsum > 0.0, lsum, 1.0))
            osw_ref[2, g] = _dot(vct_ref[g], p_c.astype(BF16))
            p_sum = p_c[:, 0:tq]
            for h in range(1, hpg):
                p_sum = p_sum + p_c[:, h * tq:(h + 1) * tq]
            p_hi = p_sum.astype(BF16)
            r1 = p_sum - p_hi.astype(F32)
            p_mid = r1.astype(BF16)
            p_lo = (r1 - p_mid.astype(F32)).astype(BF16)
            imp_t = _dot(ov_t, p_hi) + _dot(ov_t, p_mid) + _dot(ov_t, p_lo)
            score = jnp.where(forced, POS_INF, jnp.where(jr <= cur, imp_t, NEG_INF))
            rank = jnp.zeros((n_sel, tq), F32)
            for jp in range(n_sel):
                sj = score[jp:jp + 1, :]
                beats = (sj > score) | ((sj == score) & (jr > jp))
                rank = rank + jnp.where(beats, 1.0, 0.0)
            sel_ref[g] = jnp.where(rank < float(min(A_SEL_TOPK, n_sel)), 1.0, 0.0)

    qn = [jnp.sqrt(jnp.sum(jnp.square(q4t[g].astype(F32)), axis=0, keepdims=True)) for g in range(ng)]
    bound_s = [qn[g] * jnp.sqrt(kn_ref[g]) * BOUND_SLACK for g in range(ng)]
    bound_w = [qn[g] * jnp.sqrt(kn_ref[ng + g]) * BOUND_SLACK for g in range(ng)]
    worst = bound_s[0]
    for b in bound_s[1:] + bound_w:
        worst = jnp.maximum(worst, b)
    fast = jnp.max(worst) <= MAX_FAST_BOUND

    n_wb = -(-A_WINDOW // tk)
    wk = (n_wb + 1) * tk
    j0 = jnp.maximum(qi - n_wb, 0)
    off_w = pl.multiple_of(j0 * tk, tk)
    c_abs = off_w + lax.broadcasted_iota(jnp.int32, (wk, tq), 0)
    t_abs_w = t0 + lax.broadcasted_iota(jnp.int32, (wk, tq), 1)
    ok_w = (c_abs <= t_abs_w) & (c_abs > t_abs_w - A_WINDOW)
    blocks_per_iter = kt // A_SEL_BLOCK

    def normalised(acc):
        return acc[0:dh, :] * (1.0 / acc[dh:dh + 1, :])

    def value_tiles(v_ref, first, count, g):
        vt = jnp.concatenate([v_ref[first + i, g * dh:(g + 1) * dh, :] for i in range(count)], axis=1)
        return _with_ones_rows(vt)

    def emit_output():
        gs = _sigmoid(ps_ref[...])
        g0 = 2 * M_HEADS
        outs = []
        for g in range(ng):
            o_s = osw_ref[0, g]
            o_w = osw_ref[1, g]
            o_c = osw_ref[2, g]
            for h in range(hpg):
                sl = slice(h * tq, (h + 1) * tq)
                gr = g0 + 3 * (g * hpg + h)
                outs.append(gs[gr:gr + 1, :] * o_c[:, sl] + gs[gr + 1:gr + 2, :] * o_s[:, sl]
                            + gs[gr + 2:gr + 3, :] * o_w[:, sl])
        o_all = jnp.concatenate(outs, axis=0).T
        z = z_ref[...].astype(F32)
        o_ref[...] = (o_all * (z * _sigmoid(z))).astype(o_ref.dtype)

    @pl.when(fast)
    def _():
        kcol = lax.broadcasted_iota(jnp.int32, (kt, dh), 1)
        krow = lax.broadcasted_iota(jnp.int32, (kt, dh), 0)
        kblock = lax.shift_right_logical(krow, SEL_BLOCK_SHIFT)
        aux = jnp.where((kcol == 0) | (kcol == SUBLANES + kblock), 1.0, 0.0).astype(BF16)
        aux_w = jnp.where(lax.broadcasted_iota(jnp.int32, (wk, dh), 1) == 0, 1.0, 0.0).astype(BF16)
        aux_d = jnp.where(lax.broadcasted_iota(jnp.int32, (tq, dh), 1) == 0, 1.0, 0.0).astype(BF16)
        zrows = jnp.zeros((LANES - dh - BF16_SUBLANES, r), BF16)

        first_row = lax.broadcasted_iota(jnp.int32, (SUBLANES, r), 0) == 0

        def query_aug(g, shift, block_bias):
            top = jnp.where(first_row, -shift, 0.0)
            low = jnp.zeros((SUBLANES, r), F32) if block_bias is None else jnp.concatenate([block_bias] * hpg, axis=1)
            return jnp.concatenate([q4t[g], jnp.concatenate([top, low], axis=0).astype(BF16), zrows], axis=0)

        def step(j, accs):
            k_all = ks_ref[j * kt:(j + 1) * kt, :]
            out = []
            for g in range(ng):
                k_aug = jnp.concatenate([k_all[:, g * dh:(g + 1) * dh], aux], axis=1)
                bias = sel_ref[g, j * blocks_per_iter:(j + 1) * blocks_per_iter, :]
                p = jnp.exp2(_dot(k_aug, query_aug(g, bound_s[g], bias))).astype(BF16)
                out.append(accs[g] + _dot(value_tiles(vst_ref, j * tiles_per_iter, tiles_per_iter, g), p))
            return tuple(out)

        def finish(accs):
            kd_all = ks_ref[pl.ds(pl.multiple_of(t0, tq), tq), :]
            causal = (lax.broadcasted_iota(jnp.int32, (tq, tq), 0)
                      <= lax.broadcasted_iota(jnp.int32, (tq, tq), 1))
            kw_all = kw_ref[pl.ds(off_w, wk), :]
            for g in range(ng):
                kd_aug = jnp.concatenate([kd_all[:, g * dh:(g + 1) * dh], aux_d], axis=1)
                s_d = _mask_heads(_dot(kd_aug, query_aug(g, bound_s[g], None)), causal, -EXCLUDE, tq)
                acc = accs[g] + _dot(value_tiles(vst_ref, qi, 1, g), jnp.exp2(s_d).astype(BF16))
                osw_ref[0, g] = normalised(acc)
                kw_aug = jnp.concatenate([kw_all[:, g * dh:(g + 1) * dh], aux_w], axis=1)
                s_w = _mask_heads(_dot(kw_aug, query_aug(g, bound_w[g], None)), ok_w, -EXCLUDE, tq)
                p_w = jnp.exp2(s_w).astype(BF16)
                acc_w = _dot(value_tiles(vwt_ref, j0, 1, g), p_w[0:tk, :])
                for i in range(1, n_wb + 1):
                    acc_w = acc_w + _dot(value_tiles(vwt_ref, j0 + i, 1, g), p_w[i * tk:(i + 1) * tk, :])
                osw_ref[1, g] = normalised(acc_w)

        steps_needed = (t0 + kt - 1) // kt
        for n_steps in range(seq // kt + 1):
            @pl.when(steps_needed == n_steps)
            def _(n_steps=n_steps):
                compressed_and_select()
                before_diag = jr < qi * (tq // A_SEL_BLOCK)
                for g in range(ng):
                    sel_ref[g] = jnp.where((sel_ref[g] > 0.5) & before_diag, 0.0, -EXCLUDE)
                accs = tuple(jnp.zeros((da, r), F32) for _ in range(ng))
                for j in range(n_steps):
                    accs = step(j, accs)
                finish(accs)
                emit_output()

    @pl.when(jnp.logical_not(fast))
    def _():
        compressed_and_select()
        zpad = jnp.zeros((dh, r), BF16)
        q_pad = [jnp.concatenate([q4t[g] if gg == g else zpad for gg in range(ng)], axis=0) for g in range(ng)]
        sub_s = lax.broadcasted_iota(jnp.int32, (kt, tq), 0)
        t_abs_s = t0 + lax.broadcasted_iota(jnp.int32, (kt, tq), 1)

        def body(j, carry):
            off = pl.multiple_of(j * kt, kt)
            k_all = ks_ref[pl.ds(off, kt), :]
            causal = (off + sub_s) <= t_abs_s
            out = []
            for g in range(ng):
                m, acc = carry[g]
                s = _dot(k_all, q_pad[g])
                flags = jnp.concatenate(
                    [jnp.broadcast_to(sel_ref[g, pl.ds(j * blocks_per_iter + i, 1), :], (A_SEL_BLOCK, tq))
                     for i in range(blocks_per_iter)], axis=0)
                s = _mask_heads(s, (flags > 0.5) & causal, NEG_INF, tq)
                m_new = jnp.maximum(m, jnp.max(s, axis=0, keepdims=True))
                p = jnp.exp2(s - m_new).astype(BF16)
                vt = value_tiles(vst_ref, j * tiles_per_iter, tiles_per_iter, g)
                out.append((m_new, jnp.exp2(m - m_new) * acc + _dot(vt, p)))
            return tuple(out)

        init = tuple((jnp.full((1, r), NEG_INF, F32), jnp.zeros((da, r), F32)) for _ in range(ng))
        state = lax.fori_loop(0, (t0 + tq - 1) // kt + 1, body, init)
        kw_all = kw_ref[pl.ds(off_w, wk), :]
        for g in range(ng):
            osw_ref[0, g] = normalised(state[g][1])
            s_w = _mask_heads(_dot(kw_all, q_pad[g]), ok_w, NEG_INF, tq)
            p_w = jnp.exp2(s_w - jnp.max(s_w, axis=0, keepdims=True)).astype(BF16)
            acc_w = _dot(value_tiles(vwt_ref, j0, 1, g), p_w[0:tk, :])
            for i in range(1, n_wb + 1):
                acc_w = acc_w + _dot(value_tiles(vwt_ref, j0 + i, 1, g), p_w[i * tk:(i + 1) * tk, :])
            osw_ref[1, g] = normalised(acc_w)
        emit_output()


def _nsa(p, pkv, ps, kvc, vct, q_col, z_col, bsz, seq):
    t = p.shape[0]
    g = A_KV_HEADS
    dh = A_HEAD_DIM
    aw = A_HEADS * dh
    kvw = g * dh
    tq = Q_BLOCK
    nq = seq // tq
    n_cmp = kvc.shape[3]
    tk = LANES
    rows = lambda b, qi: b * nq + qi
    kv_piece = lambda i: pl.BlockSpec((seq, kvw), lambda b, qi: (b, i))
    return pl.pallas_call(
        _nsa_kernel,
        grid=(bsz, nq),
        in_specs=[
            pl.BlockSpec((tq, aw), lambda b, qi: (rows(b, qi), q_col // aw)),
            pl.BlockSpec((None, None, g, n_cmp, dh), lambda b, qi: (0, b, 0, 0, 0)),
            pl.BlockSpec((None, g, dh, n_cmp), lambda b, qi: (b, 0, 0, 0)),
            kv_piece(2), kv_piece(3), kv_piece(4), kv_piece(5),
            pl.BlockSpec((LANES, tq), lambda b, qi: (0, rows(b, qi))),
            pl.BlockSpec((tq, aw), lambda b, qi: (rows(b, qi), z_col // aw)),
        ],
        out_specs=pl.BlockSpec((tq, aw), lambda b, qi: (rows(b, qi), 0)),
        out_shape=jax.ShapeDtypeStruct((t, aw), BF16),
        scratch_shapes=[pltpu.VMEM((g, seq // A_SEL_BLOCK, tq), F32),
                        pltpu.VMEM((seq // tk, kvw, tk), BF16),
                        pltpu.VMEM((seq // tk, kvw, tk), BF16),
                        pltpu.VMEM((2 * g, 1, 1), F32),
                        pltpu.VMEM((3, g, dh, A_GROUP * tq), F32)],
        compiler_params=_params("parallel", "arbitrary"),
        name="nsa",
    )(p, kvc, vct, pkv, pkv, pkv, pkv, ps, p)


def _merge_out_kernel(ym_ref, ya_ref, wm_ref, wa_ref, gm_ref, ga_ref, wo_ref, g_ref, x_ref, o_ref):
    um = _dot(ym_ref[...], wm_ref[...])
    ua = _dot(ya_ref[...], wa_ref[...])
    mg = _sigmoid(gm_ref[...].astype(F32)) * um + _sigmoid(ga_ref[...].astype(F32)) * ua
    out = _dot(mg.astype(BF16), wo_ref[...])
    ms = jnp.mean(out * out, axis=-1, keepdims=True)
    o_ref[...] = x_ref[...] + out * lax.rsqrt(ms + RMS_EPS) * g_ref[...]


def _merge_out(ym, ya, wm, wa, p, gm_col, ga_col, wo, g, x2d, tm=256):
    t, d = x2d.shape
    kdim = ym.shape[1]
    resident = lambda shape: pl.BlockSpec(shape, lambda i: (0, 0), pipeline_mode=pl.Buffered(1))
    return pl.pallas_call(
        _merge_out_kernel,
        grid=(t // tm,),
        in_specs=[
            pl.BlockSpec((tm, kdim), lambda i: (i, 0)),
            pl.BlockSpec((tm, kdim), lambda i: (i, 0)),
            resident((kdim, d)),
            resident((kdim, d)),
            pl.BlockSpec((tm, d), lambda i: (i, gm_col // d)),
            pl.BlockSpec((tm, d), lambda i: (i, ga_col // d)),
            resident((d, d)),
            pl.BlockSpec((1, d), lambda i: (0, 0)),
            pl.BlockSpec((tm, d), lambda i: (i, 0)),
        ],
        out_specs=pl.BlockSpec((tm, d), lambda i: (i, 0)),
        out_shape=jax.ShapeDtypeStruct((t, d), F32),
        compiler_params=_params("parallel"),
        name="merge_out",
    )(ym, ya, wm, wa, p, p, wo, g, x2d)


def _layer(x, pre_g, w_in, conv_w, conv_b, w_q, w_k, b_i, b_f, norm_g, skip,
           pe_k, w1_k, w2_k, pe_v, w1_v, w2_v, w_up_m, w_up_a, w_out, post_g):
    bsz, seq, d = x.shape
    t = bsz * seq
    mw = w_up_m.shape[0]
    aw = w_up_a.shape[0]
    g = A_KV_HEADS
    dh = A_HEAD_DIM
    kvw = g * dh
    n_gate = 3 * A_HEADS

    o_i = 4 * mw
    o_f = o_i + M_HEADS
    o_q = o_f + M_HEADS
    o_g = o_q + aw + 6 * kvw
    o_z = o_g + n_gate
    o_m = o_z + aw
    o_kv = o_q + aw
    main_pieces = ((o_m, w_in.shape[1]), (0, o_i), (o_q, o_kv), (o_z, o_m))
    tail_pieces = ((o_kv, o_g),)
    w_t = w_in.T
    tile_rows = lambda pieces, tn: jnp.asarray(
        [r // SUBLANES for lo, hi in pieces for r in range(lo, hi, tn)], jnp.int32)
    row_starts = tile_rows(main_pieces, INPROJ_TN)
    small_pieces = ((o_i, o_q), (o_g, o_z))
    gm_col = 0
    ga_col = d
    mx_col = 2 * d
    q_col = mx_col + 4 * mw
    z_col = q_col + aw

    x2d = x.reshape(t, d)
    p, ps, h = _inproj(x2d, pre_g.reshape(1, d), w_t, row_starts, small_pieces, tn=INPROJ_TN)
    pkv = _inproj_tail(h, w_t, tile_rows(tail_pieces, INPROJ_TAIL_TN), tn=INPROJ_TAIL_TN)

    xc, q, k, (w_up_m_b, w_up_a_b, w_out_b) = _conv_qk(
        p, mx_col, conv_w, conv_b.reshape(1, mw), w_q.astype(BF16), w_k.astype(BF16), (w_up_m, w_up_a, w_out), bsz, seq)
    bias_m = jnp.concatenate([b_i, b_f]).reshape(2 * M_HEADS, 1)
    y_m = _mlstm(p, mx_col, xc, q, k, ps, bias_m, norm_g.reshape(1, mw), skip.reshape(1, mw), bsz, seq)

    half = A_CMP_STRIDE * dh
    pe = jnp.stack([pe_k.reshape(2, half), pe_v.reshape(2, half)])
    w1 = jnp.stack([w1_k, w1_v]).astype(BF16)
    w2 = jnp.stack([w2_k, w2_v]).astype(BF16)
    kvc = _compress(pkv, 0, pe, w1, w2, bsz, seq)
    vct = jnp.swapaxes(kvc[1], 2, 3)
    y_a = _nsa(p, pkv, ps, kvc, vct, q_col, z_col, bsz, seq)

    out = _merge_out(y_m, y_a, w_up_m_b, w_up_a_b, p, gm_col, ga_col, w_out_b, post_g.reshape(1, d), x2d)
    return out.reshape(bsz, seq, d)


def kernel(x, pre_norm_g, w_in, m_conv_w, m_conv_b, m_w_q, m_w_k, m_b_i, m_b_f, m_norm_g, m_skip, a_pe_k, a_w1_k, a_w2_k, a_pe_v, a_w1_v, a_w2_v, w_up_m, w_up_a, w_out, post_norm_g):
    depth = w_in.shape[0]
    for l in range(depth):
        x = _layer(x, pre_norm_g[l], w_in[l], m_conv_w[l], m_conv_b[l], m_w_q[l], m_w_k[l], m_b_i[l], m_b_f[l],
                   m_norm_g[l], m_skip[l], a_pe_k[l], a_w1_k[l], a_w2_k[l], a_pe_v[l], a_w1_v[l], a_w2_v[l],
                   w_up_m[l], w_up_a[l], w_out[l], post_norm_g[l])
    return x
```

```python
import functools

import jax
import jax.numpy as jnp
from jax import lax
from jax.experimental import pallas as pl
from jax.experimental.pallas import tpu as pltpu

F32 = jnp.float32
BF16 = jnp.bfloat16

M_HEADS = 4
M_CONV = 4
M_CHUNK = 128
A_HEADS = 16
A_KV_HEADS = 4
A_GROUP = A_HEADS // A_KV_HEADS
A_HEAD_DIM = 64
A_CMP_BLOCK = 32
A_CMP_STRIDE = 16
A_SEL_BLOCK = 64
A_SEL_TOPK = 8
A_WINDOW = 512
Q_BLOCK = 128
SEL_KEYS_PER_STEP = 512
EXCLUDE = 1e4
MAX_FAST_BOUND = 40.0
SEL_BLOCK_SHIFT = A_SEL_BLOCK.bit_length() - 1
KEY_NORM_ROWS = 512
BOUND_SLACK = 1.01
INPROJ_TN = 1024
INPROJ_TAIL_TN = 512
RMS_EPS = 1e-6
LN_EPS = 1e-6
NEG_INF = -1e30
POS_INF = 1e30

LANES = 128
SUBLANES = 8
BF16_SUBLANES = 16
LOG2_E = 1.4426950408889634
VMEM_LIMIT_BYTES = 48 * 1024 * 1024
INPROJ_VMEM_LIMIT_BYTES = 56 * 1024 * 1024
MERGE_VMEM_LIMIT_BYTES = 60 * 1024 * 1024


def _nt_dot(a, b):
    return lax.dot_general(a, b, (((1,), (1,)), ((), ())), preferred_element_type=F32)


def _dot(a, b):
    return jnp.dot(a, b, preferred_element_type=F32)


def _sigmoid(x):
    return 1.0 / (1.0 + jnp.exp(-x))


def _params(*sem):
    return pltpu.CompilerParams(dimension_semantics=sem, vmem_limit_bytes=VMEM_LIMIT_BYTES)


def _inproj_kernel(rows_ref, x_ref, g_ref, wt_ref, *rest):
    del rows_ref
    o_ref, ost_ref, h_ref, h_scr = rest[-4:]
    small_refs = rest[:-4]

    @pl.when(pl.program_id(1) == 0)
    def _():
        xf = x_ref[...]
        ms = jnp.mean(xf * xf, axis=-1, keepdims=True)
        hb = (xf * lax.rsqrt(ms + RMS_EPS) * g_ref[...]).astype(BF16)
        h_scr[...] = hb
        h_ref[...] = hb
        rows = [r[...] for r in small_refs]
        used = sum(r.shape[0] for r in rows)
        rows.append(jnp.zeros((ost_ref.shape[0] - used, xf.shape[1]), F32))
        ost_ref[...] = _nt_dot(jnp.concatenate(rows, axis=0).astype(BF16), hb)

    o_ref[...] = _nt_dot(h_scr[...], wt_ref[...].astype(BF16)).astype(o_ref.dtype)


def _inproj(x2d, g, w_t, row_starts, small_pieces, tm=1024, tn=512):
    t, d = x2d.shape
    n = row_starts.shape[0] * tn
    ns = LANES
    rows_of = lambda lo, hi: pl.BlockSpec((pl.Element(hi - lo), pl.Element(d)), lambda i, j, rows: (lo, 0))
    grid_spec = pltpu.PrefetchScalarGridSpec(
        num_scalar_prefetch=1,
        grid=(t // tm, n // tn),
        in_specs=[
            pl.BlockSpec((tm, d), lambda i, j, rows: (i, 0)),
            pl.BlockSpec((1, d), lambda i, j, rows: (0, 0)),
            pl.BlockSpec((pl.Element(tn), pl.Element(d)), lambda i, j, rows: (rows[j] * SUBLANES, 0)),
        ] + [rows_of(lo, hi) for lo, hi in small_pieces],
        out_specs=[
            pl.BlockSpec((tm, tn), lambda i, j, rows: (i, j)),
            pl.BlockSpec((ns, tm), lambda i, j, rows: (0, i)),
            pl.BlockSpec((tm, d), lambda i, j, rows: (i, 0)),
        ],
        scratch_shapes=[pltpu.VMEM((tm, d), BF16)],
    )
    return pl.pallas_call(
        _inproj_kernel,
        grid_spec=grid_spec,
        out_shape=[jax.ShapeDtypeStruct((t, n), BF16), jax.ShapeDtypeStruct((ns, t), F32),
                   jax.ShapeDtypeStruct((t, d), BF16)],
        compiler_params=pltpu.CompilerParams(dimension_semantics=("parallel", "arbitrary"),
                                             vmem_limit_bytes=INPROJ_VMEM_LIMIT_BYTES),
        name="inproj",
    )(row_starts, x2d, g, w_t, *([w_t] * len(small_pieces)))


def _inproj_tail_kernel(rows_ref, h_ref, wt_ref, o_ref):
    del rows_ref
    o_ref[...] = _nt_dot(h_ref[...], wt_ref[...].astype(BF16)).astype(o_ref.dtype)


def _inproj_tail(h, w_t, row_starts, tm=2048, tn=512):
    t, d = h.shape
    nj = row_starts.shape[0]
    grid_spec = pltpu.PrefetchScalarGridSpec(
        num_scalar_prefetch=1,
        grid=(t // tm, nj),
        in_specs=[
            pl.BlockSpec((tm, d), lambda i, j, rows: (i, 0)),
            pl.BlockSpec((pl.Element(tn), pl.Element(d)), lambda i, j, rows: (rows[j] * SUBLANES, 0)),
        ],
        out_specs=pl.BlockSpec((tm, tn), lambda i, j, rows: (i, j)),
    )
    return pl.pallas_call(
        _inproj_tail_kernel,
        grid_spec=grid_spec,
        out_shape=jax.ShapeDtypeStruct((t, nj * tn), BF16),
        compiler_params=_params("parallel", "arbitrary"),
        name="inproj_tail",
    )(row_starts, h, w_t)


def _conv_qk_kernel(x_ref, cw_ref, cb_ref, wq_ref, wk_ref, xc_ref, q_ref, k_ref, *, rows):
    s, c = x_ref.shape
    cw = cw_ref[...]
    cb = cb_ref[...]
    wq = wq_ref[...]
    wk = wk_ref[...]
    k_scale = c ** -0.5
    first_row = lax.broadcasted_iota(jnp.int32, (rows, c), 0) == 0
    carry = [jnp.zeros((1, c), F32) for _ in range(M_CONV - 1)]
    for r in range(s // rows):
        xr = x_ref[r * rows:(r + 1) * rows, :].astype(F32)
        a = cw[0:1, :] * xr
        last = []
        for kk in range(1, M_CONV):
            last.append(a[rows - 1:rows, :])
            a = jnp.where(first_row, carry[kk - 1], pltpu.roll(a, 1, 0)) + cw[kk:kk + 1, :] * xr
        carry = last
        acc = a + cb
        xcb = (acc * _sigmoid(acc)).astype(BF16)
        xc_ref[r * rows:(r + 1) * rows, :] = xcb
        q_ref[r * rows:(r + 1) * rows, :] = _dot(xcb, wq).astype(BF16)
        k_ref[r * rows:(r + 1) * rows, :] = (_dot(xcb, wk) * k_scale).astype(BF16)


def _conv_qk(p, mx_col, conv_w, conv_b, wq, wk, bsz, seq):
    t = p.shape[0]
    heads, hd = wq.shape[0], wq.shape[1]
    width = heads * hd
    blk = pl.BlockSpec((seq, hd), lambda b, h: (b, h))
    out = jax.ShapeDtypeStruct((t, width), BF16)
    return pl.pallas_call(
        functools.partial(_conv_qk_kernel, rows=256),
        grid=(bsz, heads),
        in_specs=[
            pl.BlockSpec((seq, hd), lambda b, h: (b, mx_col // hd + h)),
            pl.BlockSpec((M_CONV, hd), lambda b, h: (0, h)),
            pl.BlockSpec((1, hd), lambda b, h: (0, h)),
            pl.BlockSpec((None, hd, hd), lambda b, h: (h, 0, 0)),
            pl.BlockSpec((None, hd, hd), lambda b, h: (h, 0, 0)),
        ],
        out_specs=[blk, blk, blk],
        out_shape=[out, out, out],
        compiler_params=_params("parallel", "parallel"),
        name="conv_qk",
    )(p, conv_w, conv_b, wq, wk)


def _log_sigmoid(x):
    return jnp.minimum(x, 0.0) - jnp.log1p(jnp.exp(-jnp.abs(x)))


def _mlstm_kernel(bias_ref, q_ref, k_ref, v_ref, o_ref, z_ref, xc_ref, g_ref, ng_ref, sk_ref,
                  y_ref, c_scr, n_scr, m_scr):
    @pl.when(pl.program_id(1) == 0)
    def _():
        c_scr[...] = jnp.zeros_like(c_scr)
        n_scr[...] = jnp.zeros_like(n_scr)
        m_scr[...] = jnp.zeros_like(m_scr)

    ln = q_ref.shape[0]
    heads, hd = c_scr.shape[0], c_scr.shape[1]
    gates = g_ref[...] + bias_ref[...]
    fb_all = _log_sigmoid(gates[heads:2 * heads, :])
    row = lax.broadcasted_iota(jnp.int32, (ln, ln), 0)
    col = lax.broadcasted_iota(jnp.int32, (ln, ln), 1)
    tril = col <= row
    eye = col == row
    for hh in range(heads):
        sl = slice(hh * hd, (hh + 1) * hd)
        ib = gates[hh:hh + 1, :]
        fb = fb_all[hh:hh + 1, :]
        bcum_c = jnp.sum(jnp.where(tril, fb, 0.0), axis=1, keepdims=True)
        bcum_r = jnp.sum(jnp.where(eye, bcum_c, 0.0), axis=0, keepdims=True)
        g_tot = jnp.sum(fb, axis=1, keepdims=True)
        e_r = ib - bcum_r
        m_st = m_scr[hh]
        d_intra = jnp.where(tril, bcum_c + e_r, NEG_INF)
        d_inter = bcum_c + m_st
        m_t = jnp.maximum(d_inter, jnp.max(d_intra, axis=1, keepdims=True))
        w_intra = jnp.exp(d_intra - m_t)
        w_inter = jnp.exp(d_inter - m_t)

        qb = q_ref[:, sl]
        kb = k_ref[:, sl]
        vb = v_ref[:, sl]
        c_st = c_scr[hh]
        n_st = n_scr[hh]
        sc = _nt_dot(qb, kb) * w_intra
        num = _dot(sc.astype(BF16), vb) + w_inter * _dot(qb, c_st.astype(BF16))
        n_rows = jnp.broadcast_to(n_st, (SUBLANES, hd)).astype(BF16)
        qn = _nt_dot(qb, n_rows)[:, 0:1]
        den = jnp.sum(sc, axis=1, keepdims=True) + w_inter * qn
        h = num / jnp.maximum(jnp.abs(den), jnp.exp(-m_t))

        d_state = g_tot + e_r
        m_new = jnp.maximum(g_tot + m_st, jnp.max(d_state, axis=1, keepdims=True))
        w_s_r = jnp.exp(d_state - m_new)
        w_s_c = jnp.sum(jnp.where(eye, w_s_r, 0.0), axis=1, keepdims=True)
        decay = jnp.exp(g_tot + m_st - m_new)
        kw = kb * w_s_c.astype(BF16)
        kv = lax.dot_general(kw, vb, (((0,), (0,)), ((), ())), preferred_element_type=F32)
        c_scr[hh] = decay * c_st + kv
        w_rows = jnp.broadcast_to(w_s_r, (SUBLANES, ln)).astype(BF16)
        n_scr[hh] = decay * n_st + _dot(w_rows, kb)[0:1, :]
        m_scr[hh] = m_new

        hg = h * _sigmoid(o_ref[:, sl].astype(F32))
        mu = jnp.mean(hg, axis=-1, keepdims=True)
        hc = hg - mu
        var = jnp.mean(hc * hc, axis=-1, keepdims=True)
        hn = hc * lax.rsqrt(var + LN_EPS)
        z = z_ref[:, sl].astype(F32)
        y = (hn * ng_ref[:, sl] + sk_ref[:, sl] * xc_ref[:, sl].astype(F32)) * (z * _sigmoid(z))
        y_ref[:, sl] = y.astype(y_ref.dtype)


def _mlstm(p, mx_col, xc, q, k, gates, bias, norm_g, skip, bsz, seq):
    t, width = q.shape
    pc = mx_col // width
    heads = M_HEADS
    hd = width // heads
    ln = M_CHUNK
    nc = seq // ln
    rows = lambda b, c: (b * nc + c, 0)
    vec = pl.BlockSpec((1, width), lambda b, c: (0, 0))
    return pl.pallas_call(
        _mlstm_kernel,
        grid=(bsz, nc),
        in_specs=[
            pl.BlockSpec((2 * heads, 1), lambda b, c: (0, 0)),
            pl.BlockSpec((ln, width), rows),
            pl.BlockSpec((ln, width), rows),
            pl.BlockSpec((ln, width), lambda b, c: (b * nc + c, pc + 1)),
            pl.BlockSpec((ln, width), lambda b, c: (b * nc + c, pc + 2)),
            pl.BlockSpec((ln, width), lambda b, c: (b * nc + c, pc + 3)),
            pl.BlockSpec((ln, width), rows),
            pl.BlockSpec((2 * heads, ln), lambda b, c: (0, b * nc + c)),
            vec, vec,
        ],
        out_specs=pl.BlockSpec((ln, width), rows),
        out_shape=jax.ShapeDtypeStruct((t, width), BF16),
        scratch_shapes=[pltpu.VMEM((heads, hd, hd), F32), pltpu.VMEM((heads, 1, hd), F32),
                        pltpu.VMEM((heads, 1, 1), F32)],
        compiler_params=_params("parallel", "arbitrary"),
        name="mlstm",
    )(bias, q, k, p, p, p, xc, gates, norm_g, skip)


def _compress_kernel(x_ref, pe_ref, w1_ref, w2_ref, *rest):
    n_cast = (len(rest) - 2) // 2
    cast_in, o_ref, cast_out, xf_scr = rest[:n_cast], rest[n_cast], rest[n_cast + 1:-1], rest[-1]
    for src, dst in zip(cast_in, cast_out):
        dst[...] = src[...].astype(dst.dtype)
    ng, n_sub, dh = o_ref.shape
    half = A_CMP_STRIDE * dh
    n_chunk = xf_scr.shape[0]
    for c in range(n_chunk):
        xf_scr[c] = x_ref[:, c * LANES:(c + 1) * LANES].astype(F32)
    xs = [jnp.concatenate([xf_scr[c, pl.ds(r, n_sub, stride=A_CMP_STRIDE), :] for c in range(n_chunk)], axis=1)
          for r in range(A_CMP_STRIDE)]
    pe = pe_ref[...]
    rid = lax.broadcasted_iota(jnp.int32, (n_sub, dh), 0)
    for g in range(ng):
        t = jnp.concatenate([x[:, g * dh:(g + 1) * dh] for x in xs], axis=1)
        za = _dot((t + pe[0:1, :]).astype(BF16), w1_ref[0:half, :])
        zb = _dot((t + pe[1:2, :]).astype(BF16), w1_ref[half:2 * half, :])
        pre = za + pltpu.roll(zb, n_sub - 1, 0)
        hid = (pre * _sigmoid(pre)).astype(BF16)
        out = _dot(hid, w2_ref[...])
        o_ref[g] = jnp.where(rid < n_sub - 1, out, 0.0).astype(o_ref.dtype)


def _compress(p, kc_col, pe, w1, w2, cast_weights, bsz, seq):
    g = A_KV_HEADS
    dh = A_HEAD_DIM
    kvw = g * dh
    n_sub = seq // A_CMP_STRIDE
    half = A_CMP_STRIDE * dh
    hid = w1.shape[-1]
    n_steps = 2 * bsz
    slab = lambda w: pl.BlockSpec((w.shape[0] // n_steps, w.shape[1]), lambda i, b: (i * bsz + b, 0))
    outs = pl.pallas_call(
        _compress_kernel,
        grid=(2, bsz),
        in_specs=[
            pl.BlockSpec((seq, kvw), lambda i, b: (b, kc_col // kvw + i)),
            pl.BlockSpec((None, 2, half), lambda i, b: (i, 0, 0)),
            pl.BlockSpec((None, 2 * half, hid), lambda i, b: (i, 0, 0)),
            pl.BlockSpec((None, hid, dh), lambda i, b: (i, 0, 0)),
        ] + [slab(w) for w in cast_weights],
        out_specs=[pl.BlockSpec((None, None, g, n_sub, dh), lambda i, b: (i, b, 0, 0, 0))]
                  + [slab(w) for w in cast_weights],
        out_shape=[jax.ShapeDtypeStruct((2, bsz, g, n_sub, dh), BF16)]
                  + [jax.ShapeDtypeStruct(w.shape, BF16) for w in cast_weights],
        scratch_shapes=[pltpu.VMEM((kvw // LANES, seq, LANES), F32)],
        compiler_params=_params("arbitrary", "arbitrary"),
        name="compress",
    )(p, pe, w1, w2, *cast_weights)
    return outs[0], outs[1:]


def _mask_heads(x, ok, fill, tq):
    n = x.shape[1] // tq
    return jnp.concatenate([jnp.where(ok, x[:, h * tq:(h + 1) * tq], fill) for h in range(n)], axis=1)


def _with_ones_rows(vt):
    return jnp.concatenate([vt, jnp.ones((BF16_SUBLANES, vt.shape[1]), vt.dtype)], axis=0)


def _nsa_kernel(q_ref, kc_ref, vct_ref, ks_ref, vs_ref, kw_ref, vw_ref, ps_ref, z_ref, o_ref,
                sel_ref, vst_ref, vwt_ref, kn_ref, osw_ref):
    qi = pl.program_id(1)
    tq = q_ref.shape[0]
    dh = A_HEAD_DIM
    hpg = A_GROUP
    ng = A_KV_HEADS
    r = hpg * tq
    n_cmp = kc_ref.shape[1]
    tk = vwt_ref.shape[2]
    kt = SEL_KEYS_PER_STEP
    tiles_per_iter = kt // tk
    seq = ks_ref.shape[0]
    n_sel = seq // A_SEL_BLOCK
    t0 = qi * tq
    da = dh + BF16_SUBLANES

    @pl.when(qi == 0)
    def _():
        for j in range(seq // tk):
            vst_ref[j] = vs_ref[j * tk:(j + 1) * tk, :].astype(F32).T.astype(BF16)
            vwt_ref[j] = vw_ref[j * tk:(j + 1) * tk, :].astype(F32).T.astype(BF16)
        rows = KEY_NORM_ROWS
        dim_group = lax.broadcasted_iota(jnp.int32, (ng * dh, LANES), 0) // dh
        member = jnp.where(dim_group == lax.broadcasted_iota(jnp.int32, (ng * dh, LANES), 1), 1.0, 0.0).astype(BF16)
        for which, k_ref in enumerate((ks_ref, kw_ref)):
            best = jnp.zeros((1, LANES), F32)
            for c in range(seq // rows):
                kb = k_ref[c * rows:(c + 1) * rows, :]
                best = jnp.maximum(best, jnp.max(_dot(kb * kb, member), axis=0, keepdims=True))
            for g in range(ng):
                kn_ref[which * ng + g] = best[:, g:g + 1]

    qt = (q_ref[...].astype(F32) * (dh ** -0.5 * LOG2_E)).T.astype(BF16)
    q4t = [jnp.concatenate([qt[(g * hpg + h) * dh:(g * hpg + h + 1) * dh, :] for h in range(hpg)], axis=1)
           for g in range(ng)]

    cn = lax.broadcasted_iota(jnp.int32, (n_cmp, tq), 0)
    ct = t0 + lax.broadcasted_iota(jnp.int32, (n_cmp, tq), 1)
    valid = (cn * A_CMP_STRIDE + (A_CMP_BLOCK - 1)) <= ct
    jn = lax.broadcasted_iota(jnp.int32, (n_sel, n_cmp), 0)
    nn = lax.broadcasted_iota(jnp.int32, (n_sel, n_cmp), 1)
    ov = ((nn * A_CMP_STRIDE < (jn + 1) * A_SEL_BLOCK)
          & (nn * A_CMP_STRIDE + (A_CMP_BLOCK - 1) >= jn * A_SEL_BLOCK))
    ov_t = jnp.where(ov, 1.0, 0.0).astype(BF16)
    jr = lax.broadcasted_iota(jnp.int32, (n_sel, tq), 0)
    tl = t0 + lax.broadcasted_iota(jnp.int32, (n_sel, tq), 1)
    cur = lax.shift_right_logical(tl, SEL_BLOCK_SHIFT)
    forced = (jr == 0) | (jr == cur) | (jr == cur - 1)
    def compressed_and_select():
        for g in range(ng):
            s_m = _mask_heads(_dot(kc_ref[g], q4t[g]), valid, NEG_INF, tq)
            mx = jnp.max(s_m, axis=0, keepdims=True)
            e = _mask_heads(jnp.exp2(s_m - mx), valid, 0.0, tq)
            lsum = jnp.sum(e, axis=0, keepdims=True)
            p_c = e * (1.0 / jnp.where(lsum > 0.0, lsum, 1.0))
            osw_ref[2, g] = _dot(vct_ref[g], p_c.astype(BF16))
            p_sum = p_c[:, 0:tq]
            for h in range(1, hpg):
                p_sum = p_sum + p_c[:, h * tq:(h + 1) * tq]
            p_hi = p_sum.astype(BF16)
            r1 = p_sum - p_hi.astype(F32)
            p_mid = r1.astype(BF16)
            p_lo = (r1 - p_mid.astype(F32)).astype(BF16)
            imp_t = _dot(ov_t, p_hi) + _dot(ov_t, p_mid) + _dot(ov_t, p_lo)
            score = jnp.where(forced, POS_INF, jnp.where(jr <= cur, imp_t, NEG_INF))
            rank = jnp.zeros((n_sel, tq), F32)
            for jp in range(n_sel):
                sj = score[jp:jp + 1, :]
                beats = (sj > score) | ((sj == score) & (jr > jp))
                rank = rank + jnp.where(beats, 1.0, 0.0)
            sel_ref[g] = jnp.where(rank < float(min(A_SEL_TOPK, n_sel)), 1.0, 0.0)

    qn = [jnp.sqrt(jnp.sum(jnp.square(q4t[g].astype(F32)), axis=0, keepdims=True)) for g in range(ng)]
    bound_s = [qn[g] * jnp.sqrt(kn_ref[g]) * BOUND_SLACK for g in range(ng)]
    bound_w = [qn[g] * jnp.sqrt(kn_ref[ng + g]) * BOUND_SLACK for g in range(ng)]
    worst = bound_s[0]
    for b in bound_s[1:] + bound_w:
        worst = jnp.maximum(worst, b)
    fast = jnp.max(worst) <= MAX_FAST_BOUND

    n_wb = -(-A_WINDOW // tk)
    wk = (n_wb + 1) * tk
    j0 = jnp.maximum(qi - n_wb, 0)
    off_w = pl.multiple_of(j0 * tk, tk)
    c_abs = off_w + lax.broadcasted_iota(jnp.int32, (wk, tq), 0)
    t_abs_w = t0 + lax.broadcasted_iota(jnp.int32, (wk, tq), 1)
    ok_w = (c_abs <= t_abs_w) & (c_abs > t_abs_w - A_WINDOW)
    blocks_per_iter = kt // A_SEL_BLOCK

    def normalised(acc):
        return acc[0:dh, :] * (1.0 / acc[dh:dh + 1, :])

    def value_tiles(v_ref, first, count, g):
        vt = jnp.concatenate([v_ref[first + i, g * dh:(g + 1) * dh, :] for i in range(count)], axis=1)
        return _with_ones_rows(vt)

    def emit_output():
        gs = _sigmoid(ps_ref[...])
        g0 = 2 * M_HEADS
        outs = []
        for g in range(ng):
            o_s = osw_ref[0, g]
            o_w = osw_ref[1, g]
            o_c = osw_ref[2, g]
            for h in range(hpg):
                sl = slice(h * tq, (h + 1) * tq)
                gr = g0 + 3 * (g * hpg + h)
                outs.append(gs[gr:gr + 1, :] * o_c[:, sl] + gs[gr + 1:gr + 2, :] * o_s[:, sl]
                            + gs[gr + 2:gr + 3, :] * o_w[:, sl])
        o_all = jnp.concatenate(outs, axis=0).T
        z = z_ref[...].astype(F32)
        o_ref[...] = (o_all * (z * _sigmoid(z))).astype(o_ref.dtype)

    @pl.when(fast)
    def _():
        kcol = lax.broadcasted_iota(jnp.int32, (kt, dh), 1)
        krow = lax.broadcasted_iota(jnp.int32, (kt, dh), 0)
        kblock = lax.shift_right_logical(krow, SEL_BLOCK_SHIFT)
        aux = jnp.where((kcol == 0) | (kcol == SUBLANES + kblock), 1.0, 0.0).astype(BF16)
        aux_w = jnp.where(lax.broadcasted_iota(jnp.int32, (wk, dh), 1) == 0, 1.0, 0.0).astype(BF16)
        aux_d = jnp.where(lax.broadcasted_iota(jnp.int32, (tq, dh), 1) == 0, 1.0, 0.0).astype(BF16)
        zrows = jnp.zeros((LANES - dh - BF16_SUBLANES, r), BF16)

        first_row = lax.broadcasted_iota(jnp.int32, (SUBLANES, r), 0) == 0

        def query_aug(g, shift, block_bias):
            top = jnp.where(first_row, -shift, 0.0)
            low = jnp.zeros((SUBLANES, r), F32) if block_bias is None else jnp.concatenate([block_bias] * hpg, axis=1)
            return jnp.concatenate([q4t[g], jnp.concatenate([top, low], axis=0).astype(BF16), zrows], axis=0)

        def step(j, accs):
            k_all = ks_ref[j * kt:(j + 1) * kt, :]
            out = []
            for g in range(ng):
                k_aug = jnp.concatenate([k_all[:, g * dh:(g + 1) * dh], aux], axis=1)
                bias = sel_ref[g, j * blocks_per_iter:(j + 1) * blocks_per_iter, :]
                p = jnp.exp2(_dot(k_aug, query_aug(g, bound_s[g], bias))).astype(BF16)
                out.append(accs[g] + _dot(value_tiles(vst_ref, j * tiles_per_iter, tiles_per_iter, g), p))
            return tuple(out)

        def finish(accs):
            kd_all = ks_ref[pl.ds(pl.multiple_of(t0, tq), tq), :]
            causal = (lax.broadcasted_iota(jnp.int32, (tq, tq), 0)
                      <= lax.broadcasted_iota(jnp.int32, (tq, tq), 1))
            kw_all = kw_ref[pl.ds(off_w, wk), :]
            for g in range(ng):
                kd_aug = jnp.concatenate([kd_all[:, g * dh:(g + 1) * dh], aux_d], axis=1)
                s_d = _mask_heads(_dot(kd_aug, query_aug(g, bound_s[g], None)), causal, -EXCLUDE, tq)
                acc = accs[g] + _dot(value_tiles(vst_ref, qi, 1, g), jnp.exp2(s_d).astype(BF16))
                osw_ref[0, g] = normalised(acc)
                kw_aug = jnp.concatenate([kw_all[:, g * dh:(g + 1) * dh], aux_w], axis=1)
                s_w = _mask_heads(_dot(kw_aug, query_aug(g, bound_w[g], None)), ok_w, -EXCLUDE, tq)
                p_w = jnp.exp2(s_w).astype(BF16)
                acc_w = _dot(value_tiles(vwt_ref, j0, 1, g), p_w[0:tk, :])
                for i in range(1, n_wb + 1):
                    acc_w = acc_w + _dot(value_tiles(vwt_ref, j0 + i, 1, g), p_w[i * tk:(i + 1) * tk, :])
                osw_ref[1, g] = normalised(acc_w)

        steps_needed = (t0 + kt - 1) // kt
        for n_steps in range(seq // kt + 1):
            @pl.when(steps_needed == n_steps)
            def _(n_steps=n_steps):
                compressed_and_select()
                before_diag = jr < qi * (tq // A_SEL_BLOCK)
                for g in range(ng):
                    sel_ref[g] = jnp.where((sel_ref[g] > 0.5) & before_diag, 0.0, -EXCLUDE)
                accs = tuple(jnp.zeros((da, r), F32) for _ in range(ng))
                for j in range(n_steps):
                    accs = step(j, accs)
                finish(accs)
                emit_output()

    @pl.when(jnp.logical_not(fast))
    def _():
        compressed_and_select()
        zpad = jnp.zeros((dh, r), BF16)
        q_pad = [jnp.concatenate([q4t[g] if gg == g else zpad for gg in range(ng)], axis=0) for g in range(ng)]
        sub_s = lax.broadcasted_iota(jnp.int32, (kt, tq), 0)
        t_abs_s = t0 + lax.broadcasted_iota(jnp.int32, (kt, tq), 1)

        def body(j, carry):
            off = pl.multiple_of(j * kt, kt)
            k_all = ks_ref[pl.ds(off, kt), :]
            causal = (off + sub_s) <= t_abs_s
            out = []
            for g in range(ng):
                m, acc = carry[g]
                s = _dot(k_all, q_pad[g])
                flags = jnp.concatenate(
                    [jnp.broadcast_to(sel_ref[g, pl.ds(j * blocks_per_iter + i, 1), :], (A_SEL_BLOCK, tq))
                     for i in range(blocks_per_iter)], axis=0)
                s = _mask_heads(s, (flags > 0.5) & causal, NEG_INF, tq)
                m_new = jnp.maximum(m, jnp.max(s, axis=0, keepdims=True))
                p = jnp.exp2(s - m_new).astype(BF16)
                vt = value_tiles(vst_ref, j * tiles_per_iter, tiles_per_iter, g)
                out.append((m_new, jnp.exp2(m - m_new) * acc + _dot(vt, p)))
            return tuple(out)

        init = tuple((jnp.full((1, r), NEG_INF, F32), jnp.zeros((da, r), F32)) for _ in range(ng))
        state = lax.fori_loop(0, (t0 + tq - 1) // kt + 1, body, init)
        kw_all = kw_ref[pl.ds(off_w, wk), :]
        for g in range(ng):
            osw_ref[0, g] = normalised(state[g][1])
            s_w = _mask_heads(_dot(kw_all, q_pad[g]), ok_w, NEG_INF, tq)
            p_w = jnp.exp2(s_w - jnp.max(s_w, axis=0, keepdims=True)).astype(BF16)
            acc_w = _dot(value_tiles(vwt_ref, j0, 1, g), p_w[0:tk, :])
            for i in range(1, n_wb + 1):
                acc_w = acc_w + _dot(value_tiles(vwt_ref, j0 + i, 1, g), p_w[i * tk:(i + 1) * tk, :])
            osw_ref[1, g] = normalised(acc_w)
        emit_output()


def _nsa(p, pkv, ps, kvc, vct, q_col, z_col, bsz, seq):
    t = p.shape[0]
    g = A_KV_HEADS
    dh = A_HEAD_DIM
    aw = A_HEADS * dh
    kvw = g * dh
    tq = Q_BLOCK
    nq = seq // tq
    n_cmp = kvc.shape[3]
    tk = LANES
    rows = lambda b, qi: b * nq + qi
    kv_piece = lambda i: pl.BlockSpec((seq, kvw), lambda b, qi: (b, i))
    return pl.pallas_call(
        _nsa_kernel,
        grid=(bsz, nq),
        in_specs=[
            pl.BlockSpec((tq, aw), lambda b, qi: (rows(b, qi), q_col // aw)),
            pl.BlockSpec((None, None, g, n_cmp, dh), lambda b, qi: (0, b, 0, 0, 0)),
            pl.BlockSpec((None, g, dh, n_cmp), lambda b, qi: (b, 0, 0, 0)),
            kv_piece(2), kv_piece(3), kv_piece(4), kv_piece(5),
            pl.BlockSpec((LANES, tq), lambda b, qi: (0, rows(b, qi))),
            pl.BlockSpec((tq, aw), lambda b, qi: (rows(b, qi), z_col // aw)),
        ],
        out_specs=pl.BlockSpec((tq, aw), lambda b, qi: (rows(b, qi), 0)),
        out_shape=jax.ShapeDtypeStruct((t, aw), BF16),
        scratch_shapes=[pltpu.VMEM((g, seq // A_SEL_BLOCK, tq), F32),
                        pltpu.VMEM((seq // tk, kvw, tk), BF16),
                        pltpu.VMEM((seq // tk, kvw, tk), BF16),
                        pltpu.VMEM((2 * g, 1, 1), F32),
                        pltpu.VMEM((3, g, dh, A_GROUP * tq), F32)],
        compiler_params=_params("parallel", "arbitrary"),
        name="nsa",
    )(p, kvc, vct, pkv, pkv, pkv, pkv, ps, p)


def _merge_out_kernel(ym_ref, ya_ref, wm_ref, wa_ref, gm_ref, ga_ref, wo_ref, g_ref, x_ref, o_ref):
    um = _dot(ym_ref[...], wm_ref[...])
    ua = _dot(ya_ref[...], wa_ref[...])
    mg = _sigmoid(gm_ref[...].astype(F32)) * um + _sigmoid(ga_ref[...].astype(F32)) * ua
    out = _dot(mg.astype(BF16), wo_ref[...])
    ms = jnp.mean(out * out, axis=-1, keepdims=True)
    o_ref[...] = x_ref[...] + out * lax.rsqrt(ms + RMS_EPS) * g_ref[...]


def _merge_out(ym, ya, wm, wa, p, gm_col, ga_col, wo, g, x2d, tm=512):
    t, d = x2d.shape
    kdim = ym.shape[1]
    resident = lambda shape: pl.BlockSpec(shape, lambda i: (0, 0), pipeline_mode=pl.Buffered(1))
    return pl.pallas_call(
        _merge_out_kernel,
        grid=(t // tm,),
        in_specs=[
            pl.BlockSpec((tm, kdim), lambda i: (i, 0)),
            pl.BlockSpec((tm, kdim), lambda i: (i, 0)),
            resident((kdim, d)),
            resident((kdim, d)),
            pl.BlockSpec((tm, d), lambda i: (i, gm_col // d)),
            pl.BlockSpec((tm, d), lambda i: (i, ga_col // d)),
            resident((d, d)),
            pl.BlockSpec((1, d), lambda i: (0, 0)),
            pl.BlockSpec((tm, d), lambda i: (i, 0)),
        ],
        out_specs=pl.BlockSpec((tm, d), lambda i: (i, 0)),
        out_shape=jax.ShapeDtypeStruct((t, d), F32),
        compiler_params=pltpu.CompilerParams(dimension_semantics=("parallel",),
                                             vmem_limit_bytes=MERGE_VMEM_LIMIT_BYTES),
        name="merge_out",
    )(ym, ya, wm, wa, p, p, wo, g, x2d)


def _layer(x, pre_g, w_in, conv_w, conv_b, w_q, w_k, b_i, b_f, norm_g, skip,
           pe_k, w1_k, w2_k, pe_v, w1_v, w2_v, w_up_m, w_up_a, w_out, post_g):
    bsz, seq, d = x.shape
    t = bsz * seq
    mw = w_up_m.shape[0]
    aw = w_up_a.shape[0]
    g = A_KV_HEADS
    dh = A_HEAD_DIM
    kvw = g * dh
    n_gate = 3 * A_HEADS

    o_i = 4 * mw
    o_f = o_i + M_HEADS
    o_q = o_f + M_HEADS
    o_g = o_q + aw + 6 * kvw
    o_z = o_g + n_gate
    o_m = o_z + aw
    o_kv = o_q + aw
    main_pieces = ((o_m, w_in.shape[1]), (0, o_i), (o_q, o_kv), (o_z, o_m))
    tail_pieces = ((o_kv, o_g),)
    w_t = w_in.T
    tile_rows = lambda pieces, tn: jnp.asarray(
        [r // SUBLANES for lo, hi in pieces for r in range(lo, hi, tn)], jnp.int32)
    row_starts = tile_rows(main_pieces, INPROJ_TN)
    small_pieces = ((o_i, o_q), (o_g, o_z))
    gm_col = 0
    ga_col = d
    mx_col = 2 * d
    q_col = mx_col + 4 * mw
    z_col = q_col + aw

    x2d = x.reshape(t, d)
    p, ps, h = _inproj(x2d, pre_g.reshape(1, d), w_t, row_starts, small_pieces, tn=INPROJ_TN)
    pkv = _inproj_tail(h, w_t, tile_rows(tail_pieces, INPROJ_TAIL_TN), tn=INPROJ_TAIL_TN)

    xc, q, k = _conv_qk(p, mx_col, conv_w, conv_b.reshape(1, mw), w_q.astype(BF16), w_k.astype(BF16), bsz, seq)
    bias_m = jnp.concatenate([b_i, b_f]).reshape(2 * M_HEADS, 1)
    y_m = _mlstm(p, mx_col, xc, q, k, ps, bias_m, norm_g.reshape(1, mw), skip.reshape(1, mw), bsz, seq)

    half = A_CMP_STRIDE * dh
    pe = jnp.stack([pe_k.reshape(2, half), pe_v.reshape(2, half)])
    w1 = jnp.stack([w1_k, w1_v]).astype(BF16)
    w2 = jnp.stack([w2_k, w2_v]).astype(BF16)
    kvc, (w_up_m_b, w_up_a_b, w_out_b) = _compress(pkv, 0, pe, w1, w2, (w_up_m, w_up_a, w_out), bsz, seq)
    vct = jnp.swapaxes(kvc[1], 2, 3)
    y_a = _nsa(p, pkv, ps, kvc, vct, q_col, z_col, bsz, seq)

    out = _merge_out(y_m, y_a, w_up_m_b, w_up_a_b, p, gm_col, ga_col, w_out_b, post_g.reshape(1, d), x2d)
    return out.reshape(bsz, seq, d)


def kernel(x, pre_norm_g, w_in, m_conv_w, m_conv_b, m_w_q, m_w_k, m_b_i, m_b_f, m_norm_g, m_skip, a_pe_k, a_w1_k, a_w2_k, a_pe_v, a_w1_v, a_w2_v, w_up_m, w_up_a, w_out, post_norm_g):
    depth = w_in.shape[0]
    for l in range(depth):
        x = _layer(x, pre_norm_g[l], w_in[l], m_conv_w[l], m_conv_b[l], m_w_q[l], m_w_k[l], m_b_i[l], m_b_f[l],
                   m_norm_g[l], m_skip[l], a_pe_k[l], a_w1_k[l], a_w2_k[l], a_pe_v[l], a_w1_v[l], a_w2_v[l],
                   w_up_m[l], w_up_a[l], w_out[l], post_norm_g[l])
    return x
```

```python
import functools

import jax
import jax.numpy as jnp
from jax import lax
from jax.experimental import pallas as pl
from jax.experimental.pallas import tpu as pltpu

F32 = jnp.float32
BF16 = jnp.bfloat16

M_HEADS = 4
M_CONV = 4
M_CHUNK = 128
A_HEADS = 16
A_KV_HEADS = 4
A_GROUP = A_HEADS // A_KV_HEADS
A_HEAD_DIM = 64
A_CMP_BLOCK = 32
A_CMP_STRIDE = 16
A_SEL_BLOCK = 64
A_SEL_TOPK = 8
A_WINDOW = 512
Q_BLOCK = 128
SEL_KEYS_PER_STEP = 512
EXCLUDE = 1e4
MAX_FAST_BOUND = 40.0
SEL_BLOCK_SHIFT = A_SEL_BLOCK.bit_length() - 1
KEY_NORM_ROWS = 512
BOUND_SLACK = 1.01
INPROJ_TN = 1024
INPROJ_TAIL_TN = 1536
RMS_EPS = 1e-6
LN_EPS = 1e-6
NEG_INF = -1e30
POS_INF = 1e30

LANES = 128
SUBLANES = 8
BF16_SUBLANES = 16
LOG2_E = 1.4426950408889634
VMEM_LIMIT_BYTES = 48 * 1024 * 1024
INPROJ_VMEM_LIMIT_BYTES = 56 * 1024 * 1024


def _nt_dot(a, b):
    return lax.dot_general(a, b, (((1,), (1,)), ((), ())), preferred_element_type=F32)


def _dot(a, b):
    return jnp.dot(a, b, preferred_element_type=F32)


def _sigmoid(x):
    return 1.0 / (1.0 + jnp.exp(-x))


def _params(*sem):
    return pltpu.CompilerParams(dimension_semantics=sem, vmem_limit_bytes=VMEM_LIMIT_BYTES)


def _inproj_kernel(rows_ref, x_ref, g_ref, wt_ref, *rest):
    del rows_ref
    o_ref, ost_ref, h_ref, h_scr = rest[-4:]
    small_refs = rest[:-4]

    @pl.when(pl.program_id(1) == 0)
    def _():
        xf = x_ref[...]
        ms = jnp.mean(xf * xf, axis=-1, keepdims=True)
        hb = (xf * lax.rsqrt(ms + RMS_EPS) * g_ref[...]).astype(BF16)
        h_scr[...] = hb
        h_ref[...] = hb
        rows = [r[...] for r in small_refs]
        used = sum(r.shape[0] for r in rows)
        rows.append(jnp.zeros((ost_ref.shape[0] - used, xf.shape[1]), F32))
        ost_ref[...] = _nt_dot(jnp.concatenate(rows, axis=0).astype(BF16), hb)

    o_ref[...] = _nt_dot(h_scr[...], wt_ref[...].astype(BF16)).astype(o_ref.dtype)


def _inproj(x2d, g, w_t, row_starts, small_pieces, tm=1024, tn=512):
    t, d = x2d.shape
    n = row_starts.shape[0] * tn
    ns = LANES
    rows_of = lambda lo, hi: pl.BlockSpec((pl.Element(hi - lo), pl.Element(d)), lambda i, j, rows: (lo, 0))
    grid_spec = pltpu.PrefetchScalarGridSpec(
        num_scalar_prefetch=1,
        grid=(t // tm, n // tn),
        in_specs=[
            pl.BlockSpec((tm, d), lambda i, j, rows: (i, 0)),
            pl.BlockSpec((1, d), lambda i, j, rows: (0, 0)),
            pl.BlockSpec((pl.Element(tn), pl.Element(d)), lambda i, j, rows: (rows[j] * SUBLANES, 0)),
        ] + [rows_of(lo, hi) for lo, hi in small_pieces],
        out_specs=[
            pl.BlockSpec((tm, tn), lambda i, j, rows: (i, j)),
            pl.BlockSpec((ns, tm), lambda i, j, rows: (0, i)),
            pl.BlockSpec((tm, d), lambda i, j, rows: (i, 0)),
        ],
        scratch_shapes=[pltpu.VMEM((tm, d), BF16)],
    )
    return pl.pallas_call(
        _inproj_kernel,
        grid_spec=grid_spec,
        out_shape=[jax.ShapeDtypeStruct((t, n), BF16), jax.ShapeDtypeStruct((ns, t), F32),
                   jax.ShapeDtypeStruct((t, d), BF16)],
        compiler_params=pltpu.CompilerParams(dimension_semantics=("parallel", "arbitrary"),
                                             vmem_limit_bytes=INPROJ_VMEM_LIMIT_BYTES),
        name="inproj",
    )(row_starts, x2d, g, w_t, *([w_t] * len(small_pieces)))


def _inproj_tail_kernel(rows_ref, h_ref, wt_ref, o_ref):
    del rows_ref
    o_ref[...] = _nt_dot(h_ref[...], wt_ref[...].astype(BF16)).astype(o_ref.dtype)


def _inproj_tail(h, w_t, row_starts, tm=2048, tn=512):
    t, d = h.shape
    nj = row_starts.shape[0]
    grid_spec = pltpu.PrefetchScalarGridSpec(
        num_scalar_prefetch=1,
        grid=(t // tm, nj),
        in_specs=[
            pl.BlockSpec((tm, d), lambda i, j, rows: (i, 0)),
            pl.BlockSpec((pl.Element(tn), pl.Element(d)), lambda i, j, rows: (rows[j] * SUBLANES, 0)),
        ],
        out_specs=pl.BlockSpec((tm, tn), lambda i, j, rows: (i, j)),
    )
    return pl.pallas_call(
        _inproj_tail_kernel,
        grid_spec=grid_spec,
        out_shape=jax.ShapeDtypeStruct((t, nj * tn), BF16),
        compiler_params=pltpu.CompilerParams(dimension_semantics=("parallel", "arbitrary"),
                                             vmem_limit_bytes=INPROJ_VMEM_LIMIT_BYTES),
        name="inproj_tail",
    )(row_starts, h, w_t)


def _conv_qk_kernel(x_ref, cw_ref, cb_ref, wq_ref, wk_ref, xc_ref, q_ref, k_ref, *, rows):
    s, c = x_ref.shape
    cw = cw_ref[...]
    cb = cb_ref[...]
    wq = wq_ref[...]
    wk = wk_ref[...]
    k_scale = c ** -0.5
    first_row = lax.broadcasted_iota(jnp.int32, (rows, c), 0) == 0
    carry = [jnp.zeros((1, c), F32) for _ in range(M_CONV - 1)]
    for r in range(s // rows):
        xr = x_ref[r * rows:(r + 1) * rows, :].astype(F32)
        a = cw[0:1, :] * xr
        last = []
        for kk in range(1, M_CONV):
            last.append(a[rows - 1:rows, :])
            a = jnp.where(first_row, carry[kk - 1], pltpu.roll(a, 1, 0)) + cw[kk:kk + 1, :] * xr
        carry = last
        acc = a + cb
        xcb = (acc * _sigmoid(acc)).astype(BF16)
        xc_ref[r * rows:(r + 1) * rows, :] = xcb
        q_ref[r * rows:(r + 1) * rows, :] = _dot(xcb, wq).astype(BF16)
        k_ref[r * rows:(r + 1) * rows, :] = (_dot(xcb, wk) * k_scale).astype(BF16)


def _conv_qk(p, mx_col, conv_w, conv_b, wq, wk, bsz, seq):
    t = p.shape[0]
    heads, hd = wq.shape[0], wq.shape[1]
    width = heads * hd
    blk = pl.BlockSpec((seq, hd), lambda b, h: (b, h))
    out = jax.ShapeDtypeStruct((t, width), BF16)
    return pl.pallas_call(
        functools.partial(_conv_qk_kernel, rows=256),
        grid=(bsz, heads),
        in_specs=[
            pl.BlockSpec((seq, hd), lambda b, h: (b, mx_col // hd + h)),
            pl.BlockSpec((M_CONV, hd), lambda b, h: (0, h)),
            pl.BlockSpec((1, hd), lambda b, h: (0, h)),
            pl.BlockSpec((None, hd, hd), lambda b, h: (h, 0, 0)),
            pl.BlockSpec((None, hd, hd), lambda b, h: (h, 0, 0)),
        ],
        out_specs=[blk, blk, blk],
        out_shape=[out, out, out],
        compiler_params=_params("parallel", "parallel"),
        name="conv_qk",
    )(p, conv_w, conv_b, wq, wk)


def _log_sigmoid(x):
    return jnp.minimum(x, 0.0) - jnp.log1p(jnp.exp(-jnp.abs(x)))


def _mlstm_kernel(bias_ref, q_ref, k_ref, v_ref, o_ref, z_ref, xc_ref, g_ref, ng_ref, sk_ref,
                  y_ref, c_scr, n_scr, m_scr):
    @pl.when(pl.program_id(1) == 0)
    def _():
        c_scr[...] = jnp.zeros_like(c_scr)
        n_scr[...] = jnp.zeros_like(n_scr)
        m_scr[...] = jnp.zeros_like(m_scr)

    ln = q_ref.shape[0]
    heads, hd = c_scr.shape[0], c_scr.shape[1]
    gates = g_ref[...] + bias_ref[...]
    fb_all = _log_sigmoid(gates[heads:2 * heads, :])
    row = lax.broadcasted_iota(jnp.int32, (ln, ln), 0)
    col = lax.broadcasted_iota(jnp.int32, (ln, ln), 1)
    tril = col <= row
    eye = col == row
    for hh in range(heads):
        sl = slice(hh * hd, (hh + 1) * hd)
        ib = gates[hh:hh + 1, :]
        fb = fb_all[hh:hh + 1, :]
        bcum_c = jnp.sum(jnp.where(tril, fb, 0.0), axis=1, keepdims=True)
        bcum_r = jnp.sum(jnp.where(eye, bcum_c, 0.0), axis=0, keepdims=True)
        g_tot = jnp.sum(fb, axis=1, keepdims=True)
        e_r = ib - bcum_r
        m_st = m_scr[hh]
        d_intra = jnp.where(tril, bcum_c + e_r, NEG_INF)
        d_inter = bcum_c + m_st
        m_t = jnp.maximum(d_inter, jnp.max(d_intra, axis=1, keepdims=True))
        w_intra = jnp.exp(d_intra - m_t)
        w_inter = jnp.exp(d_inter - m_t)

        qb = q_ref[:, sl]
        kb = k_ref[:, sl]
        vb = v_ref[:, sl]
        c_st = c_scr[hh]
        n_st = n_scr[hh]
        sc = _nt_dot(qb, kb) * w_intra
        num = _dot(sc.astype(BF16), vb) + w_inter * _dot(qb, c_st.astype(BF16))
        n_rows = jnp.broadcast_to(n_st, (SUBLANES, hd)).astype(BF16)
        qn = _nt_dot(qb, n_rows)[:, 0:1]
        den = jnp.sum(sc, axis=1, keepdims=True) + w_inter * qn
        h = num / jnp.maximum(jnp.abs(den), jnp.exp(-m_t))

        d_state = g_tot + e_r
        m_new = jnp.maximum(g_tot + m_st, jnp.max(d_state, axis=1, keepdims=True))
        w_s_r = jnp.exp(d_state - m_new)
        w_s_c = jnp.sum(jnp.where(eye, w_s_r, 0.0), axis=1, keepdims=True)
        decay = jnp.exp(g_tot + m_st - m_new)
        kw = kb * w_s_c.astype(BF16)
        kv = lax.dot_general(kw, vb, (((0,), (0,)), ((), ())), preferred_element_type=F32)
        c_scr[hh] = decay * c_st + kv
        w_rows = jnp.broadcast_to(w_s_r, (SUBLANES, ln)).astype(BF16)
        n_scr[hh] = decay * n_st + _dot(w_rows, kb)[0:1, :]
        m_scr[hh] = m_new

        hg = h * _sigmoid(o_ref[:, sl].astype(F32))
        mu = jnp.mean(hg, axis=-1, keepdims=True)
        hc = hg - mu
        var = jnp.mean(hc * hc, axis=-1, keepdims=True)
        hn = hc * lax.rsqrt(var + LN_EPS)
        z = z_ref[:, sl].astype(F32)
        y = (hn * ng_ref[:, sl] + sk_ref[:, sl] * xc_ref[:, sl].astype(F32)) * (z * _sigmoid(z))
        y_ref[:, sl] = y.astype(y_ref.dtype)


def _mlstm(p, mx_col, xc, q, k, gates, bias, norm_g, skip, bsz, seq):
    t, width = q.shape
    pc = mx_col // width
    heads = M_HEADS
    hd = width // heads
    ln = M_CHUNK
    nc = seq // ln
    rows = lambda b, c: (b * nc + c, 0)
    vec = pl.BlockSpec((1, width), lambda b, c: (0, 0))
    return pl.pallas_call(
        _mlstm_kernel,
        grid=(bsz, nc),
        in_specs=[
            pl.BlockSpec((2 * heads, 1), lambda b, c: (0, 0)),
            pl.BlockSpec((ln, width), rows),
            pl.BlockSpec((ln, width), rows),
            pl.BlockSpec((ln, width), lambda b, c: (b * nc + c, pc + 1)),
            pl.BlockSpec((ln, width), lambda b, c: (b * nc + c, pc + 2)),
            pl.BlockSpec((ln, width), lambda b, c: (b * nc + c, pc + 3)),
            pl.BlockSpec((ln, width), rows),
            pl.BlockSpec((2 * heads, ln), lambda b, c: (0, b * nc + c)),
            vec, vec,
        ],
        out_specs=pl.BlockSpec((ln, width), rows),
        out_shape=jax.ShapeDtypeStruct((t, width), BF16),
        scratch_shapes=[pltpu.VMEM((heads, hd, hd), F32), pltpu.VMEM((heads, 1, hd), F32),
                        pltpu.VMEM((heads, 1, 1), F32)],
        compiler_params=_params("parallel", "arbitrary"),
        name="mlstm",
    )(bias, q, k, p, p, p, xc, gates, norm_g, skip)


def _compress_kernel(x_ref, pe_ref, w1_ref, w2_ref, *rest):
    n_cast = (len(rest) - 2) // 2
    cast_in, o_ref, cast_out, xf_scr = rest[:n_cast], rest[n_cast], rest[n_cast + 1:-1], rest[-1]
    for src, dst in zip(cast_in, cast_out):
        dst[...] = src[...].astype(dst.dtype)
    ng, n_sub, dh = o_ref.shape
    half = A_CMP_STRIDE * dh
    n_chunk = xf_scr.shape[0]
    for c in range(n_chunk):
        xf_scr[c] = x_ref[:, c * LANES:(c + 1) * LANES].astype(F32)
    xs = [jnp.concatenate([xf_scr[c, pl.ds(r, n_sub, stride=A_CMP_STRIDE), :] for c in range(n_chunk)], axis=1)
          for r in range(A_CMP_STRIDE)]
    pe = pe_ref[...]
    rid = lax.broadcasted_iota(jnp.int32, (n_sub, dh), 0)
    for g in range(ng):
        t = jnp.concatenate([x[:, g * dh:(g + 1) * dh] for x in xs], axis=1)
        za = _dot((t + pe[0:1, :]).astype(BF16), w1_ref[0:half, :])
        zb = _dot((t + pe[1:2, :]).astype(BF16), w1_ref[half:2 * half, :])
        pre = za + pltpu.roll(zb, n_sub - 1, 0)
        hid = (pre * _sigmoid(pre)).astype(BF16)
        out = _dot(hid, w2_ref[...])
        o_ref[g] = jnp.where(rid < n_sub - 1, out, 0.0).astype(o_ref.dtype)


def _compress(p, kc_col, pe, w1, w2, cast_weights, bsz, seq):
    g = A_KV_HEADS
    dh = A_HEAD_DIM
    kvw = g * dh
    n_sub = seq // A_CMP_STRIDE
    half = A_CMP_STRIDE * dh
    hid = w1.shape[-1]
    n_steps = 2 * bsz
    slab = lambda w: pl.BlockSpec((w.shape[0] // n_steps, w.shape[1]), lambda i, b: (i * bsz + b, 0))
    outs = pl.pallas_call(
        _compress_kernel,
        grid=(2, bsz),
        in_specs=[
            pl.BlockSpec((seq, kvw), lambda i, b: (b, kc_col // kvw + i)),
            pl.BlockSpec((None, 2, half), lambda i, b: (i, 0, 0)),
            pl.BlockSpec((None, 2 * half, hid), lambda i, b: (i, 0, 0)),
            pl.BlockSpec((None, hid, dh), lambda i, b: (i, 0, 0)),
        ] + [slab(w) for w in cast_weights],
        out_specs=[pl.BlockSpec((None, None, g, n_sub, dh), lambda i, b: (i, b, 0, 0, 0))]
                  + [slab(w) for w in cast_weights],
        out_shape=[jax.ShapeDtypeStruct((2, bsz, g, n_sub, dh), BF16)]
                  + [jax.ShapeDtypeStruct(w.shape, BF16) for w in cast_weights],
        scratch_shapes=[pltpu.VMEM((kvw // LANES, seq, LANES), F32)],
        compiler_params=_params("arbitrary", "arbitrary"),
        name="compress",
    )(p, pe, w1, w2, *cast_weights)
    return outs[0], outs[1:]


def _mask_heads(x, ok, fill, tq):
    n = x.shape[1] // tq
    return jnp.concatenate([jnp.where(ok, x[:, h * tq:(h + 1) * tq], fill) for h in range(n)], axis=1)


def _with_ones_rows(vt):
    return jnp.concatenate([vt, jnp.ones((BF16_SUBLANES, vt.shape[1]), vt.dtype)], axis=0)


def _nsa_kernel(q_ref, kc_ref, vct_ref, ks_ref, vs_ref, kw_ref, vw_ref, ps_ref, z_ref, o_ref,
                sel_ref, vst_ref, vwt_ref, kn_ref, osw_ref):
    qi = pl.program_id(1)
    tq = q_ref.shape[0]
    dh = A_HEAD_DIM
    hpg = A_GROUP
    ng = A_KV_HEADS
    r = hpg * tq
    n_cmp = kc_ref.shape[1]
    tk = vwt_ref.shape[2]
    kt = SEL_KEYS_PER_STEP
    tiles_per_iter = kt // tk
    seq = ks_ref.shape[0]
    n_sel = seq // A_SEL_BLOCK
    t0 = qi * tq
    da = dh + BF16_SUBLANES

    @pl.when(qi == 0)
    def _():
        for j in range(seq // tk):
            vst_ref[j] = vs_ref[j * tk:(j + 1) * tk, :].astype(F32).T.astype(BF16)
            vwt_ref[j] = vw_ref[j * tk:(j + 1) * tk, :].astype(F32).T.astype(BF16)
        rows = KEY_NORM_ROWS
        dim_group = lax.broadcasted_iota(jnp.int32, (ng * dh, LANES), 0) // dh
        member = jnp.where(dim_group == lax.broadcasted_iota(jnp.int32, (ng * dh, LANES), 1), 1.0, 0.0).astype(BF16)
        for which, k_ref in enumerate((ks_ref, kw_ref)):
            best = jnp.zeros((1, LANES), F32)
            for c in range(seq // rows):
                kb = k_ref[c * rows:(c + 1) * rows, :]
                best = jnp.maximum(best, jnp.max(_dot(kb * kb, member), axis=0, keepdims=True))
            for g in range(ng):
                kn_ref[which * ng + g] = best[:, g:g + 1]

    qt = (q_ref[...].astype(F32) * (dh ** -0.5 * LOG2_E)).T.astype(BF16)
    q4t = [jnp.concatenate([qt[(g * hpg + h) * dh:(g * hpg + h + 1) * dh, :] for h in range(hpg)], axis=1)
           for g in range(ng)]

    cn = lax.broadcasted_iota(jnp.int32, (n_cmp, tq), 0)
    ct = t0 + lax.broadcasted_iota(jnp.int32, (n_cmp, tq), 1)
    valid = (cn * A_CMP_STRIDE + (A_CMP_BLOCK - 1)) <= ct
    jn = lax.broadcasted_iota(jnp.int32, (n_sel, n_cmp), 0)
    nn = lax.broadcasted_iota(jnp.int32, (n_sel, n_cmp), 1)
    ov = ((nn * A_CMP_STRIDE < (jn + 1) * A_SEL_BLOCK)
          & (nn * A_CMP_STRIDE + (A_CMP_BLOCK - 1) >= jn * A_SEL_BLOCK))
    ov_t = jnp.where(ov, 1.0, 0.0).astype(BF16)
    jr = lax.broadcasted_iota(jnp.int32, (n_sel, tq), 0)
    tl = t0 + lax.broadcasted_iota(jnp.int32, (n_sel, tq), 1)
    cur = lax.shift_right_logical(tl, SEL_BLOCK_SHIFT)
    forced = (jr == 0) | (jr == cur) | (jr == cur - 1)
    def compressed_and_select():
        for g in range(ng):
            s_m = _mask_heads(_dot(kc_ref[g], q4t[g]), valid, NEG_INF, tq)
            mx = jnp.max(s_m, axis=0, keepdims=True)
            e = _mask_heads(jnp.exp2(s_m - mx), valid, 0.0, tq)
            lsum = jnp.sum(e, axis=0, keepdims=True)
            p_c = e * (1.0 / jnp.where(lsum > 0.0, lsum, 1.0))
            osw_ref[2, g] = _dot(vct_ref[g], p_c.astype(BF16))
            p_sum = p_c[:, 0:tq]
            for h in range(1, hpg):
                p_sum = p_sum + p_c[:, h * tq:(h + 1) * tq]
            p_hi = p_sum.astype(BF16)
            r1 = p_sum - p_hi.astype(F32)
            p_mid = r1.astype(BF16)
            p_lo = (r1 - p_mid.astype(F32)).astype(BF16)
            imp_t = _dot(ov_t, p_hi) + _dot(ov_t, p_mid) + _dot(ov_t, p_lo)
            score = jnp.where(forced, POS_INF, jnp.where(jr <= cur, imp_t, NEG_INF))
            rank = jnp.zeros((n_sel, tq), F32)
            for jp in range(n_sel):
                sj = score[jp:jp + 1, :]
                beats = (sj > score) | ((sj == score) & (jr > jp))
                rank = rank + jnp.where(beats, 1.0, 0.0)
            sel_ref[g] = jnp.where(rank < float(min(A_SEL_TOPK, n_sel)), 1.0, 0.0)

    qn = [jnp.sqrt(jnp.sum(jnp.square(q4t[g].astype(F32)), axis=0, keepdims=True)) for g in range(ng)]
    bound_s = [qn[g] * jnp.sqrt(kn_ref[g]) * BOUND_SLACK for g in range(ng)]
    bound_w = [qn[g] * jnp.sqrt(kn_ref[ng + g]) * BOUND_SLACK for g in range(ng)]
    worst = bound_s[0]
    for b in bound_s[1:] + bound_w:
        worst = jnp.maximum(worst, b)
    fast = jnp.max(worst) <= MAX_FAST_BOUND

    n_wb = -(-A_WINDOW // tk)
    wk = (n_wb + 1) * tk
    j0 = jnp.maximum(qi - n_wb, 0)
    off_w = pl.multiple_of(j0 * tk, tk)
    c_abs = off_w + lax.broadcasted_iota(jnp.int32, (wk, tq), 0)
    t_abs_w = t0 + lax.broadcasted_iota(jnp.int32, (wk, tq), 1)
    ok_w = (c_abs <= t_abs_w) & (c_abs > t_abs_w - A_WINDOW)
    blocks_per_iter = kt // A_SEL_BLOCK

    def normalised(acc):
        return acc[0:dh, :] * (1.0 / acc[dh:dh + 1, :])

    def value_tiles(v_ref, first, count, g):
        vt = jnp.concatenate([v_ref[first + i, g * dh:(g + 1) * dh, :] for i in range(count)], axis=1)
        return _with_ones_rows(vt)

    def emit_output():
        gs = _sigmoid(ps_ref[...])
        g0 = 2 * M_HEADS
        outs = []
        for g in range(ng):
            o_s = osw_ref[0, g]
            o_w = osw_ref[1, g]
            o_c = osw_ref[2, g]
            for h in range(hpg):
                sl = slice(h * tq, (h + 1) * tq)
                gr = g0 + 3 * (g * hpg + h)
                outs.append(gs[gr:gr + 1, :] * o_c[:, sl] + gs[gr + 1:gr + 2, :] * o_s[:, sl]
                            + gs[gr + 2:gr + 3, :] * o_w[:, sl])
        o_all = jnp.concatenate(outs, axis=0).T
        z = z_ref[...].astype(F32)
        o_ref[...] = (o_all * (z * _sigmoid(z))).astype(o_ref.dtype)

    @pl.when(fast)
    def _():
        kcol = lax.broadcasted_iota(jnp.int32, (kt, dh), 1)
        krow = lax.broadcasted_iota(jnp.int32, (kt, dh), 0)
        kblock = lax.shift_right_logical(krow, SEL_BLOCK_SHIFT)
        aux = jnp.where((kcol == 0) | (kcol == SUBLANES + kblock), 1.0, 0.0).astype(BF16)
        aux_w = jnp.where(lax.broadcasted_iota(jnp.int32, (wk, dh), 1) == 0, 1.0, 0.0).astype(BF16)
        aux_d = jnp.where(lax.broadcasted_iota(jnp.int32, (tq, dh), 1) == 0, 1.0, 0.0).astype(BF16)
        zrows = jnp.zeros((LANES - dh - BF16_SUBLANES, r), BF16)

        first_row = lax.broadcasted_iota(jnp.int32, (SUBLANES, r), 0) == 0

        def query_aug(g, shift, block_bias):
            top = jnp.where(first_row, -shift, 0.0)
            low = jnp.zeros((SUBLANES, r), F32) if block_bias is None else jnp.concatenate([block_bias] * hpg, axis=1)
            return jnp.concatenate([q4t[g], jnp.concatenate([top, low], axis=0).astype(BF16), zrows], axis=0)

        def step(j, accs):
            k_all = ks_ref[j * kt:(j + 1) * kt, :]
            out = []
            for g in range(ng):
                k_aug = jnp.concatenate([k_all[:, g * dh:(g + 1) * dh], aux], axis=1)
                bias = sel_ref[g, j * blocks_per_iter:(j + 1) * blocks_per_iter, :]
                p = jnp.exp2(_dot(k_aug, query_aug(g, bound_s[g], bias))).astype(BF16)
                out.append(accs[g] + _dot(value_tiles(vst_ref, j * tiles_per_iter, tiles_per_iter, g), p))
            return tuple(out)

        def finish(accs):
            kd_all = ks_ref[pl.ds(pl.multiple_of(t0, tq), tq), :]
            causal = (lax.broadcasted_iota(jnp.int32, (tq, tq), 0)
                      <= lax.broadcasted_iota(jnp.int32, (tq, tq), 1))
            kw_all = kw_ref[pl.ds(off_w, wk), :]
            for g in range(ng):
                kd_aug = jnp.concatenate([kd_all[:, g * dh:(g + 1) * dh], aux_d], axis=1)
                s_d = _mask_heads(_dot(kd_aug, query_aug(g, bound_s[g], None)), causal, -EXCLUDE, tq)
                acc = accs[g] + _dot(value_tiles(vst_ref, qi, 1, g), jnp.exp2(s_d).astype(BF16))
                osw_ref[0, g] = normalised(acc)
                kw_aug = jnp.concatenate([kw_all[:, g * dh:(g + 1) * dh], aux_w], axis=1)
                s_w = _mask_heads(_dot(kw_aug, query_aug(g, bound_w[g], None)), ok_w, -EXCLUDE, tq)
                p_w = jnp.exp2(s_w).astype(BF16)
                acc_w = _dot(value_tiles(vwt_ref, j0, 1, g), p_w[0:tk, :])
                for i in range(1, n_wb + 1):
                    acc_w = acc_w + _dot(value_tiles(vwt_ref, j0 + i, 1, g), p_w[i * tk:(i + 1) * tk, :])
                osw_ref[1, g] = normalised(acc_w)

        steps_needed = (t0 + kt - 1) // kt
        for n_steps in range(seq // kt + 1):
            @pl.when(steps_needed == n_steps)
            def _(n_steps=n_steps):
                compressed_and_select()
                before_diag = jr < qi * (tq // A_SEL_BLOCK)
                for g in range(ng):
                    sel_ref[g] = jnp.where((sel_ref[g] > 0.5) & before_diag, 0.0, -EXCLUDE)
                accs = tuple(jnp.zeros((da, r), F32) for _ in range(ng))
                for j in range(n_steps):
                    accs = step(j, accs)
                finish(accs)
                emit_output()

    @pl.when(jnp.logical_not(fast))
    def _():
        compressed_and_select()
        zpad = jnp.zeros((dh, r), BF16)
        q_pad = [jnp.concatenate([q4t[g] if gg == g else zpad for gg in range(ng)], axis=0) for g in range(ng)]
        sub_s = lax.broadcasted_iota(jnp.int32, (kt, tq), 0)
        t_abs_s = t0 + lax.broadcasted_iota(jnp.int32, (kt, tq), 1)

        def body(j, carry):
            off = pl.multiple_of(j * kt, kt)
            k_all = ks_ref[pl.ds(off, kt), :]
            causal = (off + sub_s) <= t_abs_s
            out = []
            for g in range(ng):
                m, acc = carry[g]
                s = _dot(k_all, q_pad[g])
                flags = jnp.concatenate(
                    [jnp.broadcast_to(sel_ref[g, pl.ds(j * blocks_per_iter + i, 1), :], (A_SEL_BLOCK, tq))
                     for i in range(blocks_per_iter)], axis=0)
                s = _mask_heads(s, (flags > 0.5) & causal, NEG_INF, tq)
                m_new = jnp.maximum(m, jnp.max(s, axis=0, keepdims=True))
                p = jnp.exp2(s - m_new).astype(BF16)
                vt = value_tiles(vst_ref, j * tiles_per_iter, tiles_per_iter, g)
                out.append((m_new, jnp.exp2(m - m_new) * acc + _dot(vt, p)))
            return tuple(out)

        init = tuple((jnp.full((1, r), NEG_INF, F32), jnp.zeros((da, r), F32)) for _ in range(ng))
        state = lax.fori_loop(0, (t0 + tq - 1) // kt + 1, body, init)
        kw_all = kw_ref[pl.ds(off_w, wk), :]
        for g in range(ng):
            osw_ref[0, g] = normalised(state[g][1])
            s_w = _mask_heads(_dot(kw_all, q_pad[g]), ok_w, NEG_INF, tq)
            p_w = jnp.exp2(s_w - jnp.max(s_w, axis=0, keepdims=True)).astype(BF16)
            acc_w = _dot(value_tiles(vwt_ref, j0, 1, g), p_w[0:tk, :])
            for i in range(1, n_wb + 1):
                acc_w = acc_w + _dot(value_tiles(vwt_ref, j0 + i, 1, g), p_w[i * tk:(i + 1) * tk, :])
            osw_ref[1, g] = normalised(acc_w)
        emit_output()


def _nsa(p, pkv, ps, kvc, vct, q_col, z_col, bsz, seq):
    t = p.shape[0]
    g = A_KV_HEADS
    dh = A_HEAD_DIM
    aw = A_HEADS * dh
    kvw = g * dh
    tq = Q_BLOCK
    nq = seq // tq
    n_cmp = kvc.shape[3]
    tk = LANES
    rows = lambda b, qi: b * nq + qi
    kv_piece = lambda i: pl.BlockSpec((seq, kvw), lambda b, qi: (b, i))
    return pl.pallas_call(
        _nsa_kernel,
        grid=(bsz, nq),
        in_specs=[
            pl.BlockSpec((tq, aw), lambda b, qi: (rows(b, qi), q_col // aw)),
            pl.BlockSpec((None, None, g, n_cmp, dh), lambda b, qi: (0, b, 0, 0, 0)),
            pl.BlockSpec((None, g, dh, n_cmp), lambda b, qi: (b, 0, 0, 0)),
            kv_piece(2), kv_piece(3), kv_piece(4), kv_piece(5),
            pl.BlockSpec((LANES, tq), lambda b, qi: (0, rows(b, qi))),
            pl.BlockSpec((tq, aw), lambda b, qi: (rows(b, qi), z_col // aw)),
        ],
        out_specs=pl.BlockSpec((tq, aw), lambda b, qi: (rows(b, qi), 0)),
        out_shape=jax.ShapeDtypeStruct((t, aw), BF16),
        scratch_shapes=[pltpu.VMEM((g, seq // A_SEL_BLOCK, tq), F32),
                        pltpu.VMEM((seq // tk, kvw, tk), BF16),
                        pltpu.VMEM((seq // tk, kvw, tk), BF16),
                        pltpu.VMEM((2 * g, 1, 1), F32),
                        pltpu.VMEM((3, g, dh, A_GROUP * tq), F32)],
        compiler_params=_params("parallel", "arbitrary"),
        name="nsa",
    )(p, kvc, vct, pkv, pkv, pkv, pkv, ps, p)


def _merge_out_kernel(ym_ref, ya_ref, wm_ref, wa_ref, gm_ref, ga_ref, wo_ref, g_ref, x_ref, o_ref):
    um = _dot(ym_ref[...], wm_ref[...])
    ua = _dot(ya_ref[...], wa_ref[...])
    mg = _sigmoid(gm_ref[...].astype(F32)) * um + _sigmoid(ga_ref[...].astype(F32)) * ua
    out = _dot(mg.astype(BF16), wo_ref[...])
    ms = jnp.mean(out * out, axis=-1, keepdims=True)
    o_ref[...] = x_ref[...] + out * lax.rsqrt(ms + RMS_EPS) * g_ref[...]


def _merge_out(ym, ya, wm, wa, p, gm_col, ga_col, wo, g, x2d, tm=256):
    t, d = x2d.shape
    kdim = ym.shape[1]
    resident = lambda shape: pl.BlockSpec(shape, lambda i: (0, 0), pipeline_mode=pl.Buffered(1))
    return pl.pallas_call(
        _merge_out_kernel,
        grid=(t // tm,),
        in_specs=[
            pl.BlockSpec((tm, kdim), lambda i: (i, 0)),
            pl.BlockSpec((tm, kdim), lambda i: (i, 0)),
            resident((kdim, d)),
            resident((kdim, d)),
            pl.BlockSpec((tm, d), lambda i: (i, gm_col // d)),
            pl.BlockSpec((tm, d), lambda i: (i, ga_col // d)),
            resident((d, d)),
            pl.BlockSpec((1, d), lambda i: (0, 0)),
            pl.BlockSpec((tm, d), lambda i: (i, 0)),
        ],
        out_specs=pl.BlockSpec((tm, d), lambda i: (i, 0)),
        out_shape=jax.ShapeDtypeStruct((t, d), F32),
        compiler_params=_params("parallel"),
        name="merge_out",
    )(ym, ya, wm, wa, p, p, wo, g, x2d)


def _layer(x, pre_g, w_in, conv_w, conv_b, w_q, w_k, b_i, b_f, norm_g, skip,
           pe_k, w1_k, w2_k, pe_v, w1_v, w2_v, w_up_m, w_up_a, w_out, post_g):
    bsz, seq, d = x.shape
    t = bsz * seq
    mw = w_up_m.shape[0]
    aw = w_up_a.shape[0]
    g = A_KV_HEADS
    dh = A_HEAD_DIM
    kvw = g * dh
    n_gate = 3 * A_HEADS

    o_i = 4 * mw
    o_f = o_i + M_HEADS
    o_q = o_f + M_HEADS
    o_g = o_q + aw + 6 * kvw
    o_z = o_g + n_gate
    o_m = o_z + aw
    o_kv = o_q + aw
    main_pieces = ((o_m, w_in.shape[1]), (0, o_i), (o_q, o_kv), (o_z, o_m))
    tail_pieces = ((o_kv, o_g),)
    w_t = w_in.T
    tile_rows = lambda pieces, tn: jnp.asarray(
        [r // SUBLANES for lo, hi in pieces for r in range(lo, hi, tn)], jnp.int32)
    row_starts = tile_rows(main_pieces, INPROJ_TN)
    small_pieces = ((o_i, o_q), (o_g, o_z))
    gm_col = 0
    ga_col = d
    mx_col = 2 * d
    q_col = mx_col + 4 * mw
    z_col = q_col + aw

    x2d = x.reshape(t, d)
    p, ps, h = _inproj(x2d, pre_g.reshape(1, d), w_t, row_starts, small_pieces, tn=INPROJ_TN)
    pkv = _inproj_tail(h, w_t, tile_rows(tail_pieces, INPROJ_TAIL_TN), tn=INPROJ_TAIL_TN)

    xc, q, k = _conv_qk(p, mx_col, conv_w, conv_b.reshape(1, mw), w_q.astype(BF16), w_k.astype(BF16), bsz, seq)
    bias_m = jnp.concatenate([b_i, b_f]).reshape(2 * M_HEADS, 1)
    y_m = _mlstm(p, mx_col, xc, q, k, ps, bias_m, norm_g.reshape(1, mw), skip.reshape(1, mw), bsz, seq)

    half = A_CMP_STRIDE * dh
    pe = jnp.stack([pe_k.reshape(2, half), pe_v.reshape(2, half)])
    w1 = jnp.stack([w1_k, w1_v]).astype(BF16)
    w2 = jnp.stack([w2_k, w2_v]).astype(BF16)
    kvc, (w_up_m_b, w_up_a_b, w_out_b) = _compress(pkv, 0, pe, w1, w2, (w_up_m, w_up_a, w_out), bsz, seq)
    vct = jnp.swapaxes(kvc[1], 2, 3)
    y_a = _nsa(p, pkv, ps, kvc, vct, q_col, z_col, bsz, seq)

    out = _merge_out(y_m, y_a, w_up_m_b, w_up_a_b, p, gm_col, ga_col, w_out_b, post_g.reshape(1, d), x2d)
    return out.reshape(bsz, seq, d)


def kernel(x, pre_norm_g, w_in, m_conv_w, m_conv_b, m_w_q, m_w_k, m_b_i, m_b_f, m_norm_g, m_skip, a_pe_k, a_w1_k, a_w2_k, a_pe_v, a_w1_v, a_w2_v, w_up_m, w_up_a, w_out, post_norm_g):
    depth = w_in.shape[0]
    for l in range(depth):
        x = _layer(x, pre_norm_g[l], w_in[l], m_conv_w[l], m_conv_b[l], m_w_q[l], m_w_k[l], m_b_i[l], m_b_f[l],
                   m_norm_g[l], m_skip[l], a_pe_k[l], a_w1_k[l], a_w2_k[l], a_pe_v[l], a_w1_v[l], a_w2_v[l],
                   w_up_m[l], w_up_a[l], w_out[l], post_norm_g[l])
    return x
```
